```python
import math
import jax, jax.numpy as jnp
from jax import lax
import numpy as np

D_MODEL = 1024
BATCH = 2
SEQ = 8192
DEPTH = 4

SB_HEADS = 8
SB_HEAD_DIM = 64
SB_WIDTH = SB_HEADS * SB_HEAD_DIM
Q_BLOCK = 128
SGU_GROUPS = 8
SGU_GROUP_DIM = 64
SGU_WIDTH = SGU_GROUPS * SGU_GROUP_DIM
CHUNK = 128
IN_SPLITS = (SB_WIDTH, SB_WIDTH, SB_WIDTH, SGU_WIDTH, SGU_WIDTH, D_MODEL, D_MODEL)
IN_WIDTH = sum(IN_SPLITS)
N_GROUPS = 4
EXPERTS_PER_GROUP = 8
N_EXPERTS = N_GROUPS * EXPERTS_PER_GROUP
TOP_K = 2
EXPERT_FF = 512
MOE_BLOCK = 128
PLE_DIM = 256
EPS = 1e-6

kernel_name = "hybrid_stickbreak_sgu_hmoe_trunk"


def rms_norm(x, g):
    xf = x.astype(jnp.float32)
    y = xf * lax.rsqrt(jnp.mean(xf * xf, axis=-1, keepdims=True) + EPS)
    return (y * g.astype(jnp.float32)).astype(x.dtype)


def stick_breaking_attention(q, k, v):
    b, h, s, dh = q.shape
    q = q.astype(jnp.float32)
    k = k.astype(jnp.float32)
    v = v.astype(jnp.float32)
    nqb = s // Q_BLOCK
    q_blocks = q.reshape(b, h, nqb, Q_BLOCK, dh).transpose(2, 0, 1, 3, 4)
    starts = jnp.arange(nqb, dtype=jnp.int32) * Q_BLOCK
    key_pos = jnp.arange(s, dtype=jnp.int32)
    scale = 1.0 / math.sqrt(dh)

    def block(args):
        q_blk, t0 = args
        z = jnp.einsum('bhqd,bhkd->bhqk', q_blk, k) * scale
        q_pos = t0 + jnp.arange(Q_BLOCK, dtype=jnp.int32)
        causal = key_pos[None, :] < q_pos[:, None]
        log_beta = jax.nn.log_sigmoid(z)
        log_keep = jnp.where(causal, log_beta - z, 0.0)
        after = lax.cumsum(log_keep, axis=3, reverse=True) - log_keep
        a = jnp.where(causal, jnp.exp(log_beta + after), 0.0)
        return jnp.einsum('bhqk,bhkd->bhqd', a, v)

    o = lax.map(block, (q_blocks, starts))
    return o.transpose(1, 2, 0, 3, 4).reshape(b, h, s, dh)


def chunk_spatial_gating(u, v, w_s, b_s):
    b, s, g, dg = v.shape
    nc = s // CHUNK
    vc = v.reshape(b, nc, CHUNK, g, dg)
    tril = jnp.tril(jnp.ones((CHUNK, CHUNK), dtype=bool))
    w = jnp.where(tril[None], w_s, 0.0).astype(v.dtype)
    mixed = jnp.einsum('gts,bcsgd->bctgd', w, vc) + b_s.T.astype(v.dtype)[None, None, :, :, None]
    return u * mixed.reshape(b, s, g, dg)


def hierarchical_moe(h, w_gr, b_gr, w_er, b_er, w_e_in, w_e_out):
    bsz, s, d = h.shape
    n = bsz * s
    hf = h.reshape(n, d)
    g_logits = (hf @ w_gr).astype(jnp.float32) + b_gr.astype(jnp.float32)
    g_prob = jax.nn.softmax(g_logits, axis=-1)
    grp = jnp.argmax(g_prob, axis=-1).astype(jnp.int32)
    grp_w = jnp.max(g_prob, axis=-1)
    e_logits = ((hf @ w_er).astype(jnp.float32) + b_er.astype(jnp.float32))
    e_logits = e_logits.reshape(n, N_GROUPS, EXPERTS_PER_GROUP)
    e_logits = jnp.take_along_axis(e_logits, grp[:, None, None], axis=1)[:, 0]
    e_prob = jax.nn.softmax(e_logits, axis=-1)
    top_w, top_i = lax.top_k(e_prob, TOP_K)
    top_w = top_w / jnp.sum(top_w, axis=-1, keepdims=True)
    wts = (grp_w[:, None] * top_w).reshape(-1)
    eid = (grp[:, None] * EXPERTS_PER_GROUP + top_i.astype(jnp.int32)).reshape(-1)
    tok = jnp.repeat(jnp.arange(n, dtype=jnp.int32), TOP_K)
    order = jnp.argsort(eid)
    se = eid[order]
    counts = jnp.bincount(eid, length=N_EXPERTS).astype(jnp.int32)
    padded = (counts + MOE_BLOCK - 1) // MOE_BLOCK * MOE_BLOCK
    off = jnp.cumsum(counts) - counts
    pend = jnp.cumsum(padded)
    poff = pend - padded
    rank = jnp.arange(n * TOP_K, dtype=jnp.int32) - off[se]
    dest = poff[se] + rank
    cap = n * TOP_K + N_EXPERTS * MOE_BLOCK
    slot_tok = jnp.full((cap,), n, dtype=jnp.int32).at[dest].set(tok[order])
    slot_w = jnp.zeros((cap,), jnp.float32).at[dest].set(wts[order])
    nblk = cap // MOE_BLOCK
    blk_start = jnp.arange(nblk, dtype=jnp.int32) * MOE_BLOCK
    blk_e = jnp.minimum(jnp.sum(pend[None, :] <= blk_start[:, None], axis=1), N_EXPERTS - 1)
    hp = jnp.concatenate([hf, jnp.zeros((1, d), hf.dtype)], axis=0)
    xs = hp[slot_tok].reshape(nblk, MOE_BLOCK, d)

    def expert_block(args):
        xb, e = args
        gate, up = jnp.split(xb @ w_e_in[e], 2, axis=-1)
        return (jax.nn.silu(gate) * up) @ w_e_out[e]

    ys = lax.map(expert_block, (xs, blk_e)).reshape(cap, d)
    ys = ys * slot_w[:, None].astype(ys.dtype)
    out = jnp.zeros((n + 1, d), ys.dtype).at[slot_tok].add(ys)[:n]
    return out.reshape(bsz, s, d)


def setup_inputs(seed: int = 0) -> dict:
    key = jax.random.key(seed)
    ks = jax.random.split(key, 24)
    f32 = jnp.float32
    nrm = lambda k, shape, scale: jax.random.normal(k, shape, f32) * scale
    gain = lambda k, shape: 1.0 + 0.01 * jax.random.normal(k, shape, f32)
    L, D = DEPTH, D_MODEL
    return {
        "x": nrm(ks[0], (BATCH, SEQ, D), 1.0),
        "p": nrm(ks[1], (DEPTH, BATCH, SEQ, PLE_DIM), 1.0),
        "norm_mix": gain(ks[2], (L, D)),
        "w_in": nrm(ks[3], (L, D, IN_WIDTH), D ** -0.5),
        "q_norm": gain(ks[4], (L, SB_HEAD_DIM)),
        "k_norm": gain(ks[5], (L, SB_HEAD_DIM)),
        "sgu_norm": gain(ks[6], (L, SGU_WIDTH)),
        "w_spatial": nrm(ks[7], (L, SGU_GROUPS, CHUNK, CHUNK), CHUNK ** -0.5),
        "b_spatial": gain(ks[8], (L, SGU_GROUPS, CHUNK)),
        "w_up_a": nrm(ks[9], (L, SB_WIDTH, D), SB_WIDTH ** -0.5),
        "w_up_b": nrm(ks[10], (L, SGU_WIDTH, D), SGU_WIDTH ** -0.5),
        "w_out": nrm(ks[11], (L, D, D), D ** -0.5),
        "norm_ffn": gain(ks[12], (L, D)),
        "w_group_router": nrm(ks[13], (L, D, N_GROUPS), D ** -0.5),
        "b_group_router": nrm(ks[14], (L, N_GROUPS), 0.01),
        "w_expert_router": nrm(ks[15], (L, D, N_EXPERTS), D ** -0.5),
        "b_expert_router": nrm(ks[16], (L, N_EXPERTS), 0.01),
        "w_expert_in": nrm(ks[17], (L, N_EXPERTS, D, 2 * EXPERT_FF), D ** -0.5),
        "w_expert_out": nrm(ks[18], (L, N_EXPERTS, EXPERT_FF, D), EXPERT_FF ** -0.5),
        "norm_ple": gain(ks[19], (L, D)),
        "w_ple_gate": nrm(ks[20], (L, D, D), D ** -0.5),
        "w_ple_proj": nrm(ks[21], (L, PLE_DIM, D), PLE_DIM ** -0.5),
    }


def reference(x, p, norm_mix, w_in, q_norm, k_norm, sgu_norm, w_spatial, b_spatial,
              w_up_a, w_up_b, w_out, norm_ffn, w_group_router, b_group_router,
              w_expert_router, b_expert_router, w_expert_in, w_expert_out,
              norm_ple, w_ple_gate, w_ple_proj):
    bsz, s, d = x.shape
    cuts = list(np.cumsum(IN_SPLITS)[:-1])
    for i in range(DEPTH):
        h = rms_norm(x, norm_mix[i])
        z = h @ w_in[i]
        q, k, v, u_s, v_s, g_a, g_b = jnp.split(z, cuts, axis=-1)
        q = rms_norm(q.reshape(bsz, s, SB_HEADS, SB_HEAD_DIM), q_norm[i]).transpose(0, 2, 1, 3)
        k = rms_norm(k.reshape(bsz, s, SB_HEADS, SB_HEAD_DIM), k_norm[i]).transpose(0, 2, 1, 3)
        v = v.reshape(bsz, s, SB_HEADS, SB_HEAD_DIM).transpose(0, 2, 1, 3)
        o_a = stick_breaking_attention(q, k, v).astype(x.dtype)
        o_a = o_a.transpose(0, 2, 1, 3).reshape(bsz, s, SB_WIDTH)
        u_s = jax.nn.gelu(u_s).reshape(bsz, s, SGU_GROUPS, SGU_GROUP_DIM)
        v_s = rms_norm(jax.nn.gelu(v_s), sgu_norm[i]).reshape(bsz, s, SGU_GROUPS, SGU_GROUP_DIM)
        o_b = chunk_spatial_gating(u_s, v_s, w_spatial[i], b_spatial[i]).reshape(bsz, s, SGU_WIDTH)
        merged = jax.nn.sigmoid(g_a) * (o_a @ w_up_a[i]) + jax.nn.sigmoid(g_b) * (o_b @ w_up_b[i])
        x = x + merged @ w_out[i]
        h2 = rms_norm(x, norm_ffn[i])
        x = x + hierarchical_moe(h2, w_group_router[i], b_group_router[i],
                                 w_expert_router[i], b_expert_router[i],
                                 w_expert_in[i], w_expert_out[i])
        gate = jax.nn.sigmoid(rms_norm(x, norm_ple[i]) @ w_ple_gate[i])
        x = x + gate * (p[i].astype(x.dtype) @ w_ple_proj[i])
    return x
```

```python
import functools
import math

import jax
import jax.numpy as jnp
from jax import lax
from jax.experimental import pallas as pl
from jax.experimental.pallas import tpu as pltpu

F32 = jnp.float32
BF16 = jnp.bfloat16

D_MODEL = 1024
HEADS = 8
HEAD_DIM = 64
WIDTH = HEADS * HEAD_DIM
CHUNK = 128
IN_WIDTH = 5 * WIDTH + 2 * D_MODEL
N_GROUPS = 4
EXPERTS_PER_GROUP = 8
N_EXPERTS = N_GROUPS * EXPERTS_PER_GROUP
EXPERT_FF = 512
PLE_DIM = 256
EPS = 1e-6

LANES = 128
ROUTE_LANE0 = N_GROUPS
TM_IN = 256
TM_MIX = 256
TM_ROW = 256
MOE_ROWS = 256
DEAD_LOG = -105.0
VMEM_LIMIT = 48 * 1024 * 1024


def _rms(x, gain):
    return x * lax.rsqrt(jnp.mean(x * x, axis=-1, keepdims=True) + EPS) * gain


def _inproj_kernel(x_ref, g_ref, w_ref, qn_ref, kn_ref, sn_ref,
                   q_ref, k_ref, v_ref, u_ref, vs_ref, ga_ref, gb_ref):
    hb = _rms(x_ref[...], g_ref[...]).astype(BF16)

    def proj(lo, hi):
        return jnp.dot(hb, w_ref[:, lo:hi], preferred_element_type=F32)

    first = lax.broadcasted_iota(jnp.int32, (1, LANES), 1) < HEAD_DIM

    def head_norm(z, gain, scale):
        outs = []
        for c in range(WIDTH // LANES):
            zc = z[:, c * LANES:(c + 1) * LANES]
            sq = zc * zc
            sa = jnp.sum(jnp.where(first, sq, 0.0), axis=-1, keepdims=True)
            sb = jnp.sum(jnp.where(first, 0.0, sq), axis=-1, keepdims=True)
            ms = jnp.where(first, sa, sb) * (1.0 / HEAD_DIM)
            outs.append(zc * lax.rsqrt(ms + EPS) * (gain * scale))
        return jnp.concatenate(outs, axis=-1)

    w = WIDTH
    q_ref[...] = head_norm(proj(0, w), qn_ref[...], 1.0 / math.sqrt(HEAD_DIM)).astype(BF16)
    k_ref[...] = head_norm(proj(w, 2 * w), kn_ref[...], 1.0).astype(BF16)
    v_ref[...] = proj(2 * w, 3 * w).astype(BF16)
    u_ref[...] = jax.nn.gelu(proj(3 * w, 4 * w)).astype(BF16)
    vs_ref[...] = _rms(jax.nn.gelu(proj(4 * w, 5 * w)), sn_ref[...]).astype(BF16)
    ga_ref[...] = jax.nn.sigmoid(proj(5 * w, 5 * w + D_MODEL)).astype(BF16)
    gb_ref[...] = jax.nn.sigmoid(proj(5 * w + D_MODEL, IN_WIDTH)).astype(BF16)


def _inproj(layer, x, norm_mix, w_in, qn, kn, sn):
    n = x.shape[0]
    row = lambda width: pl.BlockSpec((TM_IN, width), lambda i: (i, 0))
    vec = lambda width: pl.BlockSpec((None, 1, width), lambda i: (layer, 0, 0))
    out = lambda width: jax.ShapeDtypeStruct((n, width), BF16)
    return pl.pallas_call(
        _inproj_kernel,
        grid=(n // TM_IN,),
        in_specs=[row(D_MODEL), vec(D_MODEL),
                  pl.BlockSpec((None, D_MODEL, IN_WIDTH), lambda i: (layer, 0, 0)),
                  vec(LANES), vec(LANES), vec(WIDTH)],
        out_specs=[row(WIDTH)] * 5 + [row(D_MODEL)] * 2,
        out_shape=[out(WIDTH)] * 5 + [out(D_MODEL)] * 2,
        compiler_params=pltpu.CompilerParams(
            dimension_semantics=("arbitrary",), vmem_limit_bytes=VMEM_LIMIT),
        name="inproj",
    )(x, norm_mix, w_in, qn, kn, sn)


def _attn_kernel(q_ref, k_ref, v_ref, o_ref):
    qb = pl.program_id(2)
    q = q_ref[...]
    lane = lax.broadcasted_iota(jnp.int32, (CHUNK, LANES), 1)
    row = lax.broadcasted_iota(jnp.int32, (CHUNK, LANES), 0)
    first = lane < HEAD_DIM
    zero = jnp.zeros_like(q)
    q_heads = (jnp.where(first, q, zero), jnp.where(first, zero, q))
    suffix = jnp.where(row > lane, 1.0, 0.0).astype(BF16)
    causal = lane < row

    def key_block(j, carries, acc, diagonal):
        start = pl.multiple_of(j * CHUNK, CHUNK)
        kb = k_ref[pl.ds(start, CHUNK), :]
        vb = v_ref[pl.ds(start, CHUNK), :]
        outs, new_carries = [], []
        for qh, carry in zip(q_heads, carries):
            z = lax.dot_general(qh, kb, (((1,), (1,)), ((), ())), preferred_element_type=F32)
            sp = jnp.maximum(z, 0.0) + jnp.log(1.0 + jnp.exp(-jnp.abs(z)))
            log_keep = -sp
            log_beta = z - sp
            if diagonal:
                log_keep = jnp.where(causal, log_keep, 0.0)
            hi = log_keep.astype(BF16)
            lo = (log_keep - hi.astype(F32)).astype(BF16)
            after = (jnp.dot(hi, suffix, preferred_element_type=F32)
                     + jnp.dot(lo, suffix, preferred_element_type=F32))
            a = jnp.exp(log_beta + after + carry)
            if diagonal:
                a = jnp.where(causal, a, 0.0)
            outs.append(jnp.dot(a.astype(BF16), vb, preferred_element_type=F32))
            new_carries.append(carry + after[:, 0:1] + log_keep[:, 0:1])
        acc = acc + jnp.where(first, outs[0], outs[1])
        return tuple(new_carries), acc

    def alive(carries):
        return jnp.maximum(jnp.max(carries[0]), jnp.max(carries[1])) > DEAD_LOG

    zc = jnp.zeros((CHUNK, 1), F32)
    carries, acc = key_block(qb, (zc, zc), jnp.zeros((CHUNK, LANES), F32), True)

    def cond(state):
        j, live, _, _, _ = state
        return jnp.logical_and(j >= 0, live)

    def body(state):
        j, _, ca, cb, acc = state
        (ca, cb), acc = key_block(j, (ca, cb), acc, False)
        return j - 1, alive((ca, cb)), ca, cb, acc

    state = lax.while_loop(cond, body, (qb - 1, alive(carries), carries[0], carries[1], acc))
    o_ref[...] = state[4].astype(BF16)


def _attention(q, k, v, batch, seq):
    q3, k3, v3 = (t.reshape(batch, seq, WIDTH) for t in (q, k, v))
    blk = pl.BlockSpec((None, CHUNK, LANES), lambda b, hp, qb: (b, qb, hp))
    full = pl.BlockSpec((None, seq, LANES), lambda b, hp, qb: (b, 0, hp))
    o = pl.pallas_call(
        _attn_kernel,
        grid=(batch, WIDTH // LANES, seq // CHUNK),
        in_specs=[blk, full, full],
        out_specs=blk,
        out_shape=jax.ShapeDtypeStruct((batch, seq, WIDTH), BF16),
        compiler_params=pltpu.CompilerParams(
            dimension_semantics=("arbitrary", "arbitrary", "arbitrary"),
            vmem_limit_bytes=VMEM_LIMIT),
        name="attn",
    )(q3, k3, v3)
    return o.reshape(batch * seq, WIDTH)


def _mix_kernel(x_ref, oa_ref, u_ref, vs_ref, ga_ref, gb_ref,
                wsp_ref, bsp_ref, wua_ref, wub_ref, wo_ref, gf_ref, wr_ref, br_ref,
                xo_ref, h_ref, route_ref, cnt_ref, run_ref):
    step = pl.program_id(0)

    @pl.when(step == 0)
    def _():
        run_ref[...] = jnp.zeros_like(run_ref)

    lane = lax.broadcasted_iota(jnp.int32, (CHUNK, LANES), 1)
    row = lax.broadcasted_iota(jnp.int32, (CHUNK, LANES), 0)
    first = lane < HEAD_DIM
    tril = lane <= row

    ob_chunks = []
    for c in range(TM_MIX // CHUNK):
        rows = slice(c * CHUNK, (c + 1) * CHUNK)
        cols = []
        for gp in range(WIDTH // LANES):
            vpair = vs_ref[rows, gp * LANES:(gp + 1) * LANES]
            mixed = []
            for g in (2 * gp, 2 * gp + 1):
                wg = jnp.where(tril, wsp_ref[g], 0.0).astype(BF16)
                mixed.append(jnp.dot(wg, vpair, preferred_element_type=F32))
            cols.append(jnp.where(first, mixed[0], mixed[1]))
        mixed = jnp.concatenate(cols, axis=-1) + bsp_ref[...]
        ob_chunks.append((u_ref[rows, :].astype(F32) * mixed).astype(BF16))
    ob = jnp.concatenate(ob_chunks, axis=0)

    up_a = jnp.dot(oa_ref[...], wua_ref[...], preferred_element_type=F32)
    up_b = jnp.dot(ob, wub_ref[...], preferred_element_type=F32)
    merged = ga_ref[...].astype(F32) * up_a + gb_ref[...].astype(F32) * up_b
    x = x_ref[...] + jnp.dot(merged.astype(BF16), wo_ref[...], preferred_element_type=F32)
    xo_ref[...] = x

    h = _rms(x, gf_ref[...])
    h_ref[...] = h
    logits = jnp.dot(h.astype(BF16), wr_ref[...], preferred_element_type=F32) + br_ref[...]
    lane_t = lax.broadcasted_iota(jnp.int32, (TM_MIX, LANES), 1)
    lanef = lane_t.astype(F32)
    neg = -jnp.inf
    far = float(LANES)

    def first_max(vals):
        m = jnp.max(vals, axis=-1, keepdims=True)
        idx = jnp.min(jnp.where(vals == m, lanef, far), axis=-1, keepdims=True)
        return m, idx

    gl = jnp.where(lane_t < N_GROUPS, logits, neg)
    gmax, grp = first_max(gl)
    grp_w = 1.0 / jnp.sum(jnp.exp(gl - gmax), axis=-1, keepdims=True)
    lo_lane = ROUTE_LANE0 + EXPERTS_PER_GROUP * grp
    in_group = jnp.logical_and(lanef >= lo_lane, lanef < lo_lane + EXPERTS_PER_GROUP)
    el = jnp.where(in_group, logits, neg)
    m1, i1 = first_max(el)
    m2, i2 = first_max(jnp.where(lanef == i1, neg, el))
    e21 = jnp.exp(m2 - m1)
    w1 = grp_w / (1.0 + e21)
    w2 = w1 * e21

    onehot = jnp.logical_or(lanef == i1, lanef == i2)
    rt = lax.broadcasted_iota(jnp.int32, (TM_MIX, TM_MIX), 0)
    ct = lax.broadcasted_iota(jnp.int32, (TM_MIX, TM_MIX), 1)
    before = jnp.where(ct < rt, 1.0, 0.0).astype(BF16)
    prior = jnp.dot(before, jnp.where(onehot, 1.0, 0.0).astype(BF16),
                    preferred_element_type=F32) + run_ref[...]
    r1 = jnp.sum(jnp.where(lanef == i1, prior, 0.0), axis=-1, keepdims=True)
    r2 = jnp.sum(jnp.where(lanef == i2, prior, 0.0), axis=-1, keepdims=True)
    run = run_ref[...] + jnp.sum(jnp.where(onehot, 1.0, 0.0), axis=0, keepdims=True)
    run_ref[...] = run
    cnt_ref[...] = run

    fields = (i1 - ROUTE_LANE0, i2 - ROUTE_LANE0, w1, w2, r1, r2)
    route = jnp.zeros((TM_MIX, LANES), F32)
    for pos, val in enumerate(fields):
        route = jnp.where(lane_t == pos, val, route)
    route_ref[...] = route


def _mix(layer, x, oa, u, vs, ga, gb, wsp, bsp, wua, wub, wo, gf, wr, br):
    n = x.shape[0]
    row = lambda width: pl.BlockSpec((TM_MIX, width), lambda i: (i, 0))
    lay = lambda *shape: pl.BlockSpec((None,) + shape, lambda i: (layer,) + (0,) * len(shape))
    return pl.pallas_call(
        _mix_kernel,
        grid=(n // TM_MIX,),
        in_specs=[row(D_MODEL), row(WIDTH), row(WIDTH), row(WIDTH), row(D_MODEL), row(D_MODEL),
                  lay(HEADS, CHUNK, CHUNK), lay(CHUNK, WIDTH), lay(WIDTH, D_MODEL),
                  lay(WIDTH, D_MODEL), lay(D_MODEL, D_MODEL), lay(1, D_MODEL),
                  lay(D_MODEL, LANES), lay(1, LANES)],
        out_specs=[row(D_MODEL), row(D_MODEL), row(LANES),
                   pl.BlockSpec((1, LANES), lambda i: (0, 0))],
        out_shape=[jax.ShapeDtypeStruct((n, D_MODEL), F32),
                   jax.ShapeDtypeStruct((n, D_MODEL), F32),
                   jax.ShapeDtypeStruct((n, LANES), F32),
                   jax.ShapeDtypeStruct((1, LANES), F32)],
        scratch_shapes=[pltpu.VMEM((1, LANES), F32)],
        compiler_params=pltpu.CompilerParams(
            dimension_semantics=("arbitrary",), vmem_limit_bytes=VMEM_LIMIT),
        name="mix",
    )(x, oa, u, vs, ga, gb, wsp, bsp, wua, wub, wo, gf, wr, br)


def _dispatch_kernel(dest_ref, h_ref, xs_in_ref, xs_ref, sem):
    del xs_in_ref
    base = pl.program_id(0) * (2 * TM_ROW)

    def copy(t, k):
        d = dest_ref[base + 2 * t + k]
        return pltpu.make_async_copy(h_ref.at[pl.ds(t, 1), :], xs_ref.at[pl.ds(d, 1), :], sem)

    def start(t, c):
        copy(t, 0).start()
        copy(t, 1).start()
        return c

    def wait(t, c):
        copy(t, 0).wait()
        copy(t, 1).wait()
        return c

    lax.fori_loop(0, TM_ROW, start, 0)
    lax.fori_loop(0, TM_ROW, wait, 0)


def _dispatch(dest, h, cap):
    n = h.shape[0]
    return pl.pallas_call(
        _dispatch_kernel,
        grid_spec=pltpu.PrefetchScalarGridSpec(
            num_scalar_prefetch=1,
            grid=(n // TM_ROW,),
            in_specs=[pl.BlockSpec((TM_ROW, D_MODEL), lambda i, dest: (i, 0)),
                      pl.BlockSpec(memory_space=pl.ANY)],
            out_specs=pl.BlockSpec(memory_space=pl.ANY),
            scratch_shapes=[pltpu.SemaphoreType.DMA(())],
        ),
        out_shape=jax.ShapeDtypeStruct((cap, D_MODEL), F32),
        input_output_aliases={2: 0},
        compiler_params=pltpu.CompilerParams(
            dimension_semantics=("arbitrary",), has_side_effects=True),
        name="dispatch",
    )(dest, h, jnp.zeros((cap, D_MODEL), F32))


def _expert_kernel(blk_e_ref, nused_ref, xs_ref, wi_ref, wo_ref, ys_ref):
    del blk_e_ref
    used = pl.program_id(0) < nused_ref[0]

    @pl.when(jnp.logical_not(used))
    def _():
        ys_ref[...] = jnp.zeros_like(ys_ref)

    @pl.when(used)
    def _():
        gu = jnp.dot(xs_ref[...].astype(BF16), wi_ref[...], preferred_element_type=F32)
        act = jax.nn.silu(gu[:, :EXPERT_FF]) * gu[:, EXPERT_FF:]
        ys_ref[...] = jnp.dot(act.astype(BF16), wo_ref[...], preferred_element_type=F32)


def _experts(layer, blk_e, nused, xs, w_e_in, w_e_out):
    cap = xs.shape[0]
    row = pl.BlockSpec((MOE_ROWS, D_MODEL), lambda b, blk_e, nused: (b, 0))
    return pl.pallas_call(
        _expert_kernel,
        grid_spec=pltpu.PrefetchScalarGridSpec(
            num_scalar_prefetch=2,
            grid=(cap // MOE_ROWS,),
            in_specs=[row,
                      pl.BlockSpec((None, None, D_MODEL, 2 * EXPERT_FF),
                                   lambda b, blk_e, nused: (layer, blk_e[b], 0, 0)),
                      pl.BlockSpec((None, None, EXPERT_FF, D_MODEL),
                                   lambda b, blk_e, nused: (layer, blk_e[b], 0, 0))],
            out_specs=row,
        ),
        out_shape=jax.ShapeDtypeStruct((cap, D_MODEL), F32),
        compiler_params=pltpu.CompilerParams(
            dimension_semantics=("arbitrary",), vmem_limit_bytes=VMEM_LIMIT),
        name="experts",
    )(blk_e, nused, xs, w_e_in, w_e_out)


def _ple_kernel(dest_ref, x_ref, route_ref, p_ref, ys_ref, gp_ref, wg_ref, wp_ref,
                xo_ref, y0_ref, y1_ref, sem):
    base = pl.program_id(0) * (2 * TM_ROW)

    def copy(t, k):
        d = dest_ref[base + 2 * t + k]
        dst = y0_ref if k == 0 else y1_ref
        return pltpu.make_async_copy(ys_ref.at[pl.ds(d, 1), :], dst.at[pl.ds(t, 1), :], sem)

    def start(t, c):
        copy(t, 0).start()
        copy(t, 1).start()
        return c

    def wait(t, c):
        copy(t, 0).wait()
        copy(t, 1).wait()
        return c

    lax.fori_loop(0, TM_ROW, start, 0)
    pe = jnp.dot(p_ref[...].astype(BF16), wp_ref[...], preferred_element_type=F32)
    lax.fori_loop(0, TM_ROW, wait, 0)

    route = route_ref[...]
    x = x_ref[...] + route[:, 2:3] * y0_ref[...] + route[:, 3:4] * y1_ref[...]
    gate = jax.nn.sigmoid(jnp.dot(_rms(x, gp_ref[...]).astype(BF16), wg_ref[...],
                                  preferred_element_type=F32))
    xo_ref[...] = x + gate * pe


def _ple(layer, dest, x, route, p, ys, gp, wg, wp):
    n = x.shape[0]
    row = lambda width: pl.BlockSpec((TM_ROW, width), lambda i, dest: (i, 0))
    lay = lambda *shape: pl.BlockSpec((None,) + shape,
                                      lambda i, dest: (layer,) + (0,) * len(shape))
    return pl.pallas_call(
        _ple_kernel,
        grid_spec=pltpu.PrefetchScalarGridSpec(
            num_scalar_prefetch=1,
            grid=(n // TM_ROW,),
            in_specs=[row(D_MODEL), row(LANES),
                      pl.BlockSpec((None, TM_ROW, PLE_DIM), lambda i, dest: (layer, i, 0)),
                      pl.BlockSpec(memory_space=pl.ANY),
                      lay(1, D_MODEL), lay(D_MODEL, D_MODEL), lay(PLE_DIM, D_MODEL)],
            out_specs=row(D_MODEL),
            scratch_shapes=[pltpu.VMEM((TM_ROW, D_MODEL), F32),
                            pltpu.VMEM((TM_ROW, D_MODEL), F32),
                            pltpu.SemaphoreType.DMA(())],
        ),
        out_shape=jax.ShapeDtypeStruct((n, D_MODEL), F32),
        compiler_params=pltpu.CompilerParams(
            dimension_semantics=("arbitrary",), vmem_limit_bytes=VMEM_LIMIT),
        name="ple",
    )(dest, x, route, p, ys, gp, wg, wp)


def _slot_layout(route, counts, cap):
    eid = route[:, 0:2].astype(jnp.int32)
    rank = route[:, 4:6].astype(jnp.int32)
    cnt = counts[0, ROUTE_LANE0:ROUTE_LANE0 + N_EXPERTS].astype(jnp.int32)
    padded = (cnt + MOE_ROWS - 1) // MOE_ROWS * MOE_ROWS
    pend = jnp.cumsum(padded)
    poff = pend - padded
    onehot = eid[:, :, None] == jnp.arange(N_EXPERTS, dtype=jnp.int32)
    dest = jnp.sum(jnp.where(onehot, poff, 0), axis=-1) + rank
    blk_start = jnp.arange(cap // MOE_ROWS, dtype=jnp.int32) * MOE_ROWS
    blk_e = jnp.minimum(jnp.sum(pend[None, :] <= blk_start[:, None], axis=1), N_EXPERTS - 1)
    nused = (pend[-1:] // MOE_ROWS).astype(jnp.int32)
    return dest.reshape(-1).astype(jnp.int32), blk_e.astype(jnp.int32), nused


def kernel(x, p, norm_mix, w_in, q_norm, k_norm, sgu_norm, w_spatial, b_spatial, w_up_a, w_up_b,
           w_out, norm_ffn, w_group_router, b_group_router, w_expert_router, b_expert_router,
           w_expert_in, w_expert_out, norm_ple, w_ple_gate, w_ple_proj):
    batch, seq, d = x.shape
    depth = w_in.shape[0]
    n = batch * seq
    cap = 2 * n + N_EXPERTS * MOE_ROWS

    vec = lambda a: a[:, None, :]
    w_in_b, w_ua_b, w_ub_b, w_o_b = (a.astype(BF16) for a in (w_in, w_up_a, w_up_b, w_out))
    w_ei_b, w_eo_b = w_expert_in.astype(BF16), w_expert_out.astype(BF16)
    w_pg_b, w_pp_b = w_ple_gate.astype(BF16), w_ple_proj.astype(BF16)
    qn2, kn2 = vec(jnp.tile(q_norm, (1, 2))), vec(jnp.tile(k_norm, (1, 2)))
    b_sp = jnp.repeat(jnp.swapaxes(b_spatial, 1, 2), HEAD_DIM, axis=2)
    pad = LANES - N_GROUPS - N_EXPERTS
    w_r = jnp.pad(jnp.concatenate([w_group_router, w_expert_router], axis=2),
                  ((0, 0), (0, 0), (0, pad))).astype(BF16)
    b_r = vec(jnp.pad(jnp.concatenate([b_group_router, b_expert_router], axis=1),
                      ((0, 0), (0, pad))))
    p2 = p.reshape(depth, n, PLE_DIM)

    xf = x.reshape(n, d)
    for i in range(depth):
        q, k, v, u, vs, ga, gb = _inproj(i, xf, vec(norm_mix), w_in_b, qn2, kn2, vec(sgu_norm))
        oa = _attention(q, k, v, batch, seq)
        xf, h, route, counts = _mix(i, xf, oa, u, vs, ga, gb, w_spatial, b_sp, w_ua_b, w_ub_b,
                                    w_o_b, vec(norm_ffn), w_r, b_r)
        dest, blk_e, nused = _slot_layout(route, counts, cap)
        xs = _dispatch(dest, h, cap)
        ys = _experts(i, blk_e, nused, xs, w_ei_b, w_eo_b)
        xf = _ple(i, dest, xf, route, p2, ys, vec(norm_ple), w_pg_b, w_pp_b)
    return xf.reshape(batch, seq, d)
```

```python
import functools
import math

import jax
import jax.numpy as jnp
from jax import lax
from jax.experimental import pallas as pl
from jax.experimental.pallas import tpu as pltpu

F32 = jnp.float32
BF16 = jnp.bfloat16

D_MODEL = 1024
HEADS = 8
HEAD_DIM = 64
WIDTH = HEADS * HEAD_DIM
CHUNK = 128
IN_WIDTH = 5 * WIDTH + 2 * D_MODEL
N_GROUPS = 4
EXPERTS_PER_GROUP = 8
N_EXPERTS = N_GROUPS * EXPERTS_PER_GROUP
EXPERT_FF = 512
PLE_DIM = 256
EPS = 1e-6

LANES = 128
ROUTE_LANE0 = N_GROUPS
TM_IN = 256
TM_MIX = 256
TM_ROW = 256
MOE_ROWS = 256
LOG2E = 1.4426950408889634
DEAD_LOG2 = -105.0 * LOG2E
VMEM_LIMIT = 48 * 1024 * 1024


def _rms(x, gain):
    return x * lax.rsqrt(jnp.mean(x * x, axis=-1, keepdims=True) + EPS) * gain


def _inproj_kernel(x_ref, g_ref, w_ref, qn_ref, kn_ref, sn_ref,
                   q_ref, k_ref, v_ref, u_ref, vs_ref, ga_ref, gb_ref):
    hb = _rms(x_ref[...], g_ref[...]).astype(BF16)

    def proj(lo, hi):
        return jnp.dot(hb, w_ref[:, lo:hi], preferred_element_type=F32)

    first = lax.broadcasted_iota(jnp.int32, (1, LANES), 1) < HEAD_DIM

    def head_norm(z, gain, scale):
        outs = []
        for c in range(WIDTH // LANES):
            zc = z[:, c * LANES:(c + 1) * LANES]
            sq = zc * zc
            sa = jnp.sum(jnp.where(first, sq, 0.0), axis=-1, keepdims=True)
            sb = jnp.sum(jnp.where(first, 0.0, sq), axis=-1, keepdims=True)
            ms = jnp.where(first, sa, sb) * (1.0 / HEAD_DIM)
            outs.append(zc * lax.rsqrt(ms + EPS) * (gain * scale))
        return jnp.concatenate(outs, axis=-1)

    w = WIDTH
    q_ref[...] = head_norm(proj(0, w), qn_ref[...], LOG2E / math.sqrt(HEAD_DIM)).astype(BF16)
    k_ref[...] = head_norm(proj(w, 2 * w), kn_ref[...], 1.0).astype(BF16)
    v_ref[...] = proj(2 * w, 3 * w).astype(BF16)
    u_ref[...] = jax.nn.gelu(proj(3 * w, 4 * w)).astype(BF16)
    vs_ref[...] = _rms(jax.nn.gelu(proj(4 * w, 5 * w)), sn_ref[...]).astype(BF16)
    ga_ref[...] = jax.nn.sigmoid(proj(5 * w, 5 * w + D_MODEL)).astype(BF16)
    gb_ref[...] = jax.nn.sigmoid(proj(5 * w + D_MODEL, IN_WIDTH)).astype(BF16)


def _inproj(layer, x, norm_mix, w_in, qn, kn, sn):
    n = x.shape[0]
    row = lambda width: pl.BlockSpec((TM_IN, width), lambda i: (i, 0))
    vec = lambda width: pl.BlockSpec((None, 1, width), lambda i: (layer, 0, 0))
    out = lambda width: jax.ShapeDtypeStruct((n, width), BF16)
    return pl.pallas_call(
        _inproj_kernel,
        grid=(n // TM_IN,),
        in_specs=[row(D_MODEL), vec(D_MODEL),
                  pl.BlockSpec((None, D_MODEL, IN_WIDTH), lambda i: (layer, 0, 0)),
                  vec(LANES), vec(LANES), vec(WIDTH)],
        out_specs=[row(WIDTH)] * 5 + [row(D_MODEL)] * 2,
        out_shape=[out(WIDTH)] * 5 + [out(D_MODEL)] * 2,
        compiler_params=pltpu.CompilerParams(
            dimension_semantics=("arbitrary",), vmem_limit_bytes=VMEM_LIMIT),
        name="inproj",
    )(x, norm_mix, w_in, qn, kn, sn)


def _attn_kernel(q_ref, k_ref, v_ref, o_ref, carry_ref, acc_ref):
    qb = pl.program_id(1)
    pairs = WIDTH // LANES
    lane = lax.broadcasted_iota(jnp.int32, (2 * CHUNK, LANES), 1)
    row = lax.broadcasted_iota(jnp.int32, (2 * CHUNK, LANES), 0)
    upper = row < CHUNK
    query = jnp.where(upper, row, row - CHUNK)
    own = jnp.logical_xor(lane < HEAD_DIM, jnp.logical_not(upper))
    causal = lane < query
    first = lax.broadcasted_iota(jnp.int32, (CHUNK, LANES), 1) < HEAD_DIM
    kr = lax.broadcasted_iota(jnp.int32, (2 * LANES, 2 * LANES), 0) % LANES
    kc = lax.broadcasted_iota(jnp.int32, (2 * LANES, 2 * LANES), 1)
    tail = jnp.where(jnp.logical_or(kr > kc, kc >= LANES), 1.0, 0.0).astype(BF16)
    sign = jnp.uint32(0x80000000)

    cols = [slice(p * LANES, (p + 1) * LANES) for p in range(pairs)]
    q2s = []
    for p in range(pairs):
        qp = q_ref[:, cols[p]]
        q2 = jnp.concatenate([qp, qp], axis=0)
        q2s.append(jnp.where(own, q2, jnp.zeros_like(q2)))

    def key_block(j, diagonal):
        start = pl.multiple_of(j * CHUNK, CHUNK)
        zs = [lax.dot_general(q2s[p], k_ref[pl.ds(start, CHUNK), cols[p]],
                              (((1,), (1,)), ((), ())), preferred_element_type=F32)
              for p in range(pairs)]
        carries = [0.0 if diagonal else carry_ref[p] for p in range(pairs)]
        log_betas, splits = [], []
        for z in zs:
            neg_abs = lax.bitcast_convert_type(lax.bitcast_convert_type(z, jnp.uint32) | sign, F32)
            log1p = jnp.log(1.0 + jnp.exp2(neg_abs)) * LOG2E
            log_beta = jnp.minimum(z, 0.0) - log1p
            log_keep = log_beta - z
            if diagonal:
                log_keep = jnp.where(causal, log_keep, 0.0)
            hi = log_keep.astype(BF16)
            lo = (log_keep - hi.astype(F32)).astype(BF16)
            log_betas.append(log_beta)
            splits.append(jnp.concatenate([hi, lo], axis=1))
        sums = [jnp.dot(s, tail, preferred_element_type=F32) for s in splits]
        weights = []
        for p in range(pairs):
            a = jnp.exp2(log_betas[p] + sums[p][:, :LANES] + carries[p])
            if diagonal:
                a = jnp.where(causal, a, 0.0)
            weights.append(a.astype(BF16))
        outs = [jnp.dot(weights[p], v_ref[pl.ds(start, CHUNK), cols[p]],
                        preferred_element_type=F32) for p in range(pairs)]
        for p in range(pairs):
            carry_ref[p] = carries[p] + sums[p][:, LANES:]
        for p in range(pairs):
            o = jnp.where(first, outs[p][:CHUNK], outs[p][CHUNK:])
            acc_ref[:, cols[p]] = o if diagonal else acc_ref[:, cols[p]] + o

    def alive():
        return jnp.max(carry_ref[...]) > DEAD_LOG2

    key_block(qb, True)

    def cond(state):
        j, live = state
        return jnp.logical_and(j >= 0, live)

    def body(state):
        j, _ = state
        key_block(j, False)
        return j - 1, alive()

    lax.while_loop(cond, body, (qb - 1, alive()))
    o_ref[...] = acc_ref[...].astype(BF16)


def _attention(q, k, v, batch, seq):
    q3, k3, v3 = (t.reshape(batch, seq, WIDTH) for t in (q, k, v))
    blk = pl.BlockSpec((None, CHUNK, WIDTH), lambda b, qb: (b, qb, 0))
    full = pl.BlockSpec((None, seq, WIDTH), lambda b, qb: (b, 0, 0), pipeline_mode=pl.Buffered(1))
    o = pl.pallas_call(
        _attn_kernel,
        grid=(batch, seq // CHUNK),
        in_specs=[blk, full, full],
        out_specs=blk,
        out_shape=jax.ShapeDtypeStruct((batch, seq, WIDTH), BF16),
        scratch_shapes=[pltpu.VMEM((WIDTH // LANES, 2 * CHUNK, LANES), F32),
                        pltpu.VMEM((CHUNK, WIDTH), F32)],
        compiler_params=pltpu.CompilerParams(
            dimension_semantics=("arbitrary", "arbitrary"), vmem_limit_bytes=VMEM_LIMIT),
        name="attn",
    )(q3, k3, v3)
    return o.reshape(batch * seq, WIDTH)


def _mix_kernel(x_ref, oa_ref, u_ref, vs_ref, ga_ref, gb_ref,
                wsp_ref, bsp_ref, wua_ref, wub_ref, wo_ref, gf_ref, wr_ref, br_ref,
                xo_ref, h_ref, route_ref, cnt_ref, run_ref):
    step = pl.program_id(0)

    @pl.when(step == 0)
    def _():
        run_ref[...] = jnp.zeros_like(run_ref)

    lane = lax.broadcasted_iota(jnp.int32, (CHUNK, LANES), 1)
    row = lax.broadcasted_iota(jnp.int32, (CHUNK, LANES), 0)
    first = lane < HEAD_DIM
    tril = lane <= row

    ob_chunks = []
    for c in range(TM_MIX // CHUNK):
        rows = slice(c * CHUNK, (c + 1) * CHUNK)
        cols = []
        for gp in range(WIDTH // LANES):
            vpair = vs_ref[rows, gp * LANES:(gp + 1) * LANES]
            mixed = []
            for g in (2 * gp, 2 * gp + 1):
                wg = jnp.where(tril, wsp_ref[g], 0.0).astype(BF16)
                mixed.append(jnp.dot(wg, vpair, preferred_element_type=F32))
            cols.append(jnp.where(first, mixed[0], mixed[1]))
        mixed = jnp.concatenate(cols, axis=-1) + bsp_ref[...]
        ob_chunks.append((u_ref[rows, :].astype(F32) * mixed).astype(BF16))
    ob = jnp.concatenate(ob_chunks, axis=0)

    up_a = jnp.dot(oa_ref[...], wua_ref[...], preferred_element_type=F32)
    up_b = jnp.dot(ob, wub_ref[...], preferred_element_type=F32)
    merged = ga_ref[...].astype(F32) * up_a + gb_ref[...].astype(F32) * up_b
    x = x_ref[...] + jnp.dot(merged.astype(BF16), wo_ref[...], preferred_element_type=F32)
    xo_ref[...] = x

    h = _rms(x, gf_ref[...])
    h_ref[...] = h
    logits = jnp.dot(h.astype(BF16), wr_ref[...], preferred_element_type=F32) + br_ref[...]
    lane_t = lax.broadcasted_iota(jnp.int32, (TM_MIX, LANES), 1)
    lanef = lane_t.astype(F32)
    neg = -jnp.inf
    far = float(LANES)

    def first_max(vals):
        m = jnp.max(vals, axis=-1, keepdims=True)
        idx = jnp.min(jnp.where(vals == m, lanef, far), axis=-1, keepdims=True)
        return m, idx

    gl = jnp.where(lane_t < N_GROUPS, logits, neg)
    gmax, grp = first_max(gl)
    grp_w = 1.0 / jnp.sum(jnp.exp(gl - gmax), axis=-1, keepdims=True)
    lo_lane = ROUTE_LANE0 + EXPERTS_PER_GROUP * grp
    in_group = jnp.logical_and(lanef >= lo_lane, lanef < lo_lane + EXPERTS_PER_GROUP)
    el = jnp.where(in_group, logits, neg)
    m1, i1 = first_max(el)
    m2, i2 = first_max(jnp.where(lanef == i1, neg, el))
    e21 = jnp.exp(m2 - m1)
    w1 = grp_w / (1.0 + e21)
    w2 = w1 * e21

    onehot = jnp.logical_or(lanef == i1, lanef == i2)
    rt = lax.broadcasted_iota(jnp.int32, (TM_MIX, TM_MIX), 0)
    ct = lax.broadcasted_iota(jnp.int32, (TM_MIX, TM_MIX), 1)
    before = jnp.where(ct < rt, 1.0, 0.0).astype(BF16)
    prior = jnp.dot(before, jnp.where(onehot, 1.0, 0.0).astype(BF16),
                    preferred_element_type=F32) + run_ref[...]
    r1 = jnp.sum(jnp.where(lanef == i1, prior, 0.0), axis=-1, keepdims=True)
    r2 = jnp.sum(jnp.where(lanef == i2, prior, 0.0), axis=-1, keepdims=True)
    run = run_ref[...] + jnp.sum(jnp.where(onehot, 1.0, 0.0), axis=0, keepdims=True)
    run_ref[...] = run
    cnt_ref[...] = run

    fields = (i1 - ROUTE_LANE0, i2 - ROUTE_LANE0, w1, w2, r1, r2)
    route = jnp.zeros((TM_MIX, LANES), F32)
    for pos, val in enumerate(fields):
        route = jnp.where(lane_t == pos, val, route)
    route_ref[...] = route


def _mix(layer, x, oa, u, vs, ga, gb, wsp, bsp, wua, wub, wo, gf, wr, br):
    n = x.shape[0]
    row = lambda width: pl.BlockSpec((TM_MIX, width), lambda i: (i, 0))
    lay = lambda *shape: pl.BlockSpec((None,) + shape, lambda i: (layer,) + (0,) * len(shape))
    return pl.pallas_call(
        _mix_kernel,
        grid=(n // TM_MIX,),
        in_specs=[row(D_MODEL), row(WIDTH), row(WIDTH), row(WIDTH), row(D_MODEL), row(D_MODEL),
                  lay(HEADS, CHUNK, CHUNK), lay(CHUNK, WIDTH), lay(WIDTH, D_MODEL),
                  lay(WIDTH, D_MODEL), lay(D_MODEL, D_MODEL), lay(1, D_MODEL),
                  lay(D_MODEL, LANES), lay(1, LANES)],
        out_specs=[row(D_MODEL), row(D_MODEL), row(LANES),
                   pl.BlockSpec((1, LANES), lambda i: (0, 0))],
        out_shape=[jax.ShapeDtypeStruct((n, D_MODEL), F32),
                   jax.ShapeDtypeStruct((n, D_MODEL), F32),
                   jax.ShapeDtypeStruct((n, LANES), F32),
                   jax.ShapeDtypeStruct((1, LANES), F32)],
        scratch_shapes=[pltpu.VMEM((1, LANES), F32)],
        compiler_params=pltpu.CompilerParams(
            dimension_semantics=("arbitrary",), vmem_limit_bytes=VMEM_LIMIT),
        name="mix",
    )(x, oa, u, vs, ga, gb, wsp, bsp, wua, wub, wo, gf, wr, br)


def _dispatch_kernel(dest_ref, h_ref, xs_in_ref, xs_ref, sem):
    del xs_in_ref
    base = pl.program_id(0) * (2 * TM_ROW)

    def copy(t, k):
        d = dest_ref[base + 2 * t + k]
        return pltpu.make_async_copy(h_ref.at[pl.ds(t, 1), :], xs_ref.at[pl.ds(d, 1), :], sem)

    def start(t, c):
        copy(t, 0).start()
        copy(t, 1).start()
        return c

    def wait(t, c):
        copy(t, 0).wait()
        copy(t, 1).wait()
        return c

    lax.fori_loop(0, TM_ROW, start, 0)
    lax.fori_loop(0, TM_ROW, wait, 0)


def _dispatch(dest, h, cap):
    n = h.shape[0]
    return pl.pallas_call(
        _dispatch_kernel,
        grid_spec=pltpu.PrefetchScalarGridSpec(
            num_scalar_prefetch=1,
            grid=(n // TM_ROW,),
            in_specs=[pl.BlockSpec((TM_ROW, D_MODEL), lambda i, dest: (i, 0)),
                      pl.BlockSpec(memory_space=pl.ANY)],
            out_specs=pl.BlockSpec(memory_space=pl.ANY),
            scratch_shapes=[pltpu.SemaphoreType.DMA(())],
        ),
        out_shape=jax.ShapeDtypeStruct((cap, D_MODEL), F32),
        input_output_aliases={2: 0},
        compiler_params=pltpu.CompilerParams(
            dimension_semantics=("arbitrary",), has_side_effects=True),
        name="dispatch",
    )(dest, h, jnp.zeros((cap, D_MODEL), F32))


def _expert_kernel(blk_e_ref, nused_ref, xs_ref, wi_ref, wo_ref, ys_ref):
    del blk_e_ref
    used = pl.program_id(0) < nused_ref[0]

    @pl.when(jnp.logical_not(used))
    def _():
        ys_ref[...] = jnp.zeros_like(ys_ref)

    @pl.when(used)
    def _():
        gu = jnp.dot(xs_ref[...].astype(BF16), wi_ref[...], preferred_element_type=F32)
        act = jax.nn.silu(gu[:, :EXPERT_FF]) * gu[:, EXPERT_FF:]
        ys_ref[...] = jnp.dot(act.astype(BF16), wo_ref[...], preferred_element_type=F32)


def _experts(layer, blk_e, nused, xs, w_e_in, w_e_out):
    cap = xs.shape[0]
    row = pl.BlockSpec((MOE_ROWS, D_MODEL), lambda b, blk_e, nused: (b, 0))
    return pl.pallas_call(
        _expert_kernel,
        grid_spec=pltpu.PrefetchScalarGridSpec(
            num_scalar_prefetch=2,
            grid=(cap // MOE_ROWS,),
            in_specs=[row,
                      pl.BlockSpec((None, None, D_MODEL, 2 * EXPERT_FF),
                                   lambda b, blk_e, nused: (layer, blk_e[b], 0, 0)),
                      pl.BlockSpec((None, None, EXPERT_FF, D_MODEL),
                                   lambda b, blk_e, nused: (layer, blk_e[b], 0, 0))],
            out_specs=row,
        ),
        out_shape=jax.ShapeDtypeStruct((cap, D_MODEL), F32),
        compiler_params=pltpu.CompilerParams(
            dimension_semantics=("arbitrary",), vmem_limit_bytes=VMEM_LIMIT),
        name="experts",
    )(blk_e, nused, xs, w_e_in, w_e_out)


def _ple_kernel(dest_ref, x_ref, route_ref, p_ref, ys_ref, gp_ref, wg_ref, wp_ref,
                xo_ref, y0_ref, y1_ref, sem):
    base = pl.program_id(0) * (2 * TM_ROW)

    def copy(t, k):
        d = dest_ref[base + 2 * t + k]
        dst = y0_ref if k == 0 else y1_ref
        return pltpu.make_async_copy(ys_ref.at[pl.ds(d, 1), :], dst.at[pl.ds(t, 1), :], sem)

    def start(t, c):
        copy(t, 0).start()
        copy(t, 1).start()
        return c

    def wait(t, c):
        copy(t, 0).wait()
        copy(t, 1).wait()
        return c

    lax.fori_loop(0, TM_ROW, start, 0)
    pe = jnp.dot(p_ref[...].astype(BF16), wp_ref[...], preferred_element_type=F32)
    lax.fori_loop(0, TM_ROW, wait, 0)

    route = route_ref[...]
    x = x_ref[...] + route[:, 2:3] * y0_ref[...] + route[:, 3:4] * y1_ref[...]
    gate = jax.nn.sigmoid(jnp.dot(_rms(x, gp_ref[...]).astype(BF16), wg_ref[...],
                                  preferred_element_type=F32))
    xo_ref[...] = x + gate * pe


def _ple(layer, dest, x, route, p, ys, gp, wg, wp):
    n = x.shape[0]
    row = lambda width: pl.BlockSpec((TM_ROW, width), lambda i, dest: (i, 0))
    lay = lambda *shape: pl.BlockSpec((None,) + shape,
                                      lambda i, dest: (layer,) + (0,) * len(shape))
    return pl.pallas_call(
        _ple_kernel,
        grid_spec=pltpu.PrefetchScalarGridSpec(
            num_scalar_prefetch=1,
            grid=(n // TM_ROW,),
            in_specs=[row(D_MODEL), row(LANES),
                      pl.BlockSpec((None, TM_ROW, PLE_DIM), lambda i, dest: (layer, i, 0)),
                      pl.BlockSpec(memory_space=pl.ANY),
                      lay(1, D_MODEL), lay(D_MODEL, D_MODEL), lay(PLE_DIM, D_MODEL)],
            out_specs=row(D_MODEL),
            scratch_shapes=[pltpu.VMEM((TM_ROW, D_MODEL), F32),
                            pltpu.VMEM((TM_ROW, D_MODEL), F32),
                            pltpu.SemaphoreType.DMA(())],
        ),
        out_shape=jax.ShapeDtypeStruct((n, D_MODEL), F32),
        compiler_params=pltpu.CompilerParams(
            dimension_semantics=("arbitrary",), vmem_limit_bytes=VMEM_LIMIT),
        name="ple",
    )(dest, x, route, p, ys, gp, wg, wp)


def _slot_layout(route, counts, cap):
    eid = route[:, 0:2].astype(jnp.int32)
    rank = route[:, 4:6].astype(jnp.int32)
    cnt = counts[0, ROUTE_LANE0:ROUTE_LANE0 + N_EXPERTS].astype(jnp.int32)
    padded = (cnt + MOE_ROWS - 1) // MOE_ROWS * MOE_ROWS
    pend = jnp.cumsum(padded)
    poff = pend - padded
    onehot = eid[:, :, None] == jnp.arange(N_EXPERTS, dtype=jnp.int32)
    dest = jnp.sum(jnp.where(onehot, poff, 0), axis=-1) + rank
    blk_start = jnp.arange(cap // MOE_ROWS, dtype=jnp.int32) * MOE_ROWS
    blk_e = jnp.minimum(jnp.sum(pend[None, :] <= blk_start[:, None], axis=1), N_EXPERTS - 1)
    nused = (pend[-1:] // MOE_ROWS).astype(jnp.int32)
    return dest.reshape(-1).astype(jnp.int32), blk_e.astype(jnp.int32), nused


def kernel(x, p, norm_mix, w_in, q_norm, k_norm, sgu_norm, w_spatial, b_spatial, w_up_a, w_up_b,
           w_out, norm_ffn, w_group_router, b_group_router, w_expert_router, b_expert_router,
           w_expert_in, w_expert_out, norm_ple, w_ple_gate, w_ple_proj):
    batch, seq, d = x.shape
    depth = w_in.shape[0]
    n = batch * seq
    cap = 2 * n + N_EXPERTS * MOE_ROWS

    vec = lambda a: a[:, None, :]
    w_in_b, w_ua_b, w_ub_b, w_o_b = (a.astype(BF16) for a in (w_in, w_up_a, w_up_b, w_out))
    w_ei_b, w_eo_b = w_expert_in.astype(BF16), w_expert_out.astype(BF16)
    w_pg_b, w_pp_b = w_ple_gate.astype(BF16), w_ple_proj.astype(BF16)
    qn2, kn2 = vec(jnp.tile(q_norm, (1, 2))), vec(jnp.tile(k_norm, (1, 2)))
    b_sp = jnp.repeat(jnp.swapaxes(b_spatial, 1, 2), HEAD_DIM, axis=2)
    pad = LANES - N_GROUPS - N_EXPERTS
    w_r = jnp.pad(jnp.concatenate([w_group_router, w_expert_router], axis=2),
                  ((0, 0), (0, 0), (0, pad))).astype(BF16)
    b_r = vec(jnp.pad(jnp.concatenate([b_group_router, b_expert_router], axis=1),
                      ((0, 0), (0, pad))))
    p2 = p.reshape(depth, n, PLE_DIM)

    xf = x.reshape(n, d)
    for i in range(depth):
        q, k, v, u, vs, ga, gb = _inproj(i, xf, vec(norm_mix), w_in_b, qn2, kn2, vec(sgu_norm))
        oa = _attention(q, k, v, batch, seq)
        xf, h, route, counts = _mix(i, xf, oa, u, vs, ga, gb, w_spatial, b_sp, w_ua_b, w_ub_b,
                                    w_o_b, vec(norm_ffn), w_r, b_r)
        dest, blk_e, nused = _slot_layout(route, counts, cap)
        xs = _dispatch(dest, h, cap)
        ys = _experts(i, blk_e, nused, xs, w_ei_b, w_eo_b)
        xf = _ple(i, dest, xf, route, p2, ys, vec(norm_ple), w_pg_b, w_pp_b)
    return xf.reshape(batch, seq, d)
```

```python
import functools
import math

import jax
import jax.numpy as jnp
from jax import lax
from jax.experimental import pallas as pl
from jax.experimental.pallas import tpu as pltpu

F32 = jnp.float32
BF16 = jnp.bfloat16

D_MODEL = 1024
HEADS = 8
HEAD_DIM = 64
WIDTH = HEADS * HEAD_DIM
CHUNK = 128
IN_WIDTH = 5 * WIDTH + 2 * D_MODEL
N_GROUPS = 4
EXPERTS_PER_GROUP = 8
N_EXPERTS = N_GROUPS * EXPERTS_PER_GROUP
EXPERT_FF = 512
PLE_DIM = 256
EPS = 1e-6

LANES = 128
ROW_TILE = D_MODEL // LANES
ROUTE_LANE0 = N_GROUPS
TM_IN = 256
TM_MIX = 256
TM_ROW = 256
MOE_ROWS = 256
LOG2E = 1.4426950408889634
DEAD_LOG2 = -105.0 * LOG2E
VMEM_LIMIT = 48 * 1024 * 1024


def _rms(x, gain):
    return x * lax.rsqrt(jnp.mean(x * x, axis=-1, keepdims=True) + EPS) * gain


def _store_row_tiles(ref, value, lead=()):
    rows = value.shape[0]
    for c in range(ROW_TILE):
        ref[lead + (pl.ds(c, rows, stride=ROW_TILE), slice(None))] = (
            value[:, c * LANES:(c + 1) * LANES])


def _load_row_tiles(ref, rows, lead=()):
    return jnp.concatenate(
        [ref[lead + (pl.ds(c, rows, stride=ROW_TILE), slice(None))] for c in range(ROW_TILE)],
        axis=1)


def _inproj_kernel(x_ref, g_ref, w_ref, qn_ref, kn_ref, sn_ref,
                   q_ref, k_ref, v_ref, u_ref, vs_ref, ga_ref, gb_ref):
    hb = _rms(x_ref[...], g_ref[...]).astype(BF16)

    def proj(lo, hi):
        return jnp.dot(hb, w_ref[:, lo:hi], preferred_element_type=F32)

    first = lax.broadcasted_iota(jnp.int32, (1, LANES), 1) < HEAD_DIM

    def head_norm(z, gain, scale):
        outs = []
        for c in range(WIDTH // LANES):
            zc = z[:, c * LANES:(c + 1) * LANES]
            sq = zc * zc
            sa = jnp.sum(jnp.where(first, sq, 0.0), axis=-1, keepdims=True)
            sb = jnp.sum(jnp.where(first, 0.0, sq), axis=-1, keepdims=True)
            ms = jnp.where(first, sa, sb) * (1.0 / HEAD_DIM)
            outs.append(zc * lax.rsqrt(ms + EPS) * (gain * scale))
        return jnp.concatenate(outs, axis=-1)

    w = WIDTH
    q_ref[...] = head_norm(proj(0, w), qn_ref[...], LOG2E / math.sqrt(HEAD_DIM)).astype(BF16)
    k_ref[...] = head_norm(proj(w, 2 * w), kn_ref[...], 1.0).astype(BF16)
    v_ref[...] = proj(2 * w, 3 * w).astype(BF16)
    u_ref[...] = jax.nn.gelu(proj(3 * w, 4 * w)).astype(BF16)
    vs_ref[...] = _rms(jax.nn.gelu(proj(4 * w, 5 * w)), sn_ref[...]).astype(BF16)
    ga_ref[...] = jax.nn.sigmoid(proj(5 * w, 5 * w + D_MODEL)).astype(BF16)
    gb_ref[...] = jax.nn.sigmoid(proj(5 * w + D_MODEL, IN_WIDTH)).astype(BF16)


def _inproj(layer, x, norm_mix, w_in, qn, kn, sn):
    n = x.shape[0]
    row = lambda width: pl.BlockSpec((TM_IN, width), lambda i: (i, 0))
    vec = lambda width: pl.BlockSpec((None, 1, width), lambda i: (layer, 0, 0))
    out = lambda width: jax.ShapeDtypeStruct((n, width), BF16)
    return pl.pallas_call(
        _inproj_kernel,
        grid=(n // TM_IN,),
        in_specs=[row(D_MODEL), vec(D_MODEL),
                  pl.BlockSpec((None, D_MODEL, IN_WIDTH), lambda i: (layer, 0, 0)),
                  vec(LANES), vec(LANES), vec(WIDTH)],
        out_specs=[row(WIDTH)] * 5 + [row(D_MODEL)] * 2,
        out_shape=[out(WIDTH)] * 5 + [out(D_MODEL)] * 2,
        compiler_params=pltpu.CompilerParams(
            dimension_semantics=("arbitrary",), vmem_limit_bytes=VMEM_LIMIT),
        name="inproj",
    )(x, norm_mix, w_in, qn, kn, sn)


def _attn_kernel(q_ref, k_ref, v_ref, o_ref, carry_ref, acc_ref):
    qb = pl.program_id(1)
    pairs = WIDTH // LANES
    lane = lax.broadcasted_iota(jnp.int32, (2 * CHUNK, LANES), 1)
    row = lax.broadcasted_iota(jnp.int32, (2 * CHUNK, LANES), 0)
    upper = row < CHUNK
    query = jnp.where(upper, row, row - CHUNK)
    own = jnp.logical_xor(lane < HEAD_DIM, jnp.logical_not(upper))
    causal = lane < query
    first = lax.broadcasted_iota(jnp.int32, (CHUNK, LANES), 1) < HEAD_DIM
    kr = lax.broadcasted_iota(jnp.int32, (2 * LANES, 2 * LANES), 0) % LANES
    kc = lax.broadcasted_iota(jnp.int32, (2 * LANES, 2 * LANES), 1)
    tail = jnp.where(jnp.logical_or(kr > kc, kc >= LANES), 1.0, 0.0).astype(BF16)
    sign = jnp.uint32(0x80000000)

    cols = [slice(p * LANES, (p + 1) * LANES) for p in range(pairs)]
    q2s = []
    for p in range(pairs):
        qp = q_ref[:, cols[p]]
        q2 = jnp.concatenate([qp, qp], axis=0)
        q2s.append(jnp.where(own, q2, jnp.zeros_like(q2)))

    def key_block(j, diagonal):
        start = pl.multiple_of(j * CHUNK, CHUNK)
        zs = [lax.dot_general(q2s[p], k_ref[pl.ds(start, CHUNK), cols[p]],
                              (((1,), (1,)), ((), ())), preferred_element_type=F32)
              for p in range(pairs)]
        carries = [0.0 if diagonal else carry_ref[p] for p in range(pairs)]
        log_betas, splits = [], []
        for z in zs:
            neg_abs = lax.bitcast_convert_type(lax.bitcast_convert_type(z, jnp.uint32) | sign, F32)
            log1p = jnp.log(1.0 + jnp.exp2(neg_abs)) * LOG2E
            log_beta = jnp.minimum(z, 0.0) - log1p
            log_keep = log_beta - z
            if diagonal:
                log_keep = jnp.where(causal, log_keep, 0.0)
            hi = log_keep.astype(BF16)
            lo = (log_keep - hi.astype(F32)).astype(BF16)
            log_betas.append(log_beta)
            splits.append(jnp.concatenate([hi, lo], axis=1))
        sums = [jnp.dot(s, tail, preferred_element_type=F32) for s in splits]
        weights = []
        for p in range(pairs):
            a = jnp.exp2(log_betas[p] + sums[p][:, :LANES] + carries[p])
            if diagonal:
                a = jnp.where(causal, a, 0.0)
            weights.append(a.astype(BF16))
        outs = [jnp.dot(weights[p], v_ref[pl.ds(start, CHUNK), cols[p]],
                        preferred_element_type=F32) for p in range(pairs)]
        for p in range(pairs):
            carry_ref[p] = carries[p] + sums[p][:, LANES:]
        for p in range(pairs):
            o = jnp.where(first, outs[p][:CHUNK], outs[p][CHUNK:])
            acc_ref[:, cols[p]] = o if diagonal else acc_ref[:, cols[p]] + o

    def alive():
        return jnp.max(carry_ref[...]) > DEAD_LOG2

    key_block(qb, True)

    def cond(state):
        j, live = state
        return jnp.logical_and(j >= 0, live)

    def body(state):
        j, _ = state
        key_block(j, False)
        return j - 1, alive()

    lax.while_loop(cond, body, (qb - 1, alive()))
    o_ref[...] = acc_ref[...].astype(BF16)


def _attention(q, k, v, batch, seq):
    q3, k3, v3 = (t.reshape(batch, seq, WIDTH) for t in (q, k, v))
    blk = pl.BlockSpec((None, CHUNK, WIDTH), lambda b, qb: (b, qb, 0))
    full = pl.BlockSpec((None, seq, WIDTH), lambda b, qb: (b, 0, 0), pipeline_mode=pl.Buffered(1))
    o = pl.pallas_call(
        _attn_kernel,
        grid=(batch, seq // CHUNK),
        in_specs=[blk, full, full],
        out_specs=blk,
        out_shape=jax.ShapeDtypeStruct((batch, seq, WIDTH), BF16),
        scratch_shapes=[pltpu.VMEM((WIDTH // LANES, 2 * CHUNK, LANES), F32),
                        pltpu.VMEM((CHUNK, WIDTH), F32)],
        compiler_params=pltpu.CompilerParams(
            dimension_semantics=("arbitrary", "arbitrary"), vmem_limit_bytes=VMEM_LIMIT),
        name="attn",
    )(q3, k3, v3)
    return o.reshape(batch * seq, WIDTH)


def _mix_kernel(x_ref, oa_ref, u_ref, vs_ref, ga_ref, gb_ref,
                wsp_ref, bsp_ref, wua_ref, wub_ref, wo_ref, gf_ref, wr_ref, br_ref,
                xo_ref, h_ref, route_ref, cnt_ref, run_ref):
    step = pl.program_id(0)

    @pl.when(step == 0)
    def _():
        run_ref[...] = jnp.zeros_like(run_ref)

    lane = lax.broadcasted_iota(jnp.int32, (CHUNK, LANES), 1)
    row = lax.broadcasted_iota(jnp.int32, (CHUNK, LANES), 0)
    first = lane < HEAD_DIM
    tril = lane <= row

    ob_chunks = []
    for c in range(TM_MIX // CHUNK):
        rows = slice(c * CHUNK, (c + 1) * CHUNK)
        cols = []
        for gp in range(WIDTH // LANES):
            vpair = vs_ref[rows, gp * LANES:(gp + 1) * LANES]
            mixed = []
            for g in (2 * gp, 2 * gp + 1):
                wg = jnp.where(tril, wsp_ref[g], 0.0).astype(BF16)
                mixed.append(jnp.dot(wg, vpair, preferred_element_type=F32))
            cols.append(jnp.where(first, mixed[0], mixed[1]))
        mixed = jnp.concatenate(cols, axis=-1) + bsp_ref[...]
        ob_chunks.append((u_ref[rows, :].astype(F32) * mixed).astype(BF16))
    ob = jnp.concatenate(ob_chunks, axis=0)

    up_a = jnp.dot(oa_ref[...], wua_ref[...], preferred_element_type=F32)
    up_b = jnp.dot(ob, wub_ref[...], preferred_element_type=F32)
    merged = ga_ref[...].astype(F32) * up_a + gb_ref[...].astype(F32) * up_b
    x = x_ref[...] + jnp.dot(merged.astype(BF16), wo_ref[...], preferred_element_type=F32)
    xo_ref[...] = x

    h = _rms(x, gf_ref[...])
    _store_row_tiles(h_ref, h)
    logits = jnp.dot(h.astype(BF16), wr_ref[...], preferred_element_type=F32) + br_ref[...]
    lane_t = lax.broadcasted_iota(jnp.int32, (TM_MIX, LANES), 1)
    lanef = lane_t.astype(F32)
    neg = -jnp.inf
    far = float(LANES)

    def first_max(vals):
        m = jnp.max(vals, axis=-1, keepdims=True)
        idx = jnp.min(jnp.where(vals == m, lanef, far), axis=-1, keepdims=True)
        return m, idx

    gl = jnp.where(lane_t < N_GROUPS, logits, neg)
    gmax, grp = first_max(gl)
    grp_w = 1.0 / jnp.sum(jnp.exp(gl - gmax), axis=-1, keepdims=True)
    lo_lane = ROUTE_LANE0 + EXPERTS_PER_GROUP * grp
    in_group = jnp.logical_and(lanef >= lo_lane, lanef < lo_lane + EXPERTS_PER_GROUP)
    el = jnp.where(in_group, logits, neg)
    m1, i1 = first_max(el)
    m2, i2 = first_max(jnp.where(lanef == i1, neg, el))
    e21 = jnp.exp(m2 - m1)
    w1 = grp_w / (1.0 + e21)
    w2 = w1 * e21

    onehot = jnp.logical_or(lanef == i1, lanef == i2)
    rt = lax.broadcasted_iota(jnp.int32, (TM_MIX, TM_MIX), 0)
    ct = lax.broadcasted_iota(jnp.int32, (TM_MIX, TM_MIX), 1)
    before = jnp.where(ct < rt, 1.0, 0.0).astype(BF16)
    prior = jnp.dot(before, jnp.where(onehot, 1.0, 0.0).astype(BF16),
                    preferred_element_type=F32) + run_ref[...]
    r1 = jnp.sum(jnp.where(lanef == i1, prior, 0.0), axis=-1, keepdims=True)
    r2 = jnp.sum(jnp.where(lanef == i2, prior, 0.0), axis=-1, keepdims=True)
    run = run_ref[...] + jnp.sum(jnp.where(onehot, 1.0, 0.0), axis=0, keepdims=True)
    run_ref[...] = run
    cnt_ref[...] = run

    fields = (i1 - ROUTE_LANE0, i2 - ROUTE_LANE0, w1, w2, r1, r2)
    route = jnp.zeros((TM_MIX, LANES), F32)
    for pos, val in enumerate(fields):
        route = jnp.where(lane_t == pos, val, route)
    route_ref[...] = route


def _mix(layer, x, oa, u, vs, ga, gb, wsp, bsp, wua, wub, wo, gf, wr, br):
    n = x.shape[0]
    row = lambda width: pl.BlockSpec((TM_MIX, width), lambda i: (i, 0))
    lay = lambda *shape: pl.BlockSpec((None,) + shape, lambda i: (layer,) + (0,) * len(shape))
    return pl.pallas_call(
        _mix_kernel,
        grid=(n // TM_MIX,),
        in_specs=[row(D_MODEL), row(WIDTH), row(WIDTH), row(WIDTH), row(D_MODEL), row(D_MODEL),
                  lay(HEADS, CHUNK, CHUNK), lay(CHUNK, WIDTH), lay(WIDTH, D_MODEL),
                  lay(WIDTH, D_MODEL), lay(D_MODEL, D_MODEL), lay(1, D_MODEL),
                  lay(D_MODEL, LANES), lay(1, LANES)],
        out_specs=[row(D_MODEL), pl.BlockSpec((TM_MIX * ROW_TILE, LANES), lambda i: (i, 0)),
                   row(LANES), pl.BlockSpec((1, LANES), lambda i: (0, 0))],
        out_shape=[jax.ShapeDtypeStruct((n, D_MODEL), F32),
                   jax.ShapeDtypeStruct((n * ROW_TILE, LANES), F32),
                   jax.ShapeDtypeStruct((n, LANES), F32),
                   jax.ShapeDtypeStruct((1, LANES), F32)],
        scratch_shapes=[pltpu.VMEM((1, LANES), F32)],
        compiler_params=pltpu.CompilerParams(
            dimension_semantics=("arbitrary",), vmem_limit_bytes=VMEM_LIMIT),
        name="mix",
    )(x, oa, u, vs, ga, gb, wsp, bsp, wua, wub, wo, gf, wr, br)


def _expert_kernel(blk_e_ref, nused_ref, src_ref, h_ref, wi_ref, wo_ref, ys_ref,
                   xbuf, wi_b, wo_b, sem):
    b = pl.program_id(0)
    nused = nused_ref[0]
    used = b < nused
    slot = b % 2
    block_rows = MOE_ROWS * ROW_TILE

    def start_gather(blk, to_slot):
        for r in range(MOE_ROWS):
            tok = src_ref[blk * MOE_ROWS + r]
            pltpu.make_async_copy(h_ref.at[pl.ds(tok * ROW_TILE, ROW_TILE), :],
                                  xbuf.at[to_slot, pl.ds(r * ROW_TILE, ROW_TILE), :],
                                  sem.at[to_slot]).start()

    def wait_gather(in_slot):
        pltpu.make_async_copy(h_ref.at[pl.ds(0, block_rows), :], xbuf.at[in_slot],
                              sem.at[in_slot]).wait()

    @pl.when(b == 0)
    def _():
        start_gather(0, 0)

    @pl.when(jnp.logical_not(used))
    def _():
        ys_ref[...] = jnp.zeros_like(ys_ref)

    new_expert = jnp.logical_or(b == 0, blk_e_ref[b] != blk_e_ref[jnp.maximum(b - 1, 0)])

    @pl.when(jnp.logical_and(used, new_expert))
    def _():
        wi_b[...] = wi_ref[...].astype(BF16)
        wo_b[...] = wo_ref[...].astype(BF16)

    @pl.when(used)
    def _():
        wait_gather(slot)
        start_gather(jnp.minimum(b + 1, nused - 1), 1 - slot)
        xs = _load_row_tiles(xbuf, MOE_ROWS, (slot,)).astype(BF16)
        gu = jnp.dot(xs, wi_b[...], preferred_element_type=F32)
        act = jax.nn.silu(gu[:, :EXPERT_FF]) * gu[:, EXPERT_FF:]
        y = jnp.dot(act.astype(BF16), wo_b[...], preferred_element_type=F32)
        _store_row_tiles(ys_ref, y)

    @pl.when(b == nused - 1)
    def _():
        wait_gather(1 - slot)


def _experts(layer, blk_e, nused, src, h, w_e_in, w_e_out):
    cap = src.shape[0]
    return pl.pallas_call(
        _expert_kernel,
        grid_spec=pltpu.PrefetchScalarGridSpec(
            num_scalar_prefetch=3,
            grid=(cap // MOE_ROWS,),
            in_specs=[pl.BlockSpec(memory_space=pl.ANY),
                      pl.BlockSpec((None, None, D_MODEL, 2 * EXPERT_FF),
                                   lambda b, blk_e, nused, src: (layer, blk_e[b], 0, 0)),
                      pl.BlockSpec((None, None, EXPERT_FF, D_MODEL),
                                   lambda b, blk_e, nused, src: (layer, blk_e[b], 0, 0))],
            out_specs=pl.BlockSpec((MOE_ROWS * ROW_TILE, LANES),
                                   lambda b, blk_e, nused, src: (b, 0)),
            scratch_shapes=[pltpu.VMEM((2, MOE_ROWS * ROW_TILE, LANES), F32),
                            pltpu.VMEM((D_MODEL, 2 * EXPERT_FF), BF16),
                            pltpu.VMEM((EXPERT_FF, D_MODEL), BF16),
                            pltpu.SemaphoreType.DMA((2,))],
        ),
        out_shape=jax.ShapeDtypeStruct((cap * ROW_TILE, LANES), F32),
        compiler_params=pltpu.CompilerParams(
            dimension_semantics=("arbitrary",), vmem_limit_bytes=VMEM_LIMIT),
        name="experts",
    )(blk_e, nused, src, h, w_e_in, w_e_out)


def _ple_kernel(dest_ref, x_ref, route_ref, p_ref, ys_ref, gp_ref, wg_ref, wp_ref,
                xo_ref, y0_ref, y1_ref, sem):
    base = pl.program_id(0) * (2 * TM_ROW)
    bufs = (y0_ref, y1_ref)
    for t in range(TM_ROW):
        for k in range(2):
            d = dest_ref[base + 2 * t + k]
            pltpu.make_async_copy(ys_ref.at[pl.ds(d * ROW_TILE, ROW_TILE), :],
                                  bufs[k].at[pl.ds(t * ROW_TILE, ROW_TILE), :], sem).start()
    pe = jnp.dot(p_ref[...].astype(BF16), wp_ref[...], preferred_element_type=F32)
    for buf in bufs:
        pltpu.make_async_copy(ys_ref.at[pl.ds(0, TM_ROW * ROW_TILE), :], buf, sem).wait()

    route = route_ref[...]
    y0 = _load_row_tiles(y0_ref, TM_ROW)
    y1 = _load_row_tiles(y1_ref, TM_ROW)
    x = x_ref[...] + route[:, 2:3] * y0 + route[:, 3:4] * y1
    gate = jax.nn.sigmoid(jnp.dot(_rms(x, gp_ref[...]).astype(BF16), wg_ref[...],
                                  preferred_element_type=F32))
    xo_ref[...] = x + gate * pe


def _ple(layer, dest, x, route, p, ys, gp, wg, wp):
    n = x.shape[0]
    row = lambda width: pl.BlockSpec((TM_ROW, width), lambda i, dest: (i, 0))
    lay = lambda *shape: pl.BlockSpec((None,) + shape,
                                      lambda i, dest: (layer,) + (0,) * len(shape))
    return pl.pallas_call(
        _ple_kernel,
        grid_spec=pltpu.PrefetchScalarGridSpec(
            num_scalar_prefetch=1,
            grid=(n // TM_ROW,),
            in_specs=[row(D_MODEL), row(LANES),
                      pl.BlockSpec((None, TM_ROW, PLE_DIM), lambda i, dest: (layer, i, 0)),
                      pl.BlockSpec(memory_space=pl.ANY),
                      lay(1, D_MODEL), lay(D_MODEL, D_MODEL), lay(PLE_DIM, D_MODEL)],
            out_specs=row(D_MODEL),
            scratch_shapes=[pltpu.VMEM((TM_ROW * ROW_TILE, LANES), F32),
                            pltpu.VMEM((TM_ROW * ROW_TILE, LANES), F32),
                            pltpu.SemaphoreType.DMA(())],
        ),
        out_shape=jax.ShapeDtypeStruct((n, D_MODEL), F32),
        compiler_params=pltpu.CompilerParams(
            dimension_semantics=("arbitrary",), vmem_limit_bytes=VMEM_LIMIT),
        name="ple",
    )(dest, x, route, p, ys, gp, wg, wp)


def _slot_layout(route, counts, cap):
    eid = route[:, 0:2].astype(jnp.int32)
    rank = route[:, 4:6].astype(jnp.int32)
    cnt = counts[0, ROUTE_LANE0:ROUTE_LANE0 + N_EXPERTS].astype(jnp.int32)
    padded = (cnt + MOE_ROWS - 1) // MOE_ROWS * MOE_ROWS
    pend = jnp.cumsum(padded)
    poff = pend - padded
    onehot = eid[:, :, None] == jnp.arange(N_EXPERTS, dtype=jnp.int32)
    dest = jnp.sum(jnp.where(onehot, poff, 0), axis=-1) + rank
    blk_start = jnp.arange(cap // MOE_ROWS, dtype=jnp.int32) * MOE_ROWS
    blk_e = jnp.minimum(jnp.sum(pend[None, :] <= blk_start[:, None], axis=1), N_EXPERTS - 1)
    nused = (pend[-1:] // MOE_ROWS).astype(jnp.int32)
    dest = dest.reshape(-1).astype(jnp.int32)
    tok = jnp.arange(dest.shape[0], dtype=jnp.int32) // 2
    src = jnp.zeros((cap,), jnp.int32).at[dest].set(tok, unique_indices=True)
    return dest, src, blk_e.astype(jnp.int32), nused


def kernel(x, p, norm_mix, w_in, q_norm, k_norm, sgu_norm, w_spatial, b_spatial, w_up_a, w_up_b,
           w_out, norm_ffn, w_group_router, b_group_router, w_expert_router, b_expert_router,
           w_expert_in, w_expert_out, norm_ple, w_ple_gate, w_ple_proj):
    batch, seq, d = x.shape
    depth = w_in.shape[0]
    n = batch * seq
    cap = 2 * n + N_EXPERTS * MOE_ROWS

    vec = lambda a: a[:, None, :]
    w_in_b, w_ua_b, w_ub_b, w_o_b = (a.astype(BF16) for a in (w_in, w_up_a, w_up_b, w_out))
    w_pg_b, w_pp_b = w_ple_gate.astype(BF16), w_ple_proj.astype(BF16)
    qn2, kn2 = vec(jnp.tile(q_norm, (1, 2))), vec(jnp.tile(k_norm, (1, 2)))
    b_sp = jnp.repeat(jnp.swapaxes(b_spatial, 1, 2), HEAD_DIM, axis=2)
    pad = LANES - N_GROUPS - N_EXPERTS
    w_r = jnp.pad(jnp.concatenate([w_group_router, w_expert_router], axis=2),
                  ((0, 0), (0, 0), (0, pad))).astype(BF16)
    b_r = vec(jnp.pad(jnp.concatenate([b_group_router, b_expert_router], axis=1),
                      ((0, 0), (0, pad))))
    p2 = p.reshape(depth, n, PLE_DIM)

    xf = x.reshape(n, d)
    for i in range(depth):
        q, k, v, u, vs, ga, gb = _inproj(i, xf, vec(norm_mix), w_in_b, qn2, kn2, vec(sgu_norm))
        oa = _attention(q, k, v, batch, seq)
        xf, h, route, counts = _mix(i, xf, oa, u, vs, ga, gb, w_spatial, b_sp, w_ua_b, w_ub_b,
                                    w_o_b, vec(norm_ffn), w_r, b_r)
        dest, src, blk_e, nused = _slot_layout(route, counts, cap)
        ys = _experts(i, blk_e, nused, src, h, w_expert_in, w_expert_out)
        xf = _ple(i, dest, xf, route, p2, ys, vec(norm_ple), w_pg_b, w_pp_b)
    return xf.reshape(batch, seq, d)
```

```python
import functools
import math

import jax
import jax.numpy as jnp
from jax import lax
from jax.experimental import pallas as pl
from jax.experimental.pallas import tpu as pltpu

F32 = jnp.float32
BF16 = jnp.bfloat16

D_MODEL = 1024
HEADS = 8
HEAD_DIM = 64
WIDTH = HEADS * HEAD_DIM
CHUNK = 128
IN_WIDTH = 5 * WIDTH + 2 * D_MODEL
N_GROUPS = 4
EXPERTS_PER_GROUP = 8
N_EXPERTS = N_GROUPS * EXPERTS_PER_GROUP
EXPERT_FF = 512
PLE_DIM = 256
EPS = 1e-6

LANES = 128
ROW_TILE = D_MODEL // LANES
ROUTE_LANE0 = N_GROUPS
TM_IN = 256
TM_MIX = 256
TM_DISP = 512
TM_ROW = 256
MOE_ROWS = 256
LOG2E = 1.4426950408889634
DEAD_LOG2 = -105.0 * LOG2E
VMEM_LIMIT = 48 * 1024 * 1024


def _rms(x, gain):
    return x * lax.rsqrt(jnp.mean(x * x, axis=-1, keepdims=True) + EPS) * gain


def _store_row_tiles(ref, value, lead=()):
    rows = value.shape[0]
    for c in range(ROW_TILE):
        ref[lead + (pl.ds(c, rows, stride=ROW_TILE), slice(None))] = (
            value[:, c * LANES:(c + 1) * LANES])


def _load_row_tiles(ref, rows, lead=()):
    return jnp.concatenate(
        [ref[lead + (pl.ds(c, rows, stride=ROW_TILE), slice(None))] for c in range(ROW_TILE)],
        axis=1)


def _inproj_kernel(x_ref, g_ref, w_ref, qn_ref, kn_ref, sn_ref,
                   q_ref, k_ref, v_ref, u_ref, vs_ref, ga_ref, gb_ref):
    hb = _rms(x_ref[...], g_ref[...]).astype(BF16)

    def proj(lo, hi):
        return jnp.dot(hb, w_ref[:, lo:hi], preferred_element_type=F32)

    first = lax.broadcasted_iota(jnp.int32, (1, LANES), 1) < HEAD_DIM

    def head_norm(z, gain, scale):
        outs = []
        for c in range(WIDTH // LANES):
            zc = z[:, c * LANES:(c + 1) * LANES]
            sq = zc * zc
            sa = jnp.sum(jnp.where(first, sq, 0.0), axis=-1, keepdims=True)
            sb = jnp.sum(jnp.where(first, 0.0, sq), axis=-1, keepdims=True)
            ms = jnp.where(first, sa, sb) * (1.0 / HEAD_DIM)
            outs.append(zc * lax.rsqrt(ms + EPS) * (gain * scale))
        return jnp.concatenate(outs, axis=-1)

    w = WIDTH
    q_ref[...] = head_norm(proj(0, w), qn_ref[...], LOG2E / math.sqrt(HEAD_DIM)).astype(BF16)
    k_ref[...] = head_norm(proj(w, 2 * w), kn_ref[...], 1.0).astype(BF16)
    v_ref[...] = proj(2 * w, 3 * w).astype(BF16)
    u_ref[...] = jax.nn.gelu(proj(3 * w, 4 * w)).astype(BF16)
    vs_ref[...] = _rms(jax.nn.gelu(proj(4 * w, 5 * w)), sn_ref[...]).astype(BF16)
    ga_ref[...] = jax.nn.sigmoid(proj(5 * w, 5 * w + D_MODEL)).astype(BF16)
    gb_ref[...] = jax.nn.sigmoid(proj(5 * w + D_MODEL, IN_WIDTH)).astype(BF16)


def _inproj(layer, x, norm_mix, w_in, qn, kn, sn):
    n = x.shape[0]
    row = lambda width: pl.BlockSpec((TM_IN, width), lambda i: (i, 0))
    vec = lambda width: pl.BlockSpec((None, 1, width), lambda i: (layer, 0, 0))
    out = lambda width: jax.ShapeDtypeStruct((n, width), BF16)
    return pl.pallas_call(
        _inproj_kernel,
        grid=(n // TM_IN,),
        in_specs=[row(D_MODEL), vec(D_MODEL),
                  pl.BlockSpec((None, D_MODEL, IN_WIDTH), lambda i: (layer, 0, 0)),
                  vec(LANES), vec(LANES), vec(WIDTH)],
        out_specs=[row(WIDTH)] * 5 + [row(D_MODEL)] * 2,
        out_shape=[out(WIDTH)] * 5 + [out(D_MODEL)] * 2,
        compiler_params=pltpu.CompilerParams(
            dimension_semantics=("arbitrary",), vmem_limit_bytes=VMEM_LIMIT),
        name="inproj",
    )(x, norm_mix, w_in, qn, kn, sn)


def _attn_kernel(q_ref, k_ref, v_ref, o_ref, carry_ref, acc_ref):
    qb = pl.program_id(1)
    pairs = WIDTH // LANES
    lane = lax.broadcasted_iota(jnp.int32, (2 * CHUNK, LANES), 1)
    row = lax.broadcasted_iota(jnp.int32, (2 * CHUNK, LANES), 0)
    upper = row < CHUNK
    query = jnp.where(upper, row, row - CHUNK)
    own = jnp.logical_xor(lane < HEAD_DIM, jnp.logical_not(upper))
    causal = lane < query
    first = lax.broadcasted_iota(jnp.int32, (CHUNK, LANES), 1) < HEAD_DIM
    kr = lax.broadcasted_iota(jnp.int32, (2 * LANES, 2 * LANES), 0) % LANES
    kc = lax.broadcasted_iota(jnp.int32, (2 * LANES, 2 * LANES), 1)
    tail = jnp.where(jnp.logical_or(kr > kc, kc >= LANES), 1.0, 0.0).astype(BF16)
    sign = jnp.uint32(0x80000000)

    cols = [slice(p * LANES, (p + 1) * LANES) for p in range(pairs)]
    q2s = []
    for p in range(pairs):
        qp = q_ref[:, cols[p]]
        q2 = jnp.concatenate([qp, qp], axis=0)
        q2s.append(jnp.where(own, q2, jnp.zeros_like(q2)))

    def key_block(j, diagonal):
        start = pl.multiple_of(j * CHUNK, CHUNK)
        zs = [lax.dot_general(q2s[p], k_ref[pl.ds(start, CHUNK), cols[p]],
                              (((1,), (1,)), ((), ())), preferred_element_type=F32)
              for p in range(pairs)]
        carries = [0.0 if diagonal else carry_ref[p] for p in range(pairs)]
        log_betas, splits = [], []
        for z in zs:
            neg_abs = lax.bitcast_convert_type(lax.bitcast_convert_type(z, jnp.uint32) | sign, F32)
            log1p = jnp.log(1.0 + jnp.exp2(neg_abs)) * LOG2E
            log_beta = jnp.minimum(z, 0.0) - log1p
            log_keep = log_beta - z
            if diagonal:
                log_keep = jnp.where(causal, log_keep, 0.0)
            hi = log_keep.astype(BF16)
            lo = (log_keep - hi.astype(F32)).astype(BF16)
            log_betas.append(log_beta)
            splits.append(jnp.concatenate([hi, lo], axis=1))
        sums = [jnp.dot(s, tail, preferred_element_type=F32) for s in splits]
        weights = []
        for p in range(pairs):
            a = jnp.exp2(log_betas[p] + sums[p][:, :LANES] + carries[p])
            if diagonal:
                a = jnp.where(causal, a, 0.0)
            weights.append(a.astype(BF16))
        outs = [jnp.dot(weights[p], v_ref[pl.ds(start, CHUNK), cols[p]],
                        preferred_element_type=F32) for p in range(pairs)]
        for p in range(pairs):
            carry_ref[p] = carries[p] + sums[p][:, LANES:]
        for p in range(pairs):
            o = jnp.where(first, outs[p][:CHUNK], outs[p][CHUNK:])
            acc_ref[:, cols[p]] = o if diagonal else acc_ref[:, cols[p]] + o

    def alive():
        return jnp.max(carry_ref[...]) > DEAD_LOG2

    key_block(qb, True)

    def cond(state):
        j, live = state
        return jnp.logical_and(j >= 0, live)

    def body(state):
        j, _ = state
        key_block(j, False)
        return j - 1, alive()

    lax.while_loop(cond, body, (qb - 1, alive()))
    o_ref[...] = acc_ref[...].astype(BF16)


def _attention(q, k, v, batch, seq):
    q3, k3, v3 = (t.reshape(batch, seq, WIDTH) for t in (q, k, v))
    blk = pl.BlockSpec((None, CHUNK, WIDTH), lambda b, qb: (b, qb, 0))
    full = pl.BlockSpec((None, seq, WIDTH), lambda b, qb: (b, 0, 0), pipeline_mode=pl.Buffered(1))
    o = pl.pallas_call(
        _attn_kernel,
        grid=(batch, seq // CHUNK),
        in_specs=[blk, full, full],
        out_specs=blk,
        out_shape=jax.ShapeDtypeStruct((batch, seq, WIDTH), BF16),
        scratch_shapes=[pltpu.VMEM((WIDTH // LANES, 2 * CHUNK, LANES), F32),
                        pltpu.VMEM((CHUNK, WIDTH), F32)],
        compiler_params=pltpu.CompilerParams(
            dimension_semantics=("arbitrary", "arbitrary"), vmem_limit_bytes=VMEM_LIMIT),
        name="attn",
    )(q3, k3, v3)
    return o.reshape(batch * seq, WIDTH)


def _mix_kernel(x_ref, oa_ref, u_ref, vs_ref, ga_ref, gb_ref,
                wsp_ref, bsp_ref, wua_ref, wub_ref, wo_ref, gf_ref, wr_ref, br_ref,
                xo_ref, h_ref, route_ref, cnt_ref, run_ref):
    step = pl.program_id(0)

    @pl.when(step == 0)
    def _():
        run_ref[...] = jnp.zeros_like(run_ref)

    lane = lax.broadcasted_iota(jnp.int32, (CHUNK, LANES), 1)
    row = lax.broadcasted_iota(jnp.int32, (CHUNK, LANES), 0)
    first = lane < HEAD_DIM
    tril = lane <= row

    ob_chunks = []
    for c in range(TM_MIX // CHUNK):
        rows = slice(c * CHUNK, (c + 1) * CHUNK)
        cols = []
        for gp in range(WIDTH // LANES):
            vpair = vs_ref[rows, gp * LANES:(gp + 1) * LANES]
            mixed = []
            for g in (2 * gp, 2 * gp + 1):
                wg = jnp.where(tril, wsp_ref[g], 0.0).astype(BF16)
                mixed.append(jnp.dot(wg, vpair, preferred_element_type=F32))
            cols.append(jnp.where(first, mixed[0], mixed[1]))
        mixed = jnp.concatenate(cols, axis=-1) + bsp_ref[...]
        ob_chunks.append((u_ref[rows, :].astype(F32) * mixed).astype(BF16))
    ob = jnp.concatenate(ob_chunks, axis=0)

    up_a = jnp.dot(oa_ref[...], wua_ref[...], preferred_element_type=F32)
    up_b = jnp.dot(ob, wub_ref[...], preferred_element_type=F32)
    merged = ga_ref[...].astype(F32) * up_a + gb_ref[...].astype(F32) * up_b
    x = x_ref[...] + jnp.dot(merged.astype(BF16), wo_ref[...], preferred_element_type=F32)
    xo_ref[...] = x

    h = _rms(x, gf_ref[...])
    _store_row_tiles(h_ref, h)
    logits = jnp.dot(h.astype(BF16), wr_ref[...], preferred_element_type=F32) + br_ref[...]
    lane_t = lax.broadcasted_iota(jnp.int32, (TM_MIX, LANES), 1)
    lanef = lane_t.astype(F32)
    neg = -jnp.inf
    far = float(LANES)

    def first_max(vals):
        m = jnp.max(vals, axis=-1, keepdims=True)
        idx = jnp.min(jnp.where(vals == m, lanef, far), axis=-1, keepdims=True)
        return m, idx

    gl = jnp.where(lane_t < N_GROUPS, logits, neg)
    gmax, grp = first_max(gl)
    grp_w = 1.0 / jnp.sum(jnp.exp(gl - gmax), axis=-1, keepdims=True)
    lo_lane = ROUTE_LANE0 + EXPERTS_PER_GROUP * grp
    in_group = jnp.logical_and(lanef >= lo_lane, lanef < lo_lane + EXPERTS_PER_GROUP)
    el = jnp.where(in_group, logits, neg)
    m1, i1 = first_max(el)
    m2, i2 = first_max(jnp.where(lanef == i1, neg, el))
    e21 = jnp.exp(m2 - m1)
    w1 = grp_w / (1.0 + e21)
    w2 = w1 * e21

    onehot = jnp.logical_or(lanef == i1, lanef == i2)
    rt = lax.broadcasted_iota(jnp.int32, (TM_MIX, TM_MIX), 0)
    ct = lax.broadcasted_iota(jnp.int32, (TM_MIX, TM_MIX), 1)
    before = jnp.where(ct < rt, 1.0, 0.0).astype(BF16)
    prior = jnp.dot(before, jnp.where(onehot, 1.0, 0.0).astype(BF16),
                    preferred_element_type=F32) + run_ref[...]
    r1 = jnp.sum(jnp.where(lanef == i1, prior, 0.0), axis=-1, keepdims=True)
    r2 = jnp.sum(jnp.where(lanef == i2, prior, 0.0), axis=-1, keepdims=True)
    run = run_ref[...] + jnp.sum(jnp.where(onehot, 1.0, 0.0), axis=0, keepdims=True)
    run_ref[...] = run
    cnt_ref[...] = run

    fields = (i1 - ROUTE_LANE0, i2 - ROUTE_LANE0, w1, w2, r1, r2)
    route = jnp.zeros((TM_MIX, LANES), F32)
    for pos, val in enumerate(fields):
        route = jnp.where(lane_t == pos, val, route)
    route_ref[...] = route


def _mix(layer, x, oa, u, vs, ga, gb, wsp, bsp, wua, wub, wo, gf, wr, br):
    n = x.shape[0]
    row = lambda width: pl.BlockSpec((TM_MIX, width), lambda i: (i, 0))
    lay = lambda *shape: pl.BlockSpec((None,) + shape, lambda i: (layer,) + (0,) * len(shape))
    return pl.pallas_call(
        _mix_kernel,
        grid=(n // TM_MIX,),
        in_specs=[row(D_MODEL), row(WIDTH), row(WIDTH), row(WIDTH), row(D_MODEL), row(D_MODEL),
                  lay(HEADS, CHUNK, CHUNK), lay(CHUNK, WIDTH), lay(WIDTH, D_MODEL),
                  lay(WIDTH, D_MODEL), lay(D_MODEL, D_MODEL), lay(1, D_MODEL),
                  lay(D_MODEL, LANES), lay(1, LANES)],
        out_specs=[row(D_MODEL), pl.BlockSpec((TM_MIX * ROW_TILE, LANES), lambda i: (i, 0)),
                   row(LANES), pl.BlockSpec((1, LANES), lambda i: (0, 0))],
        out_shape=[jax.ShapeDtypeStruct((n, D_MODEL), F32),
                   jax.ShapeDtypeStruct((n * ROW_TILE, LANES), F32),
                   jax.ShapeDtypeStruct((n, LANES), F32),
                   jax.ShapeDtypeStruct((1, LANES), F32)],
        scratch_shapes=[pltpu.VMEM((1, LANES), F32)],
        compiler_params=pltpu.CompilerParams(
            dimension_semantics=("arbitrary",), vmem_limit_bytes=VMEM_LIMIT),
        name="mix",
    )(x, oa, u, vs, ga, gb, wsp, bsp, wua, wub, wo, gf, wr, br)


def _dispatch_kernel(dest_ref, zrow_ref, h_ref, xs_ref, zero_ref, sem, zsem):
    i = pl.program_id(0)

    @pl.when(i == 0)
    def _():
        zero_ref[...] = jnp.zeros_like(zero_ref)

        def zero_copy(e):
            rows = pl.ds(zrow_ref[e], MOE_ROWS * ROW_TILE)
            return pltpu.make_async_copy(zero_ref, xs_ref.at[rows, :], zsem)

        for e in range(N_EXPERTS):
            pl.when(zrow_ref[e] >= 0)(lambda e=e: zero_copy(e).start())
        for e in range(N_EXPERTS):
            pl.when(zrow_ref[e] >= 0)(lambda e=e: zero_copy(e).wait())

    base = i * (2 * TM_DISP)
    for t in range(TM_DISP):
        for k in range(2):
            d = dest_ref[base + 2 * t + k]
            pltpu.make_async_copy(h_ref.at[pl.ds(t * ROW_TILE, ROW_TILE), :],
                                  xs_ref.at[pl.ds(d * ROW_TILE, ROW_TILE), :], sem).start()
    for k in range(2):
        pltpu.make_async_copy(h_ref, xs_ref.at[pl.ds(0, TM_DISP * ROW_TILE), :], sem).wait()


def _dispatch(dest, zrow, h, cap):
    n = h.shape[0] // ROW_TILE
    return pl.pallas_call(
        _dispatch_kernel,
        grid_spec=pltpu.PrefetchScalarGridSpec(
            num_scalar_prefetch=2,
            grid=(n // TM_DISP,),
            in_specs=[pl.BlockSpec((TM_DISP * ROW_TILE, LANES), lambda i, dest, zrow: (i, 0))],
            out_specs=pl.BlockSpec(memory_space=pl.ANY),
            scratch_shapes=[pltpu.VMEM((MOE_ROWS * ROW_TILE, LANES), F32),
                            pltpu.SemaphoreType.DMA(()), pltpu.SemaphoreType.DMA(())],
        ),
        out_shape=jax.ShapeDtypeStruct((cap * ROW_TILE, LANES), F32),
        compiler_params=pltpu.CompilerParams(
            dimension_semantics=("arbitrary",), vmem_limit_bytes=VMEM_LIMIT),
        name="dispatch",
    )(dest, zrow, h)


def _expert_kernel(blk_e_ref, nused_ref, xs_ref, wi_ref, wo_ref, ys_ref, wi_b, wo_b):
    b = pl.program_id(0)
    used = b < nused_ref[0]

    @pl.when(jnp.logical_not(used))
    def _():
        ys_ref[...] = jnp.zeros_like(ys_ref)

    new_expert = jnp.logical_or(b == 0, blk_e_ref[b] != blk_e_ref[jnp.maximum(b - 1, 0)])

    @pl.when(jnp.logical_and(used, new_expert))
    def _():
        wi_b[...] = wi_ref[...].astype(BF16)
        wo_b[...] = wo_ref[...].astype(BF16)

    @pl.when(used)
    def _():
        xs = _load_row_tiles(xs_ref, MOE_ROWS).astype(BF16)
        gu = jnp.dot(xs, wi_b[...], preferred_element_type=F32)
        act = jax.nn.silu(gu[:, :EXPERT_FF]) * gu[:, EXPERT_FF:]
        y = jnp.dot(act.astype(BF16), wo_b[...], preferred_element_type=F32)
        _store_row_tiles(ys_ref, y)


def _experts(layer, blk_e, nused, xs, w_e_in, w_e_out):
    cap = xs.shape[0] // ROW_TILE
    rows = MOE_ROWS * ROW_TILE
    return pl.pallas_call(
        _expert_kernel,
        grid_spec=pltpu.PrefetchScalarGridSpec(
            num_scalar_prefetch=2,
            grid=(cap // MOE_ROWS,),
            in_specs=[pl.BlockSpec((rows, LANES),
                                   lambda b, blk_e, nused: (jnp.minimum(b, nused[0] - 1), 0)),
                      pl.BlockSpec((None, None, D_MODEL, 2 * EXPERT_FF),
                                   lambda b, blk_e, nused: (layer, blk_e[b], 0, 0)),
                      pl.BlockSpec((None, None, EXPERT_FF, D_MODEL),
                                   lambda b, blk_e, nused: (layer, blk_e[b], 0, 0))],
            out_specs=pl.BlockSpec((rows, LANES), lambda b, blk_e, nused: (b, 0)),
            scratch_shapes=[pltpu.VMEM((D_MODEL, 2 * EXPERT_FF), BF16),
                            pltpu.VMEM((EXPERT_FF, D_MODEL), BF16)],
        ),
        out_shape=jax.ShapeDtypeStruct((cap * ROW_TILE, LANES), F32),
        compiler_params=pltpu.CompilerParams(
            dimension_semantics=("arbitrary",), vmem_limit_bytes=VMEM_LIMIT),
        name="experts",
    )(blk_e, nused, xs, w_e_in, w_e_out)


def _ple_kernel(dest_ref, x_ref, route_ref, p_ref, ys_ref, gp_ref, wg_ref, wp_ref,
                xo_ref, y0_ref, y1_ref, sem):
    base = pl.program_id(0) * (2 * TM_ROW)
    bufs = (y0_ref, y1_ref)
    for t in range(TM_ROW):
        for k in range(2):
            d = dest_ref[base + 2 * t + k]
            pltpu.make_async_copy(ys_ref.at[pl.ds(d * ROW_TILE, ROW_TILE), :],
                                  bufs[k].at[pl.ds(t * ROW_TILE, ROW_TILE), :],
                                  sem).start(priority=k)
    pe = jnp.dot(p_ref[...].astype(BF16), wp_ref[...], preferred_element_type=F32)
    for buf in bufs:
        pltpu.make_async_copy(ys_ref.at[pl.ds(0, TM_ROW * ROW_TILE), :], buf, sem).wait()

    route = route_ref[...]
    y0 = _load_row_tiles(y0_ref, TM_ROW)
    y1 = _load_row_tiles(y1_ref, TM_ROW)
    x = x_ref[...] + route[:, 2:3] * y0 + route[:, 3:4] * y1
    gate = jax.nn.sigmoid(jnp.dot(_rms(x, gp_ref[...]).astype(BF16), wg_ref[...],
                                  preferred_element_type=F32))
    xo_ref[...] = x + gate * pe


def _ple(layer, dest, x, route, p, ys, gp, wg, wp):
    n = x.shape[0]
    row = lambda width: pl.BlockSpec((TM_ROW, width), lambda i, dest: (i, 0))
    lay = lambda *shape: pl.BlockSpec((None,) + shape,
                                      lambda i, dest: (layer,) + (0,) * len(shape))
    return pl.pallas_call(
        _ple_kernel,
        grid_spec=pltpu.PrefetchScalarGridSpec(
            num_scalar_prefetch=1,
            grid=(n // TM_ROW,),
            in_specs=[row(D_MODEL), row(LANES),
                      pl.BlockSpec((None, TM_ROW, PLE_DIM), lambda i, dest: (layer, i, 0)),
                      pl.BlockSpec(memory_space=pl.ANY),
                      lay(1, D_MODEL), lay(D_MODEL, D_MODEL), lay(PLE_DIM, D_MODEL)],
            out_specs=row(D_MODEL),
            scratch_shapes=[pltpu.VMEM((TM_ROW * ROW_TILE, LANES), F32),
                            pltpu.VMEM((TM_ROW * ROW_TILE, LANES), F32),
                            pltpu.SemaphoreType.DMA(())],
        ),
        out_shape=jax.ShapeDtypeStruct((n, D_MODEL), F32),
        compiler_params=pltpu.CompilerParams(
            dimension_semantics=("arbitrary",), vmem_limit_bytes=VMEM_LIMIT),
        name="ple",
    )(dest, x, route, p, ys, gp, wg, wp)


def _slot_layout(route, counts, cap):
    eid = route[:, 0:2].astype(jnp.int32)
    rank = route[:, 4:6].astype(jnp.int32)
    cnt = counts[0, ROUTE_LANE0:ROUTE_LANE0 + N_EXPERTS].astype(jnp.int32)
    padded = (cnt + MOE_ROWS - 1) // MOE_ROWS * MOE_ROWS
    pend = jnp.cumsum(padded)
    poff = pend - padded
    onehot = eid[:, :, None] == jnp.arange(N_EXPERTS, dtype=jnp.int32)
    dest = jnp.sum(jnp.where(onehot, poff, 0), axis=-1) + rank
    blk_start = jnp.arange(cap // MOE_ROWS, dtype=jnp.int32) * MOE_ROWS
    blk_e = jnp.minimum(jnp.sum(pend[None, :] <= blk_start[:, None], axis=1), N_EXPERTS - 1)
    nused = (pend[-1:] // MOE_ROWS).astype(jnp.int32)
    dest = dest.reshape(-1).astype(jnp.int32)
    zrow = jnp.where(cnt > 0, (pend - MOE_ROWS) * ROW_TILE, -1).astype(jnp.int32)
    return dest, zrow, blk_e.astype(jnp.int32), nused


def kernel(x, p, norm_mix, w_in, q_norm, k_norm, sgu_norm, w_spatial, b_spatial, w_up_a, w_up_b,
           w_out, norm_ffn, w_group_router, b_group_router, w_expert_router, b_expert_router,
           w_expert_in, w_expert_out, norm_ple, w_ple_gate, w_ple_proj):
    batch, seq, d = x.shape
    depth = w_in.shape[0]
    n = batch * seq
    cap = 2 * n + N_EXPERTS * MOE_ROWS

    vec = lambda a: a[:, None, :]
    w_in_b, w_ua_b, w_ub_b, w_o_b = (a.astype(BF16) for a in (w_in, w_up_a, w_up_b, w_out))
    w_pg_b, w_pp_b = w_ple_gate.astype(BF16), w_ple_proj.astype(BF16)
    qn2, kn2 = vec(jnp.tile(q_norm, (1, 2))), vec(jnp.tile(k_norm, (1, 2)))
    b_sp = jnp.repeat(jnp.swapaxes(b_spatial, 1, 2), HEAD_DIM, axis=2)
    pad = LANES - N_GROUPS - N_EXPERTS
    w_r = jnp.pad(jnp.concatenate([w_group_router, w_expert_router], axis=2),
                  ((0, 0), (0, 0), (0, pad))).astype(BF16)
    b_r = vec(jnp.pad(jnp.concatenate([b_group_router, b_expert_router], axis=1),
                      ((0, 0), (0, pad))))
    p2 = p.reshape(depth, n, PLE_DIM)

    xf = x.reshape(n, d)
    for i in range(depth):
        q, k, v, u, vs, ga, gb = _inproj(i, xf, vec(norm_mix), w_in_b, qn2, kn2, vec(sgu_norm))
        oa = _attention(q, k, v, batch, seq)
        xf, h, route, counts = _mix(i, xf, oa, u, vs, ga, gb, w_spatial, b_sp, w_ua_b, w_ub_b,
                                    w_o_b, vec(norm_ffn), w_r, b_r)
        dest, zrow, blk_e, nused = _slot_layout(route, counts, cap)
        xs = _dispatch(dest, zrow, h, cap)
        ys = _experts(i, blk_e, nused, xs, w_expert_in, w_expert_out)
        xf = _ple(i, dest, xf, route, p2, ys, vec(norm_ple), w_pg_b, w_pp_b)
    return xf.reshape(batch, seq, d)
```

```python
import functools
import math

import jax
import jax.numpy as jnp
from jax import lax
from jax.experimental import pallas as pl
from jax.experimental.pallas import tpu as pltpu

F32 = jnp.float32
BF16 = jnp.bfloat16

D_MODEL = 1024
HEADS = 8
HEAD_DIM = 64
WIDTH = HEADS * HEAD_DIM
CHUNK = 128
IN_WIDTH = 5 * WIDTH + 2 * D_MODEL
N_GROUPS = 4
EXPERTS_PER_GROUP = 8
N_EXPERTS = N_GROUPS * EXPERTS_PER_GROUP
EXPERT_FF = 512
PLE_DIM = 256
EPS = 1e-6

LANES = 128
ROW_TILE = D_MODEL // LANES
ROUTE_LANE0 = N_GROUPS
TM_IN = 256
TM_MIX = 256
TM_DISP = 512
TM_ROW = 256
MOE_ROWS = 256
LOG2E = 1.4426950408889634
DEAD_LOG2 = -105.0 * LOG2E
VMEM_LIMIT = 48 * 1024 * 1024


def _rms(x, gain):
    return x * lax.rsqrt(jnp.mean(x * x, axis=-1, keepdims=True) + EPS) * gain


def _store_row_tiles(ref, value, lead=()):
    rows = value.shape[0]
    for c in range(ROW_TILE):
        ref[lead + (pl.ds(c, rows, stride=ROW_TILE), slice(None))] = (
            value[:, c * LANES:(c + 1) * LANES])


def _load_row_tiles(ref, rows, lead=()):
    return jnp.concatenate(
        [ref[lead + (pl.ds(c, rows, stride=ROW_TILE), slice(None))] for c in range(ROW_TILE)],
        axis=1)


def _inproj_kernel(x_ref, g_ref, w_ref, qn_ref, kn_ref, sn_ref,
                   q_ref, k_ref, v_ref, u_ref, vs_ref, ga_ref, gb_ref):
    hb = _rms(x_ref[...], g_ref[...]).astype(BF16)

    def proj(lo, hi):
        return jnp.dot(hb, w_ref[:, lo:hi], preferred_element_type=F32)

    first = lax.broadcasted_iota(jnp.int32, (1, LANES), 1) < HEAD_DIM

    def head_norm(z, gain, scale):
        outs = []
        for c in range(WIDTH // LANES):
            zc = z[:, c * LANES:(c + 1) * LANES]
            sq = zc * zc
            sa = jnp.sum(jnp.where(first, sq, 0.0), axis=-1, keepdims=True)
            sb = jnp.sum(jnp.where(first, 0.0, sq), axis=-1, keepdims=True)
            ms = jnp.where(first, sa, sb) * (1.0 / HEAD_DIM)
            outs.append(zc * lax.rsqrt(ms + EPS) * (gain * scale))
        return jnp.concatenate(outs, axis=-1)

    w = WIDTH
    q_ref[...] = head_norm(proj(0, w), qn_ref[...], LOG2E / math.sqrt(HEAD_DIM)).astype(BF16)
    k_ref[...] = head_norm(proj(w, 2 * w), kn_ref[...], 1.0).astype(BF16)
    v_ref[...] = proj(2 * w, 3 * w).astype(BF16)
    u_ref[...] = jax.nn.gelu(proj(3 * w, 4 * w)).astype(BF16)
    vs_ref[...] = _rms(jax.nn.gelu(proj(4 * w, 5 * w)), sn_ref[...]).astype(BF16)
    ga_ref[...] = jax.nn.sigmoid(proj(5 * w, 5 * w + D_MODEL)).astype(BF16)
    gb_ref[...] = jax.nn.sigmoid(proj(5 * w + D_MODEL, IN_WIDTH)).astype(BF16)


def _inproj(layer, x, norm_mix, w_in, qn, kn, sn):
    n = x.shape[0]
    row = lambda width: pl.BlockSpec((TM_IN, width), lambda i: (i, 0))
    vec = lambda width: pl.BlockSpec((None, 1, width), lambda i: (layer, 0, 0))
    out = lambda width: jax.ShapeDtypeStruct((n, width), BF16)
    return pl.pallas_call(
        _inproj_kernel,
        grid=(n // TM_IN,),
        in_specs=[row(D_MODEL), vec(D_MODEL),
                  pl.BlockSpec((None, D_MODEL, IN_WIDTH), lambda i: (layer, 0, 0)),
                  vec(LANES), vec(LANES), vec(WIDTH)],
        out_specs=[row(WIDTH)] * 5 + [row(D_MODEL)] * 2,
        out_shape=[out(WIDTH)] * 5 + [out(D_MODEL)] * 2,
        compiler_params=pltpu.CompilerParams(
            dimension_semantics=("arbitrary",), vmem_limit_bytes=VMEM_LIMIT),
        name="inproj",
    )(x, norm_mix, w_in, qn, kn, sn)


def _attn_kernel(q_ref, k_ref, v_ref, o_ref, carry_ref, acc_ref):
    qb = pl.program_id(1)
    pairs = WIDTH // LANES
    lane = lax.broadcasted_iota(jnp.int32, (2 * CHUNK, LANES), 1)
    row = lax.broadcasted_iota(jnp.int32, (2 * CHUNK, LANES), 0)
    upper = row < CHUNK
    query = jnp.where(upper, row, row - CHUNK)
    own = jnp.logical_xor(lane < HEAD_DIM, jnp.logical_not(upper))
    causal = lane < query
    first = lax.broadcasted_iota(jnp.int32, (CHUNK, LANES), 1) < HEAD_DIM
    kr = lax.broadcasted_iota(jnp.int32, (2 * LANES, 2 * LANES), 0) % LANES
    kc = lax.broadcasted_iota(jnp.int32, (2 * LANES, 2 * LANES), 1)
    tail = jnp.where(jnp.logical_or(kr > kc, kc >= LANES), 1.0, 0.0).astype(BF16)
    sign = jnp.uint32(0x80000000)

    cols = [slice(p * LANES, (p + 1) * LANES) for p in range(pairs)]
    q2s = []
    for p in range(pairs):
        qp = q_ref[:, cols[p]]
        q2 = jnp.concatenate([qp, qp], axis=0)
        q2s.append(jnp.where(own, q2, jnp.zeros_like(q2)))

    def key_block(j, diagonal):
        start = pl.multiple_of(j * CHUNK, CHUNK)
        zs = [lax.dot_general(q2s[p], k_ref[pl.ds(start, CHUNK), cols[p]],
                              (((1,), (1,)), ((), ())), preferred_element_type=F32)
              for p in range(pairs)]
        carries = [0.0 if diagonal else carry_ref[p] for p in range(pairs)]
        log_betas, splits = [], []
        for z in zs:
            neg_abs = lax.bitcast_convert_type(lax.bitcast_convert_type(z, jnp.uint32) | sign, F32)
            log1p = jnp.log(1.0 + jnp.exp2(neg_abs)) * LOG2E
            log_beta = jnp.minimum(z, 0.0) - log1p
            log_keep = log_beta - z
            if diagonal:
                log_keep = jnp.where(causal, log_keep, 0.0)
            hi = log_keep.astype(BF16)
            lo = (log_keep - hi.astype(F32)).astype(BF16)
            log_betas.append(log_beta)
            splits.append(jnp.concatenate([hi, lo], axis=1))
        sums = [jnp.dot(s, tail, preferred_element_type=F32) for s in splits]
        weights = []
        for p in range(pairs):
            a = jnp.exp2(log_betas[p] + sums[p][:, :LANES] + carries[p])
            if diagonal:
                a = jnp.where(causal, a, 0.0)
            weights.append(a.astype(BF16))
        outs = [jnp.dot(weights[p], v_ref[pl.ds(start, CHUNK), cols[p]],
                        preferred_element_type=F32) for p in range(pairs)]
        for p in range(pairs):
            carry_ref[p] = carries[p] + sums[p][:, LANES:]
        for p in range(pairs):
            o = jnp.where(first, outs[p][:CHUNK], outs[p][CHUNK:])
            acc_ref[:, cols[p]] = o if diagonal else acc_ref[:, cols[p]] + o

    def alive():
        return jnp.max(carry_ref[...]) > DEAD_LOG2

    key_block(qb, True)

    def cond(state):
        j, live = state
        return jnp.logical_and(j >= 0, live)

    def body(state):
        j, _ = state
        key_block(j, False)
        return j - 1, alive()

    lax.while_loop(cond, body, (qb - 1, alive()))
    o_ref[...] = acc_ref[...].astype(BF16)


def _attention(q, k, v, batch, seq):
    q3, k3, v3 = (t.reshape(batch, seq, WIDTH) for t in (q, k, v))
    blk = pl.BlockSpec((None, CHUNK, WIDTH), lambda b, qb: (b, qb, 0))
    full = pl.BlockSpec((None, seq, WIDTH), lambda b, qb: (b, 0, 0), pipeline_mode=pl.Buffered(1))
    o = pl.pallas_call(
        _attn_kernel,
        grid=(batch, seq // CHUNK),
        in_specs=[blk, full, full],
        out_specs=blk,
        out_shape=jax.ShapeDtypeStruct((batch, seq, WIDTH), BF16),
        scratch_shapes=[pltpu.VMEM((WIDTH // LANES, 2 * CHUNK, LANES), F32),
                        pltpu.VMEM((CHUNK, WIDTH), F32)],
        compiler_params=pltpu.CompilerParams(
            dimension_semantics=("arbitrary", "arbitrary"), vmem_limit_bytes=VMEM_LIMIT),
        name="attn",
    )(q3, k3, v3)
    return o.reshape(batch * seq, WIDTH)


def _mix_kernel(x_ref, oa_ref, u_ref, vs_ref, ga_ref, gb_ref,
                wsp_ref, bsp_ref, wua_ref, wub_ref, wo_ref, gf_ref, wr_ref, br_ref,
                xo_ref, h_ref, route_ref, cnt_ref, run_ref):
    step = pl.program_id(0)

    @pl.when(step == 0)
    def _():
        run_ref[...] = jnp.zeros_like(run_ref)

    lane = lax.broadcasted_iota(jnp.int32, (CHUNK, LANES), 1)
    row = lax.broadcasted_iota(jnp.int32, (CHUNK, LANES), 0)
    first = lane < HEAD_DIM
    tril = lane <= row

    ob_chunks = []
    for c in range(TM_MIX // CHUNK):
        rows = slice(c * CHUNK, (c + 1) * CHUNK)
        cols = []
        for gp in range(WIDTH // LANES):
            vpair = vs_ref[rows, gp * LANES:(gp + 1) * LANES]
            mixed = []
            for g in (2 * gp, 2 * gp + 1):
                wg = jnp.where(tril, wsp_ref[g], 0.0).astype(BF16)
                mixed.append(jnp.dot(wg, vpair, preferred_element_type=F32))
            cols.append(jnp.where(first, mixed[0], mixed[1]))
        mixed = jnp.concatenate(cols, axis=-1) + bsp_ref[...]
        ob_chunks.append((u_ref[rows, :].astype(F32) * mixed).astype(BF16))
    ob = jnp.concatenate(ob_chunks, axis=0)

    up_a = jnp.dot(oa_ref[...], wua_ref[...], preferred_element_type=F32)
    up_b = jnp.dot(ob, wub_ref[...], preferred_element_type=F32)
    merged = ga_ref[...].astype(F32) * up_a + gb_ref[...].astype(F32) * up_b
    x = x_ref[...] + jnp.dot(merged.astype(BF16), wo_ref[...], preferred_element_type=F32)
    xo_ref[...] = x

    h = _rms(x, gf_ref[...])
    _store_row_tiles(h_ref, h)
    logits = jnp.dot(h.astype(BF16), wr_ref[...], preferred_element_type=F32) + br_ref[...]
    lane_t = lax.broadcasted_iota(jnp.int32, (TM_MIX, LANES), 1)
    lanef = lane_t.astype(F32)
    neg = -jnp.inf
    far = float(LANES)

    def first_max(vals):
        m = jnp.max(vals, axis=-1, keepdims=True)
        idx = jnp.min(jnp.where(vals == m, lanef, far), axis=-1, keepdims=True)
        return m, idx

    gl = jnp.where(lane_t < N_GROUPS, logits, neg)
    gmax, grp = first_max(gl)
    grp_w = 1.0 / jnp.sum(jnp.exp(gl - gmax), axis=-1, keepdims=True)
    lo_lane = ROUTE_LANE0 + EXPERTS_PER_GROUP * grp
    in_group = jnp.logical_and(lanef >= lo_lane, lanef < lo_lane + EXPERTS_PER_GROUP)
    el = jnp.where(in_group, logits, neg)
    m1, i1 = first_max(el)
    m2, i2 = first_max(jnp.where(lanef == i1, neg, el))
    e21 = jnp.exp(m2 - m1)
    w1 = grp_w / (1.0 + e21)
    w2 = w1 * e21

    onehot = jnp.logical_or(lanef == i1, lanef == i2)
    rt = lax.broadcasted_iota(jnp.int32, (TM_MIX, TM_MIX), 0)
    ct = lax.broadcasted_iota(jnp.int32, (TM_MIX, TM_MIX), 1)
    before = jnp.where(ct < rt, 1.0, 0.0).astype(BF16)
    prior = jnp.dot(before, jnp.where(onehot, 1.0, 0.0).astype(BF16),
                    preferred_element_type=F32) + run_ref[...]
    r1 = jnp.sum(jnp.where(lanef == i1, prior, 0.0), axis=-1, keepdims=True)
    r2 = jnp.sum(jnp.where(lanef == i2, prior, 0.0), axis=-1, keepdims=True)
    run = run_ref[...] + jnp.sum(jnp.where(onehot, 1.0, 0.0), axis=0, keepdims=True)
    run_ref[...] = run
    cnt_ref[...] = run

    fields = (i1 - ROUTE_LANE0, i2 - ROUTE_LANE0, w1, w2, r1, r2)
    route = jnp.zeros((TM_MIX, LANES), F32)
    for pos, val in enumerate(fields):
        route = jnp.where(lane_t == pos, val, route)
    route_ref[...] = route


def _mix(layer, x, oa, u, vs, ga, gb, wsp, bsp, wua, wub, wo, gf, wr, br):
    n = x.shape[0]
    row = lambda width: pl.BlockSpec((TM_MIX, width), lambda i: (i, 0))
    lay = lambda *shape: pl.BlockSpec((None,) + shape, lambda i: (layer,) + (0,) * len(shape))
    return pl.pallas_call(
        _mix_kernel,
        grid=(n // TM_MIX,),
        in_specs=[row(D_MODEL), row(WIDTH), row(WIDTH), row(WIDTH), row(D_MODEL), row(D_MODEL),
                  lay(HEADS, CHUNK, CHUNK), lay(CHUNK, WIDTH), lay(WIDTH, D_MODEL),
                  lay(WIDTH, D_MODEL), lay(D_MODEL, D_MODEL), lay(1, D_MODEL),
                  lay(D_MODEL, LANES), lay(1, LANES)],
        out_specs=[row(D_MODEL), pl.BlockSpec((TM_MIX * ROW_TILE, LANES), lambda i: (i, 0)),
                   row(LANES), pl.BlockSpec((1, LANES), lambda i: (0, 0))],
        out_shape=[jax.ShapeDtypeStruct((n, D_MODEL), F32),
                   jax.ShapeDtypeStruct((n * ROW_TILE, LANES), F32),
                   jax.ShapeDtypeStruct((n, LANES), F32),
                   jax.ShapeDtypeStruct((1, LANES), F32)],
        scratch_shapes=[pltpu.VMEM((1, LANES), F32)],
        compiler_params=pltpu.CompilerParams(
            dimension_semantics=("arbitrary",), vmem_limit_bytes=VMEM_LIMIT),
        name="mix",
    )(x, oa, u, vs, ga, gb, wsp, bsp, wua, wub, wo, gf, wr, br)


def _dispatch_kernel(dest_ref, zrow_ref, h_ref, xs_ref, zero_ref, sem, zsem, tsem):
    i = pl.program_id(0)
    block_rows = MOE_ROWS * ROW_TILE
    nblk = xs_ref.shape[0] // block_rows
    nused = zrow_ref[N_EXPERTS]

    def zero_copy(first_row, zero_sem):
        return pltpu.make_async_copy(zero_ref, xs_ref.at[pl.ds(first_row, block_rows), :],
                                     zero_sem)

    def tail_copy(j):
        return zero_copy((nused + j) * block_rows, tsem)

    @pl.when(i == 0)
    def _():
        zero_ref[...] = jnp.zeros_like(zero_ref)
        for j in range(N_EXPERTS):
            pl.when(nused + j < nblk)(lambda j=j: tail_copy(j).start())
        for e in range(N_EXPERTS):
            pl.when(zrow_ref[e] >= 0)(lambda e=e: zero_copy(zrow_ref[e], zsem).start())
        for e in range(N_EXPERTS):
            pl.when(zrow_ref[e] >= 0)(lambda e=e: zero_copy(zrow_ref[e], zsem).wait())

    base = i * (2 * TM_DISP)
    for t in range(TM_DISP):
        for k in range(2):
            d = dest_ref[base + 2 * t + k]
            pltpu.make_async_copy(h_ref.at[pl.ds(t * ROW_TILE, ROW_TILE), :],
                                  xs_ref.at[pl.ds(d * ROW_TILE, ROW_TILE), :],
                                  sem).start(priority=k)
    for k in range(2):
        pltpu.make_async_copy(h_ref, xs_ref.at[pl.ds(0, TM_DISP * ROW_TILE), :], sem).wait()

    @pl.when(i == pl.num_programs(0) - 1)
    def _():
        for j in range(N_EXPERTS):
            pl.when(nused + j < nblk)(lambda j=j: tail_copy(j).wait())


def _dispatch(dest, zrow, h, cap):
    n = h.shape[0] // ROW_TILE
    return pl.pallas_call(
        _dispatch_kernel,
        grid_spec=pltpu.PrefetchScalarGridSpec(
            num_scalar_prefetch=2,
            grid=(n // TM_DISP,),
            in_specs=[pl.BlockSpec((TM_DISP * ROW_TILE, LANES), lambda i, dest, zrow: (i, 0))],
            out_specs=pl.BlockSpec(memory_space=pl.ANY),
            scratch_shapes=[pltpu.VMEM((MOE_ROWS * ROW_TILE, LANES), F32),
                            pltpu.SemaphoreType.DMA(()), pltpu.SemaphoreType.DMA(()),
                            pltpu.SemaphoreType.DMA(())],
        ),
        out_shape=jax.ShapeDtypeStruct((cap * ROW_TILE, LANES), F32),
        compiler_params=pltpu.CompilerParams(
            dimension_semantics=("arbitrary",), vmem_limit_bytes=VMEM_LIMIT),
        name="dispatch",
    )(dest, zrow, h)


def _expert_kernel(blk_e_ref, nused_ref, xs_ref, wi_ref, wo_ref, ys_ref, wi_b, wo_b):
    b = pl.program_id(0)
    used = b < nused_ref[0]

    @pl.when(jnp.logical_not(used))
    def _():
        ys_ref[...] = jnp.zeros_like(ys_ref)

    new_expert = jnp.logical_or(b == 0, blk_e_ref[b] != blk_e_ref[jnp.maximum(b - 1, 0)])

    @pl.when(jnp.logical_and(used, new_expert))
    def _():
        wi_b[...] = wi_ref[...].astype(BF16)
        wo_b[...] = wo_ref[...].astype(BF16)

    @pl.when(used)
    def _():
        xs = _load_row_tiles(xs_ref, MOE_ROWS).astype(BF16)
        gu = jnp.dot(xs, wi_b[...], preferred_element_type=F32)
        act = jax.nn.silu(gu[:, :EXPERT_FF]) * gu[:, EXPERT_FF:]
        y = jnp.dot(act.astype(BF16), wo_b[...], preferred_element_type=F32)
        _store_row_tiles(ys_ref, y)


def _experts(layer, blk_e, nused, xs, w_e_in, w_e_out):
    cap = xs.shape[0] // ROW_TILE
    rows = MOE_ROWS * ROW_TILE
    return pl.pallas_call(
        _expert_kernel,
        grid_spec=pltpu.PrefetchScalarGridSpec(
            num_scalar_prefetch=2,
            grid=(cap // MOE_ROWS,),
            in_specs=[pl.BlockSpec((rows, LANES),
                                   lambda b, blk_e, nused: (jnp.minimum(b, nused[0] - 1), 0)),
                      pl.BlockSpec((None, None, D_MODEL, 2 * EXPERT_FF),
                                   lambda b, blk_e, nused: (layer, blk_e[b], 0, 0)),
                      pl.BlockSpec((None, None, EXPERT_FF, D_MODEL),
                                   lambda b, blk_e, nused: (layer, blk_e[b], 0, 0))],
            out_specs=pl.BlockSpec((rows, LANES), lambda b, blk_e, nused: (b, 0)),
            scratch_shapes=[pltpu.VMEM((D_MODEL, 2 * EXPERT_FF), BF16),
                            pltpu.VMEM((EXPERT_FF, D_MODEL), BF16)],
        ),
        out_shape=jax.ShapeDtypeStruct((cap * ROW_TILE, LANES), F32),
        compiler_params=pltpu.CompilerParams(
            dimension_semantics=("arbitrary",), vmem_limit_bytes=VMEM_LIMIT),
        name="experts",
    )(blk_e, nused, xs, w_e_in, w_e_out)


def _ple_kernel(dest_ref, x_ref, route_ref, p_ref, ys_ref, gp_ref, wg_ref, wp_ref,
                xo_ref, ybuf, sem):
    j = pl.program_id(0)
    last_step = pl.num_programs(0) - 1

    def start_gather(tile, half):
        base = tile * (2 * TM_ROW)
        for t in range(TM_ROW):
            for k in range(2):
                d = dest_ref[base + 2 * t + k]
                pltpu.make_async_copy(ys_ref.at[pl.ds(d * ROW_TILE, ROW_TILE), :],
                                      ybuf.at[half, k, pl.ds(t * ROW_TILE, ROW_TILE), :],
                                      sem.at[half]).start(priority=k)

    def wait_gather(half):
        for k in range(2):
            pltpu.make_async_copy(ys_ref.at[pl.ds(0, TM_ROW * ROW_TILE), :],
                                  ybuf.at[half, k], sem.at[half]).wait()

    def combine(half):
        rows = slice(half * TM_ROW, (half + 1) * TM_ROW)
        pe = jnp.dot(p_ref[rows, :].astype(BF16), wp_ref[...], preferred_element_type=F32)
        wait_gather(half)
        route = route_ref[rows, :]
        y0 = _load_row_tiles(ybuf, TM_ROW, (half, 0))
        y1 = _load_row_tiles(ybuf, TM_ROW, (half, 1))
        x = x_ref[rows, :] + route[:, 2:3] * y0 + route[:, 3:4] * y1
        gate = jax.nn.sigmoid(jnp.dot(_rms(x, gp_ref[...]).astype(BF16), wg_ref[...],
                                      preferred_element_type=F32))
        xo_ref[rows, :] = x + gate * pe

    @pl.when(j == 0)
    def _():
        start_gather(0, 0)

    start_gather(2 * j + 1, 1)
    combine(0)
    start_gather(jnp.minimum(2 * j + 2, 2 * last_step + 1), 0)
    combine(1)

    @pl.when(j == last_step)
    def _():
        wait_gather(0)


def _ple(layer, dest, x, route, p, ys, gp, wg, wp):
    n = x.shape[0]
    row = lambda width: pl.BlockSpec((2 * TM_ROW, width), lambda j, dest: (j, 0))
    lay = lambda *shape: pl.BlockSpec((None,) + shape,
                                      lambda j, dest: (layer,) + (0,) * len(shape))
    return pl.pallas_call(
        _ple_kernel,
        grid_spec=pltpu.PrefetchScalarGridSpec(
            num_scalar_prefetch=1,
            grid=(n // (2 * TM_ROW),),
            in_specs=[row(D_MODEL), row(LANES),
                      pl.BlockSpec((None, 2 * TM_ROW, PLE_DIM), lambda j, dest: (layer, j, 0)),
                      pl.BlockSpec(memory_space=pl.ANY),
                      lay(1, D_MODEL), lay(D_MODEL, D_MODEL), lay(PLE_DIM, D_MODEL)],
            out_specs=row(D_MODEL),
            scratch_shapes=[pltpu.VMEM((2, 2, TM_ROW * ROW_TILE, LANES), F32),
                            pltpu.SemaphoreType.DMA((2,))],
        ),
        out_shape=jax.ShapeDtypeStruct((n, D_MODEL), F32),
        compiler_params=pltpu.CompilerParams(
            dimension_semantics=("arbitrary",), vmem_limit_bytes=VMEM_LIMIT),
        name="ple",
    )(dest, x, route, p, ys, gp, wg, wp)


def _slot_layout(route, counts, cap):
    eid = route[:, 0:2].astype(jnp.int32)
    rank = route[:, 4:6].astype(jnp.int32)
    cnt = counts[0, ROUTE_LANE0:ROUTE_LANE0 + N_EXPERTS].astype(jnp.int32)
    padded = (cnt + MOE_ROWS - 1) // MOE_ROWS * MOE_ROWS
    pend = jnp.cumsum(padded)
    poff = pend - padded
    onehot = eid[:, :, None] == jnp.arange(N_EXPERTS, dtype=jnp.int32)
    dest = jnp.sum(jnp.where(onehot, poff, 0), axis=-1) + rank
    blk_start = jnp.arange(cap // MOE_ROWS, dtype=jnp.int32) * MOE_ROWS
    blk_e = jnp.minimum(jnp.sum(pend[None, :] <= blk_start[:, None], axis=1), N_EXPERTS - 1)
    nused = (pend[-1:] // MOE_ROWS).astype(jnp.int32)
    dest = dest.reshape(-1).astype(jnp.int32)
    zrow = jnp.where(cnt > 0, (pend - MOE_ROWS) * ROW_TILE, -1).astype(jnp.int32)
    return dest, jnp.concatenate([zrow, nused]), blk_e.astype(jnp.int32), nused


def kernel(x, p, norm_mix, w_in, q_norm, k_norm, sgu_norm, w_spatial, b_spatial, w_up_a, w_up_b,
           w_out, norm_ffn, w_group_router, b_group_router, w_expert_router, b_expert_router,
           w_expert_in, w_expert_out, norm_ple, w_ple_gate, w_ple_proj):
    batch, seq, d = x.shape
    depth = w_in.shape[0]
    n = batch * seq
    cap = 2 * n + N_EXPERTS * MOE_ROWS

    vec = lambda a: a[:, None, :]
    w_in_b, w_ua_b, w_ub_b, w_o_b = (a.astype(BF16) for a in (w_in, w_up_a, w_up_b, w_out))
    w_pg_b, w_pp_b = w_ple_gate.astype(BF16), w_ple_proj.astype(BF16)
    qn2, kn2 = vec(jnp.tile(q_norm, (1, 2))), vec(jnp.tile(k_norm, (1, 2)))
    b_sp = jnp.repeat(jnp.swapaxes(b_spatial, 1, 2), HEAD_DIM, axis=2)
    pad = LANES - N_GROUPS - N_EXPERTS
    w_r = jnp.pad(jnp.concatenate([w_group_router, w_expert_router], axis=2),
                  ((0, 0), (0, 0), (0, pad))).astype(BF16)
    b_r = vec(jnp.pad(jnp.concatenate([b_group_router, b_expert_router], axis=1),
                      ((0, 0), (0, pad))))
    p2 = p.reshape(depth, n, PLE_DIM)

    xf = x.reshape(n, d)
    for i in range(depth):
        q, k, v, u, vs, ga, gb = _inproj(i, xf, vec(norm_mix), w_in_b, qn2, kn2, vec(sgu_norm))
        oa = _attention(q, k, v, batch, seq)
        xf, h, route, counts = _mix(i, xf, oa, u, vs, ga, gb, w_spatial, b_sp, w_ua_b, w_ub_b,
                                    w_o_b, vec(norm_ffn), w_r, b_r)
        dest, zrow, blk_e, nused = _slot_layout(route, counts, cap)
        xs = _dispatch(dest, zrow, h, cap)
        ys = _experts(i, blk_e, nused, xs, w_expert_in, w_expert_out)
        xf = _ple(i, dest, xf, route, p2, ys, vec(norm_ple), w_pg_b, w_pp_b)
    return xf.reshape(batch, seq, d)
```

```python
import functools
import math

import jax
import jax.numpy as jnp
from jax import lax
from jax.experimental import pallas as pl
from jax.experimental.pallas import tpu as pltpu

F32 = jnp.float32
BF16 = jnp.bfloat16

D_MODEL = 1024
HEADS = 8
HEAD_DIM = 64
WIDTH = HEADS * HEAD_DIM
CHUNK = 128
IN_WIDTH = 5 * WIDTH + 2 * D_MODEL
N_GROUPS = 4
EXPERTS_PER_GROUP = 8
N_EXPERTS = N_GROUPS * EXPERTS_PER_GROUP
EXPERT_FF = 512
PLE_DIM = 256
EPS = 1e-6

LANES = 128
ROW_TILE = D_MODEL // LANES
ROUTE_LANE0 = N_GROUPS
TM_IN = 256
TM_MIX = 256
TM_DISP = 512
TM_ROW = 256
MOE_ROWS = 256
LOG2E = 1.4426950408889634
DEAD_LOG2 = -105.0 * LOG2E
VMEM_LIMIT = 48 * 1024 * 1024


def _rms(x, gain):
    return x * lax.rsqrt(jnp.mean(x * x, axis=-1, keepdims=True) + EPS) * gain


def _store_row_tiles(ref, value, lead=()):
    rows = value.shape[0]
    for c in range(ROW_TILE):
        ref[lead + (pl.ds(c, rows, stride=ROW_TILE), slice(None))] = (
            value[:, c * LANES:(c + 1) * LANES])


def _load_row_tiles(ref, rows, lead=()):
    return jnp.concatenate(
        [ref[lead + (pl.ds(c, rows, stride=ROW_TILE), slice(None))] for c in range(ROW_TILE)],
        axis=1)


def _inproj_kernel(x_ref, g_ref, w_ref, qn_ref, kn_ref, sn_ref,
                   q_ref, k_ref, v_ref, u_ref, vs_ref, ga_ref, gb_ref):
    hb = _rms(x_ref[...], g_ref[...]).astype(BF16)

    def proj(lo, hi):
        return jnp.dot(hb, w_ref[:, lo:hi], preferred_element_type=F32)

    first = lax.broadcasted_iota(jnp.int32, (1, LANES), 1) < HEAD_DIM

    def head_norm(z, gain, scale):
        outs = []
        for c in range(WIDTH // LANES):
            zc = z[:, c * LANES:(c + 1) * LANES]
            sq = zc * zc
            sa = jnp.sum(jnp.where(first, sq, 0.0), axis=-1, keepdims=True)
            sb = jnp.sum(jnp.where(first, 0.0, sq), axis=-1, keepdims=True)
            ms = jnp.where(first, sa, sb) * (1.0 / HEAD_DIM)
            outs.append(zc * lax.rsqrt(ms + EPS) * (gain * scale))
        return jnp.concatenate(outs, axis=-1)

    w = WIDTH
    q_ref[...] = head_norm(proj(0, w), qn_ref[...], LOG2E / math.sqrt(HEAD_DIM)).astype(BF16)
    k_ref[...] = head_norm(proj(w, 2 * w), kn_ref[...], 1.0).astype(BF16)
    v_ref[...] = proj(2 * w, 3 * w).astype(BF16)
    u_ref[...] = jax.nn.gelu(proj(3 * w, 4 * w)).astype(BF16)
    vs_ref[...] = _rms(jax.nn.gelu(proj(4 * w, 5 * w)), sn_ref[...]).astype(BF16)
    ga_ref[...] = jax.nn.sigmoid(proj(5 * w, 5 * w + D_MODEL)).astype(BF16)
    gb_ref[...] = jax.nn.sigmoid(proj(5 * w + D_MODEL, IN_WIDTH)).astype(BF16)


def _inproj(layer, x, norm_mix, w_in, qn, kn, sn):
    n = x.shape[0]
    row = lambda width: pl.BlockSpec((TM_IN, width), lambda i: (i, 0))
    vec = lambda width: pl.BlockSpec((None, 1, width), lambda i: (layer, 0, 0))
    out = lambda width: jax.ShapeDtypeStruct((n, width), BF16)
    return pl.pallas_call(
        _inproj_kernel,
        grid=(n // TM_IN,),
        in_specs=[row(D_MODEL), vec(D_MODEL),
                  pl.BlockSpec((None, D_MODEL, IN_WIDTH), lambda i: (layer, 0, 0)),
                  vec(LANES), vec(LANES), vec(WIDTH)],
        out_specs=[row(WIDTH)] * 5 + [row(D_MODEL)] * 2,
        out_shape=[out(WIDTH)] * 5 + [out(D_MODEL)] * 2,
        compiler_params=pltpu.CompilerParams(
            dimension_semantics=("arbitrary",), vmem_limit_bytes=VMEM_LIMIT),
        name="inproj",
    )(x, norm_mix, w_in, qn, kn, sn)


def _attn_kernel(q_ref, k_ref, v_ref, o_ref, carry_ref, acc_ref, z_ref):
    qb = pl.program_id(1)
    pairs = WIDTH // LANES
    lane = lax.broadcasted_iota(jnp.int32, (2 * CHUNK, LANES), 1)
    row = lax.broadcasted_iota(jnp.int32, (2 * CHUNK, LANES), 0)
    upper = row < CHUNK
    query = jnp.where(upper, row, row - CHUNK)
    own = jnp.logical_xor(lane < HEAD_DIM, jnp.logical_not(upper))
    causal = lane < query
    first = lax.broadcasted_iota(jnp.int32, (CHUNK, LANES), 1) < HEAD_DIM
    kr = lax.broadcasted_iota(jnp.int32, (LANES, 2 * LANES), 0)
    kc = lax.broadcasted_iota(jnp.int32, (LANES, 2 * LANES), 1)
    tail = jnp.where(jnp.logical_or(kr > kc, kc >= LANES), 1.0, 0.0).astype(BF16)
    sign = jnp.uint32(0x80000000)

    cols = [slice(p * LANES, (p + 1) * LANES) for p in range(pairs)]
    q2s = []
    for p in range(pairs):
        qp = q_ref[:, cols[p]]
        q2 = jnp.concatenate([qp, qp], axis=0)
        q2s.append(jnp.where(own, q2, jnp.zeros_like(q2)))

    def scores(j):
        start = pl.multiple_of(j * CHUNK, CHUNK)
        return [lax.dot_general(q2s[p], k_ref[pl.ds(start, CHUNK), cols[p]],
                                (((1,), (1,)), ((), ())), preferred_element_type=F32)
                for p in range(pairs)]

    def key_block(j, diagonal):
        start = pl.multiple_of(j * CHUNK, CHUNK)
        zs = scores(j) if diagonal else [z_ref[p] for p in range(pairs)]
        for p, z in enumerate(scores(jnp.maximum(j - 1, 0))):
            z_ref[p] = z
        carries = [0.0 if diagonal else carry_ref[p] for p in range(pairs)]
        log_betas, splits = [], []
        for z in zs:
            neg_abs = lax.bitcast_convert_type(lax.bitcast_convert_type(z, jnp.uint32) | sign, F32)
            log1p = jnp.log(1.0 + jnp.exp2(neg_abs)) * LOG2E
            log_beta = jnp.minimum(z, 0.0) - log1p
            log_keep = log_beta - z
            if diagonal:
                log_keep = jnp.where(causal, log_keep, 0.0)
            log_betas.append(log_beta)
            splits.append(log_keep.astype(BF16))
        sums = [jnp.dot(s, tail, preferred_element_type=F32) for s in splits]
        weights = []
        for p in range(pairs):
            a = jnp.exp2(log_betas[p] + sums[p][:, :LANES] + carries[p])
            if diagonal:
                a = jnp.where(causal, a, 0.0)
            weights.append(a.astype(BF16))
        outs = [jnp.dot(weights[p], v_ref[pl.ds(start, CHUNK), cols[p]],
                        preferred_element_type=F32) for p in range(pairs)]
        for p in range(pairs):
            carry_ref[p] = carries[p] + sums[p][:, LANES:]
        for p in range(pairs):
            o = jnp.where(first, outs[p][:CHUNK], outs[p][CHUNK:])
            acc_ref[:, cols[p]] = o if diagonal else acc_ref[:, cols[p]] + o

    def alive():
        return jnp.max(carry_ref[...]) > DEAD_LOG2

    key_block(qb, True)

    def cond(state):
        j, live = state
        return jnp.logical_and(j >= 0, live)

    def body(state):
        j, _ = state
        key_block(j, False)
        return j - 1, alive()

    lax.while_loop(cond, body, (qb - 1, alive()))
    o_ref[...] = acc_ref[...].astype(BF16)


def _attention(q, k, v, batch, seq):
    q3, k3, v3 = (t.reshape(batch, seq, WIDTH) for t in (q, k, v))
    blk = pl.BlockSpec((None, CHUNK, WIDTH), lambda b, qb: (b, qb, 0))
    full = pl.BlockSpec((None, seq, WIDTH), lambda b, qb: (b, 0, 0), pipeline_mode=pl.Buffered(1))
    o = pl.pallas_call(
        _attn_kernel,
        grid=(batch, seq // CHUNK),
        in_specs=[blk, full, full],
        out_specs=blk,
        out_shape=jax.ShapeDtypeStruct((batch, seq, WIDTH), BF16),
        scratch_shapes=[pltpu.VMEM((WIDTH // LANES, 2 * CHUNK, LANES), F32),
                        pltpu.VMEM((CHUNK, WIDTH), F32),
                        pltpu.VMEM((WIDTH // LANES, 2 * CHUNK, LANES), F32)],
        compiler_params=pltpu.CompilerParams(
            dimension_semantics=("arbitrary", "arbitrary"), vmem_limit_bytes=VMEM_LIMIT),
        name="attn",
    )(q3, k3, v3)
    return o.reshape(batch * seq, WIDTH)


def _mix_kernel(x_ref, oa_ref, u_ref, vs_ref, ga_ref, gb_ref,
                wsp_ref, bsp_ref, wua_ref, wub_ref, wo_ref, gf_ref, wr_ref, br_ref,
                xo_ref, h_ref, route_ref, cnt_ref, run_ref):
    step = pl.program_id(0)

    @pl.when(step == 0)
    def _():
        run_ref[...] = jnp.zeros_like(run_ref)

    lane = lax.broadcasted_iota(jnp.int32, (CHUNK, LANES), 1)
    row = lax.broadcasted_iota(jnp.int32, (CHUNK, LANES), 0)
    first = lane < HEAD_DIM
    tril = lane <= row

    ob_chunks = []
    for c in range(TM_MIX // CHUNK):
        rows = slice(c * CHUNK, (c + 1) * CHUNK)
        cols = []
        for gp in range(WIDTH // LANES):
            vpair = vs_ref[rows, gp * LANES:(gp + 1) * LANES]
            mixed = []
            for g in (2 * gp, 2 * gp + 1):
                wg = jnp.where(tril, wsp_ref[g], 0.0).astype(BF16)
                mixed.append(jnp.dot(wg, vpair, preferred_element_type=F32))
            cols.append(jnp.where(first, mixed[0], mixed[1]))
        mixed = jnp.concatenate(cols, axis=-1) + bsp_ref[...]
        ob_chunks.append((u_ref[rows, :].astype(F32) * mixed).astype(BF16))
    ob = jnp.concatenate(ob_chunks, axis=0)

    up_a = jnp.dot(oa_ref[...], wua_ref[...], preferred_element_type=F32)
    up_b = jnp.dot(ob, wub_ref[...], preferred_element_type=F32)
    merged = ga_ref[...].astype(F32) * up_a + gb_ref[...].astype(F32) * up_b
    x = x_ref[...] + jnp.dot(merged.astype(BF16), wo_ref[...], preferred_element_type=F32)
    xo_ref[...] = x

    h = _rms(x, gf_ref[...])
    _store_row_tiles(h_ref, h)
    logits = jnp.dot(h.astype(BF16), wr_ref[...], preferred_element_type=F32) + br_ref[...]
    lane_t = lax.broadcasted_iota(jnp.int32, (TM_MIX, LANES), 1)
    lanef = lane_t.astype(F32)
    neg = -jnp.inf
    far = float(LANES)

    def first_max(vals):
        m = jnp.max(vals, axis=-1, keepdims=True)
        idx = jnp.min(jnp.where(vals == m, lanef, far), axis=-1, keepdims=True)
        return m, idx

    gl = jnp.where(lane_t < N_GROUPS, logits, neg)
    gmax, grp = first_max(gl)
    grp_w = 1.0 / jnp.sum(jnp.exp(gl - gmax), axis=-1, keepdims=True)
    lo_lane = ROUTE_LANE0 + EXPERTS_PER_GROUP * grp
    in_group = jnp.logical_and(lanef >= lo_lane, lanef < lo_lane + EXPERTS_PER_GROUP)
    el = jnp.where(in_group, logits, neg)
    m1, i1 = first_max(el)
    m2, i2 = first_max(jnp.where(lanef == i1, neg, el))
    e21 = jnp.exp(m2 - m1)
    w1 = grp_w / (1.0 + e21)
    w2 = w1 * e21

    onehot = jnp.logical_or(lanef == i1, lanef == i2)
    rt = lax.broadcasted_iota(jnp.int32, (TM_MIX, TM_MIX), 0)
    ct = lax.broadcasted_iota(jnp.int32, (TM_MIX, TM_MIX), 1)
    before = jnp.where(ct < rt, 1.0, 0.0).astype(BF16)
    prior = jnp.dot(before, jnp.where(onehot, 1.0, 0.0).astype(BF16),
                    preferred_element_type=F32) + run_ref[...]
    r1 = jnp.sum(jnp.where(lanef == i1, prior, 0.0), axis=-1, keepdims=True)
    r2 = jnp.sum(jnp.where(lanef == i2, prior, 0.0), axis=-1, keepdims=True)
    run = run_ref[...] + jnp.sum(jnp.where(onehot, 1.0, 0.0), axis=0, keepdims=True)
    run_ref[...] = run
    cnt_ref[...] = run

    fields = (i1 - ROUTE_LANE0, i2 - ROUTE_LANE0, w1, w2, r1, r2)
    route = jnp.zeros((TM_MIX, LANES), F32)
    for pos, val in enumerate(fields):
        route = jnp.where(lane_t == pos, val, route)
    route_ref[...] = route


def _mix(layer, x, oa, u, vs, ga, gb, wsp, bsp, wua, wub, wo, gf, wr, br):
    n = x.shape[0]
    row = lambda width: pl.BlockSpec((TM_MIX, width), lambda i: (i, 0))
    lay = lambda *shape: pl.BlockSpec((None,) + shape, lambda i: (layer,) + (0,) * len(shape))
    return pl.pallas_call(
        _mix_kernel,
        grid=(n // TM_MIX,),
        in_specs=[row(D_MODEL), row(WIDTH), row(WIDTH), row(WIDTH), row(D_MODEL), row(D_MODEL),
                  lay(HEADS, CHUNK, CHUNK), lay(CHUNK, WIDTH), lay(WIDTH, D_MODEL),
                  lay(WIDTH, D_MODEL), lay(D_MODEL, D_MODEL), lay(1, D_MODEL),
                  lay(D_MODEL, LANES), lay(1, LANES)],
        out_specs=[row(D_MODEL), pl.BlockSpec((TM_MIX * ROW_TILE, LANES), lambda i: (i, 0)),
                   row(LANES), pl.BlockSpec((1, LANES), lambda i: (0, 0))],
        out_shape=[jax.ShapeDtypeStruct((n, D_MODEL), F32),
                   jax.ShapeDtypeStruct((n * ROW_TILE, LANES), F32),
                   jax.ShapeDtypeStruct((n, LANES), F32),
                   jax.ShapeDtypeStruct((1, LANES), F32)],
        scratch_shapes=[pltpu.VMEM((1, LANES), F32)],
        compiler_params=pltpu.CompilerParams(
            dimension_semantics=("arbitrary",), vmem_limit_bytes=VMEM_LIMIT),
        name="mix",
    )(x, oa, u, vs, ga, gb, wsp, bsp, wua, wub, wo, gf, wr, br)


def _dispatch_kernel(dest_ref, zrow_ref, h_ref, xs_ref, zero_ref, sem, zsem, tsem):
    i = pl.program_id(0)
    block_rows = MOE_ROWS * ROW_TILE
    nblk = xs_ref.shape[0] // block_rows
    nused = zrow_ref[N_EXPERTS]

    def zero_copy(first_row, zero_sem):
        return pltpu.make_async_copy(zero_ref, xs_ref.at[pl.ds(first_row, block_rows), :],
                                     zero_sem)

    def tail_copy(j):
        return zero_copy((nused + j) * block_rows, tsem)

    @pl.when(i == 0)
    def _():
        zero_ref[...] = jnp.zeros_like(zero_ref)
        for j in range(N_EXPERTS):
            pl.when(nused + j < nblk)(lambda j=j: tail_copy(j).start())
        for e in range(N_EXPERTS):
            pl.when(zrow_ref[e] >= 0)(lambda e=e: zero_copy(zrow_ref[e], zsem).start())
        for e in range(N_EXPERTS):
            pl.when(zrow_ref[e] >= 0)(lambda e=e: zero_copy(zrow_ref[e], zsem).wait())

    base = i * (2 * TM_DISP)
    for t in range(TM_DISP):
        for k in range(2):
            d = dest_ref[base + 2 * t + k]
            pltpu.make_async_copy(h_ref.at[pl.ds(t * ROW_TILE, ROW_TILE), :],
                                  xs_ref.at[pl.ds(d * ROW_TILE, ROW_TILE), :],
                                  sem).start(priority=k)
    for k in range(2):
        pltpu.make_async_copy(h_ref, xs_ref.at[pl.ds(0, TM_DISP * ROW_TILE), :], sem).wait()

    @pl.when(i == pl.num_programs(0) - 1)
    def _():
        for j in range(N_EXPERTS):
            pl.when(nused + j < nblk)(lambda j=j: tail_copy(j).wait())


def _dispatch(dest, zrow, h, cap):
    n = h.shape[0] // ROW_TILE
    return pl.pallas_call(
        _dispatch_kernel,
        grid_spec=pltpu.PrefetchScalarGridSpec(
            num_scalar_prefetch=2,
            grid=(n // TM_DISP,),
            in_specs=[pl.BlockSpec((TM_DISP * ROW_TILE, LANES), lambda i, dest, zrow: (i, 0))],
            out_specs=pl.BlockSpec(memory_space=pl.ANY),
            scratch_shapes=[pltpu.VMEM((MOE_ROWS * ROW_TILE, LANES), F32),
                            pltpu.SemaphoreType.DMA(()), pltpu.SemaphoreType.DMA(()),
                            pltpu.SemaphoreType.DMA(())],
        ),
        out_shape=jax.ShapeDtypeStruct((cap * ROW_TILE, LANES), F32),
        compiler_params=pltpu.CompilerParams(
            dimension_semantics=("arbitrary",), vmem_limit_bytes=VMEM_LIMIT),
        name="dispatch",
    )(dest, zrow, h)


def _expert_kernel(layer, blk_e_ref, nused_ref, half_ref, next_ref, xs_ref, wi_hbm, wo_hbm,
                   ys_ref, wi_f, wo_f, wi_b, wo_b, sem):
    b = pl.program_id(0)
    used = b < nused_ref[0]
    expert = blk_e_ref[b]

    def fetch(e, half):
        return (pltpu.make_async_copy(wi_hbm.at[layer, e], wi_f.at[half], sem.at[half, 0]),
                pltpu.make_async_copy(wo_hbm.at[layer, e], wo_f.at[half], sem.at[half, 1]))

    @pl.when(b == 0)
    def _():
        for copy in fetch(expert, half_ref[expert]):
            copy.start()

    @pl.when(jnp.logical_not(used))
    def _():
        ys_ref[...] = jnp.zeros_like(ys_ref)

    new_expert = jnp.logical_or(b == 0, expert != blk_e_ref[jnp.maximum(b - 1, 0)])

    @pl.when(jnp.logical_and(used, new_expert))
    def _():
        half = half_ref[expert]
        for copy in fetch(expert, half):
            copy.wait()
        following = next_ref[expert]

        @pl.when(following >= 0)
        def _():
            for copy in fetch(following, 1 - half):
                copy.start()

        wi_b[...] = wi_f[half].astype(BF16)
        wo_b[...] = wo_f[half].astype(BF16)

    @pl.when(used)
    def _():
        xs = _load_row_tiles(xs_ref, MOE_ROWS).astype(BF16)
        gu = jnp.dot(xs, wi_b[...], preferred_element_type=F32)
        act = jax.nn.silu(gu[:, :EXPERT_FF]) * gu[:, EXPERT_FF:]
        y = jnp.dot(act.astype(BF16), wo_b[...], preferred_element_type=F32)
        _store_row_tiles(ys_ref, y)


def _experts(layer, blk_e, nused, half, following, xs, w_e_in, w_e_out):
    cap = xs.shape[0] // ROW_TILE
    rows = MOE_ROWS * ROW_TILE
    return pl.pallas_call(
        functools.partial(_expert_kernel, layer),
        grid_spec=pltpu.PrefetchScalarGridSpec(
            num_scalar_prefetch=4,
            grid=(cap // MOE_ROWS,),
            in_specs=[pl.BlockSpec((rows, LANES),
                                   lambda b, blk_e, nused, *_: (jnp.minimum(b, nused[0] - 1), 0)),
                      pl.BlockSpec(memory_space=pl.ANY),
                      pl.BlockSpec(memory_space=pl.ANY)],
            out_specs=pl.BlockSpec((rows, LANES), lambda b, *_: (b, 0)),
            scratch_shapes=[pltpu.VMEM((2, D_MODEL, 2 * EXPERT_FF), F32),
                            pltpu.VMEM((2, EXPERT_FF, D_MODEL), F32),
                            pltpu.VMEM((D_MODEL, 2 * EXPERT_FF), BF16),
                            pltpu.VMEM((EXPERT_FF, D_MODEL), BF16),
                            pltpu.SemaphoreType.DMA((2, 2))],
        ),
        out_shape=jax.ShapeDtypeStruct((cap * ROW_TILE, LANES), F32),
        compiler_params=pltpu.CompilerParams(
            dimension_semantics=("arbitrary",), vmem_limit_bytes=VMEM_LIMIT),
        name="experts",
    )(blk_e, nused, half, following, xs, w_e_in, w_e_out)


def _ple_kernel(dest_ref, x_ref, route_ref, p_ref, ys_ref, gp_ref, wg_ref, wp_ref,
                xo_ref, ybuf, sem):
    j = pl.program_id(0)
    last_step = pl.num_programs(0) - 1

    def start_gather(tile, half):
        base = tile * (2 * TM_ROW)
        for t in range(TM_ROW):
            for k in range(2):
                d = dest_ref[base + 2 * t + k]
                pltpu.make_async_copy(ys_ref.at[pl.ds(d * ROW_TILE, ROW_TILE), :],
                                      ybuf.at[half, k, pl.ds(t * ROW_TILE, ROW_TILE), :],
                                      sem.at[half]).start(priority=k)

    def wait_gather(half):
        for k in range(2):
            pltpu.make_async_copy(ys_ref.at[pl.ds(0, TM_ROW * ROW_TILE), :],
                                  ybuf.at[half, k], sem.at[half]).wait()

    def combine(half):
        rows = slice(half * TM_ROW, (half + 1) * TM_ROW)
        pe = jnp.dot(p_ref[rows, :].astype(BF16), wp_ref[...], preferred_element_type=F32)
        wait_gather(half)
        route = route_ref[rows, :]
        y0 = _load_row_tiles(ybuf, TM_ROW, (half, 0))
        y1 = _load_row_tiles(ybuf, TM_ROW, (half, 1))
        x = x_ref[rows, :] + route[:, 2:3] * y0 + route[:, 3:4] * y1
        gate = jax.nn.sigmoid(jnp.dot(_rms(x, gp_ref[...]).astype(BF16), wg_ref[...],
                                      preferred_element_type=F32))
        xo_ref[rows, :] = x + gate * pe

    @pl.when(j == 0)
    def _():
        start_gather(0, 0)

    start_gather(2 * j + 1, 1)
    combine(0)
    start_gather(jnp.minimum(2 * j + 2, 2 * last_step + 1), 0)
    combine(1)

    @pl.when(j == last_step)
    def _():
        wait_gather(0)


def _ple(layer, dest, x, route, p, ys, gp, wg, wp):
    n = x.shape[0]
    row = lambda width: pl.BlockSpec((2 * TM_ROW, width), lambda j, dest: (j, 0))
    lay = lambda *shape: pl.BlockSpec((None,) + shape,
                                      lambda j, dest: (layer,) + (0,) * len(shape))
    return pl.pallas_call(
        _ple_kernel,
        grid_spec=pltpu.PrefetchScalarGridSpec(
            num_scalar_prefetch=1,
            grid=(n // (2 * TM_ROW),),
            in_specs=[row(D_MODEL), row(LANES),
                      pl.BlockSpec((None, 2 * TM_ROW, PLE_DIM), lambda j, dest: (layer, j, 0)),
                      pl.BlockSpec(memory_space=pl.ANY),
                      lay(1, D_MODEL), lay(D_MODEL, D_MODEL), lay(PLE_DIM, D_MODEL)],
            out_specs=row(D_MODEL),
            scratch_shapes=[pltpu.VMEM((2, 2, TM_ROW * ROW_TILE, LANES), F32),
                            pltpu.SemaphoreType.DMA((2,))],
        ),
        out_shape=jax.ShapeDtypeStruct((n, D_MODEL), F32),
        compiler_params=pltpu.CompilerParams(
            dimension_semantics=("arbitrary",), vmem_limit_bytes=VMEM_LIMIT),
        name="ple",
    )(dest, x, route, p, ys, gp, wg, wp)


def _slot_layout(route, counts, cap):
    eid = route[:, 0:2].astype(jnp.int32)
    rank = route[:, 4:6].astype(jnp.int32)
    cnt = counts[0, ROUTE_LANE0:ROUTE_LANE0 + N_EXPERTS].astype(jnp.int32)
    padded = (cnt + MOE_ROWS - 1) // MOE_ROWS * MOE_ROWS
    pend = jnp.cumsum(padded)
    poff = pend - padded
    onehot = eid[:, :, None] == jnp.arange(N_EXPERTS, dtype=jnp.int32)
    dest = jnp.sum(jnp.where(onehot, poff, 0), axis=-1) + rank
    blk_start = jnp.arange(cap // MOE_ROWS, dtype=jnp.int32) * MOE_ROWS
    blk_e = jnp.minimum(jnp.sum(pend[None, :] <= blk_start[:, None], axis=1), N_EXPERTS - 1)
    nused = (pend[-1:] // MOE_ROWS).astype(jnp.int32)
    dest = dest.reshape(-1).astype(jnp.int32)
    zrow = jnp.where(cnt > 0, (pend - MOE_ROWS) * ROW_TILE, -1).astype(jnp.int32)
    ids = jnp.arange(N_EXPERTS, dtype=jnp.int32)
    nonempty = cnt > 0
    half = ((jnp.cumsum(nonempty) - 1) % 2).astype(jnp.int32)
    later = jnp.logical_and(nonempty[None, :], ids[None, :] > ids[:, None])
    following = jnp.min(jnp.where(later, ids[None, :], N_EXPERTS), axis=1)
    following = jnp.where(following < N_EXPERTS, following, -1).astype(jnp.int32)
    return (dest, jnp.concatenate([zrow, nused]), blk_e.astype(jnp.int32), nused, half,
            following)


def kernel(x, p, norm_mix, w_in, q_norm, k_norm, sgu_norm, w_spatial, b_spatial, w_up_a, w_up_b,
           w_out, norm_ffn, w_group_router, b_group_router, w_expert_router, b_expert_router,
           w_expert_in, w_expert_out, norm_ple, w_ple_gate, w_ple_proj):
    batch, seq, d = x.shape
    depth = w_in.shape[0]
    n = batch * seq
    cap = 2 * n + N_EXPERTS * MOE_ROWS

    vec = lambda a: a[:, None, :]
    w_in_b, w_ua_b, w_ub_b, w_o_b = (a.astype(BF16) for a in (w_in, w_up_a, w_up_b, w_out))
    w_pg_b, w_pp_b = w_ple_gate.astype(BF16), w_ple_proj.astype(BF16)
    qn2, kn2 = vec(jnp.tile(q_norm, (1, 2))), vec(jnp.tile(k_norm, (1, 2)))
    b_sp = jnp.repeat(jnp.swapaxes(b_spatial, 1, 2), HEAD_DIM, axis=2)
    pad = LANES - N_GROUPS - N_EXPERTS
    w_r = jnp.pad(jnp.concatenate([w_group_router, w_expert_router], axis=2),
                  ((0, 0), (0, 0), (0, pad))).astype(BF16)
    b_r = vec(jnp.pad(jnp.concatenate([b_group_router, b_expert_router], axis=1),
                      ((0, 0), (0, pad))))
    p2 = p.reshape(depth, n, PLE_DIM)

    xf = x.reshape(n, d)
    for i in range(depth):
        q, k, v, u, vs, ga, gb = _inproj(i, xf, vec(norm_mix), w_in_b, qn2, kn2, vec(sgu_norm))
        oa = _attention(q, k, v, batch, seq)
        xf, h, route, counts = _mix(i, xf, oa, u, vs, ga, gb, w_spatial, b_sp, w_ua_b, w_ub_b,
                                    w_o_b, vec(norm_ffn), w_r, b_r)
        dest, zrow, blk_e, nused, half, following = _slot_layout(route, counts, cap)
        xs = _dispatch(dest, zrow, h, cap)
        ys = _experts(i, blk_e, nused, half, following, xs, w_expert_in, w_expert_out)
        xf = _ple(i, dest, xf, route, p2, ys, vec(norm_ple), w_pg_b, w_pp_b)
    return xf.reshape(batch, seq, d)
```

```python
import functools
import math

import jax
import jax.numpy as jnp
from jax import lax
from jax.experimental import pallas as pl
from jax.experimental.pallas import tpu as pltpu

F32 = jnp.float32
BF16 = jnp.bfloat16

D_MODEL = 1024
HEADS = 8
HEAD_DIM = 64
WIDTH = HEADS * HEAD_DIM
CHUNK = 128
IN_WIDTH = 5 * WIDTH + 2 * D_MODEL
N_GROUPS = 4
EXPERTS_PER_GROUP = 8
N_EXPERTS = N_GROUPS * EXPERTS_PER_GROUP
EXPERT_FF = 512
PLE_DIM = 256
EPS = 1e-6

LANES = 128
ROW_TILE = D_MODEL // LANES
ROUTE_LANE0 = N_GROUPS
TM_IN = 256
TM_MIX = 256
TM_DISP = 512
TM_ROW = 256
MOE_ROWS = 256
LOG2E = 1.4426950408889634
DEAD_LOG2 = -105.0 * LOG2E
VMEM_LIMIT = 48 * 1024 * 1024


def _rms(x, gain):
    return x * lax.rsqrt(jnp.mean(x * x, axis=-1, keepdims=True) + EPS) * gain


def _store_row_tiles(ref, value, lead=()):
    rows = value.shape[0]
    for c in range(ROW_TILE):
        ref[lead + (pl.ds(c, rows, stride=ROW_TILE), slice(None))] = (
            value[:, c * LANES:(c + 1) * LANES])


def _load_row_tiles(ref, rows, lead=()):
    return jnp.concatenate(
        [ref[lead + (pl.ds(c, rows, stride=ROW_TILE), slice(None))] for c in range(ROW_TILE)],
        axis=1)


def _inproj_kernel(x_ref, g_ref, w_ref, qn_ref, kn_ref, sn_ref,
                   q_ref, k_ref, v_ref, u_ref, vs_ref, ga_ref, gb_ref):
    hb = _rms(x_ref[...], g_ref[...]).astype(BF16)

    def proj(lo, hi):
        return jnp.dot(hb, w_ref[:, lo:hi], preferred_element_type=F32)

    first = lax.broadcasted_iota(jnp.int32, (1, LANES), 1) < HEAD_DIM

    def head_norm(z, gain, scale):
        outs = []
        for c in range(WIDTH // LANES):
            zc = z[:, c * LANES:(c + 1) * LANES]
            sq = zc * zc
            sa = jnp.sum(jnp.where(first, sq, 0.0), axis=-1, keepdims=True)
            sb = jnp.sum(jnp.where(first, 0.0, sq), axis=-1, keepdims=True)
            ms = jnp.where(first, sa, sb) * (1.0 / HEAD_DIM)
            outs.append(zc * lax.rsqrt(ms + EPS) * (gain * scale))
        return jnp.concatenate(outs, axis=-1)

    w = WIDTH
    q_ref[...] = head_norm(proj(0, w), qn_ref[...], LOG2E / math.sqrt(HEAD_DIM)).astype(BF16)
    k_ref[...] = head_norm(proj(w, 2 * w), kn_ref[...], 1.0).astype(BF16)
    v_ref[...] = proj(2 * w, 3 * w).astype(BF16)
    u_ref[...] = jax.nn.gelu(proj(3 * w, 4 * w)).astype(BF16)
    vs_ref[...] = _rms(jax.nn.gelu(proj(4 * w, 5 * w)), sn_ref[...]).astype(BF16)
    ga_ref[...] = jax.nn.sigmoid(proj(5 * w, 5 * w + D_MODEL)).astype(BF16)
    gb_ref[...] = jax.nn.sigmoid(proj(5 * w + D_MODEL, IN_WIDTH)).astype(BF16)


def _inproj(layer, x, norm_mix, w_in, qn, kn, sn):
    n = x.shape[0]
    row = lambda width: pl.BlockSpec((TM_IN, width), lambda i: (i, 0))
    vec = lambda width: pl.BlockSpec((None, 1, width), lambda i: (layer, 0, 0))
    out = lambda width: jax.ShapeDtypeStruct((n, width), BF16)
    return pl.pallas_call(
        _inproj_kernel,
        grid=(n // TM_IN,),
        in_specs=[row(D_MODEL), vec(D_MODEL),
                  pl.BlockSpec((None, D_MODEL, IN_WIDTH), lambda i: (layer, 0, 0)),
                  vec(LANES), vec(LANES), vec(WIDTH)],
        out_specs=[row(WIDTH)] * 5 + [row(D_MODEL)] * 2,
        out_shape=[out(WIDTH)] * 5 + [out(D_MODEL)] * 2,
        compiler_params=pltpu.CompilerParams(
            dimension_semantics=("arbitrary",), vmem_limit_bytes=VMEM_LIMIT),
        name="inproj",
    )(x, norm_mix, w_in, qn, kn, sn)


def _attn_kernel(q_ref, k_ref, v_ref, o_ref, carry_ref, acc_ref, z_ref):
    qb = pl.program_id(1)
    pairs = WIDTH // LANES
    lane = lax.broadcasted_iota(jnp.int32, (2 * CHUNK, LANES), 1)
    row = lax.broadcasted_iota(jnp.int32, (2 * CHUNK, LANES), 0)
    upper = row < CHUNK
    query = jnp.where(upper, row, row - CHUNK)
    own = jnp.logical_xor(lane < HEAD_DIM, jnp.logical_not(upper))
    causal = lane < query
    first = lax.broadcasted_iota(jnp.int32, (CHUNK, LANES), 1) < HEAD_DIM
    kr = lax.broadcasted_iota(jnp.int32, (LANES, 2 * LANES), 0)
    kc = lax.broadcasted_iota(jnp.int32, (LANES, 2 * LANES), 1)
    tail = jnp.where(jnp.logical_or(kr > kc, kc >= LANES), 1.0, 0.0).astype(BF16)
    sign = jnp.uint32(0x80000000)

    cols = [slice(p * LANES, (p + 1) * LANES) for p in range(pairs)]
    q2s = []
    for p in range(pairs):
        qp = q_ref[:, cols[p]]
        q2 = jnp.concatenate([qp, qp], axis=0)
        q2s.append(jnp.where(own, q2, jnp.zeros_like(q2)))

    def scores(j):
        start = pl.multiple_of(j * CHUNK, CHUNK)
        return [lax.dot_general(q2s[p], k_ref[pl.ds(start, CHUNK), cols[p]],
                                (((1,), (1,)), ((), ())), preferred_element_type=F32)
                for p in range(pairs)]

    def key_block(j, diagonal):
        start = pl.multiple_of(j * CHUNK, CHUNK)
        zs = scores(j) if diagonal else [z_ref[p] for p in range(pairs)]
        for p, z in enumerate(scores(jnp.maximum(j - 1, 0))):
            z_ref[p] = z
        carries = [0.0 if diagonal else carry_ref[p] for p in range(pairs)]
        log_betas, splits = [], []
        for z in zs:
            neg_abs = lax.bitcast_convert_type(lax.bitcast_convert_type(z, jnp.uint32) | sign, F32)
            log1p = jnp.log(1.0 + jnp.exp2(neg_abs)) * LOG2E
            log_beta = jnp.minimum(z, 0.0) - log1p
            log_keep = log_beta - z
            if diagonal:
                log_keep = jnp.where(causal, log_keep, 0.0)
            log_betas.append(log_beta)
            splits.append(log_keep.astype(BF16))
        sums = [jnp.dot(s, tail, preferred_element_type=F32) for s in splits]
        weights = []
        for p in range(pairs):
            a = jnp.exp2(log_betas[p] + sums[p][:, :LANES] + carries[p])
            if diagonal:
                a = jnp.where(causal, a, 0.0)
            weights.append(a.astype(BF16))
        outs = [jnp.dot(weights[p], v_ref[pl.ds(start, CHUNK), cols[p]],
                        preferred_element_type=F32) for p in range(pairs)]
        for p in range(pairs):
            carry_ref[p] = carries[p] + sums[p][:, LANES:]
        for p in range(pairs):
            o = jnp.where(first, outs[p][:CHUNK], outs[p][CHUNK:])
            acc_ref[:, cols[p]] = o if diagonal else acc_ref[:, cols[p]] + o

    def alive():
        return jnp.max(carry_ref[...]) > DEAD_LOG2

    key_block(qb, True)

    def cond(state):
        j, live = state
        return jnp.logical_and(j >= 0, live)

    def body(state):
        j, _ = state
        key_block(j, False)
        return j - 1, alive()

    lax.while_loop(cond, body, (qb - 1, alive()))
    o_ref[...] = acc_ref[...].astype(BF16)


def _attention(q, k, v, batch, seq):
    q3, k3, v3 = (t.reshape(batch, seq, WIDTH) for t in (q, k, v))
    blk = pl.BlockSpec((None, CHUNK, WIDTH), lambda b, qb: (b, qb, 0))
    full = pl.BlockSpec((None, seq, WIDTH), lambda b, qb: (b, 0, 0), pipeline_mode=pl.Buffered(1))
    o = pl.pallas_call(
        _attn_kernel,
        grid=(batch, seq // CHUNK),
        in_specs=[blk, full, full],
        out_specs=blk,
        out_shape=jax.ShapeDtypeStruct((batch, seq, WIDTH), BF16),
        scratch_shapes=[pltpu.VMEM((WIDTH // LANES, 2 * CHUNK, LANES), F32),
                        pltpu.VMEM((CHUNK, WIDTH), F32),
                        pltpu.VMEM((WIDTH // LANES, 2 * CHUNK, LANES), F32)],
        compiler_params=pltpu.CompilerParams(
            dimension_semantics=("arbitrary", "arbitrary"), vmem_limit_bytes=VMEM_LIMIT),
        name="attn",
    )(q3, k3, v3)
    return o.reshape(batch * seq, WIDTH)


def _choose_tile(hb, wr_ref, br_ref):
    logits = jnp.dot(hb, wr_ref[...], preferred_element_type=F32) + br_ref[...]
    lane_t = lax.broadcasted_iota(jnp.int32, (TM_MIX, LANES), 1)
    lanef = lane_t.astype(F32)
    neg = -jnp.inf
    far = float(LANES)

    def first_max(vals):
        m = jnp.max(vals, axis=-1, keepdims=True)
        idx = jnp.min(jnp.where(vals == m, lanef, far), axis=-1, keepdims=True)
        return m, idx

    gl = jnp.where(lane_t < N_GROUPS, logits, neg)
    gmax, grp = first_max(gl)
    grp_w = 1.0 / jnp.sum(jnp.exp(gl - gmax), axis=-1, keepdims=True)
    lo_lane = ROUTE_LANE0 + EXPERTS_PER_GROUP * grp
    in_group = jnp.logical_and(lanef >= lo_lane, lanef < lo_lane + EXPERTS_PER_GROUP)
    el = jnp.where(in_group, logits, neg)
    m1, i1 = first_max(el)
    m2, i2 = first_max(jnp.where(lanef == i1, neg, el))
    e21 = jnp.exp(m2 - m1)
    w1 = grp_w / (1.0 + e21)
    w2 = w1 * e21
    return i1, i2, w1, w2


def _rank_tile(choice, run_ref, valid):
    i1, i2, w1, w2 = choice
    lane_t = lax.broadcasted_iota(jnp.int32, (TM_MIX, LANES), 1)
    lanef = lane_t.astype(F32)
    onehot = jnp.logical_and(jnp.logical_or(lanef == i1, lanef == i2), valid)
    rt = lax.broadcasted_iota(jnp.int32, (TM_MIX, TM_MIX), 0)
    ct = lax.broadcasted_iota(jnp.int32, (TM_MIX, TM_MIX), 1)
    before = jnp.where(ct < rt, 1.0, 0.0).astype(BF16)
    prior = jnp.dot(before, jnp.where(onehot, 1.0, 0.0).astype(BF16),
                    preferred_element_type=F32) + run_ref[...]
    r1 = jnp.sum(jnp.where(lanef == i1, prior, 0.0), axis=-1, keepdims=True)
    r2 = jnp.sum(jnp.where(lanef == i2, prior, 0.0), axis=-1, keepdims=True)
    run_ref[...] += jnp.sum(jnp.where(onehot, 1.0, 0.0), axis=0, keepdims=True)

    fields = (i1 - ROUTE_LANE0, i2 - ROUTE_LANE0, w1, w2, r1, r2)
    route = jnp.zeros((TM_MIX, LANES), F32)
    for pos, val in enumerate(fields):
        route = jnp.where(lane_t == pos, val, route)
    return route


def _mix_kernel(x_ref, oa_ref, u_ref, vs_ref, ga_ref, gb_ref,
                wsp_ref, bsp_ref, wua_ref, wub_ref, wo_ref, gf_ref, wr_ref, br_ref,
                xo_ref, h_ref, route_ref, cnt_ref, run_ref, hprev_ref):
    step = pl.program_id(0)

    @pl.when(step == 0)
    def _():
        run_ref[...] = jnp.zeros_like(run_ref)
        hprev_ref[...] = jnp.zeros_like(hprev_ref)

    choice = _choose_tile(hprev_ref[...], wr_ref, br_ref)

    lane = lax.broadcasted_iota(jnp.int32, (CHUNK, LANES), 1)
    row = lax.broadcasted_iota(jnp.int32, (CHUNK, LANES), 0)
    first = lane < HEAD_DIM
    tril = lane <= row

    ob_chunks = []
    for c in range(TM_MIX // CHUNK):
        rows = slice(c * CHUNK, (c + 1) * CHUNK)
        cols = []
        for gp in range(WIDTH // LANES):
            vpair = vs_ref[rows, gp * LANES:(gp + 1) * LANES]
            mixed = []
            for g in (2 * gp, 2 * gp + 1):
                wg = jnp.where(tril, wsp_ref[g], 0.0).astype(BF16)
                mixed.append(jnp.dot(wg, vpair, preferred_element_type=F32))
            cols.append(jnp.where(first, mixed[0], mixed[1]))
        mixed = jnp.concatenate(cols, axis=-1) + bsp_ref[...]
        ob_chunks.append((u_ref[rows, :].astype(F32) * mixed).astype(BF16))
    ob = jnp.concatenate(ob_chunks, axis=0)

    up_a = jnp.dot(oa_ref[...], wua_ref[...], preferred_element_type=F32)
    up_b = jnp.dot(ob, wub_ref[...], preferred_element_type=F32)
    merged = ga_ref[...].astype(F32) * up_a + gb_ref[...].astype(F32) * up_b
    x = x_ref[...] + jnp.dot(merged.astype(BF16), wo_ref[...], preferred_element_type=F32)
    xo_ref[...] = x

    h = _rms(x, gf_ref[...])
    _store_row_tiles(h_ref, h)
    hprev_ref[...] = h.astype(BF16)

    route_ref[...] = _rank_tile(choice, run_ref, step > 0)
    cnt_ref[...] = run_ref[...]


def _mix(layer, x, oa, u, vs, ga, gb, wsp, bsp, wua, wub, wo, gf, wr, br):
    n = x.shape[0]
    tiles = n // TM_MIX
    cur = lambda i: jnp.minimum(i, tiles - 1)
    row = lambda width: pl.BlockSpec((TM_MIX, width), lambda i: (cur(i), 0))
    lay = lambda *shape: pl.BlockSpec((None,) + shape, lambda i: (layer,) + (0,) * len(shape))
    return pl.pallas_call(
        _mix_kernel,
        grid=(tiles + 1,),
        in_specs=[row(D_MODEL), row(WIDTH), row(WIDTH), row(WIDTH), row(D_MODEL), row(D_MODEL),
                  lay(HEADS, CHUNK, CHUNK), lay(CHUNK, WIDTH), lay(WIDTH, D_MODEL),
                  lay(WIDTH, D_MODEL), lay(D_MODEL, D_MODEL), lay(1, D_MODEL),
                  lay(D_MODEL, LANES), lay(1, LANES)],
        out_specs=[row(D_MODEL),
                   pl.BlockSpec((TM_MIX * ROW_TILE, LANES), lambda i: (cur(i), 0)),
                   pl.BlockSpec((TM_MIX, LANES), lambda i: (jnp.maximum(i - 1, 0), 0)),
                   pl.BlockSpec((1, LANES), lambda i: (0, 0))],
        out_shape=[jax.ShapeDtypeStruct((n, D_MODEL), F32),
                   jax.ShapeDtypeStruct((n * ROW_TILE, LANES), F32),
                   jax.ShapeDtypeStruct((n, LANES), F32),
                   jax.ShapeDtypeStruct((1, LANES), F32)],
        scratch_shapes=[pltpu.VMEM((1, LANES), F32), pltpu.VMEM((TM_MIX, D_MODEL), BF16)],
        compiler_params=pltpu.CompilerParams(
            dimension_semantics=("arbitrary",), vmem_limit_bytes=VMEM_LIMIT),
        name="mix",
    )(x, oa, u, vs, ga, gb, wsp, bsp, wua, wub, wo, gf, wr, br)


def _dispatch_kernel(dest_ref, zrow_ref, h_ref, xs_ref, zero_ref, sem, zsem, tsem):
    i = pl.program_id(0)
    block_rows = MOE_ROWS * ROW_TILE
    nblk = xs_ref.shape[0] // block_rows
    nused = zrow_ref[N_EXPERTS]

    def zero_copy(first_row, zero_sem):
        return pltpu.make_async_copy(zero_ref, xs_ref.at[pl.ds(first_row, block_rows), :],
                                     zero_sem)

    def tail_copy(j):
        return zero_copy((nused + j) * block_rows, tsem)

    @pl.when(i == 0)
    def _():
        zero_ref[...] = jnp.zeros_like(zero_ref)
        for j in range(N_EXPERTS):
            pl.when(nused + j < nblk)(lambda j=j: tail_copy(j).start())
        for e in range(N_EXPERTS):
            pl.when(zrow_ref[e] >= 0)(lambda e=e: zero_copy(zrow_ref[e], zsem).start())
        for e in range(N_EXPERTS):
            pl.when(zrow_ref[e] >= 0)(lambda e=e: zero_copy(zrow_ref[e], zsem).wait())

    base = i * (2 * TM_DISP)
    for t in range(TM_DISP):
        for k in range(2):
            d = dest_ref[base + 2 * t + k]
            pltpu.make_async_copy(h_ref.at[pl.ds(t * ROW_TILE, ROW_TILE), :],
                                  xs_ref.at[pl.ds(d * ROW_TILE, ROW_TILE), :],
                                  sem).start(priority=k)
    for k in range(2):
        pltpu.make_async_copy(h_ref, xs_ref.at[pl.ds(0, TM_DISP * ROW_TILE), :], sem).wait()

    @pl.when(i == pl.num_programs(0) - 1)
    def _():
        for j in range(N_EXPERTS):
            pl.when(nused + j < nblk)(lambda j=j: tail_copy(j).wait())


def _dispatch(dest, zrow, h, cap):
    n = h.shape[0] // ROW_TILE
    return pl.pallas_call(
        _dispatch_kernel,
        grid_spec=pltpu.PrefetchScalarGridSpec(
            num_scalar_prefetch=2,
            grid=(n // TM_DISP,),
            in_specs=[pl.BlockSpec((TM_DISP * ROW_TILE, LANES), lambda i, dest, zrow: (i, 0))],
            out_specs=pl.BlockSpec(memory_space=pl.ANY),
            scratch_shapes=[pltpu.VMEM((MOE_ROWS * ROW_TILE, LANES), F32),
                            pltpu.SemaphoreType.DMA(()), pltpu.SemaphoreType.DMA(()),
                            pltpu.SemaphoreType.DMA(())],
        ),
        out_shape=jax.ShapeDtypeStruct((cap * ROW_TILE, LANES), F32),
        compiler_params=pltpu.CompilerParams(
            dimension_semantics=("arbitrary",), vmem_limit_bytes=VMEM_LIMIT),
        name="dispatch",
    )(dest, zrow, h)


def _expert_kernel(layer, blk_e_ref, nused_ref, half_ref, next_ref, xs_ref, wi_hbm, wo_hbm,
                   ys_ref, wi_f, wo_f, wi_b, wo_b, sem):
    b = pl.program_id(0)
    used = b < nused_ref[0]
    expert = blk_e_ref[b]

    def fetch(e, half):
        return (pltpu.make_async_copy(wi_hbm.at[layer, e], wi_f.at[half], sem.at[half, 0]),
                pltpu.make_async_copy(wo_hbm.at[layer, e], wo_f.at[half], sem.at[half, 1]))

    @pl.when(b == 0)
    def _():
        for copy in fetch(expert, half_ref[expert]):
            copy.start()

    @pl.when(jnp.logical_not(used))
    def _():
        ys_ref[...] = jnp.zeros_like(ys_ref)

    new_expert = jnp.logical_or(b == 0, expert != blk_e_ref[jnp.maximum(b - 1, 0)])

    @pl.when(jnp.logical_and(used, new_expert))
    def _():
        half = half_ref[expert]
        for copy in fetch(expert, half):
            copy.wait()
        following = next_ref[expert]

        @pl.when(following >= 0)
        def _():
            for copy in fetch(following, 1 - half):
                copy.start()

        wi_b[...] = wi_f[half].astype(BF16)
        wo_b[...] = wo_f[half].astype(BF16)

    @pl.when(used)
    def _():
        xs = _load_row_tiles(xs_ref, MOE_ROWS).astype(BF16)
        gu = jnp.dot(xs, wi_b[...], preferred_element_type=F32)
        act = jax.nn.silu(gu[:, :EXPERT_FF]) * gu[:, EXPERT_FF:]
        y = jnp.dot(act.astype(BF16), wo_b[...], preferred_element_type=F32)
        _store_row_tiles(ys_ref, y)


def _experts(layer, blk_e, nused, half, following, xs, w_e_in, w_e_out):
    cap = xs.shape[0] // ROW_TILE
    rows = MOE_ROWS * ROW_TILE
    return pl.pallas_call(
        functools.partial(_expert_kernel, layer),
        grid_spec=pltpu.PrefetchScalarGridSpec(
            num_scalar_prefetch=4,
            grid=(cap // MOE_ROWS,),
            in_specs=[pl.BlockSpec((rows, LANES),
                                   lambda b, blk_e, nused, *_: (jnp.minimum(b, nused[0] - 1), 0)),
                      pl.BlockSpec(memory_space=pl.ANY),
                      pl.BlockSpec(memory_space=pl.ANY)],
            out_specs=pl.BlockSpec((rows, LANES), lambda b, *_: (b, 0)),
            scratch_shapes=[pltpu.VMEM((2, D_MODEL, 2 * EXPERT_FF), F32),
                            pltpu.VMEM((2, EXPERT_FF, D_MODEL), F32),
                            pltpu.VMEM((D_MODEL, 2 * EXPERT_FF), BF16),
                            pltpu.VMEM((EXPERT_FF, D_MODEL), BF16),
                            pltpu.SemaphoreType.DMA((2, 2))],
        ),
        out_shape=jax.ShapeDtypeStruct((cap * ROW_TILE, LANES), F32),
        compiler_params=pltpu.CompilerParams(
            dimension_semantics=("arbitrary",), vmem_limit_bytes=VMEM_LIMIT),
        name="experts",
    )(blk_e, nused, half, following, xs, w_e_in, w_e_out)


def _ple_kernel(dest_ref, x_ref, route_ref, p_ref, ys_ref, gp_ref, wg_ref, wp_ref,
                xo_ref, y00, y01, y10, y11, sem):
    ybuf = ((y00, y01), (y10, y11))
    j = pl.program_id(0)
    last_step = pl.num_programs(0) - 1

    def start_gather(tile, half):
        base = tile * (2 * TM_ROW)
        for t in range(TM_ROW):
            for k in range(2):
                d = dest_ref[base + 2 * t + k]
                pltpu.make_async_copy(ys_ref.at[pl.ds(d * ROW_TILE, ROW_TILE), :],
                                      ybuf[half][k].at[pl.ds(t * ROW_TILE, ROW_TILE), :],
                                      sem.at[half]).start(priority=k)

    def wait_gather(half):
        for k in range(2):
            pltpu.make_async_copy(ys_ref.at[pl.ds(0, TM_ROW * ROW_TILE), :],
                                  ybuf[half][k], sem.at[half]).wait()

    def combine(half):
        rows = slice(half * TM_ROW, (half + 1) * TM_ROW)
        pe = jnp.dot(p_ref[rows, :].astype(BF16), wp_ref[...], preferred_element_type=F32)
        wait_gather(half)
        route = route_ref[rows, :]
        y0 = _load_row_tiles(ybuf[half][0], TM_ROW)
        y1 = _load_row_tiles(ybuf[half][1], TM_ROW)
        x = x_ref[rows, :] + route[:, 2:3] * y0 + route[:, 3:4] * y1
        gate = jax.nn.sigmoid(jnp.dot(_rms(x, gp_ref[...]).astype(BF16), wg_ref[...],
                                      preferred_element_type=F32))
        xo_ref[rows, :] = x + gate * pe

    @pl.when(j == 0)
    def _():
        start_gather(0, 0)

    start_gather(2 * j + 1, 1)
    combine(0)
    start_gather(jnp.minimum(2 * j + 2, 2 * last_step + 1), 0)
    combine(1)

    @pl.when(j == last_step)
    def _():
        wait_gather(0)


def _ple(layer, dest, x, route, p, ys, gp, wg, wp):
    n = x.shape[0]
    row = lambda width: pl.BlockSpec((2 * TM_ROW, width), lambda j, dest: (j, 0))
    lay = lambda *shape: pl.BlockSpec((None,) + shape,
                                      lambda j, dest: (layer,) + (0,) * len(shape))
    return pl.pallas_call(
        _ple_kernel,
        grid_spec=pltpu.PrefetchScalarGridSpec(
            num_scalar_prefetch=1,
            grid=(n // (2 * TM_ROW),),
            in_specs=[row(D_MODEL), row(LANES),
                      pl.BlockSpec((None, 2 * TM_ROW, PLE_DIM), lambda j, dest: (layer, j, 0)),
                      pl.BlockSpec(memory_space=pl.ANY),
                      lay(1, D_MODEL), lay(D_MODEL, D_MODEL), lay(PLE_DIM, D_MODEL)],
            out_specs=row(D_MODEL),
            scratch_shapes=[pltpu.VMEM((TM_ROW * ROW_TILE, LANES), F32)] * 4
            + [pltpu.SemaphoreType.DMA((2,))],
        ),
        out_shape=jax.ShapeDtypeStruct((n, D_MODEL), F32),
        compiler_params=pltpu.CompilerParams(
            dimension_semantics=("arbitrary",), vmem_limit_bytes=VMEM_LIMIT),
        name="ple",
    )(dest, x, route, p, ys, gp, wg, wp)


def _slot_layout(route, counts, cap):
    eid = route[:, 0:2].astype(jnp.int32)
    rank = route[:, 4:6].astype(jnp.int32)
    cnt = counts[0, ROUTE_LANE0:ROUTE_LANE0 + N_EXPERTS].astype(jnp.int32)
    padded = (cnt + MOE_ROWS - 1) // MOE_ROWS * MOE_ROWS
    pend = jnp.cumsum(padded)
    poff = pend - padded
    onehot = eid[:, :, None] == jnp.arange(N_EXPERTS, dtype=jnp.int32)
    dest = jnp.sum(jnp.where(onehot, poff, 0), axis=-1) + rank
    blk_start = jnp.arange(cap // MOE_ROWS, dtype=jnp.int32) * MOE_ROWS
    blk_e = jnp.minimum(jnp.sum(pend[None, :] <= blk_start[:, None], axis=1), N_EXPERTS - 1)
    nused = (pend[-1:] // MOE_ROWS).astype(jnp.int32)
    dest = dest.reshape(-1).astype(jnp.int32)
    zrow = jnp.where(cnt > 0, (pend - MOE_ROWS) * ROW_TILE, -1).astype(jnp.int32)
    ids = jnp.arange(N_EXPERTS, dtype=jnp.int32)
    nonempty = cnt > 0
    half = ((jnp.cumsum(nonempty) - 1) % 2).astype(jnp.int32)
    later = jnp.logical_and(nonempty[None, :], ids[None, :] > ids[:, None])
    following = jnp.min(jnp.where(later, ids[None, :], N_EXPERTS), axis=1)
    following = jnp.where(following < N_EXPERTS, following, -1).astype(jnp.int32)
    return (dest, jnp.concatenate([zrow, nused]), blk_e.astype(jnp.int32), nused, half,
            following)


def kernel(x, p, norm_mix, w_in, q_norm, k_norm, sgu_norm, w_spatial, b_spatial, w_up_a, w_up_b,
           w_out, norm_ffn, w_group_router, b_group_router, w_expert_router, b_expert_router,
           w_expert_in, w_expert_out, norm_ple, w_ple_gate, w_ple_proj):
    batch, seq, d = x.shape
    depth = w_in.shape[0]
    n = batch * seq
    cap = 2 * n + N_EXPERTS * MOE_ROWS

    vec = lambda a: a[:, None, :]
    w_in_b, w_ua_b, w_ub_b, w_o_b = (a.astype(BF16) for a in (w_in, w_up_a, w_up_b, w_out))
    w_pg_b, w_pp_b = w_ple_gate.astype(BF16), w_ple_proj.astype(BF16)
    qn2, kn2 = vec(jnp.tile(q_norm, (1, 2))), vec(jnp.tile(k_norm, (1, 2)))
    b_sp = jnp.repeat(jnp.swapaxes(b_spatial, 1, 2), HEAD_DIM, axis=2)
    pad = LANES - N_GROUPS - N_EXPERTS
    w_r = jnp.pad(jnp.concatenate([w_group_router, w_expert_router], axis=2),
                  ((0, 0), (0, 0), (0, pad))).astype(BF16)
    b_r = vec(jnp.pad(jnp.concatenate([b_group_router, b_expert_router], axis=1),
                      ((0, 0), (0, pad))))
    p2 = p.reshape(depth, n, PLE_DIM)

    xf = x.reshape(n, d)
    for i in range(depth):
        q, k, v, u, vs, ga, gb = _inproj(i, xf, vec(norm_mix), w_in_b, qn2, kn2, vec(sgu_norm))
        oa = _attention(q, k, v, batch, seq)
        xf, h, route, counts = _mix(i, xf, oa, u, vs, ga, gb, w_spatial, b_sp, w_ua_b, w_ub_b,
                                    w_o_b, vec(norm_ffn), w_r, b_r)
        dest, zrow, blk_e, nused, half, following = _slot_layout(route, counts, cap)
        xs = _dispatch(dest, zrow, h, cap)
        ys = _experts(i, blk_e, nused, half, following, xs, w_expert_in, w_expert_out)
        xf = _ple(i, dest, xf, route, p2, ys, vec(norm_ple), w_pg_b, w_pp_b)
    return xf.reshape(batch, seq, d)
```

```python
import functools
import math

import jax
import jax.numpy as jnp
from jax import lax
from jax.experimental import pallas as pl
from jax.experimental.pallas import tpu as pltpu

F32 = jnp.float32
BF16 = jnp.bfloat16

D_MODEL = 1024
HEADS = 8
HEAD_DIM = 64
WIDTH = HEADS * HEAD_DIM
CHUNK = 128
IN_WIDTH = 5 * WIDTH + 2 * D_MODEL
N_GROUPS = 4
EXPERTS_PER_GROUP = 8
N_EXPERTS = N_GROUPS * EXPERTS_PER_GROUP
EXPERT_FF = 512
PLE_DIM = 256
EPS = 1e-6

LANES = 128
ROW_TILE = D_MODEL // (2 * LANES)
U32 = jnp.uint32
ROUTE_LANE0 = N_GROUPS
TM_IN = 256
TM_MIX = 256
TM_DISP = 512
TM_ROW = 256
MOE_ROWS = 256
LOG2E = 1.4426950408889634
DEAD_LOG2 = -105.0 * LOG2E
VMEM_LIMIT = 48 * 1024 * 1024


def _rms(x, gain):
    return x * lax.rsqrt(jnp.mean(x * x, axis=-1, keepdims=True) + EPS) * gain


def _store_row_tiles(ref, value, lead=()):
    rows = value.shape[0]
    bits = lambda v: lax.bitcast_convert_type(v.astype(BF16).astype(F32), U32)
    for c in range(ROW_TILE):
        low = bits(value[:, 2 * c * LANES:(2 * c + 1) * LANES]) >> 16
        high = bits(value[:, (2 * c + 1) * LANES:(2 * c + 2) * LANES]) & U32(0xFFFF0000)
        ref[lead + (pl.ds(c, rows, stride=ROW_TILE), slice(None))] = low | high


def _load_row_tiles(ref, rows, lead=()):
    chunks = []
    for c in range(ROW_TILE):
        words = ref[lead + (pl.ds(c, rows, stride=ROW_TILE), slice(None))]
        chunks.append(lax.bitcast_convert_type(words << 16, F32))
        chunks.append(lax.bitcast_convert_type(words & U32(0xFFFF0000), F32))
    return jnp.concatenate(chunks, axis=1)


def _inproj_kernel(x_ref, g_ref, w_ref, qn_ref, kn_ref, sn_ref,
                   q_ref, k_ref, v_ref, u_ref, vs_ref, ga_ref, gb_ref):
    hb = _rms(x_ref[...], g_ref[...]).astype(BF16)

    def proj(lo, hi):
        return jnp.dot(hb, w_ref[:, lo:hi], preferred_element_type=F32)

    first = lax.broadcasted_iota(jnp.int32, (1, LANES), 1) < HEAD_DIM

    def head_norm(z, gain, scale):
        outs = []
        for c in range(WIDTH // LANES):
            zc = z[:, c * LANES:(c + 1) * LANES]
            sq = zc * zc
            sa = jnp.sum(jnp.where(first, sq, 0.0), axis=-1, keepdims=True)
            sb = jnp.sum(jnp.where(first, 0.0, sq), axis=-1, keepdims=True)
            ms = jnp.where(first, sa, sb) * (1.0 / HEAD_DIM)
            outs.append(zc * lax.rsqrt(ms + EPS) * (gain * scale))
        return jnp.concatenate(outs, axis=-1)

    w = WIDTH
    q_ref[...] = head_norm(proj(0, w), qn_ref[...], LOG2E / math.sqrt(HEAD_DIM)).astype(BF16)
    k_ref[...] = head_norm(proj(w, 2 * w), kn_ref[...], 1.0).astype(BF16)
    v_ref[...] = proj(2 * w, 3 * w).astype(BF16)
    u_ref[...] = jax.nn.gelu(proj(3 * w, 4 * w)).astype(BF16)
    vs_ref[...] = _rms(jax.nn.gelu(proj(4 * w, 5 * w)), sn_ref[...]).astype(BF16)
    ga_ref[...] = jax.nn.sigmoid(proj(5 * w, 5 * w + D_MODEL)).astype(BF16)
    gb_ref[...] = jax.nn.sigmoid(proj(5 * w + D_MODEL, IN_WIDTH)).astype(BF16)


def _inproj(layer, x, norm_mix, w_in, qn, kn, sn):
    n = x.shape[0]
    row = lambda width: pl.BlockSpec((TM_IN, width), lambda i: (i, 0))
    vec = lambda width: pl.BlockSpec((None, 1, width), lambda i: (layer, 0, 0))
    out = lambda width: jax.ShapeDtypeStruct((n, width), BF16)
    return pl.pallas_call(
        _inproj_kernel,
        grid=(n // TM_IN,),
        in_specs=[row(D_MODEL), vec(D_MODEL),
                  pl.BlockSpec((None, D_MODEL, IN_WIDTH), lambda i: (layer, 0, 0)),
                  vec(LANES), vec(LANES), vec(WIDTH)],
        out_specs=[row(WIDTH)] * 5 + [row(D_MODEL)] * 2,
        out_shape=[out(WIDTH)] * 5 + [out(D_MODEL)] * 2,
        compiler_params=pltpu.CompilerParams(
            dimension_semantics=("arbitrary",), vmem_limit_bytes=VMEM_LIMIT),
        name="inproj",
    )(x, norm_mix, w_in, qn, kn, sn)


def _attn_kernel(q_ref, k_ref, v_ref, o_ref, carry_ref, acc_ref, z_ref):
    qb = pl.program_id(1)
    pairs = WIDTH // LANES
    lane = lax.broadcasted_iota(jnp.int32, (2 * CHUNK, LANES), 1)
    row = lax.broadcasted_iota(jnp.int32, (2 * CHUNK, LANES), 0)
    upper = row < CHUNK
    query = jnp.where(upper, row, row - CHUNK)
    own = jnp.logical_xor(lane < HEAD_DIM, jnp.logical_not(upper))
    causal = lane < query
    first = lax.broadcasted_iota(jnp.int32, (CHUNK, LANES), 1) < HEAD_DIM
    kr = lax.broadcasted_iota(jnp.int32, (LANES, 2 * LANES), 0)
    kc = lax.broadcasted_iota(jnp.int32, (LANES, 2 * LANES), 1)
    tail = jnp.where(jnp.logical_or(kr > kc, kc >= LANES), 1.0, 0.0).astype(BF16)
    sign = jnp.uint32(0x80000000)

    cols = [slice(p * LANES, (p + 1) * LANES) for p in range(pairs)]
    q2s = []
    for p in range(pairs):
        qp = q_ref[:, cols[p]]
        q2 = jnp.concatenate([qp, qp], axis=0)
        q2s.append(jnp.where(own, q2, jnp.zeros_like(q2)))

    def scores(j):
        start = pl.multiple_of(j * CHUNK, CHUNK)
        return [lax.dot_general(q2s[p], k_ref[pl.ds(start, CHUNK), cols[p]],
                                (((1,), (1,)), ((), ())), preferred_element_type=F32)
                for p in range(pairs)]

    def key_block(j, diagonal):
        start = pl.multiple_of(j * CHUNK, CHUNK)
        zs = scores(j) if diagonal else [z_ref[p] for p in range(pairs)]
        for p, z in enumerate(scores(jnp.maximum(j - 1, 0))):
            z_ref[p] = z
        carries = [0.0 if diagonal else carry_ref[p] for p in range(pairs)]
        log_betas, splits = [], []
        for z in zs:
            neg_abs = lax.bitcast_convert_type(lax.bitcast_convert_type(z, jnp.uint32) | sign, F32)
            log1p = jnp.log(1.0 + jnp.exp2(neg_abs)) * LOG2E
            log_beta = jnp.minimum(z, 0.0) - log1p
            log_keep = log_beta - z
            if diagonal:
                log_keep = jnp.where(causal, log_keep, 0.0)
            log_betas.append(log_beta)
            splits.append(log_keep.astype(BF16))
        sums = [jnp.dot(s, tail, preferred_element_type=F32) for s in splits]
        weights = []
        for p in range(pairs):
            a = jnp.exp2(log_betas[p] + sums[p][:, :LANES] + carries[p])
            if diagonal:
                a = jnp.where(causal, a, 0.0)
            weights.append(a.astype(BF16))
        outs = [jnp.dot(weights[p], v_ref[pl.ds(start, CHUNK), cols[p]],
                        preferred_element_type=F32) for p in range(pairs)]
        for p in range(pairs):
            carry_ref[p] = carries[p] + sums[p][:, LANES:]
        for p in range(pairs):
            o = jnp.where(first, outs[p][:CHUNK], outs[p][CHUNK:])
            acc_ref[:, cols[p]] = o if diagonal else acc_ref[:, cols[p]] + o

    def alive():
        return jnp.max(carry_ref[...]) > DEAD_LOG2

    key_block(qb, True)

    def cond(state):
        j, live = state
        return jnp.logical_and(j >= 0, live)

    def body(state):
        j, _ = state
        key_block(j, False)
        return j - 1, alive()

    lax.while_loop(cond, body, (qb - 1, alive()))
    o_ref[...] = acc_ref[...].astype(BF16)


def _attention(q, k, v, batch, seq):
    q3, k3, v3 = (t.reshape(batch, seq, WIDTH) for t in (q, k, v))
    blk = pl.BlockSpec((None, CHUNK, WIDTH), lambda b, qb: (b, qb, 0))
    full = pl.BlockSpec((None, seq, WIDTH), lambda b, qb: (b, 0, 0), pipeline_mode=pl.Buffered(1))
    o = pl.pallas_call(
        _attn_kernel,
        grid=(batch, seq // CHUNK),
        in_specs=[blk, full, full],
        out_specs=blk,
        out_shape=jax.ShapeDtypeStruct((batch, seq, WIDTH), BF16),
        scratch_shapes=[pltpu.VMEM((WIDTH // LANES, 2 * CHUNK, LANES), F32),
                        pltpu.VMEM((CHUNK, WIDTH), F32),
                        pltpu.VMEM((WIDTH // LANES, 2 * CHUNK, LANES), F32)],
        compiler_params=pltpu.CompilerParams(
            dimension_semantics=("arbitrary", "arbitrary"), vmem_limit_bytes=VMEM_LIMIT),
        name="attn",
    )(q3, k3, v3)
    return o.reshape(batch * seq, WIDTH)


def _choose_tile(hb, wr_ref, br_ref):
    logits = jnp.dot(hb, wr_ref[...], preferred_element_type=F32) + br_ref[...]
    lane_t = lax.broadcasted_iota(jnp.int32, (TM_MIX, LANES), 1)
    lanef = lane_t.astype(F32)
    neg = -jnp.inf
    far = float(LANES)

    def first_max(vals):
        m = jnp.max(vals, axis=-1, keepdims=True)
        idx = jnp.min(jnp.where(vals == m, lanef, far), axis=-1, keepdims=True)
        return m, idx

    gl = jnp.where(lane_t < N_GROUPS, logits, neg)
    gmax, grp = first_max(gl)
    grp_w = 1.0 / jnp.sum(jnp.exp(gl - gmax), axis=-1, keepdims=True)
    lo_lane = ROUTE_LANE0 + EXPERTS_PER_GROUP * grp
    in_group = jnp.logical_and(lanef >= lo_lane, lanef < lo_lane + EXPERTS_PER_GROUP)
    el = jnp.where(in_group, logits, neg)
    m1, i1 = first_max(el)
    m2, i2 = first_max(jnp.where(lanef == i1, neg, el))
    e21 = jnp.exp(m2 - m1)
    w1 = grp_w / (1.0 + e21)
    w2 = w1 * e21
    return i1, i2, w1, w2


def _rank_tile(choice, run_ref, valid):
    i1, i2, w1, w2 = choice
    lane_t = lax.broadcasted_iota(jnp.int32, (TM_MIX, LANES), 1)
    lanef = lane_t.astype(F32)
    onehot = jnp.logical_and(jnp.logical_or(lanef == i1, lanef == i2), valid)
    rt = lax.broadcasted_iota(jnp.int32, (TM_MIX, TM_MIX), 0)
    ct = lax.broadcasted_iota(jnp.int32, (TM_MIX, TM_MIX), 1)
    before = jnp.where(ct < rt, 1.0, 0.0).astype(BF16)
    prior = jnp.dot(before, jnp.where(onehot, 1.0, 0.0).astype(BF16),
                    preferred_element_type=F32) + run_ref[...]
    r1 = jnp.sum(jnp.where(lanef == i1, prior, 0.0), axis=-1, keepdims=True)
    r2 = jnp.sum(jnp.where(lanef == i2, prior, 0.0), axis=-1, keepdims=True)
    run_ref[...] += jnp.sum(jnp.where(onehot, 1.0, 0.0), axis=0, keepdims=True)

    fields = (i1 - ROUTE_LANE0, i2 - ROUTE_LANE0, w1, w2, r1, r2)
    route = jnp.zeros((TM_MIX, LANES), F32)
    for pos, val in enumerate(fields):
        route = jnp.where(lane_t == pos, val, route)
    return route


def _mix_kernel(x_ref, oa_ref, u_ref, vs_ref, ga_ref, gb_ref,
                wsp_ref, bsp_ref, wua_ref, wub_ref, wo_ref, gf_ref, wr_ref, br_ref,
                xo_ref, h_ref, route_ref, cnt_ref, run_ref, hprev_ref):
    step = pl.program_id(0)

    @pl.when(step == 0)
    def _():
        run_ref[...] = jnp.zeros_like(run_ref)
        hprev_ref[...] = jnp.zeros_like(hprev_ref)

    choice = _choose_tile(hprev_ref[...], wr_ref, br_ref)

    lane = lax.broadcasted_iota(jnp.int32, (CHUNK, LANES), 1)
    row = lax.broadcasted_iota(jnp.int32, (CHUNK, LANES), 0)
    first = lane < HEAD_DIM
    tril = lane <= row

    ob_chunks = []
    for c in range(TM_MIX // CHUNK):
        rows = slice(c * CHUNK, (c + 1) * CHUNK)
        cols = []
        for gp in range(WIDTH // LANES):
            vpair = vs_ref[rows, gp * LANES:(gp + 1) * LANES]
            mixed = []
            for g in (2 * gp, 2 * gp + 1):
                wg = jnp.where(tril, wsp_ref[g], 0.0).astype(BF16)
                mixed.append(jnp.dot(wg, vpair, preferred_element_type=F32))
            cols.append(jnp.where(first, mixed[0], mixed[1]))
        mixed = jnp.concatenate(cols, axis=-1) + bsp_ref[...]
        ob_chunks.append((u_ref[rows, :].astype(F32) * mixed).astype(BF16))
    ob = jnp.concatenate(ob_chunks, axis=0)

    up_a = jnp.dot(oa_ref[...], wua_ref[...], preferred_element_type=F32)
    up_b = jnp.dot(ob, wub_ref[...], preferred_element_type=F32)
    merged = ga_ref[...].astype(F32) * up_a + gb_ref[...].astype(F32) * up_b
    x = x_ref[...] + jnp.dot(merged.astype(BF16), wo_ref[...], preferred_element_type=F32)
    xo_ref[...] = x

    h = _rms(x, gf_ref[...])
    _store_row_tiles(h_ref, h)
    hprev_ref[...] = h.astype(BF16)

    route_ref[...] = _rank_tile(choice, run_ref, step > 0)
    cnt_ref[...] = run_ref[...]


def _mix(layer, x, oa, u, vs, ga, gb, wsp, bsp, wua, wub, wo, gf, wr, br):
    n = x.shape[0]
    tiles = n // TM_MIX
    cur = lambda i: jnp.minimum(i, tiles - 1)
    row = lambda width: pl.BlockSpec((TM_MIX, width), lambda i: (cur(i), 0))
    lay = lambda *shape: pl.BlockSpec((None,) + shape, lambda i: (layer,) + (0,) * len(shape))
    return pl.pallas_call(
        _mix_kernel,
        grid=(tiles + 1,),
        in_specs=[row(D_MODEL), row(WIDTH), row(WIDTH), row(WIDTH), row(D_MODEL), row(D_MODEL),
                  lay(HEADS, CHUNK, CHUNK), lay(CHUNK, WIDTH), lay(WIDTH, D_MODEL),
                  lay(WIDTH, D_MODEL), lay(D_MODEL, D_MODEL), lay(1, D_MODEL),
                  lay(D_MODEL, LANES), lay(1, LANES)],
        out_specs=[row(D_MODEL),
                   pl.BlockSpec((TM_MIX * ROW_TILE, LANES), lambda i: (cur(i), 0)),
                   pl.BlockSpec((TM_MIX, LANES), lambda i: (jnp.maximum(i - 1, 0), 0)),
                   pl.BlockSpec((1, LANES), lambda i: (0, 0))],
        out_shape=[jax.ShapeDtypeStruct((n, D_MODEL), F32),
                   jax.ShapeDtypeStruct((n * ROW_TILE, LANES), U32),
                   jax.ShapeDtypeStruct((n, LANES), F32),
                   jax.ShapeDtypeStruct((1, LANES), F32)],
        scratch_shapes=[pltpu.VMEM((1, LANES), F32), pltpu.VMEM((TM_MIX, D_MODEL), BF16)],
        compiler_params=pltpu.CompilerParams(
            dimension_semantics=("arbitrary",), vmem_limit_bytes=VMEM_LIMIT),
        name="mix",
    )(x, oa, u, vs, ga, gb, wsp, bsp, wua, wub, wo, gf, wr, br)


def _dispatch_kernel(dest_ref, zrow_ref, h_ref, xs_ref, zero_ref, sem, zsem, tsem):
    i = pl.program_id(0)
    block_rows = MOE_ROWS * ROW_TILE
    nblk = xs_ref.shape[0] // block_rows
    nused = zrow_ref[N_EXPERTS]

    def zero_copy(first_row, zero_sem):
        return pltpu.make_async_copy(zero_ref, xs_ref.at[pl.ds(first_row, block_rows), :],
                                     zero_sem)

    def tail_copy(j):
        return zero_copy((nused + j) * block_rows, tsem)

    @pl.when(i == 0)
    def _():
        zero_ref[...] = jnp.zeros_like(zero_ref)
        for j in range(N_EXPERTS):
            pl.when(nused + j < nblk)(lambda j=j: tail_copy(j).start())
        for e in range(N_EXPERTS):
            pl.when(zrow_ref[e] >= 0)(lambda e=e: zero_copy(zrow_ref[e], zsem).start())
        for e in range(N_EXPERTS):
            pl.when(zrow_ref[e] >= 0)(lambda e=e: zero_copy(zrow_ref[e], zsem).wait())

    base = i * (2 * TM_DISP)
    for t in range(TM_DISP):
        for k in range(2):
            d = dest_ref[base + 2 * t + k]
            pltpu.make_async_copy(h_ref.at[pl.ds(t * ROW_TILE, ROW_TILE), :],
                                  xs_ref.at[pl.ds(d * ROW_TILE, ROW_TILE), :],
                                  sem).start(priority=k)
    for k in range(2):
        pltpu.make_async_copy(h_ref, xs_ref.at[pl.ds(0, TM_DISP * ROW_TILE), :], sem).wait()

    @pl.when(i == pl.num_programs(0) - 1)
    def _():
        for j in range(N_EXPERTS):
            pl.when(nused + j < nblk)(lambda j=j: tail_copy(j).wait())


def _dispatch(dest, zrow, h, cap):
    n = h.shape[0] // ROW_TILE
    return pl.pallas_call(
        _dispatch_kernel,
        grid_spec=pltpu.PrefetchScalarGridSpec(
            num_scalar_prefetch=2,
            grid=(n // TM_DISP,),
            in_specs=[pl.BlockSpec((TM_DISP * ROW_TILE, LANES), lambda i, dest, zrow: (i, 0))],
            out_specs=pl.BlockSpec(memory_space=pl.ANY),
            scratch_shapes=[pltpu.VMEM((MOE_ROWS * ROW_TILE, LANES), U32),
                            pltpu.SemaphoreType.DMA(()), pltpu.SemaphoreType.DMA(()),
                            pltpu.SemaphoreType.DMA(())],
        ),
        out_shape=jax.ShapeDtypeStruct((cap * ROW_TILE, LANES), U32),
        compiler_params=pltpu.CompilerParams(
            dimension_semantics=("arbitrary",), vmem_limit_bytes=VMEM_LIMIT),
        name="dispatch",
    )(dest, zrow, h)


def _expert_kernel(layer, blk_e_ref, nused_ref, half_ref, next_ref, xs_ref, wi_hbm, wo_hbm,
                   ys_ref, wi_f, wo_f, wi_b, wo_b, sem):
    b = pl.program_id(0)
    used = b < nused_ref[0]
    expert = blk_e_ref[b]

    def fetch(e, half):
        return (pltpu.make_async_copy(wi_hbm.at[layer, e], wi_f.at[half], sem.at[half, 0]),
                pltpu.make_async_copy(wo_hbm.at[layer, e], wo_f.at[half], sem.at[half, 1]))

    @pl.when(b == 0)
    def _():
        for copy in fetch(expert, half_ref[expert]):
            copy.start()

    @pl.when(jnp.logical_not(used))
    def _():
        ys_ref[...] = jnp.zeros_like(ys_ref)

    new_expert = jnp.logical_or(b == 0, expert != blk_e_ref[jnp.maximum(b - 1, 0)])

    @pl.when(jnp.logical_and(used, new_expert))
    def _():
        half = half_ref[expert]
        for copy in fetch(expert, half):
            copy.wait()
        following = next_ref[expert]

        @pl.when(following >= 0)
        def _():
            for copy in fetch(following, 1 - half):
                copy.start()

        wi_b[...] = wi_f[half].astype(BF16)
        wo_b[...] = wo_f[half].astype(BF16)

    @pl.when(used)
    def _():
        xs = _load_row_tiles(xs_ref, MOE_ROWS).astype(BF16)
        gu = jnp.dot(xs, wi_b[...], preferred_element_type=F32)
        act = jax.nn.silu(gu[:, :EXPERT_FF]) * gu[:, EXPERT_FF:]
        y = jnp.dot(act.astype(BF16), wo_b[...], preferred_element_type=F32)
        _store_row_tiles(ys_ref, y)


def _experts(layer, blk_e, nused, half, following, xs, w_e_in, w_e_out):
    cap = xs.shape[0] // ROW_TILE
    rows = MOE_ROWS * ROW_TILE
    return pl.pallas_call(
        functools.partial(_expert_kernel, layer),
        grid_spec=pltpu.PrefetchScalarGridSpec(
            num_scalar_prefetch=4,
            grid=(cap // MOE_ROWS,),
            in_specs=[pl.BlockSpec((rows, LANES),
                                   lambda b, blk_e, nused, *_: (jnp.minimum(b, nused[0] - 1), 0)),
                      pl.BlockSpec(memory_space=pl.ANY),
                      pl.BlockSpec(memory_space=pl.ANY)],
            out_specs=pl.BlockSpec((rows, LANES), lambda b, *_: (b, 0)),
            scratch_shapes=[pltpu.VMEM((2, D_MODEL, 2 * EXPERT_FF), F32),
                            pltpu.VMEM((2, EXPERT_FF, D_MODEL), F32),
                            pltpu.VMEM((D_MODEL, 2 * EXPERT_FF), BF16),
                            pltpu.VMEM((EXPERT_FF, D_MODEL), BF16),
                            pltpu.SemaphoreType.DMA((2, 2))],
        ),
        out_shape=jax.ShapeDtypeStruct((cap * ROW_TILE, LANES), U32),
        compiler_params=pltpu.CompilerParams(
            dimension_semantics=("arbitrary",), vmem_limit_bytes=VMEM_LIMIT),
        name="experts",
    )(blk_e, nused, half, following, xs, w_e_in, w_e_out)


def _ple_kernel(dest_ref, x_ref, route_ref, p_ref, ys_ref, gp_ref, wg_ref, wp_ref,
                xo_ref, y00, y01, y10, y11, sem):
    ybuf = ((y00, y01), (y10, y11))
    j = pl.program_id(0)
    last_step = pl.num_programs(0) - 1

    def start_gather(tile, half):
        base = tile * (2 * TM_ROW)
        for t in range(TM_ROW):
            for k in range(2):
                d = dest_ref[base + 2 * t + k]
                pltpu.make_async_copy(ys_ref.at[pl.ds(d * ROW_TILE, ROW_TILE), :],
                                      ybuf[half][k].at[pl.ds(t * ROW_TILE, ROW_TILE), :],
                                      sem.at[half]).start(priority=k)

    def wait_gather(half):
        for k in range(2):
            pltpu.make_async_copy(ys_ref.at[pl.ds(0, TM_ROW * ROW_TILE), :],
                                  ybuf[half][k], sem.at[half]).wait()

    def combine(half):
        rows = slice(half * TM_ROW, (half + 1) * TM_ROW)
        pe = jnp.dot(p_ref[rows, :].astype(BF16), wp_ref[...], preferred_element_type=F32)
        wait_gather(half)
        route = route_ref[rows, :]
        y0 = _load_row_tiles(ybuf[half][0], TM_ROW)
        y1 = _load_row_tiles(ybuf[half][1], TM_ROW)
        x = x_ref[rows, :] + route[:, 2:3] * y0 + route[:, 3:4] * y1
        gate = jax.nn.sigmoid(jnp.dot(_rms(x, gp_ref[...]).astype(BF16), wg_ref[...],
                                      preferred_element_type=F32))
        xo_ref[rows, :] = x + gate * pe

    @pl.when(j == 0)
    def _():
        start_gather(0, 0)

    start_gather(2 * j + 1, 1)
    combine(0)
    start_gather(jnp.minimum(2 * j + 2, 2 * last_step + 1), 0)
    combine(1)

    @pl.when(j == last_step)
    def _():
        wait_gather(0)


def _ple(layer, dest, x, route, p, ys, gp, wg, wp):
    n = x.shape[0]
    row = lambda width: pl.BlockSpec((2 * TM_ROW, width), lambda j, dest: (j, 0))
    lay = lambda *shape: pl.BlockSpec((None,) + shape,
                                      lambda j, dest: (layer,) + (0,) * len(shape))
    return pl.pallas_call(
        _ple_kernel,
        grid_spec=pltpu.PrefetchScalarGridSpec(
            num_scalar_prefetch=1,
            grid=(n // (2 * TM_ROW),),
            in_specs=[row(D_MODEL), row(LANES),
                      pl.BlockSpec((None, 2 * TM_ROW, PLE_DIM), lambda j, dest: (layer, j, 0)),
                      pl.BlockSpec(memory_space=pl.ANY),
                      lay(1, D_MODEL), lay(D_MODEL, D_MODEL), lay(PLE_DIM, D_MODEL)],
            out_specs=row(D_MODEL),
            scratch_shapes=[pltpu.VMEM((TM_ROW * ROW_TILE, LANES), U32)] * 4
            + [pltpu.SemaphoreType.DMA((2,))],
        ),
        out_shape=jax.ShapeDtypeStruct((n, D_MODEL), F32),
        compiler_params=pltpu.CompilerParams(
            dimension_semantics=("arbitrary",), vmem_limit_bytes=VMEM_LIMIT),
        name="ple",
    )(dest, x, route, p, ys, gp, wg, wp)


def _slot_layout(route, counts, cap):
    eid = route[:, 0:2].astype(jnp.int32)
    rank = route[:, 4:6].astype(jnp.int32)
    cnt = counts[0, ROUTE_LANE0:ROUTE_LANE0 + N_EXPERTS].astype(jnp.int32)
    padded = (cnt + MOE_ROWS - 1) // MOE_ROWS * MOE_ROWS
    pend = jnp.cumsum(padded)
    poff = pend - padded
    onehot = eid[:, :, None] == jnp.arange(N_EXPERTS, dtype=jnp.int32)
    dest = jnp.sum(jnp.where(onehot, poff, 0), axis=-1) + rank
    blk_start = jnp.arange(cap // MOE_ROWS, dtype=jnp.int32) * MOE_ROWS
    blk_e = jnp.minimum(jnp.sum(pend[None, :] <= blk_start[:, None], axis=1), N_EXPERTS - 1)
    nused = (pend[-1:] // MOE_ROWS).astype(jnp.int32)
    dest = dest.reshape(-1).astype(jnp.int32)
    zrow = jnp.where(cnt > 0, (pend - MOE_ROWS) * ROW_TILE, -1).astype(jnp.int32)
    ids = jnp.arange(N_EXPERTS, dtype=jnp.int32)
    nonempty = cnt > 0
    half = ((jnp.cumsum(nonempty) - 1) % 2).astype(jnp.int32)
    later = jnp.logical_and(nonempty[None, :], ids[None, :] > ids[:, None])
    following = jnp.min(jnp.where(later, ids[None, :], N_EXPERTS), axis=1)
    following = jnp.where(following < N_EXPERTS, following, -1).astype(jnp.int32)
    return (dest, jnp.concatenate([zrow, nused]), blk_e.astype(jnp.int32), nused, half,
            following)


def kernel(x, p, norm_mix, w_in, q_norm, k_norm, sgu_norm, w_spatial, b_spatial, w_up_a, w_up_b,
           w_out, norm_ffn, w_group_router, b_group_router, w_expert_router, b_expert_router,
           w_expert_in, w_expert_out, norm_ple, w_ple_gate, w_ple_proj):
    batch, seq, d = x.shape
    depth = w_in.shape[0]
    n = batch * seq
    cap = 2 * n + N_EXPERTS * MOE_ROWS

    vec = lambda a: a[:, None, :]
    w_in_b, w_ua_b, w_ub_b, w_o_b = (a.astype(BF16) for a in (w_in, w_up_a, w_up_b, w_out))
    w_pg_b, w_pp_b = w_ple_gate.astype(BF16), w_ple_proj.astype(BF16)
    qn2, kn2 = vec(jnp.tile(q_norm, (1, 2))), vec(jnp.tile(k_norm, (1, 2)))
    b_sp = jnp.repeat(jnp.swapaxes(b_spatial, 1, 2), HEAD_DIM, axis=2)
    pad = LANES - N_GROUPS - N_EXPERTS
    w_r = jnp.pad(jnp.concatenate([w_group_router, w_expert_router], axis=2),
                  ((0, 0), (0, 0), (0, pad))).astype(BF16)
    b_r = vec(jnp.pad(jnp.concatenate([b_group_router, b_expert_router], axis=1),
                      ((0, 0), (0, pad))))
    p2 = p.reshape(depth, n, PLE_DIM)

    xf = x.reshape(n, d)
    for i in range(depth):
        q, k, v, u, vs, ga, gb = _inproj(i, xf, vec(norm_mix), w_in_b, qn2, kn2, vec(sgu_norm))
        oa = _attention(q, k, v, batch, seq)
        xf, h, route, counts = _mix(i, xf, oa, u, vs, ga, gb, w_spatial, b_sp, w_ua_b, w_ub_b,
                                    w_o_b, vec(norm_ffn), w_r, b_r)
        dest, zrow, blk_e, nused, half, following = _slot_layout(route, counts, cap)
        xs = _dispatch(dest, zrow, h, cap)
        ys = _experts(i, blk_e, nused, half, following, xs, w_expert_in, w_expert_out)
        xf = _ple(i, dest, xf, route, p2, ys, vec(norm_ple), w_pg_b, w_pp_b)
    return xf.reshape(batch, seq, d)
```

```python
import functools
import math

import jax
import jax.numpy as jnp
from jax import lax
from jax.experimental import pallas as pl
from jax.experimental.pallas import tpu as pltpu

F32 = jnp.float32
BF16 = jnp.bfloat16

D_MODEL = 1024
HEADS = 8
HEAD_DIM = 64
WIDTH = HEADS * HEAD_DIM
CHUNK = 128
IN_WIDTH = 5 * WIDTH + 2 * D_MODEL
N_GROUPS = 4
EXPERTS_PER_GROUP = 8
N_EXPERTS = N_GROUPS * EXPERTS_PER_GROUP
EXPERT_FF = 512
PLE_DIM = 256
EPS = 1e-6

LANES = 128
ROW_TILE = D_MODEL // (2 * LANES)
U32 = jnp.uint32
ROUTE_LANE0 = N_GROUPS
TM_IN = 256
TM_MIX = 256
TM_DISP = 512
TM_ROW = 256
MOE_ROWS = 256
LOG2E = 1.4426950408889634
DEAD_LOG2 = -105.0 * LOG2E
VMEM_LIMIT = 48 * 1024 * 1024


def _rms(x, gain):
    return x * lax.rsqrt(jnp.mean(x * x, axis=-1, keepdims=True) + EPS) * gain


def _store_row_tiles(ref, value, lead=()):
    rows = value.shape[0]
    bits = lambda v: lax.bitcast_convert_type(v.astype(BF16).astype(F32), U32)
    for c in range(ROW_TILE):
        low = bits(value[:, 2 * c * LANES:(2 * c + 1) * LANES]) >> 16
        high = bits(value[:, (2 * c + 1) * LANES:(2 * c + 2) * LANES]) & U32(0xFFFF0000)
        ref[lead + (pl.ds(c, rows, stride=ROW_TILE), slice(None))] = low | high


def _load_row_tiles(ref, rows, lead=()):
    chunks = []
    for c in range(ROW_TILE):
        words = ref[lead + (pl.ds(c, rows, stride=ROW_TILE), slice(None))]
        chunks.append(lax.bitcast_convert_type(words << 16, F32))
        chunks.append(lax.bitcast_convert_type(words & U32(0xFFFF0000), F32))
    return jnp.concatenate(chunks, axis=1)


def _inproj_kernel(x_ref, g_ref, wf_ref, qn_ref, kn_ref, sn_ref,
                   q_ref, k_ref, v_ref, u_ref, vs_ref, ga_ref, gb_ref, w_ref):
    @pl.when(pl.program_id(0) == 0)
    def _():
        w_ref[...] = wf_ref[...].astype(BF16)

    hb = _rms(x_ref[...], g_ref[...]).astype(BF16)

    def proj(lo, hi):
        return jnp.dot(hb, w_ref[:, lo:hi], preferred_element_type=F32)

    first = lax.broadcasted_iota(jnp.int32, (1, LANES), 1) < HEAD_DIM

    def head_norm(z, gain, scale):
        outs = []
        for c in range(WIDTH // LANES):
            zc = z[:, c * LANES:(c + 1) * LANES]
            sq = zc * zc
            sa = jnp.sum(jnp.where(first, sq, 0.0), axis=-1, keepdims=True)
            sb = jnp.sum(jnp.where(first, 0.0, sq), axis=-1, keepdims=True)
            ms = jnp.where(first, sa, sb) * (1.0 / HEAD_DIM)
            outs.append(zc * lax.rsqrt(ms + EPS) * (gain * scale))
        return jnp.concatenate(outs, axis=-1)

    w = WIDTH
    q_ref[...] = head_norm(proj(0, w), qn_ref[...], LOG2E / math.sqrt(HEAD_DIM)).astype(BF16)
    k_ref[...] = head_norm(proj(w, 2 * w), kn_ref[...], 1.0).astype(BF16)
    v_ref[...] = proj(2 * w, 3 * w).astype(BF16)
    u_ref[...] = jax.nn.gelu(proj(3 * w, 4 * w)).astype(BF16)
    vs_ref[...] = _rms(jax.nn.gelu(proj(4 * w, 5 * w)), sn_ref[...]).astype(BF16)
    ga_ref[...] = jax.nn.sigmoid(proj(5 * w, 5 * w + D_MODEL)).astype(BF16)
    gb_ref[...] = jax.nn.sigmoid(proj(5 * w + D_MODEL, IN_WIDTH)).astype(BF16)


def _inproj(layer, x, norm_mix, w_in, qn, kn, sn):
    n = x.shape[0]
    row = lambda width: pl.BlockSpec((TM_IN, width), lambda i: (i, 0))
    vec = lambda width: pl.BlockSpec((None, 1, width), lambda i: (layer, 0, 0))
    out = lambda width: jax.ShapeDtypeStruct((n, width), BF16)
    return pl.pallas_call(
        _inproj_kernel,
        grid=(n // TM_IN,),
        in_specs=[row(D_MODEL), vec(D_MODEL),
                  pl.BlockSpec((None, D_MODEL, IN_WIDTH), lambda i: (layer, 0, 0),
                               pipeline_mode=pl.Buffered(1)),
                  vec(LANES), vec(LANES), vec(WIDTH)],
        out_specs=[row(WIDTH)] * 5 + [row(D_MODEL)] * 2,
        out_shape=[out(WIDTH)] * 5 + [out(D_MODEL)] * 2,
        scratch_shapes=[pltpu.VMEM((D_MODEL, IN_WIDTH), BF16)],
        compiler_params=pltpu.CompilerParams(
            dimension_semantics=("arbitrary",), vmem_limit_bytes=VMEM_LIMIT),
        name="inproj",
    )(x, norm_mix, w_in, qn, kn, sn)


def _attn_kernel(q_ref, k_ref, v_ref, o_ref, carry_ref, acc_ref, z_ref):
    qb = pl.program_id(1)
    pairs = WIDTH // LANES
    lane = lax.broadcasted_iota(jnp.int32, (2 * CHUNK, LANES), 1)
    row = lax.broadcasted_iota(jnp.int32, (2 * CHUNK, LANES), 0)
    upper = row < CHUNK
    query = jnp.where(upper, row, row - CHUNK)
    own = jnp.logical_xor(lane < HEAD_DIM, jnp.logical_not(upper))
    causal = lane < query
    first = lax.broadcasted_iota(jnp.int32, (CHUNK, LANES), 1) < HEAD_DIM
    kr = lax.broadcasted_iota(jnp.int32, (LANES, 2 * LANES), 0)
    kc = lax.broadcasted_iota(jnp.int32, (LANES, 2 * LANES), 1)
    tail = jnp.where(jnp.logical_or(kr > kc, kc >= LANES), 1.0, 0.0).astype(BF16)
    sign = jnp.uint32(0x80000000)

    cols = [slice(p * LANES, (p + 1) * LANES) for p in range(pairs)]
    q2s = []
    for p in range(pairs):
        qp = q_ref[:, cols[p]]
        q2 = jnp.concatenate([qp, qp], axis=0)
        q2s.append(jnp.where(own, q2, jnp.zeros_like(q2)))

    def scores(j):
        start = pl.multiple_of(j * CHUNK, CHUNK)
        return [lax.dot_general(q2s[p], k_ref[pl.ds(start, CHUNK), cols[p]],
                                (((1,), (1,)), ((), ())), preferred_element_type=F32)
                for p in range(pairs)]

    def key_block(j, diagonal):
        start = pl.multiple_of(j * CHUNK, CHUNK)
        zs = scores(j) if diagonal else [z_ref[p] for p in range(pairs)]
        for p, z in enumerate(scores(jnp.maximum(j - 1, 0))):
            z_ref[p] = z
        carries = [0.0 if diagonal else carry_ref[p] for p in range(pairs)]
        log_betas, splits = [], []
        for z in zs:
            neg_abs = lax.bitcast_convert_type(lax.bitcast_convert_type(z, jnp.uint32) | sign, F32)
            log1p = jnp.log(1.0 + jnp.exp2(neg_abs)) * LOG2E
            log_beta = jnp.minimum(z, 0.0) - log1p
            log_keep = log_beta - z
            if diagonal:
                log_keep = jnp.where(causal, log_keep, 0.0)
            log_betas.append(log_beta)
            splits.append(log_keep.astype(BF16))
        sums = [jnp.dot(s, tail, preferred_element_type=F32) for s in splits]
        weights = []
        for p in range(pairs):
            a = jnp.exp2(log_betas[p] + sums[p][:, :LANES] + carries[p])
            if diagonal:
                a = jnp.where(causal, a, 0.0)
            weights.append(a.astype(BF16))
        outs = [jnp.dot(weights[p], v_ref[pl.ds(start, CHUNK), cols[p]],
                        preferred_element_type=F32) for p in range(pairs)]
        for p in range(pairs):
            carry_ref[p] = carries[p] + sums[p][:, LANES:]
        for p in range(pairs):
            o = jnp.where(first, outs[p][:CHUNK], outs[p][CHUNK:])
            acc_ref[:, cols[p]] = o if diagonal else acc_ref[:, cols[p]] + o

    def alive():
        return jnp.max(carry_ref[...]) > DEAD_LOG2

    key_block(qb, True)

    def cond(state):
        j, live = state
        return jnp.logical_and(j >= 0, live)

    def body(state):
        j, _ = state
        key_block(j, False)
        return j - 1, alive()

    lax.while_loop(cond, body, (qb - 1, alive()))
    o_ref[...] = acc_ref[...].astype(BF16)


def _attention(q, k, v, batch, seq):
    q3, k3, v3 = (t.reshape(batch, seq, WIDTH) for t in (q, k, v))
    blk = pl.BlockSpec((None, CHUNK, WIDTH), lambda b, qb: (b, qb, 0))
    full = pl.BlockSpec((None, seq, WIDTH), lambda b, qb: (b, 0, 0), pipeline_mode=pl.Buffered(1))
    o = pl.pallas_call(
        _attn_kernel,
        grid=(batch, seq // CHUNK),
        in_specs=[blk, full, full],
        out_specs=blk,
        out_shape=jax.ShapeDtypeStruct((batch, seq, WIDTH), BF16),
        scratch_shapes=[pltpu.VMEM((WIDTH // LANES, 2 * CHUNK, LANES), F32),
                        pltpu.VMEM((CHUNK, WIDTH), F32),
                        pltpu.VMEM((WIDTH // LANES, 2 * CHUNK, LANES), F32)],
        compiler_params=pltpu.CompilerParams(
            dimension_semantics=("arbitrary", "arbitrary"), vmem_limit_bytes=VMEM_LIMIT),
        name="attn",
    )(q3, k3, v3)
    return o.reshape(batch * seq, WIDTH)


def _choose_tile(hb, wr_ref, br_ref):
    logits = jnp.dot(hb, wr_ref[...], preferred_element_type=F32) + br_ref[...]
    lane_t = lax.broadcasted_iota(jnp.int32, (TM_MIX, LANES), 1)
    lanef = lane_t.astype(F32)
    neg = -jnp.inf
    far = float(LANES)

    def first_max(vals):
        m = jnp.max(vals, axis=-1, keepdims=True)
        idx = jnp.min(jnp.where(vals == m, lanef, far), axis=-1, keepdims=True)
        return m, idx

    gl = jnp.where(lane_t < N_GROUPS, logits, neg)
    gmax, grp = first_max(gl)
    grp_w = 1.0 / jnp.sum(jnp.exp(gl - gmax), axis=-1, keepdims=True)
    lo_lane = ROUTE_LANE0 + EXPERTS_PER_GROUP * grp
    in_group = jnp.logical_and(lanef >= lo_lane, lanef < lo_lane + EXPERTS_PER_GROUP)
    el = jnp.where(in_group, logits, neg)
    m1, i1 = first_max(el)
    m2, i2 = first_max(jnp.where(lanef == i1, neg, el))
    e21 = jnp.exp(m2 - m1)
    w1 = grp_w / (1.0 + e21)
    w2 = w1 * e21
    return i1, i2, w1, w2


def _rank_tile(choice, run_ref, valid):
    i1, i2, w1, w2 = choice
    lane_t = lax.broadcasted_iota(jnp.int32, (TM_MIX, LANES), 1)
    lanef = lane_t.astype(F32)
    onehot = jnp.logical_and(jnp.logical_or(lanef == i1, lanef == i2), valid)
    rt = lax.broadcasted_iota(jnp.int32, (TM_MIX, TM_MIX), 0)
    ct = lax.broadcasted_iota(jnp.int32, (TM_MIX, TM_MIX), 1)
    before = jnp.where(ct < rt, 1.0, 0.0).astype(BF16)
    prior = jnp.dot(before, jnp.where(onehot, 1.0, 0.0).astype(BF16),
                    preferred_element_type=F32) + run_ref[...]
    r1 = jnp.sum(jnp.where(lanef == i1, prior, 0.0), axis=-1, keepdims=True)
    r2 = jnp.sum(jnp.where(lanef == i2, prior, 0.0), axis=-1, keepdims=True)
    run_ref[...] += jnp.sum(jnp.where(onehot, 1.0, 0.0), axis=0, keepdims=True)

    fields = (i1 - ROUTE_LANE0, i2 - ROUTE_LANE0, w1, w2, r1, r2)
    route = jnp.zeros((TM_MIX, LANES), F32)
    for pos, val in enumerate(fields):
        route = jnp.where(lane_t == pos, val, route)
    return route


def _mix_kernel(x_ref, oa_ref, u_ref, vs_ref, ga_ref, gb_ref,
                wsp_ref, bsp_ref, wua_ref, wub_ref, wo_ref, gf_ref, wr_ref, br_ref,
                xo_ref, h_ref, route_ref, cnt_ref, run_ref, hprev_ref):
    step = pl.program_id(0)

    @pl.when(step == 0)
    def _():
        run_ref[...] = jnp.zeros_like(run_ref)
        hprev_ref[...] = jnp.zeros_like(hprev_ref)

    choice = _choose_tile(hprev_ref[...], wr_ref, br_ref)

    lane = lax.broadcasted_iota(jnp.int32, (CHUNK, LANES), 1)
    row = lax.broadcasted_iota(jnp.int32, (CHUNK, LANES), 0)
    first = lane < HEAD_DIM
    tril = lane <= row

    ob_chunks = []
    for c in range(TM_MIX // CHUNK):
        rows = slice(c * CHUNK, (c + 1) * CHUNK)
        cols = []
        for gp in range(WIDTH // LANES):
            vpair = vs_ref[rows, gp * LANES:(gp + 1) * LANES]
            mixed = []
            for g in (2 * gp, 2 * gp + 1):
                wg = jnp.where(tril, wsp_ref[g], 0.0).astype(BF16)
                mixed.append(jnp.dot(wg, vpair, preferred_element_type=F32))
            cols.append(jnp.where(first, mixed[0], mixed[1]))
        mixed = jnp.concatenate(cols, axis=-1) + bsp_ref[...]
        ob_chunks.append((u_ref[rows, :].astype(F32) * mixed).astype(BF16))
    ob = jnp.concatenate(ob_chunks, axis=0)

    up_a = jnp.dot(oa_ref[...], wua_ref[...], preferred_element_type=F32)
    up_b = jnp.dot(ob, wub_ref[...], preferred_element_type=F32)
    merged = ga_ref[...].astype(F32) * up_a + gb_ref[...].astype(F32) * up_b
    x = x_ref[...] + jnp.dot(merged.astype(BF16), wo_ref[...], preferred_element_type=F32)
    xo_ref[...] = x

    h = _rms(x, gf_ref[...])
    _store_row_tiles(h_ref, h)
    hprev_ref[...] = h.astype(BF16)

    route_ref[...] = _rank_tile(choice, run_ref, step > 0)
    cnt_ref[...] = run_ref[...]


def _mix(layer, x, oa, u, vs, ga, gb, wsp, bsp, wua, wub, wo, gf, wr, br):
    n = x.shape[0]
    tiles = n // TM_MIX
    cur = lambda i: jnp.minimum(i, tiles - 1)
    row = lambda width: pl.BlockSpec((TM_MIX, width), lambda i: (cur(i), 0))
    lay = lambda *shape: pl.BlockSpec((None,) + shape, lambda i: (layer,) + (0,) * len(shape))
    return pl.pallas_call(
        _mix_kernel,
        grid=(tiles + 1,),
        in_specs=[row(D_MODEL), row(WIDTH), row(WIDTH), row(WIDTH), row(D_MODEL), row(D_MODEL),
                  lay(HEADS, CHUNK, CHUNK), lay(CHUNK, WIDTH), lay(WIDTH, D_MODEL),
                  lay(WIDTH, D_MODEL), lay(D_MODEL, D_MODEL), lay(1, D_MODEL),
                  lay(D_MODEL, LANES), lay(1, LANES)],
        out_specs=[row(D_MODEL),
                   pl.BlockSpec((TM_MIX * ROW_TILE, LANES), lambda i: (cur(i), 0)),
                   pl.BlockSpec((TM_MIX, LANES), lambda i: (jnp.maximum(i - 1, 0), 0)),
                   pl.BlockSpec((1, LANES), lambda i: (0, 0))],
        out_shape=[jax.ShapeDtypeStruct((n, D_MODEL), F32),
                   jax.ShapeDtypeStruct((n * ROW_TILE, LANES), U32),
                   jax.ShapeDtypeStruct((n, LANES), F32),
                   jax.ShapeDtypeStruct((1, LANES), F32)],
        scratch_shapes=[pltpu.VMEM((1, LANES), F32), pltpu.VMEM((TM_MIX, D_MODEL), BF16)],
        compiler_params=pltpu.CompilerParams(
            dimension_semantics=("arbitrary",), vmem_limit_bytes=VMEM_LIMIT),
        name="mix",
    )(x, oa, u, vs, ga, gb, wsp, bsp, wua, wub, wo, gf, wr, br)


def _dispatch_kernel(dest_ref, zrow_ref, h_ref, xs_ref, zero_ref, sem, zsem, tsem):
    i = pl.program_id(0)
    block_rows = MOE_ROWS * ROW_TILE
    nblk = xs_ref.shape[0] // block_rows
    nused = zrow_ref[N_EXPERTS]

    def zero_copy(first_row, zero_sem):
        return pltpu.make_async_copy(zero_ref, xs_ref.at[pl.ds(first_row, block_rows), :],
                                     zero_sem)

    def tail_copy(j):
        return zero_copy((nused + j) * block_rows, tsem)

    @pl.when(i == 0)
    def _():
        zero_ref[...] = jnp.zeros_like(zero_ref)
        for j in range(N_EXPERTS):
            pl.when(nused + j < nblk)(lambda j=j: tail_copy(j).start())
        for e in range(N_EXPERTS):
            pl.when(zrow_ref[e] >= 0)(lambda e=e: zero_copy(zrow_ref[e], zsem).start())
        for e in range(N_EXPERTS):
            pl.when(zrow_ref[e] >= 0)(lambda e=e: zero_copy(zrow_ref[e], zsem).wait())

    base = i * (2 * TM_DISP)
    for t in range(TM_DISP):
        for k in range(2):
            d = dest_ref[base + 2 * t + k]
            pltpu.make_async_copy(h_ref.at[pl.ds(t * ROW_TILE, ROW_TILE), :],
                                  xs_ref.at[pl.ds(d * ROW_TILE, ROW_TILE), :],
                                  sem).start(priority=k)
    for k in range(2):
        pltpu.make_async_copy(h_ref, xs_ref.at[pl.ds(0, TM_DISP * ROW_TILE), :], sem).wait()

    @pl.when(i == pl.num_programs(0) - 1)
    def _():
        for j in range(N_EXPERTS):
            pl.when(nused + j < nblk)(lambda j=j: tail_copy(j).wait())


def _dispatch(dest, zrow, h, cap):
    n = h.shape[0] // ROW_TILE
    return pl.pallas_call(
        _dispatch_kernel,
        grid_spec=pltpu.PrefetchScalarGridSpec(
            num_scalar_prefetch=2,
            grid=(n // TM_DISP,),
            in_specs=[pl.BlockSpec((TM_DISP * ROW_TILE, LANES), lambda i, dest, zrow: (i, 0))],
            out_specs=pl.BlockSpec(memory_space=pl.ANY),
            scratch_shapes=[pltpu.VMEM((MOE_ROWS * ROW_TILE, LANES), U32),
                            pltpu.SemaphoreType.DMA(()), pltpu.SemaphoreType.DMA(()),
                            pltpu.SemaphoreType.DMA(())],
        ),
        out_shape=jax.ShapeDtypeStruct((cap * ROW_TILE, LANES), U32),
        compiler_params=pltpu.CompilerParams(
            dimension_semantics=("arbitrary",), vmem_limit_bytes=VMEM_LIMIT),
        name="dispatch",
    )(dest, zrow, h)


def _expert_kernel(layer, blk_e_ref, nused_ref, half_ref, next_ref, xs_ref, wi_hbm, wo_hbm,
                   ys_ref, wi_f, wo_f, wi_b, wo_b, sem):
    b = pl.program_id(0)
    used = b < nused_ref[0]
    expert = blk_e_ref[b]

    def fetch(e, half):
        return (pltpu.make_async_copy(wi_hbm.at[layer, e], wi_f.at[half], sem.at[half, 0]),
                pltpu.make_async_copy(wo_hbm.at[layer, e], wo_f.at[half], sem.at[half, 1]))

    @pl.when(b == 0)
    def _():
        for copy in fetch(expert, half_ref[expert]):
            copy.start()

    @pl.when(jnp.logical_not(used))
    def _():
        ys_ref[...] = jnp.zeros_like(ys_ref)

    new_expert = jnp.logical_or(b == 0, expert != blk_e_ref[jnp.maximum(b - 1, 0)])

    @pl.when(jnp.logical_and(used, new_expert))
    def _():
        half = half_ref[expert]
        for copy in fetch(expert, half):
            copy.wait()
        following = next_ref[expert]

        @pl.when(following >= 0)
        def _():
            for copy in fetch(following, 1 - half):
                copy.start(priority=1)

        wi_b[...] = wi_f[half].astype(BF16)
        wo_b[...] = wo_f[half].astype(BF16)

    @pl.when(used)
    def _():
        xs = _load_row_tiles(xs_ref, MOE_ROWS).astype(BF16)
        gu = jnp.dot(xs, wi_b[...], preferred_element_type=F32)
        act = jax.nn.silu(gu[:, :EXPERT_FF]) * gu[:, EXPERT_FF:]
        y = jnp.dot(act.astype(BF16), wo_b[...], preferred_element_type=F32)
        _store_row_tiles(ys_ref, y)


def _experts(layer, blk_e, nused, half, following, xs, w_e_in, w_e_out):
    cap = xs.shape[0] // ROW_TILE
    rows = MOE_ROWS * ROW_TILE
    return pl.pallas_call(
        functools.partial(_expert_kernel, layer),
        grid_spec=pltpu.PrefetchScalarGridSpec(
            num_scalar_prefetch=4,
            grid=(cap // MOE_ROWS,),
            in_specs=[pl.BlockSpec((rows, LANES),
                                   lambda b, blk_e, nused, *_: (jnp.minimum(b, nused[0] - 1), 0)),
                      pl.BlockSpec(memory_space=pl.ANY),
                      pl.BlockSpec(memory_space=pl.ANY)],
            out_specs=pl.BlockSpec((rows, LANES), lambda b, *_: (b, 0)),
            scratch_shapes=[pltpu.VMEM((2, D_MODEL, 2 * EXPERT_FF), F32),
                            pltpu.VMEM((2, EXPERT_FF, D_MODEL), F32),
                            pltpu.VMEM((D_MODEL, 2 * EXPERT_FF), BF16),
                            pltpu.VMEM((EXPERT_FF, D_MODEL), BF16),
                            pltpu.SemaphoreType.DMA((2, 2))],
        ),
        out_shape=jax.ShapeDtypeStruct((cap * ROW_TILE, LANES), U32),
        compiler_params=pltpu.CompilerParams(
            dimension_semantics=("arbitrary",), vmem_limit_bytes=VMEM_LIMIT),
        name="experts",
    )(blk_e, nused, half, following, xs, w_e_in, w_e_out)


def _ple_kernel(dest_ref, x_ref, route_ref, p_ref, ys_ref, gp_ref, wg_ref, wp_ref,
                xo_ref, y00, y01, y10, y11, sem):
    ybuf = ((y00, y01), (y10, y11))
    j = pl.program_id(0)
    last_step = pl.num_programs(0) - 1

    def start_gather(tile, half):
        base = tile * (2 * TM_ROW)
        for t in range(TM_ROW):
            for k in range(2):
                d = dest_ref[base + 2 * t + k]
                pltpu.make_async_copy(ys_ref.at[pl.ds(d * ROW_TILE, ROW_TILE), :],
                                      ybuf[half][k].at[pl.ds(t * ROW_TILE, ROW_TILE), :],
                                      sem.at[half]).start(priority=k)

    def wait_gather(half):
        for k in range(2):
            pltpu.make_async_copy(ys_ref.at[pl.ds(0, TM_ROW * ROW_TILE), :],
                                  ybuf[half][k], sem.at[half]).wait()

    def combine(half):
        rows = slice(half * TM_ROW, (half + 1) * TM_ROW)
        pe = jnp.dot(p_ref[rows, :].astype(BF16), wp_ref[...], preferred_element_type=F32)
        wait_gather(half)
        route = route_ref[rows, :]
        y0 = _load_row_tiles(ybuf[half][0], TM_ROW)
        y1 = _load_row_tiles(ybuf[half][1], TM_ROW)
        x = x_ref[rows, :] + route[:, 2:3] * y0 + route[:, 3:4] * y1
        gate = jax.nn.sigmoid(jnp.dot(_rms(x, gp_ref[...]).astype(BF16), wg_ref[...],
                                      preferred_element_type=F32))
        xo_ref[rows, :] = x + gate * pe

    @pl.when(j == 0)
    def _():
        start_gather(0, 0)

    start_gather(2 * j + 1, 1)
    combine(0)
    start_gather(jnp.minimum(2 * j + 2, 2 * last_step + 1), 0)
    combine(1)

    @pl.when(j == last_step)
    def _():
        wait_gather(0)


def _ple(layer, dest, x, route, p, ys, gp, wg, wp):
    n = x.shape[0]
    row = lambda width: pl.BlockSpec((2 * TM_ROW, width), lambda j, dest: (j, 0))
    lay = lambda *shape: pl.BlockSpec((None,) + shape,
                                      lambda j, dest: (layer,) + (0,) * len(shape))
    return pl.pallas_call(
        _ple_kernel,
        grid_spec=pltpu.PrefetchScalarGridSpec(
            num_scalar_prefetch=1,
            grid=(n // (2 * TM_ROW),),
            in_specs=[row(D_MODEL), row(LANES),
                      pl.BlockSpec((None, 2 * TM_ROW, PLE_DIM), lambda j, dest: (layer, j, 0)),
                      pl.BlockSpec(memory_space=pl.ANY),
                      lay(1, D_MODEL), lay(D_MODEL, D_MODEL), lay(PLE_DIM, D_MODEL)],
            out_specs=row(D_MODEL),
            scratch_shapes=[pltpu.VMEM((TM_ROW * ROW_TILE, LANES), U32)] * 4
            + [pltpu.SemaphoreType.DMA((2,))],
        ),
        out_shape=jax.ShapeDtypeStruct((n, D_MODEL), F32),
        compiler_params=pltpu.CompilerParams(
            dimension_semantics=("arbitrary",), vmem_limit_bytes=VMEM_LIMIT),
        name="ple",
    )(dest, x, route, p, ys, gp, wg, wp)


def _slot_layout(route, counts, cap):
    eid = route[:, 0:2].astype(jnp.int32)
    rank = route[:, 4:6].astype(jnp.int32)
    cnt = counts[0, ROUTE_LANE0:ROUTE_LANE0 + N_EXPERTS].astype(jnp.int32)
    padded = (cnt + MOE_ROWS - 1) // MOE_ROWS * MOE_ROWS
    pend = jnp.cumsum(padded)
    poff = pend - padded
    onehot = eid[:, :, None] == jnp.arange(N_EXPERTS, dtype=jnp.int32)
    dest = jnp.sum(jnp.where(onehot, poff, 0), axis=-1) + rank
    blk_start = jnp.arange(cap // MOE_ROWS, dtype=jnp.int32) * MOE_ROWS
    blk_e = jnp.minimum(jnp.sum(pend[None, :] <= blk_start[:, None], axis=1), N_EXPERTS - 1)
    nused = (pend[-1:] // MOE_ROWS).astype(jnp.int32)
    dest = dest.reshape(-1).astype(jnp.int32)
    zrow = jnp.where(cnt > 0, (pend - MOE_ROWS) * ROW_TILE, -1).astype(jnp.int32)
    ids = jnp.arange(N_EXPERTS, dtype=jnp.int32)
    nonempty = cnt > 0
    half = ((jnp.cumsum(nonempty) - 1) % 2).astype(jnp.int32)
    later = jnp.logical_and(nonempty[None, :], ids[None, :] > ids[:, None])
    following = jnp.min(jnp.where(later, ids[None, :], N_EXPERTS), axis=1)
    following = jnp.where(following < N_EXPERTS, following, -1).astype(jnp.int32)
    return (dest, jnp.concatenate([zrow, nused]), blk_e.astype(jnp.int32), nused, half,
            following)


def kernel(x, p, norm_mix, w_in, q_norm, k_norm, sgu_norm, w_spatial, b_spatial, w_up_a, w_up_b,
           w_out, norm_ffn, w_group_router, b_group_router, w_expert_router, b_expert_router,
           w_expert_in, w_expert_out, norm_ple, w_ple_gate, w_ple_proj):
    batch, seq, d = x.shape
    depth = w_in.shape[0]
    n = batch * seq
    cap = 2 * n + N_EXPERTS * MOE_ROWS

    vec = lambda a: a[:, None, :]
    w_ua_b, w_ub_b, w_o_b = (a.astype(BF16) for a in (w_up_a, w_up_b, w_out))
    w_pg_b, w_pp_b = w_ple_gate.astype(BF16), w_ple_proj.astype(BF16)
    qn2, kn2 = vec(jnp.tile(q_norm, (1, 2))), vec(jnp.tile(k_norm, (1, 2)))
    b_sp = jnp.repeat(jnp.swapaxes(b_spatial, 1, 2), HEAD_DIM, axis=2)
    pad = LANES - N_GROUPS - N_EXPERTS
    w_r = jnp.pad(jnp.concatenate([w_group_router, w_expert_router], axis=2),
                  ((0, 0), (0, 0), (0, pad))).astype(BF16)
    b_r = vec(jnp.pad(jnp.concatenate([b_group_router, b_expert_router], axis=1),
                      ((0, 0), (0, pad))))
    p2 = p.reshape(depth, n, PLE_DIM)

    xf = x.reshape(n, d)
    for i in range(depth):
        q, k, v, u, vs, ga, gb = _inproj(i, xf, vec(norm_mix), w_in, qn2, kn2, vec(sgu_norm))
        oa = _attention(q, k, v, batch, seq)
        xf, h, route, counts = _mix(i, xf, oa, u, vs, ga, gb, w_spatial, b_sp, w_ua_b, w_ub_b,
                                    w_o_b, vec(norm_ffn), w_r, b_r)
        dest, zrow, blk_e, nused, half, following = _slot_layout(route, counts, cap)
        xs = _dispatch(dest, zrow, h, cap)
        ys = _experts(i, blk_e, nused, half, following, xs, w_expert_in, w_expert_out)
        xf = _ple(i, dest, xf, route, p2, ys, vec(norm_ple), w_pg_b, w_pp_b)
    return xf.reshape(batch, seq, d)
```

```python
import functools
import math

import jax
import jax.numpy as jnp
from jax import lax
from jax.experimental import pallas as pl
from jax.experimental.pallas import tpu as pltpu

F32 = jnp.float32
BF16 = jnp.bfloat16

D_MODEL = 1024
HEADS = 8
HEAD_DIM = 64
WIDTH = HEADS * HEAD_DIM
CHUNK = 128
IN_WIDTH = 5 * WIDTH + 2 * D_MODEL
N_GROUPS = 4
EXPERTS_PER_GROUP = 8
N_EXPERTS = N_GROUPS * EXPERTS_PER_GROUP
EXPERT_FF = 512
PLE_DIM = 256
EPS = 1e-6

LANES = 128
ROW_TILE = D_MODEL // (2 * LANES)
U32 = jnp.uint32
ROUTE_LANE0 = N_GROUPS
TM_IN = 256
TM_MIX = 256
TM_DISP = 512
TM_ROW = 256
MOE_ROWS = 256
LOG2E = 1.4426950408889634
DEAD_LOG2 = -105.0 * LOG2E
VMEM_LIMIT = 48 * 1024 * 1024


def _rms(x, gain):
    return x * lax.rsqrt(jnp.mean(x * x, axis=-1, keepdims=True) + EPS) * gain


def _store_row_tiles(ref, value, lead=()):
    rows = value.shape[0]
    bits = lambda v: lax.bitcast_convert_type(v.astype(BF16).astype(F32), U32)
    for c in range(ROW_TILE):
        low = bits(value[:, 2 * c * LANES:(2 * c + 1) * LANES]) >> 16
        high = bits(value[:, (2 * c + 1) * LANES:(2 * c + 2) * LANES]) & U32(0xFFFF0000)
        ref[lead + (pl.ds(c, rows, stride=ROW_TILE), slice(None))] = low | high


def _load_row_tiles(ref, rows, lead=()):
    chunks = []
    for c in range(ROW_TILE):
        words = ref[lead + (pl.ds(c, rows, stride=ROW_TILE), slice(None))]
        chunks.append(lax.bitcast_convert_type(words << 16, F32))
        chunks.append(lax.bitcast_convert_type(words & U32(0xFFFF0000), F32))
    return jnp.concatenate(chunks, axis=1)


def _inproj_kernel(x_ref, g_ref, wf_ref, qn_ref, kn_ref, sn_ref,
                   q_ref, k_ref, v_ref, u_ref, vs_ref, ga_ref, gb_ref, w_ref):
    @pl.when(pl.program_id(0) == 0)
    def _():
        w_ref[...] = wf_ref[...].astype(BF16)

    hb = _rms(x_ref[...], g_ref[...]).astype(BF16)

    def proj(lo, hi):
        return jnp.dot(hb, w_ref[:, lo:hi], preferred_element_type=F32)

    first = lax.broadcasted_iota(jnp.int32, (1, LANES), 1) < HEAD_DIM

    def head_norm(z, gain, scale):
        outs = []
        for c in range(WIDTH // LANES):
            zc = z[:, c * LANES:(c + 1) * LANES]
            sq = zc * zc
            sa = jnp.sum(jnp.where(first, sq, 0.0), axis=-1, keepdims=True)
            sb = jnp.sum(jnp.where(first, 0.0, sq), axis=-1, keepdims=True)
            ms = jnp.where(first, sa, sb) * (1.0 / HEAD_DIM)
            outs.append(zc * lax.rsqrt(ms + EPS) * (gain * scale))
        return jnp.concatenate(outs, axis=-1)

    w = WIDTH
    q_ref[...] = head_norm(proj(0, w), qn_ref[...], LOG2E / math.sqrt(HEAD_DIM)).astype(BF16)
    k_ref[...] = head_norm(proj(w, 2 * w), kn_ref[...], 1.0).astype(BF16)
    v_ref[...] = proj(2 * w, 3 * w).astype(BF16)
    u_ref[...] = jax.nn.gelu(proj(3 * w, 4 * w)).astype(BF16)
    vs_ref[...] = _rms(jax.nn.gelu(proj(4 * w, 5 * w)), sn_ref[...]).astype(BF16)
    ga_ref[...] = jax.nn.sigmoid(proj(5 * w, 5 * w + D_MODEL)).astype(BF16)
    gb_ref[...] = jax.nn.sigmoid(proj(5 * w + D_MODEL, IN_WIDTH)).astype(BF16)


def _inproj(layer, x, norm_mix, w_in, qn, kn, sn):
    n = x.shape[0]
    row = lambda width: pl.BlockSpec((TM_IN, width), lambda i: (i, 0))
    vec = lambda width: pl.BlockSpec((None, 1, width), lambda i: (layer, 0, 0))
    out = lambda width: jax.ShapeDtypeStruct((n, width), BF16)
    return pl.pallas_call(
        _inproj_kernel,
        grid=(n // TM_IN,),
        in_specs=[row(D_MODEL), vec(D_MODEL),
                  pl.BlockSpec((None, D_MODEL, IN_WIDTH), lambda i: (layer, 0, 0),
                               pipeline_mode=pl.Buffered(1)),
                  vec(LANES), vec(LANES), vec(WIDTH)],
        out_specs=[row(WIDTH)] * 5 + [row(D_MODEL)] * 2,
        out_shape=[out(WIDTH)] * 5 + [out(D_MODEL)] * 2,
        scratch_shapes=[pltpu.VMEM((D_MODEL, IN_WIDTH), BF16)],
        compiler_params=pltpu.CompilerParams(
            dimension_semantics=("arbitrary",), vmem_limit_bytes=VMEM_LIMIT),
        name="inproj",
    )(x, norm_mix, w_in, qn, kn, sn)


def _attn_kernel(q_ref, k_ref, v_ref, o_ref, carry_ref, acc_ref, z_ref):
    qb = pl.program_id(1)
    pairs = WIDTH // LANES
    lane = lax.broadcasted_iota(jnp.int32, (2 * CHUNK, LANES), 1)
    row = lax.broadcasted_iota(jnp.int32, (2 * CHUNK, LANES), 0)
    upper = row < CHUNK
    query = jnp.where(upper, row, row - CHUNK)
    own = jnp.logical_xor(lane < HEAD_DIM, jnp.logical_not(upper))
    causal = lane < query
    first = lax.broadcasted_iota(jnp.int32, (CHUNK, LANES), 1) < HEAD_DIM
    kr = lax.broadcasted_iota(jnp.int32, (LANES, 2 * LANES), 0)
    kc = lax.broadcasted_iota(jnp.int32, (LANES, 2 * LANES), 1)
    tail = jnp.where(jnp.logical_or(kr > kc, kc >= LANES), 1.0, 0.0).astype(BF16)
    sign = jnp.uint32(0x80000000)

    cols = [slice(p * LANES, (p + 1) * LANES) for p in range(pairs)]
    q2s = []
    for p in range(pairs):
        qp = q_ref[:, cols[p]]
        q2 = jnp.concatenate([qp, qp], axis=0)
        q2s.append(jnp.where(own, q2, jnp.zeros_like(q2)))

    def scores(j):
        start = pl.multiple_of(j * CHUNK, CHUNK)
        return [lax.dot_general(q2s[p], k_ref[pl.ds(start, CHUNK), cols[p]],
                                (((1,), (1,)), ((), ())), preferred_element_type=F32)
                for p in range(pairs)]

    def key_block(j, diagonal):
        start = pl.multiple_of(j * CHUNK, CHUNK)
        zs = scores(j) if diagonal else [z_ref[p] for p in range(pairs)]
        for p, z in enumerate(scores(jnp.maximum(j - 1, 0))):
            z_ref[p] = z
        carries = [0.0 if diagonal else carry_ref[p] for p in range(pairs)]
        log_betas, splits = [], []
        for z in zs:
            neg_abs = lax.bitcast_convert_type(lax.bitcast_convert_type(z, jnp.uint32) | sign, F32)
            log1p = jnp.log(1.0 + jnp.exp2(neg_abs)) * LOG2E
            log_beta = jnp.minimum(z, 0.0) - log1p
            log_keep = log_beta - z
            if diagonal:
                log_keep = jnp.where(causal, log_keep, 0.0)
            log_betas.append(log_beta)
            splits.append(log_keep.astype(BF16))
        sums = [jnp.dot(s, tail, preferred_element_type=F32) for s in splits]
        weights = []
        for p in range(pairs):
            a = jnp.exp2(log_betas[p] + sums[p][:, :LANES] + carries[p])
            if diagonal:
                a = jnp.where(causal, a, 0.0)
            weights.append(a.astype(BF16))
        outs = [jnp.dot(weights[p], v_ref[pl.ds(start, CHUNK), cols[p]],
                        preferred_element_type=F32) for p in range(pairs)]
        for p in range(pairs):
            carry_ref[p] = carries[p] + sums[p][:, LANES:]
        for p in range(pairs):
            o = jnp.where(first, outs[p][:CHUNK], outs[p][CHUNK:])
            acc_ref[:, cols[p]] = o if diagonal else acc_ref[:, cols[p]] + o

    def alive():
        return jnp.max(carry_ref[...]) > DEAD_LOG2

    key_block(qb, True)

    def cond(state):
        j, live = state
        return jnp.logical_and(j >= 0, live)

    def body(state):
        j, _ = state
        key_block(j, False)
        return j - 1, alive()

    lax.while_loop(cond, body, (qb - 1, alive()))
    o_ref[...] = acc_ref[...].astype(BF16)


def _attention(q, k, v, batch, seq):
    q3, k3, v3 = (t.reshape(batch, seq, WIDTH) for t in (q, k, v))
    blk = pl.BlockSpec((None, CHUNK, WIDTH), lambda b, qb: (b, qb, 0))
    full = pl.BlockSpec((None, seq, WIDTH), lambda b, qb: (b, 0, 0), pipeline_mode=pl.Buffered(1))
    o = pl.pallas_call(
        _attn_kernel,
        grid=(batch, seq // CHUNK),
        in_specs=[blk, full, full],
        out_specs=blk,
        out_shape=jax.ShapeDtypeStruct((batch, seq, WIDTH), BF16),
        scratch_shapes=[pltpu.VMEM((WIDTH // LANES, 2 * CHUNK, LANES), F32),
                        pltpu.VMEM((CHUNK, WIDTH), F32),
                        pltpu.VMEM((WIDTH // LANES, 2 * CHUNK, LANES), F32)],
        compiler_params=pltpu.CompilerParams(
            dimension_semantics=("arbitrary", "arbitrary"), vmem_limit_bytes=VMEM_LIMIT),
        name="attn",
    )(q3, k3, v3)
    return o.reshape(batch * seq, WIDTH)


def _choose_tile(hb, wr_ref, br_ref):
    logits = jnp.dot(hb, wr_ref[...], preferred_element_type=F32) + br_ref[...]
    lane_t = lax.broadcasted_iota(jnp.int32, (TM_MIX, LANES), 1)
    lanef = lane_t.astype(F32)
    neg = -jnp.inf
    far = float(LANES)

    def first_max(vals):
        m = jnp.max(vals, axis=-1, keepdims=True)
        idx = jnp.min(jnp.where(vals == m, lanef, far), axis=-1, keepdims=True)
        return m, idx

    gl = jnp.where(lane_t < N_GROUPS, logits, neg)
    gmax, grp = first_max(gl)
    grp_w = 1.0 / jnp.sum(jnp.exp(gl - gmax), axis=-1, keepdims=True)
    lo_lane = ROUTE_LANE0 + EXPERTS_PER_GROUP * grp
    in_group = jnp.logical_and(lanef >= lo_lane, lanef < lo_lane + EXPERTS_PER_GROUP)
    el = jnp.where(in_group, logits, neg)
    m1, i1 = first_max(el)
    m2, i2 = first_max(jnp.where(lanef == i1, neg, el))
    e21 = jnp.exp(m2 - m1)
    w1 = grp_w / (1.0 + e21)
    w2 = w1 * e21
    return i1, i2, w1, w2


def _rank_tile(choice, run_ref, valid):
    i1, i2, w1, w2 = choice
    lane_t = lax.broadcasted_iota(jnp.int32, (TM_MIX, LANES), 1)
    lanef = lane_t.astype(F32)
    onehot = jnp.logical_and(jnp.logical_or(lanef == i1, lanef == i2), valid)
    rt = lax.broadcasted_iota(jnp.int32, (TM_MIX, TM_MIX), 0)
    ct = lax.broadcasted_iota(jnp.int32, (TM_MIX, TM_MIX), 1)
    before = jnp.where(ct < rt, 1.0, 0.0).astype(BF16)
    prior = jnp.dot(before, jnp.where(onehot, 1.0, 0.0).astype(BF16),
                    preferred_element_type=F32) + run_ref[...]
    r1 = jnp.sum(jnp.where(lanef == i1, prior, 0.0), axis=-1, keepdims=True)
    r2 = jnp.sum(jnp.where(lanef == i2, prior, 0.0), axis=-1, keepdims=True)
    run_ref[...] += jnp.sum(jnp.where(onehot, 1.0, 0.0), axis=0, keepdims=True)

    fields = (i1 - ROUTE_LANE0, i2 - ROUTE_LANE0, w1, w2, r1, r2)
    route = jnp.zeros((TM_MIX, LANES), F32)
    for pos, val in enumerate(fields):
        route = jnp.where(lane_t == pos, val, route)
    return route


def _mix_kernel(x_ref, oa_ref, u_ref, vs_ref, ga_ref, gb_ref,
                wsp_ref, bsp_ref, wua_ref, wub_ref, wo_ref, gf_ref, wr_ref, br_ref,
                xo_ref, h_ref, route_ref, cnt_ref, run_ref, xprev_ref):
    step = pl.program_id(0)

    @pl.when(step == 0)
    def _():
        run_ref[...] = jnp.zeros_like(run_ref)
        xprev_ref[...] = jnp.zeros_like(xprev_ref)

    lane = lax.broadcasted_iota(jnp.int32, (CHUNK, LANES), 1)
    row = lax.broadcasted_iota(jnp.int32, (CHUNK, LANES), 0)
    first = lane < HEAD_DIM
    tril = lane <= row

    w_pairs = []
    for gp in range(WIDTH // LANES):
        w_pairs.append(jnp.concatenate(
            [jnp.where(tril, wsp_ref[g], 0.0).astype(BF16) for g in (2 * gp, 2 * gp + 1)],
            axis=1))
    ob_chunks = []
    for c in range(TM_MIX // CHUNK):
        rows = slice(c * CHUNK, (c + 1) * CHUNK)
        cols = []
        for gp in range(WIDTH // LANES):
            vpair = vs_ref[rows, gp * LANES:(gp + 1) * LANES]
            zero = jnp.zeros_like(vpair)
            stacked = jnp.concatenate([jnp.where(first, vpair, zero),
                                       jnp.where(first, zero, vpair)], axis=0)
            cols.append(jnp.dot(w_pairs[gp], stacked, preferred_element_type=F32))
        mixed = jnp.concatenate(cols, axis=-1) + bsp_ref[...]
        ob_chunks.append((u_ref[rows, :].astype(F32) * mixed).astype(BF16))
    ob = jnp.concatenate(ob_chunks, axis=0)

    h = _rms(xprev_ref[...], gf_ref[...])
    _store_row_tiles(h_ref, h)
    choice = _choose_tile(h.astype(BF16), wr_ref, br_ref)

    up_a = jnp.dot(oa_ref[...], wua_ref[...], preferred_element_type=F32)
    up_b = jnp.dot(ob, wub_ref[...], preferred_element_type=F32)
    merged = ga_ref[...].astype(F32) * up_a + gb_ref[...].astype(F32) * up_b
    x = x_ref[...] + jnp.dot(merged.astype(BF16), wo_ref[...], preferred_element_type=F32)
    xo_ref[...] = x
    xprev_ref[...] = x

    route_ref[...] = _rank_tile(choice, run_ref, step > 0)
    cnt_ref[...] = run_ref[...]


def _mix(layer, x, oa, u, vs, ga, gb, wsp, bsp, wua, wub, wo, gf, wr, br):
    n = x.shape[0]
    tiles = n // TM_MIX
    cur = lambda i: jnp.minimum(i, tiles - 1)
    prev = lambda i: jnp.maximum(i - 1, 0)
    row = lambda width: pl.BlockSpec((TM_MIX, width), lambda i: (cur(i), 0))
    lay = lambda *shape: pl.BlockSpec((None,) + shape, lambda i: (layer,) + (0,) * len(shape))
    return pl.pallas_call(
        _mix_kernel,
        grid=(tiles + 1,),
        in_specs=[row(D_MODEL), row(WIDTH), row(WIDTH), row(WIDTH), row(D_MODEL), row(D_MODEL),
                  lay(HEADS, CHUNK, CHUNK), lay(CHUNK, WIDTH), lay(WIDTH, D_MODEL),
                  lay(WIDTH, D_MODEL), lay(D_MODEL, D_MODEL), lay(1, D_MODEL),
                  lay(D_MODEL, LANES), lay(1, LANES)],
        out_specs=[row(D_MODEL),
                   pl.BlockSpec((TM_MIX * ROW_TILE, LANES), lambda i: (prev(i), 0)),
                   pl.BlockSpec((TM_MIX, LANES), lambda i: (prev(i), 0)),
                   pl.BlockSpec((1, LANES), lambda i: (0, 0))],
        out_shape=[jax.ShapeDtypeStruct((n, D_MODEL), F32),
                   jax.ShapeDtypeStruct((n * ROW_TILE, LANES), U32),
                   jax.ShapeDtypeStruct((n, LANES), F32),
                   jax.ShapeDtypeStruct((1, LANES), F32)],
        scratch_shapes=[pltpu.VMEM((1, LANES), F32), pltpu.VMEM((TM_MIX, D_MODEL), F32)],
        compiler_params=pltpu.CompilerParams(
            dimension_semantics=("arbitrary",), vmem_limit_bytes=VMEM_LIMIT),
        name="mix",
    )(x, oa, u, vs, ga, gb, wsp, bsp, wua, wub, wo, gf, wr, br)


def _dispatch_kernel(dest_ref, zrow_ref, h_ref, xs_ref, zero_ref, sem, zsem, tsem):
    i = pl.program_id(0)
    block_rows = MOE_ROWS * ROW_TILE
    nblk = xs_ref.shape[0] // block_rows
    nused = zrow_ref[N_EXPERTS]

    def zero_copy(first_row, zero_sem):
        return pltpu.make_async_copy(zero_ref, xs_ref.at[pl.ds(first_row, block_rows), :],
                                     zero_sem)

    def tail_copy(j):
        return zero_copy((nused + j) * block_rows, tsem)

    @pl.when(i == 0)
    def _():
        zero_ref[...] = jnp.zeros_like(zero_ref)
        for j in range(N_EXPERTS):
            pl.when(nused + j < nblk)(lambda j=j: tail_copy(j).start())
        for e in range(N_EXPERTS):
            pl.when(zrow_ref[e] >= 0)(lambda e=e: zero_copy(zrow_ref[e], zsem).start())
        for e in range(N_EXPERTS):
            pl.when(zrow_ref[e] >= 0)(lambda e=e: zero_copy(zrow_ref[e], zsem).wait())

    n = dest_ref.shape[0] // 2
    base = i * TM_DISP
    for t in range(TM_DISP):
        for k in range(2):
            d = dest_ref[k * n + base + t]
            pltpu.make_async_copy(h_ref.at[pl.ds(t * ROW_TILE, ROW_TILE), :],
                                  xs_ref.at[pl.ds(d * ROW_TILE, ROW_TILE), :],
                                  sem).start(priority=k)
    for k in range(2):
        pltpu.make_async_copy(h_ref, xs_ref.at[pl.ds(0, TM_DISP * ROW_TILE), :], sem).wait()

    @pl.when(i == pl.num_programs(0) - 1)
    def _():
        for j in range(N_EXPERTS):
            pl.when(nused + j < nblk)(lambda j=j: tail_copy(j).wait())


def _dispatch(dest, zrow, h, cap):
    n = h.shape[0] // ROW_TILE
    return pl.pallas_call(
        _dispatch_kernel,
        grid_spec=pltpu.PrefetchScalarGridSpec(
            num_scalar_prefetch=2,
            grid=(n // TM_DISP,),
            in_specs=[pl.BlockSpec((TM_DISP * ROW_TILE, LANES), lambda i, dest, zrow: (i, 0))],
            out_specs=pl.BlockSpec(memory_space=pl.ANY),
            scratch_shapes=[pltpu.VMEM((MOE_ROWS * ROW_TILE, LANES), U32),
                            pltpu.SemaphoreType.DMA(()), pltpu.SemaphoreType.DMA(()),
                            pltpu.SemaphoreType.DMA(())],
        ),
        out_shape=jax.ShapeDtypeStruct((cap * ROW_TILE, LANES), U32),
        compiler_params=pltpu.CompilerParams(
            dimension_semantics=("arbitrary",), vmem_limit_bytes=VMEM_LIMIT),
        name="dispatch",
    )(dest, zrow, h)


def _expert_kernel(layer, blk_e_ref, nused_ref, half_ref, next_ref, xs_ref, wi_hbm, wo_hbm,
                   ys_ref, wi_f, wo_f, wi_b, wo_b, sem):
    b = pl.program_id(0)
    used = b < nused_ref[0]
    expert = blk_e_ref[b]

    def fetch(e, half):
        return (pltpu.make_async_copy(wi_hbm.at[layer, e], wi_f.at[half], sem.at[half, 0]),
                pltpu.make_async_copy(wo_hbm.at[layer, e], wo_f.at[half], sem.at[half, 1]))

    @pl.when(b == 0)
    def _():
        for copy in fetch(expert, half_ref[expert]):
            copy.start()

    @pl.when(jnp.logical_not(used))
    def _():
        ys_ref[...] = jnp.zeros_like(ys_ref)

    new_expert = jnp.logical_or(b == 0, expert != blk_e_ref[jnp.maximum(b - 1, 0)])

    @pl.when(jnp.logical_and(used, new_expert))
    def _():
        half = half_ref[expert]
        for copy in fetch(expert, half):
            copy.wait()
        following = next_ref[expert]

        @pl.when(following >= 0)
        def _():
            for copy in fetch(following, 1 - half):
                copy.start(priority=1)

        wi_b[...] = wi_f[half].astype(BF16)
        wo_b[...] = wo_f[half].astype(BF16)

    @pl.when(used)
    def _():
        xs = _load_row_tiles(xs_ref, MOE_ROWS).astype(BF16)
        gu = jnp.dot(xs, wi_b[...], preferred_element_type=F32)
        act = jax.nn.silu(gu[:, :EXPERT_FF]) * gu[:, EXPERT_FF:]
        y = jnp.dot(act.astype(BF16), wo_b[...], preferred_element_type=F32)
        _store_row_tiles(ys_ref, y)


def _experts(layer, blk_e, nused, half, following, xs, w_e_in, w_e_out):
    cap = xs.shape[0] // ROW_TILE
    rows = MOE_ROWS * ROW_TILE
    return pl.pallas_call(
        functools.partial(_expert_kernel, layer),
        grid_spec=pltpu.PrefetchScalarGridSpec(
            num_scalar_prefetch=4,
            grid=(cap // MOE_ROWS,),
            in_specs=[pl.BlockSpec((rows, LANES),
                                   lambda b, blk_e, nused, *_: (jnp.minimum(b, nused[0] - 1), 0)),
                      pl.BlockSpec(memory_space=pl.ANY),
                      pl.BlockSpec(memory_space=pl.ANY)],
            out_specs=pl.BlockSpec((rows, LANES), lambda b, *_: (b, 0)),
            scratch_shapes=[pltpu.VMEM((2, D_MODEL, 2 * EXPERT_FF), F32),
                            pltpu.VMEM((2, EXPERT_FF, D_MODEL), F32),
                            pltpu.VMEM((D_MODEL, 2 * EXPERT_FF), BF16),
                            pltpu.VMEM((EXPERT_FF, D_MODEL), BF16),
                            pltpu.SemaphoreType.DMA((2, 2))],
        ),
        out_shape=jax.ShapeDtypeStruct((cap * ROW_TILE, LANES), U32),
        compiler_params=pltpu.CompilerParams(
            dimension_semantics=("arbitrary",), vmem_limit_bytes=VMEM_LIMIT),
        name="experts",
    )(blk_e, nused, half, following, xs, w_e_in, w_e_out)


def _ple_kernel(dest_ref, x_ref, route_ref, p_ref, ys_ref, gp_ref, wg_ref, wp_ref,
                xo_ref, y00, y01, y10, y11, sem):
    ybuf = ((y00, y01), (y10, y11))
    j = pl.program_id(0)
    last_step = pl.num_programs(0) - 1

    def start_gather(tile, half):
        n = dest_ref.shape[0] // 2
        base = tile * TM_ROW
        for t in range(TM_ROW):
            for k in range(2):
                d = dest_ref[k * n + base + t]
                pltpu.make_async_copy(ys_ref.at[pl.ds(d * ROW_TILE, ROW_TILE), :],
                                      ybuf[half][k].at[pl.ds(t * ROW_TILE, ROW_TILE), :],
                                      sem.at[half]).start(priority=k)

    def wait_gather(half):
        for k in range(2):
            pltpu.make_async_copy(ys_ref.at[pl.ds(0, TM_ROW * ROW_TILE), :],
                                  ybuf[half][k], sem.at[half]).wait()

    def combine(half):
        rows = slice(half * TM_ROW, (half + 1) * TM_ROW)
        pe = jnp.dot(p_ref[rows, :].astype(BF16), wp_ref[...], preferred_element_type=F32)
        wait_gather(half)
        route = route_ref[rows, :]
        y0 = _load_row_tiles(ybuf[half][0], TM_ROW)
        y1 = _load_row_tiles(ybuf[half][1], TM_ROW)
        x = x_ref[rows, :] + route[:, 2:3] * y0 + route[:, 3:4] * y1
        gate = jax.nn.sigmoid(jnp.dot(_rms(x, gp_ref[...]).astype(BF16), wg_ref[...],
                                      preferred_element_type=F32))
        xo_ref[rows, :] = x + gate * pe

    @pl.when(j == 0)
    def _():
        start_gather(0, 0)

    start_gather(2 * j + 1, 1)
    combine(0)
    start_gather(jnp.minimum(2 * j + 2, 2 * last_step + 1), 0)
    combine(1)

    @pl.when(j == last_step)
    def _():
        wait_gather(0)


def _ple(layer, dest, x, route, p, ys, gp, wg, wp):
    n = x.shape[0]
    row = lambda width: pl.BlockSpec((2 * TM_ROW, width), lambda j, dest: (j, 0))
    lay = lambda *shape: pl.BlockSpec((None,) + shape,
                                      lambda j, dest: (layer,) + (0,) * len(shape))
    return pl.pallas_call(
        _ple_kernel,
        grid_spec=pltpu.PrefetchScalarGridSpec(
            num_scalar_prefetch=1,
            grid=(n // (2 * TM_ROW),),
            in_specs=[row(D_MODEL), row(LANES),
                      pl.BlockSpec((None, 2 * TM_ROW, PLE_DIM), lambda j, dest: (layer, j, 0)),
                      pl.BlockSpec(memory_space=pl.ANY),
                      lay(1, D_MODEL), lay(D_MODEL, D_MODEL), lay(PLE_DIM, D_MODEL)],
            out_specs=row(D_MODEL),
            scratch_shapes=[pltpu.VMEM((TM_ROW * ROW_TILE, LANES), U32)] * 4
            + [pltpu.SemaphoreType.DMA((2,))],
        ),
        out_shape=jax.ShapeDtypeStruct((n, D_MODEL), F32),
        compiler_params=pltpu.CompilerParams(
            dimension_semantics=("arbitrary",), vmem_limit_bytes=VMEM_LIMIT),
        name="ple",
    )(dest, x, route, p, ys, gp, wg, wp)


def _slot_layout(route, counts, cap):
    cnt = counts[0, ROUTE_LANE0:ROUTE_LANE0 + N_EXPERTS].astype(jnp.int32)
    padded = (cnt + MOE_ROWS - 1) // MOE_ROWS * MOE_ROWS
    pend = jnp.cumsum(padded)
    poff = pend - padded
    ids = jnp.arange(N_EXPERTS, dtype=jnp.int32)

    def slots(k):
        eid = route[:, k].astype(jnp.int32)
        rank = route[:, 4 + k].astype(jnp.int32)
        return jnp.sum(jnp.where(eid[:, None] == ids, poff, 0), axis=-1) + rank

    dest = jnp.concatenate([slots(0), slots(1)]).astype(jnp.int32)
    blk_start = jnp.arange(cap // MOE_ROWS, dtype=jnp.int32) * MOE_ROWS
    blk_e = jnp.minimum(jnp.sum(pend[None, :] <= blk_start[:, None], axis=1), N_EXPERTS - 1)
    nused = (pend[-1:] // MOE_ROWS).astype(jnp.int32)
    zrow = jnp.where(cnt > 0, (pend - MOE_ROWS) * ROW_TILE, -1).astype(jnp.int32)
    nonempty = cnt > 0
    half = ((jnp.cumsum(nonempty) - 1) % 2).astype(jnp.int32)
    later = jnp.logical_and(nonempty[None, :], ids[None, :] > ids[:, None])
    following = jnp.min(jnp.where(later, ids[None, :], N_EXPERTS), axis=1)
    following = jnp.where(following < N_EXPERTS, following, -1).astype(jnp.int32)
    return (dest, jnp.concatenate([zrow, nused]), blk_e.astype(jnp.int32), nused, half,
            following)


def kernel(x, p, norm_mix, w_in, q_norm, k_norm, sgu_norm, w_spatial, b_spatial, w_up_a, w_up_b,
           w_out, norm_ffn, w_group_router, b_group_router, w_expert_router, b_expert_router,
           w_expert_in, w_expert_out, norm_ple, w_ple_gate, w_ple_proj):
    batch, seq, d = x.shape
    depth = w_in.shape[0]
    n = batch * seq
    cap = 2 * n + N_EXPERTS * MOE_ROWS

    vec = lambda a: a[:, None, :]
    w_ua_b, w_ub_b, w_o_b = (a.astype(BF16) for a in (w_up_a, w_up_b, w_out))
    w_pg_b, w_pp_b = w_ple_gate.astype(BF16), w_ple_proj.astype(BF16)
    qn2, kn2 = vec(jnp.tile(q_norm, (1, 2))), vec(jnp.tile(k_norm, (1, 2)))
    b_sp = jnp.repeat(jnp.swapaxes(b_spatial, 1, 2), HEAD_DIM, axis=2)
    pad = LANES - N_GROUPS - N_EXPERTS
    w_r = jnp.pad(jnp.concatenate([w_group_router, w_expert_router], axis=2),
                  ((0, 0), (0, 0), (0, pad))).astype(BF16)
    b_r = vec(jnp.pad(jnp.concatenate([b_group_router, b_expert_router], axis=1),
                      ((0, 0), (0, pad))))
    p2 = p.reshape(depth, n, PLE_DIM)

    xf = x.reshape(n, d)
    for i in range(depth):
        q, k, v, u, vs, ga, gb = _inproj(i, xf, vec(norm_mix), w_in, qn2, kn2, vec(sgu_norm))
        oa = _attention(q, k, v, batch, seq)
        xf, h, route, counts = _mix(i, xf, oa, u, vs, ga, gb, w_spatial, b_sp, w_ua_b, w_ub_b,
                                    w_o_b, vec(norm_ffn), w_r, b_r)
        dest, zrow, blk_e, nused, half, following = _slot_layout(route, counts, cap)
        xs = _dispatch(dest, zrow, h, cap)
        ys = _experts(i, blk_e, nused, half, following, xs, w_expert_in, w_expert_out)
        xf = _ple(i, dest, xf, route, p2, ys, vec(norm_ple), w_pg_b, w_pp_b)
    return xf.reshape(batch, seq, d)
```

```python
import functools
import math

import jax
import jax.numpy as jnp
from jax import lax
from jax.experimental import pallas as pl
from jax.experimental.pallas import tpu as pltpu

F32 = jnp.float32
BF16 = jnp.bfloat16

D_MODEL = 1024
HEADS = 8
HEAD_DIM = 64
WIDTH = HEADS * HEAD_DIM
CHUNK = 128
IN_WIDTH = 5 * WIDTH + 2 * D_MODEL
N_GROUPS = 4
EXPERTS_PER_GROUP = 8
N_EXPERTS = N_GROUPS * EXPERTS_PER_GROUP
EXPERT_FF = 512
PLE_DIM = 256
EPS = 1e-6

LANES = 128
ROW_TILE = D_MODEL // (2 * LANES)
U32 = jnp.uint32
ROUTE_LANE0 = N_GROUPS
TM_IN = 256
TM_MIX = 256
TM_DISP = 512
TM_ROW = 256
MOE_ROWS = 256
LOG2E = 1.4426950408889634
DEAD_LOG2 = -105.0 * LOG2E
VMEM_LIMIT = 48 * 1024 * 1024


def _rms(x, gain):
    return x * lax.rsqrt(jnp.mean(x * x, axis=-1, keepdims=True) + EPS) * gain


def _store_row_tiles(ref, value, lead=()):
    rows = value.shape[0]
    bits = lambda v: lax.bitcast_convert_type(v.astype(BF16).astype(F32), U32)
    for c in range(ROW_TILE):
        low = bits(value[:, 2 * c * LANES:(2 * c + 1) * LANES]) >> 16
        high = bits(value[:, (2 * c + 1) * LANES:(2 * c + 2) * LANES]) & U32(0xFFFF0000)
        ref[lead + (pl.ds(c, rows, stride=ROW_TILE), slice(None))] = low | high


def _load_row_tiles(ref, rows, lead=()):
    chunks = []
    for c in range(ROW_TILE):
        words = ref[lead + (pl.ds(c, rows, stride=ROW_TILE), slice(None))]
        chunks.append(lax.bitcast_convert_type(words << 16, F32))
        chunks.append(lax.bitcast_convert_type(words & U32(0xFFFF0000), F32))
    return jnp.concatenate(chunks, axis=1)


def _inproj_kernel(x_ref, g_ref, wf_ref, qn_ref, kn_ref, sn_ref,
                   q_ref, k_ref, v_ref, u_ref, vs_ref, ga_ref, gb_ref, w_ref):
    @pl.when(pl.program_id(0) == 0)
    def _():
        w_ref[...] = wf_ref[...].astype(BF16)

    hb = _rms(x_ref[...], g_ref[...]).astype(BF16)

    def proj(lo, hi):
        return jnp.dot(hb, w_ref[:, lo:hi], preferred_element_type=F32)

    first = lax.broadcasted_iota(jnp.int32, (1, LANES), 1) < HEAD_DIM

    def head_norm(z, gain, scale):
        outs = []
        for c in range(WIDTH // LANES):
            zc = z[:, c * LANES:(c + 1) * LANES]
            sq = zc * zc
            sa = jnp.sum(jnp.where(first, sq, 0.0), axis=-1, keepdims=True)
            sb = jnp.sum(jnp.where(first, 0.0, sq), axis=-1, keepdims=True)
            ms = jnp.where(first, sa, sb) * (1.0 / HEAD_DIM)
            outs.append(zc * lax.rsqrt(ms + EPS) * (gain * scale))
        return jnp.concatenate(outs, axis=-1)

    w = WIDTH
    q_ref[...] = head_norm(proj(0, w), qn_ref[...], LOG2E / math.sqrt(HEAD_DIM)).astype(BF16)
    k_ref[...] = head_norm(proj(w, 2 * w), kn_ref[...], 1.0).astype(BF16)
    v_ref[...] = proj(2 * w, 3 * w).astype(BF16)
    u_ref[...] = jax.nn.gelu(proj(3 * w, 4 * w)).astype(BF16)
    vs_ref[...] = _rms(jax.nn.gelu(proj(4 * w, 5 * w)), sn_ref[...]).astype(BF16)
    ga_ref[...] = jax.nn.sigmoid(proj(5 * w, 5 * w + D_MODEL)).astype(BF16)
    gb_ref[...] = jax.nn.sigmoid(proj(5 * w + D_MODEL, IN_WIDTH)).astype(BF16)


def _inproj(layer, x, norm_mix, w_in, qn, kn, sn):
    n = x.shape[0]
    row = lambda width: pl.BlockSpec((TM_IN, width), lambda i: (i, 0))
    vec = lambda width: pl.BlockSpec((None, 1, width), lambda i: (layer, 0, 0))
    out = lambda width: jax.ShapeDtypeStruct((n, width), BF16)
    return pl.pallas_call(
        _inproj_kernel,
        grid=(n // TM_IN,),
        in_specs=[row(D_MODEL), vec(D_MODEL),
                  pl.BlockSpec((None, D_MODEL, IN_WIDTH), lambda i: (layer, 0, 0),
                               pipeline_mode=pl.Buffered(1)),
                  vec(LANES), vec(LANES), vec(WIDTH)],
        out_specs=[row(WIDTH)] * 5 + [row(D_MODEL)] * 2,
        out_shape=[out(WIDTH)] * 5 + [out(D_MODEL)] * 2,
        scratch_shapes=[pltpu.VMEM((D_MODEL, IN_WIDTH), BF16)],
        compiler_params=pltpu.CompilerParams(
            dimension_semantics=("arbitrary",), vmem_limit_bytes=VMEM_LIMIT),
        name="inproj",
    )(x, norm_mix, w_in, qn, kn, sn)


def _attn_kernel(q_ref, k_ref, v_ref, o_ref, carry_ref, acc_ref, z_ref):
    qb = pl.program_id(1)
    pairs = WIDTH // LANES
    lane = lax.broadcasted_iota(jnp.int32, (2 * CHUNK, LANES), 1)
    row = lax.broadcasted_iota(jnp.int32, (2 * CHUNK, LANES), 0)
    upper = row < CHUNK
    query = jnp.where(upper, row, row - CHUNK)
    own = jnp.logical_xor(lane < HEAD_DIM, jnp.logical_not(upper))
    causal = lane < query
    first = lax.broadcasted_iota(jnp.int32, (CHUNK, LANES), 1) < HEAD_DIM
    kr = lax.broadcasted_iota(jnp.int32, (LANES, 2 * LANES), 0)
    kc = lax.broadcasted_iota(jnp.int32, (LANES, 2 * LANES), 1)
    tail = jnp.where(jnp.logical_or(kr > kc, kc >= LANES), 1.0, 0.0).astype(BF16)
    sign = jnp.uint32(0x80000000)

    cols = [slice(p * LANES, (p + 1) * LANES) for p in range(pairs)]
    q2s = []
    for p in range(pairs):
        qp = q_ref[:, cols[p]]
        q2 = jnp.concatenate([qp, qp], axis=0)
        q2s.append(jnp.where(own, q2, jnp.zeros_like(q2)))

    def scores(j):
        start = pl.multiple_of(j * CHUNK, CHUNK)
        return [lax.dot_general(q2s[p], k_ref[pl.ds(start, CHUNK), cols[p]],
                                (((1,), (1,)), ((), ())), preferred_element_type=F32)
                for p in range(pairs)]

    def sweep(j, count, diagonal):
        blocks = range(count)
        starts = [pl.multiple_of((j - b) * CHUNK, CHUNK) for b in blocks]
        zs = [scores(j - b) if diagonal else None for b in blocks]
        if not diagonal:
            zs[0] = [z_ref[p] for p in range(pairs)]
        for p, z in enumerate(scores(jnp.maximum(j - count, 0))):
            z_ref[p] = z
        log_betas, splits = {}, {}
        for b in blocks:
            for p in range(pairs):
                z = zs[b][p]
                neg_abs = lax.bitcast_convert_type(
                    lax.bitcast_convert_type(z, jnp.uint32) | sign, F32)
                log1p = jnp.log(1.0 + jnp.exp2(neg_abs)) * LOG2E
                log_beta = jnp.minimum(z, 0.0) - log1p
                log_keep = log_beta - z
                if diagonal and b == 0:
                    log_keep = jnp.where(causal, log_keep, 0.0)
                log_betas[b, p] = log_beta
                splits[b, p] = log_keep.astype(BF16)
        sums = {key: jnp.dot(s, tail, preferred_element_type=F32) for key, s in splits.items()}
        carries = {(0, p): 0.0 if diagonal else carry_ref[p] for p in range(pairs)}
        for b in blocks:
            for p in range(pairs):
                carries[b + 1, p] = carries[b, p] + sums[b, p][:, LANES:]
        final = [carries[count, p] for p in range(pairs)]
        live = jnp.max(functools.reduce(jnp.maximum, final)) > DEAD_LOG2
        outs = {}
        for b in blocks:
            for p in range(pairs):
                a = jnp.exp2(log_betas[b, p] + sums[b, p][:, :LANES] + carries[b, p])
                if diagonal and b == 0:
                    a = jnp.where(causal, a, 0.0)
                outs[b, p] = jnp.dot(a.astype(BF16), v_ref[pl.ds(starts[b], CHUNK), cols[p]],
                                     preferred_element_type=F32)
        for p in range(pairs):
            carry_ref[p] = final[p]
        for p in range(pairs):
            o2 = functools.reduce(jnp.add, [outs[b, p] for b in blocks])
            o = jnp.where(first, o2[:CHUNK], o2[CHUNK:])
            acc_ref[:, cols[p]] = o if diagonal else acc_ref[:, cols[p]] + o
        return live

    def cond(state):
        j, live = state
        return jnp.logical_and(j >= 0, live)

    def body(state):
        j, _ = state
        return j - 1, sweep(j, 1, False)

    for count in (1, 2):
        @pl.when(qb == count - 1)
        def _(count=count):
            sweep(qb, count, True)

    @pl.when(qb >= 2)
    def _():
        lax.while_loop(cond, body, (qb - 3, sweep(qb, 3, True)))

    o_ref[...] = acc_ref[...].astype(BF16)


def _attention(q, k, v, batch, seq):
    q3, k3, v3 = (t.reshape(batch, seq, WIDTH) for t in (q, k, v))
    blk = pl.BlockSpec((None, CHUNK, WIDTH), lambda b, qb: (b, qb, 0))
    full = pl.BlockSpec((None, seq, WIDTH), lambda b, qb: (b, 0, 0), pipeline_mode=pl.Buffered(1))
    o = pl.pallas_call(
        _attn_kernel,
        grid=(batch, seq // CHUNK),
        in_specs=[blk, full, full],
        out_specs=blk,
        out_shape=jax.ShapeDtypeStruct((batch, seq, WIDTH), BF16),
        scratch_shapes=[pltpu.VMEM((WIDTH // LANES, 2 * CHUNK, LANES), F32),
                        pltpu.VMEM((CHUNK, WIDTH), F32),
                        pltpu.VMEM((WIDTH // LANES, 2 * CHUNK, LANES), F32)],
        compiler_params=pltpu.CompilerParams(
            dimension_semantics=("arbitrary", "arbitrary"), vmem_limit_bytes=VMEM_LIMIT),
        name="attn",
    )(q3, k3, v3)
    return o.reshape(batch * seq, WIDTH)


def _choose_tile(hb, wr_ref, br_ref):
    logits = jnp.dot(hb, wr_ref[...], preferred_element_type=F32) + br_ref[...]
    lane_t = lax.broadcasted_iota(jnp.int32, (TM_MIX, LANES), 1)
    lanef = lane_t.astype(F32)
    neg = -jnp.inf
    far = float(LANES)

    def first_max(vals):
        m = jnp.max(vals, axis=-1, keepdims=True)
        idx = jnp.min(jnp.where(vals == m, lanef, far), axis=-1, keepdims=True)
        return m, idx

    gl = jnp.where(lane_t < N_GROUPS, logits, neg)
    gmax, grp = first_max(gl)
    grp_w = 1.0 / jnp.sum(jnp.exp(gl - gmax), axis=-1, keepdims=True)
    lo_lane = ROUTE_LANE0 + EXPERTS_PER_GROUP * grp
    in_group = jnp.logical_and(lanef >= lo_lane, lanef < lo_lane + EXPERTS_PER_GROUP)
    el = jnp.where(in_group, logits, neg)
    m1, i1 = first_max(el)
    m2, i2 = first_max(jnp.where(lanef == i1, neg, el))
    e21 = jnp.exp(m2 - m1)
    w1 = grp_w / (1.0 + e21)
    w2 = w1 * e21
    return i1, i2, w1, w2


def _rank_tile(choice, run_ref, valid):
    i1, i2, w1, w2 = choice
    lane_t = lax.broadcasted_iota(jnp.int32, (TM_MIX, LANES), 1)
    lanef = lane_t.astype(F32)
    onehot = jnp.logical_and(jnp.logical_or(lanef == i1, lanef == i2), valid)
    rt = lax.broadcasted_iota(jnp.int32, (TM_MIX, TM_MIX), 0)
    ct = lax.broadcasted_iota(jnp.int32, (TM_MIX, TM_MIX), 1)
    before = jnp.where(ct < rt, 1.0, 0.0).astype(BF16)
    prior = jnp.dot(before, jnp.where(onehot, 1.0, 0.0).astype(BF16),
                    preferred_element_type=F32) + run_ref[...]
    r1 = jnp.sum(jnp.where(lanef == i1, prior, 0.0), axis=-1, keepdims=True)
    r2 = jnp.sum(jnp.where(lanef == i2, prior, 0.0), axis=-1, keepdims=True)
    run_ref[...] += jnp.sum(jnp.where(onehot, 1.0, 0.0), axis=0, keepdims=True)

    fields = (i1 - ROUTE_LANE0, i2 - ROUTE_LANE0, w1, w2, r1, r2)
    route = jnp.zeros((TM_MIX, LANES), F32)
    for pos, val in enumerate(fields):
        route = jnp.where(lane_t == pos, val, route)
    return route


def _mix_kernel(x_ref, oa_ref, u_ref, vs_ref, ga_ref, gb_ref,
                wsp_ref, bsp_ref, wua_ref, wub_ref, wo_ref, gf_ref, wr_ref, br_ref,
                xo_ref, h_ref, route_ref, cnt_ref, run_ref, xprev_ref):
    step = pl.program_id(0)

    @pl.when(step == 0)
    def _():
        run_ref[...] = jnp.zeros_like(run_ref)
        xprev_ref[...] = jnp.zeros_like(xprev_ref)

    lane = lax.broadcasted_iota(jnp.int32, (CHUNK, LANES), 1)
    row = lax.broadcasted_iota(jnp.int32, (CHUNK, LANES), 0)
    first = lane < HEAD_DIM
    tril = lane <= row

    w_pairs = []
    for gp in range(WIDTH // LANES):
        w_pairs.append(jnp.concatenate(
            [jnp.where(tril, wsp_ref[g], 0.0).astype(BF16) for g in (2 * gp, 2 * gp + 1)],
            axis=1))
    ob_chunks = []
    for c in range(TM_MIX // CHUNK):
        rows = slice(c * CHUNK, (c + 1) * CHUNK)
        cols = []
        for gp in range(WIDTH // LANES):
            vpair = vs_ref[rows, gp * LANES:(gp + 1) * LANES]
            zero = jnp.zeros_like(vpair)
            stacked = jnp.concatenate([jnp.where(first, vpair, zero),
                                       jnp.where(first, zero, vpair)], axis=0)
            cols.append(jnp.dot(w_pairs[gp], stacked, preferred_element_type=F32))
        mixed = jnp.concatenate(cols, axis=-1) + bsp_ref[...]
        ob_chunks.append((u_ref[rows, :].astype(F32) * mixed).astype(BF16))
    ob = jnp.concatenate(ob_chunks, axis=0)

    h = _rms(xprev_ref[...], gf_ref[...])
    _store_row_tiles(h_ref, h)
    choice = _choose_tile(h.astype(BF16), wr_ref, br_ref)

    up_a = jnp.dot(oa_ref[...], wua_ref[...], preferred_element_type=F32)
    up_b = jnp.dot(ob, wub_ref[...], preferred_element_type=F32)
    merged = ga_ref[...].astype(F32) * up_a + gb_ref[...].astype(F32) * up_b
    x = x_ref[...] + jnp.dot(merged.astype(BF16), wo_ref[...], preferred_element_type=F32)
    xo_ref[...] = x
    xprev_ref[...] = x

    route_ref[...] = _rank_tile(choice, run_ref, step > 0)
    cnt_ref[...] = run_ref[...]


def _mix(layer, x, oa, u, vs, ga, gb, wsp, bsp, wua, wub, wo, gf, wr, br):
    n = x.shape[0]
    tiles = n // TM_MIX
    cur = lambda i: jnp.minimum(i, tiles - 1)
    prev = lambda i: jnp.maximum(i - 1, 0)
    row = lambda width: pl.BlockSpec((TM_MIX, width), lambda i: (cur(i), 0))
    lay = lambda *shape: pl.BlockSpec((None,) + shape, lambda i: (layer,) + (0,) * len(shape))
    return pl.pallas_call(
        _mix_kernel,
        grid=(tiles + 1,),
        in_specs=[row(D_MODEL), row(WIDTH), row(WIDTH), row(WIDTH), row(D_MODEL), row(D_MODEL),
                  lay(HEADS, CHUNK, CHUNK), lay(CHUNK, WIDTH), lay(WIDTH, D_MODEL),
                  lay(WIDTH, D_MODEL), lay(D_MODEL, D_MODEL), lay(1, D_MODEL),
                  lay(D_MODEL, LANES), lay(1, LANES)],
        out_specs=[row(D_MODEL),
                   pl.BlockSpec((TM_MIX * ROW_TILE, LANES), lambda i: (prev(i), 0)),
                   pl.BlockSpec((TM_MIX, LANES), lambda i: (prev(i), 0)),
                   pl.BlockSpec((1, LANES), lambda i: (0, 0))],
        out_shape=[jax.ShapeDtypeStruct((n, D_MODEL), F32),
                   jax.ShapeDtypeStruct((n * ROW_TILE, LANES), U32),
                   jax.ShapeDtypeStruct((n, LANES), F32),
                   jax.ShapeDtypeStruct((1, LANES), F32)],
        scratch_shapes=[pltpu.VMEM((1, LANES), F32), pltpu.VMEM((TM_MIX, D_MODEL), F32)],
        compiler_params=pltpu.CompilerParams(
            dimension_semantics=("arbitrary",), vmem_limit_bytes=VMEM_LIMIT),
        name="mix",
    )(x, oa, u, vs, ga, gb, wsp, bsp, wua, wub, wo, gf, wr, br)


def _dispatch_kernel(dest_ref, zrow_ref, h_ref, xs_ref, zero_ref, sem, zsem, tsem):
    i = pl.program_id(0)
    block_rows = MOE_ROWS * ROW_TILE
    nblk = xs_ref.shape[0] // block_rows
    nused = zrow_ref[N_EXPERTS]

    def zero_copy(first_row, zero_sem):
        return pltpu.make_async_copy(zero_ref, xs_ref.at[pl.ds(first_row, block_rows), :],
                                     zero_sem)

    def tail_copy(j):
        return zero_copy((nused + j) * block_rows, tsem)

    @pl.when(i == 0)
    def _():
        zero_ref[...] = jnp.zeros_like(zero_ref)
        for j in range(N_EXPERTS):
            pl.when(nused + j < nblk)(lambda j=j: tail_copy(j).start())
        for e in range(N_EXPERTS):
            pl.when(zrow_ref[e] >= 0)(lambda e=e: zero_copy(zrow_ref[e], zsem).start())
        for e in range(N_EXPERTS):
            pl.when(zrow_ref[e] >= 0)(lambda e=e: zero_copy(zrow_ref[e], zsem).wait())

    n = dest_ref.shape[0] // 2
    base = i * TM_DISP
    for t in range(TM_DISP):
        for k in range(2):
            d = dest_ref[k * n + base + t]
            pltpu.make_async_copy(h_ref.at[pl.ds(t * ROW_TILE, ROW_TILE), :],
                                  xs_ref.at[pl.ds(d * ROW_TILE, ROW_TILE), :],
                                  sem).start(priority=k)
    for k in range(2):
        pltpu.make_async_copy(h_ref, xs_ref.at[pl.ds(0, TM_DISP * ROW_TILE), :], sem).wait()

    @pl.when(i == pl.num_programs(0) - 1)
    def _():
        for j in range(N_EXPERTS):
            pl.when(nused + j < nblk)(lambda j=j: tail_copy(j).wait())


def _dispatch(dest, zrow, h, cap):
    n = h.shape[0] // ROW_TILE
    return pl.pallas_call(
        _dispatch_kernel,
        grid_spec=pltpu.PrefetchScalarGridSpec(
            num_scalar_prefetch=2,
            grid=(n // TM_DISP,),
            in_specs=[pl.BlockSpec((TM_DISP * ROW_TILE, LANES), lambda i, dest, zrow: (i, 0))],
            out_specs=pl.BlockSpec(memory_space=pl.ANY),
            scratch_shapes=[pltpu.VMEM((MOE_ROWS * ROW_TILE, LANES), U32),
                            pltpu.SemaphoreType.DMA(()), pltpu.SemaphoreType.DMA(()),
                            pltpu.SemaphoreType.DMA(())],
        ),
        out_shape=jax.ShapeDtypeStruct((cap * ROW_TILE, LANES), U32),
        compiler_params=pltpu.CompilerParams(
            dimension_semantics=("arbitrary",), vmem_limit_bytes=VMEM_LIMIT),
        name="dispatch",
    )(dest, zrow, h)


def _expert_kernel(layer, blk_e_ref, nused_ref, half_ref, next_ref, xs_ref, wi_hbm, wo_hbm,
                   ys_ref, wi_f, wo_f, wi_b, wo_b, sem):
    b = pl.program_id(0)
    used = b < nused_ref[0]
    expert = blk_e_ref[b]

    def fetch(e, half):
        return (pltpu.make_async_copy(wi_hbm.at[layer, e], wi_f.at[half], sem.at[half, 0]),
                pltpu.make_async_copy(wo_hbm.at[layer, e], wo_f.at[half], sem.at[half, 1]))

    @pl.when(b == 0)
    def _():
        for copy in fetch(expert, half_ref[expert]):
            copy.start()

    @pl.when(jnp.logical_not(used))
    def _():
        ys_ref[...] = jnp.zeros_like(ys_ref)

    new_expert = jnp.logical_or(b == 0, expert != blk_e_ref[jnp.maximum(b - 1, 0)])

    @pl.when(jnp.logical_and(used, new_expert))
    def _():
        half = half_ref[expert]
        for copy in fetch(expert, half):
            copy.wait()
        following = next_ref[expert]

        @pl.when(following >= 0)
        def _():
            for copy in fetch(following, 1 - half):
                copy.start(priority=1)

        wi_b[...] = wi_f[half].astype(BF16)
        wo_b[...] = wo_f[half].astype(BF16)

    @pl.when(used)
    def _():
        xs = _load_row_tiles(xs_ref, MOE_ROWS).astype(BF16)
        gu = jnp.dot(xs, wi_b[...], preferred_element_type=F32)
        act = jax.nn.silu(gu[:, :EXPERT_FF]) * gu[:, EXPERT_FF:]
        y = jnp.dot(act.astype(BF16), wo_b[...], preferred_element_type=F32)
        _store_row_tiles(ys_ref, y)


def _experts(layer, blk_e, nused, half, following, xs, w_e_in, w_e_out):
    cap = xs.shape[0] // ROW_TILE
    rows = MOE_ROWS * ROW_TILE
    return pl.pallas_call(
        functools.partial(_expert_kernel, layer),
        grid_spec=pltpu.PrefetchScalarGridSpec(
            num_scalar_prefetch=4,
            grid=(cap // MOE_ROWS,),
            in_specs=[pl.BlockSpec((rows, LANES),
                                   lambda b, blk_e, nused, *_: (jnp.minimum(b, nused[0] - 1), 0)),
                      pl.BlockSpec(memory_space=pl.ANY),
                      pl.BlockSpec(memory_space=pl.ANY)],
            out_specs=pl.BlockSpec((rows, LANES), lambda b, *_: (b, 0)),
            scratch_shapes=[pltpu.VMEM((2, D_MODEL, 2 * EXPERT_FF), F32),
                            pltpu.VMEM((2, EXPERT_FF, D_MODEL), F32),
                            pltpu.VMEM((D_MODEL, 2 * EXPERT_FF), BF16),
                            pltpu.VMEM((EXPERT_FF, D_MODEL), BF16),
                            pltpu.SemaphoreType.DMA((2, 2))],
        ),
        out_shape=jax.ShapeDtypeStruct((cap * ROW_TILE, LANES), U32),
        compiler_params=pltpu.CompilerParams(
            dimension_semantics=("arbitrary",), vmem_limit_bytes=VMEM_LIMIT),
        name="experts",
    )(blk_e, nused, half, following, xs, w_e_in, w_e_out)


def _ple_kernel(dest_ref, x_ref, route_ref, p_ref, ys_ref, gp_ref, wg_ref, wp_ref,
                xo_ref, y00, y01, y10, y11, sem):
    ybuf = ((y00, y01), (y10, y11))
    j = pl.program_id(0)
    last_step = pl.num_programs(0) - 1

    def start_gather(tile, half):
        n = dest_ref.shape[0] // 2
        base = tile * TM_ROW
        for t in range(TM_ROW):
            for k in range(2):
                d = dest_ref[k * n + base + t]
                pltpu.make_async_copy(ys_ref.at[pl.ds(d * ROW_TILE, ROW_TILE), :],
                                      ybuf[half][k].at[pl.ds(t * ROW_TILE, ROW_TILE), :],
                                      sem.at[half]).start(priority=k)

    def wait_gather(half):
        for k in range(2):
            pltpu.make_async_copy(ys_ref.at[pl.ds(0, TM_ROW * ROW_TILE), :],
                                  ybuf[half][k], sem.at[half]).wait()

    def combine(half):
        rows = slice(half * TM_ROW, (half + 1) * TM_ROW)
        pe = jnp.dot(p_ref[rows, :].astype(BF16), wp_ref[...], preferred_element_type=F32)
        wait_gather(half)
        route = route_ref[rows, :]
        y0 = _load_row_tiles(ybuf[half][0], TM_ROW)
        y1 = _load_row_tiles(ybuf[half][1], TM_ROW)
        x = x_ref[rows, :] + route[:, 2:3] * y0 + route[:, 3:4] * y1
        gate = jax.nn.sigmoid(jnp.dot(_rms(x, gp_ref[...]).astype(BF16), wg_ref[...],
                                      preferred_element_type=F32))
        xo_ref[rows, :] = x + gate * pe

    @pl.when(j == 0)
    def _():
        start_gather(0, 0)

    start_gather(2 * j + 1, 1)
    combine(0)
    start_gather(jnp.minimum(2 * j + 2, 2 * last_step + 1), 0)
    combine(1)

    @pl.when(j == last_step)
    def _():
        wait_gather(0)


def _ple(layer, dest, x, route, p, ys, gp, wg, wp):
    n = x.shape[0]
    row = lambda width: pl.BlockSpec((2 * TM_ROW, width), lambda j, dest: (j, 0))
    lay = lambda *shape: pl.BlockSpec((None,) + shape,
                                      lambda j, dest: (layer,) + (0,) * len(shape))
    return pl.pallas_call(
        _ple_kernel,
        grid_spec=pltpu.PrefetchScalarGridSpec(
            num_scalar_prefetch=1,
            grid=(n // (2 * TM_ROW),),
            in_specs=[row(D_MODEL), row(LANES),
                      pl.BlockSpec((None, 2 * TM_ROW, PLE_DIM), lambda j, dest: (layer, j, 0)),
                      pl.BlockSpec(memory_space=pl.ANY),
                      lay(1, D_MODEL), lay(D_MODEL, D_MODEL), lay(PLE_DIM, D_MODEL)],
            out_specs=row(D_MODEL),
            scratch_shapes=[pltpu.VMEM((TM_ROW * ROW_TILE, LANES), U32)] * 4
            + [pltpu.SemaphoreType.DMA((2,))],
        ),
        out_shape=jax.ShapeDtypeStruct((n, D_MODEL), F32),
        compiler_params=pltpu.CompilerParams(
            dimension_semantics=("arbitrary",), vmem_limit_bytes=VMEM_LIMIT),
        name="ple",
    )(dest, x, route, p, ys, gp, wg, wp)


def _slot_layout(route, counts, cap):
    cnt = counts[0, ROUTE_LANE0:ROUTE_LANE0 + N_EXPERTS].astype(jnp.int32)
    padded = (cnt + MOE_ROWS - 1) // MOE_ROWS * MOE_ROWS
    pend = jnp.cumsum(padded)
    poff = pend - padded
    ids = jnp.arange(N_EXPERTS, dtype=jnp.int32)

    def slots(k):
        eid = route[:, k].astype(jnp.int32)
        rank = route[:, 4 + k].astype(jnp.int32)
        return jnp.sum(jnp.where(eid[:, None] == ids, poff, 0), axis=-1) + rank

    dest = jnp.concatenate([slots(0), slots(1)]).astype(jnp.int32)
    blk_start = jnp.arange(cap // MOE_ROWS, dtype=jnp.int32) * MOE_ROWS
    blk_e = jnp.minimum(jnp.sum(pend[None, :] <= blk_start[:, None], axis=1), N_EXPERTS - 1)
    nused = (pend[-1:] // MOE_ROWS).astype(jnp.int32)
    zrow = jnp.where(cnt > 0, (pend - MOE_ROWS) * ROW_TILE, -1).astype(jnp.int32)
    nonempty = cnt > 0
    half = ((jnp.cumsum(nonempty) - 1) % 2).astype(jnp.int32)
    later = jnp.logical_and(nonempty[None, :], ids[None, :] > ids[:, None])
    following = jnp.min(jnp.where(later, ids[None, :], N_EXPERTS), axis=1)
    following = jnp.where(following < N_EXPERTS, following, -1).astype(jnp.int32)
    return (dest, jnp.concatenate([zrow, nused]), blk_e.astype(jnp.int32), nused, half,
            following)


def kernel(x, p, norm_mix, w_in, q_norm, k_norm, sgu_norm, w_spatial, b_spatial, w_up_a, w_up_b,
           w_out, norm_ffn, w_group_router, b_group_router, w_expert_router, b_expert_router,
           w_expert_in, w_expert_out, norm_ple, w_ple_gate, w_ple_proj):
    batch, seq, d = x.shape
    depth = w_in.shape[0]
    n = batch * seq
    cap = 2 * n + N_EXPERTS * MOE_ROWS

    vec = lambda a: a[:, None, :]
    w_ua_b, w_ub_b, w_o_b = (a.astype(BF16) for a in (w_up_a, w_up_b, w_out))
    w_pg_b, w_pp_b = w_ple_gate.astype(BF16), w_ple_proj.astype(BF16)
    qn2, kn2 = vec(jnp.tile(q_norm, (1, 2))), vec(jnp.tile(k_norm, (1, 2)))
    b_sp = jnp.repeat(jnp.swapaxes(b_spatial, 1, 2), HEAD_DIM, axis=2)
    pad = LANES - N_GROUPS - N_EXPERTS
    w_r = jnp.pad(jnp.concatenate([w_group_router, w_expert_router], axis=2),
                  ((0, 0), (0, 0), (0, pad))).astype(BF16)
    b_r = vec(jnp.pad(jnp.concatenate([b_group_router, b_expert_router], axis=1),
                      ((0, 0), (0, pad))))
    p2 = p.reshape(depth, n, PLE_DIM)

    xf = x.reshape(n, d)
    for i in range(depth):
        q, k, v, u, vs, ga, gb = _inproj(i, xf, vec(norm_mix), w_in, qn2, kn2, vec(sgu_norm))
        oa = _attention(q, k, v, batch, seq)
        xf, h, route, counts = _mix(i, xf, oa, u, vs, ga, gb, w_spatial, b_sp, w_ua_b, w_ub_b,
                                    w_o_b, vec(norm_ffn), w_r, b_r)
        dest, zrow, blk_e, nused, half, following = _slot_layout(route, counts, cap)
        xs = _dispatch(dest, zrow, h, cap)
        ys = _experts(i, blk_e, nused, half, following, xs, w_expert_in, w_expert_out)
        xf = _ple(i, dest, xf, route, p2, ys, vec(norm_ple), w_pg_b, w_pp_b)
    return xf.reshape(batch, seq, d)
```

```python
import functools
import math

import jax
import jax.numpy as jnp
from jax import lax
from jax.experimental import pallas as pl
from jax.experimental.pallas import tpu as pltpu

F32 = jnp.float32
BF16 = jnp.bfloat16

D_MODEL = 1024
HEADS = 8
HEAD_DIM = 64
WIDTH = HEADS * HEAD_DIM
CHUNK = 128
IN_WIDTH = 5 * WIDTH + 2 * D_MODEL
N_GROUPS = 4
EXPERTS_PER_GROUP = 8
N_EXPERTS = N_GROUPS * EXPERTS_PER_GROUP
EXPERT_FF = 512
PLE_DIM = 256
EPS = 1e-6

LANES = 128
ROW_TILE = D_MODEL // (2 * LANES)
U32 = jnp.uint32
ROUTE_LANE0 = N_GROUPS
TM_IN = 256
TM_MIX = 256
TM_DISP = 512
TM_ROW = 256
MOE_ROWS = 256
TOP_ROWS = 32
LOG2E = 1.4426950408889634
DEAD_LOG2 = -105.0 * LOG2E
VMEM_LIMIT = 48 * 1024 * 1024


def _rms(x, gain):
    return x * lax.rsqrt(jnp.mean(x * x, axis=-1, keepdims=True) + EPS) * gain


def _store_row_tiles(ref, value, lead=()):
    rows = value.shape[0]
    bits = lambda v: lax.bitcast_convert_type(v.astype(BF16).astype(F32), U32)
    for c in range(ROW_TILE):
        low = bits(value[:, 2 * c * LANES:(2 * c + 1) * LANES]) >> 16
        high = bits(value[:, (2 * c + 1) * LANES:(2 * c + 2) * LANES]) & U32(0xFFFF0000)
        ref[lead + (pl.ds(c, rows, stride=ROW_TILE), slice(None))] = low | high


def _load_row_tiles(ref, rows, lead=()):
    chunks = []
    for c in range(ROW_TILE):
        words = ref[lead + (pl.ds(c, rows, stride=ROW_TILE), slice(None))]
        chunks.append(lax.bitcast_convert_type(words << 16, F32))
        chunks.append(lax.bitcast_convert_type(words & U32(0xFFFF0000), F32))
    return jnp.concatenate(chunks, axis=1)


def _inproj_kernel(x_ref, g_ref, wf_ref, qn_ref, kn_ref, sn_ref,
                   q_ref, k_ref, v_ref, u_ref, vs_ref, ga_ref, gb_ref, w_ref):
    @pl.when(pl.program_id(0) == 0)
    def _():
        w_ref[...] = wf_ref[...].astype(BF16)

    hb = _rms(x_ref[...], g_ref[...]).astype(BF16)

    def proj(lo, hi):
        return jnp.dot(hb, w_ref[:, lo:hi], preferred_element_type=F32)

    first = lax.broadcasted_iota(jnp.int32, (1, LANES), 1) < HEAD_DIM

    def head_norm(z, gain, scale):
        outs = []
        for c in range(WIDTH // LANES):
            zc = z[:, c * LANES:(c + 1) * LANES]
            sq = zc * zc
            sa = jnp.sum(jnp.where(first, sq, 0.0), axis=-1, keepdims=True)
            sb = jnp.sum(jnp.where(first, 0.0, sq), axis=-1, keepdims=True)
            ms = jnp.where(first, sa, sb) * (1.0 / HEAD_DIM)
            outs.append(zc * lax.rsqrt(ms + EPS) * (gain * scale))
        return jnp.concatenate(outs, axis=-1)

    w = WIDTH
    q_ref[...] = head_norm(proj(0, w), qn_ref[...], LOG2E / math.sqrt(HEAD_DIM)).astype(BF16)
    k_ref[...] = head_norm(proj(w, 2 * w), kn_ref[...], 1.0).astype(BF16)
    v_ref[...] = proj(2 * w, 3 * w).astype(BF16)
    u_ref[...] = jax.nn.gelu(proj(3 * w, 4 * w)).astype(BF16)
    vs_ref[...] = _rms(jax.nn.gelu(proj(4 * w, 5 * w)), sn_ref[...]).astype(BF16)
    ga_ref[...] = jax.nn.sigmoid(proj(5 * w, 5 * w + D_MODEL)).astype(BF16)
    gb_ref[...] = jax.nn.sigmoid(proj(5 * w + D_MODEL, IN_WIDTH)).astype(BF16)


def _inproj(layer, x, norm_mix, w_in, qn, kn, sn):
    n = x.shape[0]
    row = lambda width: pl.BlockSpec((TM_IN, width), lambda i: (i, 0))
    vec = lambda width: pl.BlockSpec((None, 1, width), lambda i: (layer, 0, 0))
    out = lambda width: jax.ShapeDtypeStruct((n, width), BF16)
    return pl.pallas_call(
        _inproj_kernel,
        grid=(n // TM_IN,),
        in_specs=[row(D_MODEL), vec(D_MODEL),
                  pl.BlockSpec((None, D_MODEL, IN_WIDTH), lambda i: (layer, 0, 0),
                               pipeline_mode=pl.Buffered(1)),
                  vec(LANES), vec(LANES), vec(WIDTH)],
        out_specs=[row(WIDTH)] * 5 + [row(D_MODEL)] * 2,
        out_shape=[out(WIDTH)] * 5 + [out(D_MODEL)] * 2,
        scratch_shapes=[pltpu.VMEM((D_MODEL, IN_WIDTH), BF16)],
        compiler_params=pltpu.CompilerParams(
            dimension_semantics=("arbitrary",), vmem_limit_bytes=VMEM_LIMIT),
        name="inproj",
    )(x, norm_mix, w_in, qn, kn, sn)


def _attn_kernel(q_ref, k_ref, v_ref, o_ref, carry_ref, acc_ref, z_ref):
    qb = pl.program_id(1)
    pairs = WIDTH // LANES
    first = lax.broadcasted_iota(jnp.int32, (CHUNK, LANES), 1) < HEAD_DIM
    first_top = lax.broadcasted_iota(jnp.int32, (TOP_ROWS, LANES), 1) < HEAD_DIM
    kr = lax.broadcasted_iota(jnp.int32, (LANES, 2 * LANES), 0)
    kc = lax.broadcasted_iota(jnp.int32, (LANES, 2 * LANES), 1)
    tail = jnp.where(jnp.logical_or(kr > kc, kc >= LANES), 1.0, 0.0).astype(BF16)
    sign = jnp.uint32(0x80000000)
    cols = [slice(p * LANES, (p + 1) * LANES) for p in range(pairs)]

    def stack(value, rows):
        if rows == CHUNK:
            return value
        return jnp.concatenate([value[:rows], value[CHUNK:CHUNK + rows]], axis=0)

    queries, causal = {}, {}
    for rows in (CHUNK, TOP_ROWS):
        lane = lax.broadcasted_iota(jnp.int32, (2 * rows, LANES), 1)
        row = lax.broadcasted_iota(jnp.int32, (2 * rows, LANES), 0)
        upper = row < rows
        own = jnp.logical_xor(lane < HEAD_DIM, jnp.logical_not(upper))
        causal[rows] = lane < jnp.where(upper, row, row - rows)
        queries[rows] = []
        for p in range(pairs):
            qp = q_ref[:rows, cols[p]]
            q2 = jnp.concatenate([qp, qp], axis=0)
            queries[rows].append(jnp.where(own, q2, jnp.zeros_like(q2)))
    full_row = lax.broadcasted_iota(jnp.int32, (2 * CHUNK, LANES), 0)
    is_top = jnp.where(full_row < CHUNK, full_row, full_row - CHUNK) < TOP_ROWS

    def scores(j, rows):
        start = pl.multiple_of(j * CHUNK, CHUNK)
        return [lax.dot_general(queries[rows][p], k_ref[pl.ds(start, CHUNK), cols[p]],
                                (((1,), (1,)), ((), ())), preferred_element_type=F32)
                for p in range(pairs)]

    def sweep(j, heights, diagonal=False, fresh=False, ahead=True, rest_only=False):
        blocks = range(len(heights))
        full = [b for b in blocks if heights[b] == CHUNK]
        top = [b for b in blocks if heights[b] == TOP_ROWS]
        starts = [pl.multiple_of((j - b) * CHUNK, CHUNK) for b in blocks]
        zs = []
        for b in blocks:
            if b > 0 or fresh or diagonal:
                zs.append(scores(j - b, heights[b]))
            else:
                zs.append([stack(z_ref[p], heights[b]) for p in range(pairs)])
        if ahead:
            for p, z in enumerate(scores(jnp.maximum(j - len(heights), 0), CHUNK)):
                z_ref[p] = z
        log_betas, splits = {}, {}
        for b in blocks:
            for p in range(pairs):
                z = zs[b][p]
                neg_abs = lax.bitcast_convert_type(
                    lax.bitcast_convert_type(z, jnp.uint32) | sign, F32)
                log1p = jnp.log(1.0 + jnp.exp2(neg_abs)) * LOG2E
                log_beta = jnp.minimum(z, 0.0) - log1p
                log_keep = log_beta - z
                if diagonal and b == 0:
                    log_keep = jnp.where(causal[heights[b]], log_keep, 0.0)
                if rest_only:
                    log_keep = jnp.where(is_top, 0.0, log_keep)
                log_betas[b, p] = log_beta
                splits[b, p] = log_keep.astype(BF16)
        sums = {key: jnp.dot(s, tail, preferred_element_type=F32) for key, s in splits.items()}
        entering, after_full, after_top = {}, [], []
        for p in range(pairs):
            carry = 0.0 if diagonal else carry_ref[p]
            for b in full:
                entering[b, p] = carry
                carry = carry + sums[b, p][:, LANES:]
            after_full.append(carry)
            carry = stack(carry, TOP_ROWS) if top else None
            for b in top:
                entering[b, p] = carry
                carry = carry + sums[b, p][:, LANES:]
            after_top.append(carry)
        alive = lambda c: jnp.max(c) > DEAD_LOG2
        worst = functools.reduce(jnp.maximum, after_full)
        live_rest = alive(jnp.where(is_top, -jnp.inf, worst)) if full else jnp.bool_(False)
        if top:
            live_top = alive(functools.reduce(jnp.maximum, after_top))
        else:
            live_top = alive(jnp.where(is_top, worst, -jnp.inf))
        outs = {}
        for b in blocks:
            for p in range(pairs):
                a = jnp.exp2(log_betas[b, p] + sums[b, p][:, :LANES] + entering[b, p])
                if diagonal and b == 0:
                    a = jnp.where(causal[heights[b]], a, 0.0)
                if rest_only:
                    a = jnp.where(is_top, 0.0, a)
                outs[b, p] = jnp.dot(a.astype(BF16), v_ref[pl.ds(starts[b], CHUNK), cols[p]],
                                     preferred_element_type=F32)
        for p in range(pairs):
            if full:
                carry_ref[p] = after_full[p]
            if top:
                carry_ref[p, 0:TOP_ROWS, :] = after_top[p][:TOP_ROWS]
                carry_ref[p, CHUNK:CHUNK + TOP_ROWS, :] = after_top[p][TOP_ROWS:]
        for p in range(pairs):
            if full:
                o2 = functools.reduce(jnp.add, [outs[b, p] for b in full])
                o = jnp.where(first, o2[:CHUNK], o2[CHUNK:])
                acc_ref[:, cols[p]] = o if diagonal else acc_ref[:, cols[p]] + o
            if top:
                o2 = functools.reduce(jnp.add, [outs[b, p] for b in top])
                o = jnp.where(first_top, o2[:TOP_ROWS], o2[TOP_ROWS:])
                acc_ref[0:TOP_ROWS, cols[p]] += o
        return live_top, live_rest

    def sweep_left(j, live_top, live_rest, height):
        def cond(state):
            j, live_top, live_rest = state
            return jnp.logical_and(j >= 0, jnp.logical_or(live_top, live_rest))

        def body(state):
            return (state[0] - 1,) + sweep(state[0], [height])

        lax.while_loop(cond, body, (j, live_top, live_rest))

    for count in (1, 2):
        @pl.when(qb == count - 1)
        def _(count=count):
            sweep(qb, [CHUNK] * count, diagonal=True)

    @pl.when(qb >= 2)
    def _():
        live_top, live_rest = sweep(qb, [CHUNK, CHUNK, TOP_ROWS], diagonal=True)

        @pl.when(live_rest)
        def _():
            flags = sweep(qb - 2, [CHUNK], fresh=True, ahead=False, rest_only=True)
            sweep_left(qb - 3, *flags, CHUNK)

        @pl.when(jnp.logical_not(live_rest))
        def _():
            sweep_left(qb - 3, live_top, jnp.bool_(False), TOP_ROWS)

    o_ref[...] = acc_ref[...].astype(BF16)


def _attention(q, k, v, batch, seq):
    q3, k3, v3 = (t.reshape(batch, seq, WIDTH) for t in (q, k, v))
    blk = pl.BlockSpec((None, CHUNK, WIDTH), lambda b, qb: (b, qb, 0))
    full = pl.BlockSpec((None, seq, WIDTH), lambda b, qb: (b, 0, 0), pipeline_mode=pl.Buffered(1))
    o = pl.pallas_call(
        _attn_kernel,
        grid=(batch, seq // CHUNK),
        in_specs=[blk, full, full],
        out_specs=blk,
        out_shape=jax.ShapeDtypeStruct((batch, seq, WIDTH), BF16),
        scratch_shapes=[pltpu.VMEM((WIDTH // LANES, 2 * CHUNK, LANES), F32),
                        pltpu.VMEM((CHUNK, WIDTH), F32),
                        pltpu.VMEM((WIDTH // LANES, 2 * CHUNK, LANES), F32)],
        compiler_params=pltpu.CompilerParams(
            dimension_semantics=("arbitrary", "arbitrary"), vmem_limit_bytes=VMEM_LIMIT),
        name="attn",
    )(q3, k3, v3)
    return o.reshape(batch * seq, WIDTH)


def _choose_tile(hb, wr_ref, br_ref):
    logits = jnp.dot(hb, wr_ref[...], preferred_element_type=F32) + br_ref[...]
    lane_t = lax.broadcasted_iota(jnp.int32, (TM_MIX, LANES), 1)
    lanef = lane_t.astype(F32)
    neg = -jnp.inf
    far = float(LANES)

    def first_max(vals):
        m = jnp.max(vals, axis=-1, keepdims=True)
        idx = jnp.min(jnp.where(vals == m, lanef, far), axis=-1, keepdims=True)
        return m, idx

    gl = jnp.where(lane_t < N_GROUPS, logits, neg)
    gmax, grp = first_max(gl)
    grp_w = 1.0 / jnp.sum(jnp.exp(gl - gmax), axis=-1, keepdims=True)
    lo_lane = ROUTE_LANE0 + EXPERTS_PER_GROUP * grp
    in_group = jnp.logical_and(lanef >= lo_lane, lanef < lo_lane + EXPERTS_PER_GROUP)
    el = jnp.where(in_group, logits, neg)
    m1, i1 = first_max(el)
    m2, i2 = first_max(jnp.where(lanef == i1, neg, el))
    e21 = jnp.exp(m2 - m1)
    w1 = grp_w / (1.0 + e21)
    w2 = w1 * e21
    return i1, i2, w1, w2


def _rank_tile(choice, run_ref, valid):
    i1, i2, w1, w2 = choice
    lane_t = lax.broadcasted_iota(jnp.int32, (TM_MIX, LANES), 1)
    lanef = lane_t.astype(F32)
    onehot = jnp.logical_and(jnp.logical_or(lanef == i1, lanef == i2), valid)
    rt = lax.broadcasted_iota(jnp.int32, (TM_MIX, TM_MIX), 0)
    ct = lax.broadcasted_iota(jnp.int32, (TM_MIX, TM_MIX), 1)
    before = jnp.where(ct < rt, 1.0, 0.0).astype(BF16)
    prior = jnp.dot(before, jnp.where(onehot, 1.0, 0.0).astype(BF16),
                    preferred_element_type=F32) + run_ref[...]
    r1 = jnp.sum(jnp.where(lanef == i1, prior, 0.0), axis=-1, keepdims=True)
    r2 = jnp.sum(jnp.where(lanef == i2, prior, 0.0), axis=-1, keepdims=True)
    run_ref[...] += jnp.sum(jnp.where(onehot, 1.0, 0.0), axis=0, keepdims=True)

    fields = (i1 - ROUTE_LANE0, i2 - ROUTE_LANE0, w1, w2, r1, r2)
    route = jnp.zeros((TM_MIX, LANES), F32)
    for pos, val in enumerate(fields):
        route = jnp.where(lane_t == pos, val, route)
    return route


def _mix_kernel(x_ref, oa_ref, u_ref, vs_ref, ga_ref, gb_ref,
                wsp_ref, bsp_ref, wua_ref, wub_ref, wo_ref, gf_ref, wr_ref, br_ref,
                xo_ref, h_ref, route_ref, cnt_ref, run_ref, xprev_ref):
    step = pl.program_id(0)

    @pl.when(step == 0)
    def _():
        run_ref[...] = jnp.zeros_like(run_ref)
        xprev_ref[...] = jnp.zeros_like(xprev_ref)

    lane = lax.broadcasted_iota(jnp.int32, (CHUNK, LANES), 1)
    row = lax.broadcasted_iota(jnp.int32, (CHUNK, LANES), 0)
    first = lane < HEAD_DIM
    tril = lane <= row

    w_pairs = []
    for gp in range(WIDTH // LANES):
        w_pairs.append(jnp.concatenate(
            [jnp.where(tril, wsp_ref[g], 0.0).astype(BF16) for g in (2 * gp, 2 * gp + 1)],
            axis=1))
    ob_chunks = []
    for c in range(TM_MIX // CHUNK):
        rows = slice(c * CHUNK, (c + 1) * CHUNK)
        cols = []
        for gp in range(WIDTH // LANES):
            vpair = vs_ref[rows, gp * LANES:(gp + 1) * LANES]
            zero = jnp.zeros_like(vpair)
            stacked = jnp.concatenate([jnp.where(first, vpair, zero),
                                       jnp.where(first, zero, vpair)], axis=0)
            cols.append(jnp.dot(w_pairs[gp], stacked, preferred_element_type=F32))
        mixed = jnp.concatenate(cols, axis=-1) + bsp_ref[...]
        ob_chunks.append((u_ref[rows, :].astype(F32) * mixed).astype(BF16))
    ob = jnp.concatenate(ob_chunks, axis=0)

    h = _rms(xprev_ref[...], gf_ref[...])
    _store_row_tiles(h_ref, h)
    choice = _choose_tile(h.astype(BF16), wr_ref, br_ref)

    up_a = jnp.dot(oa_ref[...], wua_ref[...], preferred_element_type=F32)
    up_b = jnp.dot(ob, wub_ref[...], preferred_element_type=F32)
    merged = ga_ref[...].astype(F32) * up_a + gb_ref[...].astype(F32) * up_b
    x = x_ref[...] + jnp.dot(merged.astype(BF16), wo_ref[...], preferred_element_type=F32)
    xo_ref[...] = x
    xprev_ref[...] = x

    route_ref[...] = _rank_tile(choice, run_ref, step > 0)
    cnt_ref[...] = run_ref[...]


def _mix(layer, x, oa, u, vs, ga, gb, wsp, bsp, wua, wub, wo, gf, wr, br):
    n = x.shape[0]
    tiles = n // TM_MIX
    cur = lambda i: jnp.minimum(i, tiles - 1)
    prev = lambda i: jnp.maximum(i - 1, 0)
    row = lambda width: pl.BlockSpec((TM_MIX, width), lambda i: (cur(i), 0))
    lay = lambda *shape: pl.BlockSpec((None,) + shape, lambda i: (layer,) + (0,) * len(shape))
    return pl.pallas_call(
        _mix_kernel,
        grid=(tiles + 1,),
        in_specs=[row(D_MODEL), row(WIDTH), row(WIDTH), row(WIDTH), row(D_MODEL), row(D_MODEL),
                  lay(HEADS, CHUNK, CHUNK), lay(CHUNK, WIDTH), lay(WIDTH, D_MODEL),
                  lay(WIDTH, D_MODEL), lay(D_MODEL, D_MODEL), lay(1, D_MODEL),
                  lay(D_MODEL, LANES), lay(1, LANES)],
        out_specs=[row(D_MODEL),
                   pl.BlockSpec((TM_MIX * ROW_TILE, LANES), lambda i: (prev(i), 0)),
                   pl.BlockSpec((TM_MIX, LANES), lambda i: (prev(i), 0)),
                   pl.BlockSpec((1, LANES), lambda i: (0, 0))],
        out_shape=[jax.ShapeDtypeStruct((n, D_MODEL), F32),
                   jax.ShapeDtypeStruct((n * ROW_TILE, LANES), U32),
                   jax.ShapeDtypeStruct((n, LANES), F32),
                   jax.ShapeDtypeStruct((1, LANES), F32)],
        scratch_shapes=[pltpu.VMEM((1, LANES), F32), pltpu.VMEM((TM_MIX, D_MODEL), F32)],
        compiler_params=pltpu.CompilerParams(
            dimension_semantics=("arbitrary",), vmem_limit_bytes=VMEM_LIMIT),
        name="mix",
    )(x, oa, u, vs, ga, gb, wsp, bsp, wua, wub, wo, gf, wr, br)


def _dispatch_kernel(dest_ref, zrow_ref, h_ref, xs_ref, zero_ref, sem, zsem, tsem):
    i = pl.program_id(0)
    block_rows = MOE_ROWS * ROW_TILE
    nblk = xs_ref.shape[0] // block_rows
    nused = zrow_ref[N_EXPERTS]

    def zero_copy(first_row, zero_sem):
        return pltpu.make_async_copy(zero_ref, xs_ref.at[pl.ds(first_row, block_rows), :],
                                     zero_sem)

    def tail_copy(j):
        return zero_copy((nused + j) * block_rows, tsem)

    @pl.when(i == 0)
    def _():
        zero_ref[...] = jnp.zeros_like(zero_ref)
        for j in range(N_EXPERTS):
            pl.when(nused + j < nblk)(lambda j=j: tail_copy(j).start())
        for e in range(N_EXPERTS):
            pl.when(zrow_ref[e] >= 0)(lambda e=e: zero_copy(zrow_ref[e], zsem).start())
        for e in range(N_EXPERTS):
            pl.when(zrow_ref[e] >= 0)(lambda e=e: zero_copy(zrow_ref[e], zsem).wait())

    n = dest_ref.shape[0] // 2
    base = i * TM_DISP
    for t in range(TM_DISP):
        for k in range(2):
            d = dest_ref[k * n + base + t]
            pltpu.make_async_copy(h_ref.at[pl.ds(t * ROW_TILE, ROW_TILE), :],
                                  xs_ref.at[pl.ds(d * ROW_TILE, ROW_TILE), :],
                                  sem).start(priority=k)
    for k in range(2):
        pltpu.make_async_copy(h_ref, xs_ref.at[pl.ds(0, TM_DISP * ROW_TILE), :], sem).wait()

    @pl.when(i == pl.num_programs(0) - 1)
    def _():
        for j in range(N_EXPERTS):
            pl.when(nused + j < nblk)(lambda j=j: tail_copy(j).wait())


def _dispatch(dest, zrow, h, cap):
    n = h.shape[0] // ROW_TILE
    return pl.pallas_call(
        _dispatch_kernel,
        grid_spec=pltpu.PrefetchScalarGridSpec(
            num_scalar_prefetch=2,
            grid=(n // TM_DISP,),
            in_specs=[pl.BlockSpec((TM_DISP * ROW_TILE, LANES), lambda i, dest, zrow: (i, 0))],
            out_specs=pl.BlockSpec(memory_space=pl.ANY),
            scratch_shapes=[pltpu.VMEM((MOE_ROWS * ROW_TILE, LANES), U32),
                            pltpu.SemaphoreType.DMA(()), pltpu.SemaphoreType.DMA(()),
                            pltpu.SemaphoreType.DMA(())],
        ),
        out_shape=jax.ShapeDtypeStruct((cap * ROW_TILE, LANES), U32),
        compiler_params=pltpu.CompilerParams(
            dimension_semantics=("arbitrary",), vmem_limit_bytes=VMEM_LIMIT),
        name="dispatch",
    )(dest, zrow, h)


def _expert_kernel(layer, blk_e_ref, nused_ref, half_ref, next_ref, xs_ref, wi_hbm, wo_hbm,
                   ys_ref, wi_f, wo_f, wi_b, wo_b, sem):
    b = pl.program_id(0)
    used = b < nused_ref[0]
    expert = blk_e_ref[b]

    def fetch(e, half):
        return (pltpu.make_async_copy(wi_hbm.at[layer, e], wi_f.at[half], sem.at[half, 0]),
                pltpu.make_async_copy(wo_hbm.at[layer, e], wo_f.at[half], sem.at[half, 1]))

    @pl.when(b == 0)
    def _():
        for copy in fetch(expert, half_ref[expert]):
            copy.start()

    @pl.when(jnp.logical_not(used))
    def _():
        ys_ref[...] = jnp.zeros_like(ys_ref)

    new_expert = jnp.logical_or(b == 0, expert != blk_e_ref[jnp.maximum(b - 1, 0)])

    @pl.when(jnp.logical_and(used, new_expert))
    def _():
        half = half_ref[expert]
        for copy in fetch(expert, half):
            copy.wait()
        following = next_ref[expert]

        @pl.when(following >= 0)
        def _():
            for copy in fetch(following, 1 - half):
                copy.start(priority=1)

        wi_b[...] = wi_f[half].astype(BF16)
        wo_b[...] = wo_f[half].astype(BF16)

    @pl.when(used)
    def _():
        xs = _load_row_tiles(xs_ref, MOE_ROWS).astype(BF16)
        gu = jnp.dot(xs, wi_b[...], preferred_element_type=F32)
        act = jax.nn.silu(gu[:, :EXPERT_FF]) * gu[:, EXPERT_FF:]
        y = jnp.dot(act.astype(BF16), wo_b[...], preferred_element_type=F32)
        _store_row_tiles(ys_ref, y)


def _experts(layer, blk_e, nused, half, following, xs, w_e_in, w_e_out):
    cap = xs.shape[0] // ROW_TILE
    rows = MOE_ROWS * ROW_TILE
    return pl.pallas_call(
        functools.partial(_expert_kernel, layer),
        grid_spec=pltpu.PrefetchScalarGridSpec(
            num_scalar_prefetch=4,
            grid=(cap // MOE_ROWS,),
            in_specs=[pl.BlockSpec((rows, LANES),
                                   lambda b, blk_e, nused, *_: (jnp.minimum(b, nused[0] - 1), 0)),
                      pl.BlockSpec(memory_space=pl.ANY),
                      pl.BlockSpec(memory_space=pl.ANY)],
            out_specs=pl.BlockSpec((rows, LANES), lambda b, *_: (b, 0)),
            scratch_shapes=[pltpu.VMEM((2, D_MODEL, 2 * EXPERT_FF), F32),
                            pltpu.VMEM((2, EXPERT_FF, D_MODEL), F32),
                            pltpu.VMEM((D_MODEL, 2 * EXPERT_FF), BF16),
                            pltpu.VMEM((EXPERT_FF, D_MODEL), BF16),
                            pltpu.SemaphoreType.DMA((2, 2))],
        ),
        out_shape=jax.ShapeDtypeStruct((cap * ROW_TILE, LANES), U32),
        compiler_params=pltpu.CompilerParams(
            dimension_semantics=("arbitrary",), vmem_limit_bytes=VMEM_LIMIT),
        name="experts",
    )(blk_e, nused, half, following, xs, w_e_in, w_e_out)


def _ple_kernel(dest_ref, x_ref, route_ref, p_ref, ys_ref, gp_ref, wg_ref, wp_ref,
                xo_ref, y00, y01, y10, y11, sem):
    ybuf = ((y00, y01), (y10, y11))
    j = pl.program_id(0)
    last_step = pl.num_programs(0) - 1

    def start_gather(tile, half):
        n = dest_ref.shape[0] // 2
        base = tile * TM_ROW
        for t in range(TM_ROW):
            for k in range(2):
                d = dest_ref[k * n + base + t]
                pltpu.make_async_copy(ys_ref.at[pl.ds(d * ROW_TILE, ROW_TILE), :],
                                      ybuf[half][k].at[pl.ds(t * ROW_TILE, ROW_TILE), :],
                                      sem.at[half]).start(priority=k)

    def wait_gather(half):
        for k in range(2):
            pltpu.make_async_copy(ys_ref.at[pl.ds(0, TM_ROW * ROW_TILE), :],
                                  ybuf[half][k], sem.at[half]).wait()

    def combine(half):
        rows = slice(half * TM_ROW, (half + 1) * TM_ROW)
        pe = jnp.dot(p_ref[rows, :].astype(BF16), wp_ref[...], preferred_element_type=F32)
        wait_gather(half)
        route = route_ref[rows, :]
        y0 = _load_row_tiles(ybuf[half][0], TM_ROW)
        y1 = _load_row_tiles(ybuf[half][1], TM_ROW)
        x = x_ref[rows, :] + route[:, 2:3] * y0 + route[:, 3:4] * y1
        gate = jax.nn.sigmoid(jnp.dot(_rms(x, gp_ref[...]).astype(BF16), wg_ref[...],
                                      preferred_element_type=F32))
        xo_ref[rows, :] = x + gate * pe

    @pl.when(j == 0)
    def _():
        start_gather(0, 0)

    start_gather(2 * j + 1, 1)
    combine(0)
    start_gather(jnp.minimum(2 * j + 2, 2 * last_step + 1), 0)
    combine(1)

    @pl.when(j == last_step)
    def _():
        wait_gather(0)


def _ple(layer, dest, x, route, p, ys, gp, wg, wp):
    n = x.shape[0]
    row = lambda width: pl.BlockSpec((2 * TM_ROW, width), lambda j, dest: (j, 0))
    lay = lambda *shape: pl.BlockSpec((None,) + shape,
                                      lambda j, dest: (layer,) + (0,) * len(shape))
    return pl.pallas_call(
        _ple_kernel,
        grid_spec=pltpu.PrefetchScalarGridSpec(
            num_scalar_prefetch=1,
            grid=(n // (2 * TM_ROW),),
            in_specs=[row(D_MODEL), row(LANES),
                      pl.BlockSpec((None, 2 * TM_ROW, PLE_DIM), lambda j, dest: (layer, j, 0)),
                      pl.BlockSpec(memory_space=pl.ANY),
                      lay(1, D_MODEL), lay(D_MODEL, D_MODEL), lay(PLE_DIM, D_MODEL)],
            out_specs=row(D_MODEL),
            scratch_shapes=[pltpu.VMEM((TM_ROW * ROW_TILE, LANES), U32)] * 4
            + [pltpu.SemaphoreType.DMA((2,))],
        ),
        out_shape=jax.ShapeDtypeStruct((n, D_MODEL), F32),
        compiler_params=pltpu.CompilerParams(
            dimension_semantics=("arbitrary",), vmem_limit_bytes=VMEM_LIMIT),
        name="ple",
    )(dest, x, route, p, ys, gp, wg, wp)


def _slot_layout(route, counts, cap):
    cnt = counts[0, ROUTE_LANE0:ROUTE_LANE0 + N_EXPERTS].astype(jnp.int32)
    padded = (cnt + MOE_ROWS - 1) // MOE_ROWS * MOE_ROWS
    pend = jnp.cumsum(padded)
    poff = pend - padded
    ids = jnp.arange(N_EXPERTS, dtype=jnp.int32)

    def slots(k):
        eid = route[:, k].astype(jnp.int32)
        rank = route[:, 4 + k].astype(jnp.int32)
        return jnp.sum(jnp.where(eid[:, None] == ids, poff, 0), axis=-1) + rank

    dest = jnp.concatenate([slots(0), slots(1)]).astype(jnp.int32)
    blk_start = jnp.arange(cap // MOE_ROWS, dtype=jnp.int32) * MOE_ROWS
    blk_e = jnp.minimum(jnp.sum(pend[None, :] <= blk_start[:, None], axis=1), N_EXPERTS - 1)
    nused = (pend[-1:] // MOE_ROWS).astype(jnp.int32)
    zrow = jnp.where(cnt > 0, (pend - MOE_ROWS) * ROW_TILE, -1).astype(jnp.int32)
    nonempty = cnt > 0
    half = ((jnp.cumsum(nonempty) - 1) % 2).astype(jnp.int32)
    later = jnp.logical_and(nonempty[None, :], ids[None, :] > ids[:, None])
    following = jnp.min(jnp.where(later, ids[None, :], N_EXPERTS), axis=1)
    following = jnp.where(following < N_EXPERTS, following, -1).astype(jnp.int32)
    return (dest, jnp.concatenate([zrow, nused]), blk_e.astype(jnp.int32), nused, half,
            following)


def kernel(x, p, norm_mix, w_in, q_norm, k_norm, sgu_norm, w_spatial, b_spatial, w_up_a, w_up_b,
           w_out, norm_ffn, w_group_router, b_group_router, w_expert_router, b_expert_router,
           w_expert_in, w_expert_out, norm_ple, w_ple_gate, w_ple_proj):
    batch, seq, d = x.shape
    depth = w_in.shape[0]
    n = batch * seq
    cap = 2 * n + N_EXPERTS * MOE_ROWS

    vec = lambda a: a[:, None, :]
    w_ua_b, w_ub_b, w_o_b = (a.astype(BF16) for a in (w_up_a, w_up_b, w_out))
    w_pg_b, w_pp_b = w_ple_gate.astype(BF16), w_ple_proj.astype(BF16)
    qn2, kn2 = vec(jnp.tile(q_norm, (1, 2))), vec(jnp.tile(k_norm, (1, 2)))
    b_sp = jnp.repeat(jnp.swapaxes(b_spatial, 1, 2), HEAD_DIM, axis=2)
    pad = LANES - N_GROUPS - N_EXPERTS
    w_r = jnp.pad(jnp.concatenate([w_group_router, w_expert_router], axis=2),
                  ((0, 0), (0, 0), (0, pad))).astype(BF16)
    b_r = vec(jnp.pad(jnp.concatenate([b_group_router, b_expert_router], axis=1),
                      ((0, 0), (0, pad))))
    p2 = p.reshape(depth, n, PLE_DIM)

    xf = x.reshape(n, d)
    for i in range(depth):
        q, k, v, u, vs, ga, gb = _inproj(i, xf, vec(norm_mix), w_in, qn2, kn2, vec(sgu_norm))
        oa = _attention(q, k, v, batch, seq)
        xf, h, route, counts = _mix(i, xf, oa, u, vs, ga, gb, w_spatial, b_sp, w_ua_b, w_ub_b,
                                    w_o_b, vec(norm_ffn), w_r, b_r)
        dest, zrow, blk_e, nused, half, following = _slot_layout(route, counts, cap)
        xs = _dispatch(dest, zrow, h, cap)
        ys = _experts(i, blk_e, nused, half, following, xs, w_expert_in, w_expert_out)
        xf = _ple(i, dest, xf, route, p2, ys, vec(norm_ple), w_pg_b, w_pp_b)
    return xf.reshape(batch, seq, d)
```

```python
import functools
import math

import jax
import jax.numpy as jnp
from jax import lax
from jax.experimental import pallas as pl
from jax.experimental.pallas import tpu as pltpu

F32 = jnp.float32
BF16 = jnp.bfloat16

D_MODEL = 1024
HEADS = 8
HEAD_DIM = 64
WIDTH = HEADS * HEAD_DIM
CHUNK = 128
IN_WIDTH = 5 * WIDTH + 2 * D_MODEL
N_GROUPS = 4
EXPERTS_PER_GROUP = 8
N_EXPERTS = N_GROUPS * EXPERTS_PER_GROUP
EXPERT_FF = 512
PLE_DIM = 256
EPS = 1e-6

LANES = 128
ROW_TILE = D_MODEL // (2 * LANES)
U32 = jnp.uint32
ROUTE_LANE0 = N_GROUPS
TM_IN = 256
TM_MIX = 256
TM_DISP = 512
TM_ROW = 256
MOE_ROWS = 256
TOP_ROWS = 64
LOG2E = 1.4426950408889634
DEAD_LOG2 = -105.0 * LOG2E
VMEM_LIMIT = 48 * 1024 * 1024


def _rms(x, gain):
    return x * lax.rsqrt(jnp.mean(x * x, axis=-1, keepdims=True) + EPS) * gain


def _store_row_tiles(ref, value, lead=()):
    rows = value.shape[0]
    bits = lambda v: lax.bitcast_convert_type(v.astype(BF16).astype(F32), U32)
    for c in range(ROW_TILE):
        low = bits(value[:, 2 * c * LANES:(2 * c + 1) * LANES]) >> 16
        high = bits(value[:, (2 * c + 1) * LANES:(2 * c + 2) * LANES]) & U32(0xFFFF0000)
        ref[lead + (pl.ds(c, rows, stride=ROW_TILE), slice(None))] = low | high


def _load_row_tiles(ref, rows, lead=()):
    chunks = []
    for c in range(ROW_TILE):
        words = ref[lead + (pl.ds(c, rows, stride=ROW_TILE), slice(None))]
        chunks.append(lax.bitcast_convert_type(words << 16, F32))
        chunks.append(lax.bitcast_convert_type(words & U32(0xFFFF0000), F32))
    return jnp.concatenate(chunks, axis=1)


def _inproj_kernel(x_ref, g_ref, wf_ref, qn_ref, kn_ref, sn_ref,
                   q_ref, k_ref, v_ref, u_ref, vs_ref, ga_ref, gb_ref, w_ref):
    @pl.when(pl.program_id(0) == 0)
    def _():
        w_ref[...] = wf_ref[...].astype(BF16)

    hb = _rms(x_ref[...], g_ref[...]).astype(BF16)

    def proj(lo, hi):
        return jnp.dot(hb, w_ref[:, lo:hi], preferred_element_type=F32)

    first = lax.broadcasted_iota(jnp.int32, (1, LANES), 1) < HEAD_DIM

    def head_norm(z, gain, scale):
        outs = []
        for c in range(WIDTH // LANES):
            zc = z[:, c * LANES:(c + 1) * LANES]
            sq = zc * zc
            sa = jnp.sum(jnp.where(first, sq, 0.0), axis=-1, keepdims=True)
            sb = jnp.sum(jnp.where(first, 0.0, sq), axis=-1, keepdims=True)
            ms = jnp.where(first, sa, sb) * (1.0 / HEAD_DIM)
            outs.append(zc * lax.rsqrt(ms + EPS) * (gain * scale))
        return jnp.concatenate(outs, axis=-1)

    w = WIDTH
    q_ref[...] = head_norm(proj(0, w), qn_ref[...], LOG2E / math.sqrt(HEAD_DIM)).astype(BF16)
    k_ref[...] = head_norm(proj(w, 2 * w), kn_ref[...], 1.0).astype(BF16)
    v_ref[...] = proj(2 * w, 3 * w).astype(BF16)
    u_ref[...] = jax.nn.gelu(proj(3 * w, 4 * w)).astype(BF16)
    vs_ref[...] = _rms(jax.nn.gelu(proj(4 * w, 5 * w)), sn_ref[...]).astype(BF16)
    ga_ref[...] = jax.nn.sigmoid(proj(5 * w, 5 * w + D_MODEL)).astype(BF16)
    gb_ref[...] = jax.nn.sigmoid(proj(5 * w + D_MODEL, IN_WIDTH)).astype(BF16)


def _inproj(layer, x, norm_mix, w_in, qn, kn, sn):
    n = x.shape[0]
    row = lambda width: pl.BlockSpec((TM_IN, width), lambda i: (i, 0))
    vec = lambda width: pl.BlockSpec((None, 1, width), lambda i: (layer, 0, 0))
    out = lambda width: jax.ShapeDtypeStruct((n, width), BF16)
    return pl.pallas_call(
        _inproj_kernel,
        grid=(n // TM_IN,),
        in_specs=[row(D_MODEL), vec(D_MODEL),
                  pl.BlockSpec((None, D_MODEL, IN_WIDTH), lambda i: (layer, 0, 0),
                               pipeline_mode=pl.Buffered(1)),
                  vec(LANES), vec(LANES), vec(WIDTH)],
        out_specs=[row(WIDTH)] * 5 + [row(D_MODEL)] * 2,
        out_shape=[out(WIDTH)] * 5 + [out(D_MODEL)] * 2,
        scratch_shapes=[pltpu.VMEM((D_MODEL, IN_WIDTH), BF16)],
        compiler_params=pltpu.CompilerParams(
            dimension_semantics=("arbitrary",), vmem_limit_bytes=VMEM_LIMIT),
        name="inproj",
    )(x, norm_mix, w_in, qn, kn, sn)


def _attn_kernel(q_ref, k_ref, v_ref, o_ref, carry_ref, acc_ref, z_ref):
    qb = pl.program_id(1)
    pairs = WIDTH // LANES
    first = lax.broadcasted_iota(jnp.int32, (CHUNK, LANES), 1) < HEAD_DIM
    first_top = lax.broadcasted_iota(jnp.int32, (TOP_ROWS, LANES), 1) < HEAD_DIM
    kr = lax.broadcasted_iota(jnp.int32, (LANES, 2 * LANES), 0)
    kc = lax.broadcasted_iota(jnp.int32, (LANES, 2 * LANES), 1)
    tail = jnp.where(jnp.logical_or(kr > kc, kc >= LANES), 1.0, 0.0).astype(BF16)
    sign = jnp.uint32(0x80000000)
    cols = [slice(p * LANES, (p + 1) * LANES) for p in range(pairs)]

    def stack(value, rows):
        if rows == CHUNK:
            return value
        return jnp.concatenate([value[:rows], value[CHUNK:CHUNK + rows]], axis=0)

    queries, causal = {}, {}
    for rows in (CHUNK, TOP_ROWS):
        lane = lax.broadcasted_iota(jnp.int32, (2 * rows, LANES), 1)
        row = lax.broadcasted_iota(jnp.int32, (2 * rows, LANES), 0)
        upper = row < rows
        own = jnp.logical_xor(lane < HEAD_DIM, jnp.logical_not(upper))
        causal[rows] = lane < jnp.where(upper, row, row - rows)
        queries[rows] = []
        for p in range(pairs):
            qp = q_ref[:rows, cols[p]]
            q2 = jnp.concatenate([qp, qp], axis=0)
            queries[rows].append(jnp.where(own, q2, jnp.zeros_like(q2)))
    full_row = lax.broadcasted_iota(jnp.int32, (2 * CHUNK, LANES), 0)
    is_top = jnp.where(full_row < CHUNK, full_row, full_row - CHUNK) < TOP_ROWS

    def scores(j, rows):
        start = pl.multiple_of(j * CHUNK, CHUNK)
        return [lax.dot_general(queries[rows][p], k_ref[pl.ds(start, CHUNK), cols[p]],
                                (((1,), (1,)), ((), ())), preferred_element_type=F32)
                for p in range(pairs)]

    def sweep(j, heights, diagonal=False, fresh=False, ahead=True, rest_only=False):
        blocks = range(len(heights))
        full = [b for b in blocks if heights[b] == CHUNK]
        top = [b for b in blocks if heights[b] == TOP_ROWS]
        starts = [pl.multiple_of((j - b) * CHUNK, CHUNK) for b in blocks]
        zs = []
        for b in blocks:
            if b > 0 or fresh or diagonal:
                zs.append(scores(j - b, heights[b]))
            else:
                zs.append([stack(z_ref[p], heights[b]) for p in range(pairs)])
        if ahead:
            for p, z in enumerate(scores(jnp.maximum(j - len(heights), 0), CHUNK)):
                z_ref[p] = z
        log_betas, splits = {}, {}
        for b in blocks:
            for p in range(pairs):
                z = zs[b][p]
                neg_abs = lax.bitcast_convert_type(
                    lax.bitcast_convert_type(z, jnp.uint32) | sign, F32)
                log1p = jnp.log(1.0 + jnp.exp2(neg_abs)) * LOG2E
                log_beta = jnp.minimum(z, 0.0) - log1p
                log_keep = log_beta - z
                if diagonal and b == 0:
                    log_keep = jnp.where(causal[heights[b]], log_keep, 0.0)
                if rest_only:
                    log_keep = jnp.where(is_top, 0.0, log_keep)
                log_betas[b, p] = log_beta
                splits[b, p] = log_keep.astype(BF16)
        sums = {key: jnp.dot(s, tail, preferred_element_type=F32) for key, s in splits.items()}
        entering, after_full, after_top = {}, [], []
        for p in range(pairs):
            carry = 0.0 if diagonal else carry_ref[p]
            for b in full:
                entering[b, p] = carry
                carry = carry + sums[b, p][:, LANES:]
            after_full.append(carry)
            carry = stack(carry, TOP_ROWS) if top else None
            for b in top:
                entering[b, p] = carry
                carry = carry + sums[b, p][:, LANES:]
            after_top.append(carry)
        alive = lambda c: jnp.max(c) > DEAD_LOG2
        worst = functools.reduce(jnp.maximum, after_full)
        live_rest = alive(jnp.where(is_top, -jnp.inf, worst)) if full else jnp.bool_(False)
        if top:
            live_top = alive(functools.reduce(jnp.maximum, after_top))
        else:
            live_top = alive(jnp.where(is_top, worst, -jnp.inf))
        outs = {}
        for b in blocks:
            for p in range(pairs):
                a = jnp.exp2(log_betas[b, p] + sums[b, p][:, :LANES] + entering[b, p])
                if diagonal and b == 0:
                    a = jnp.where(causal[heights[b]], a, 0.0)
                if rest_only:
                    a = jnp.where(is_top, 0.0, a)
                outs[b, p] = jnp.dot(a.astype(BF16), v_ref[pl.ds(starts[b], CHUNK), cols[p]],
                                     preferred_element_type=F32)
        for p in range(pairs):
            if full:
                carry_ref[p] = after_full[p]
            if top:
                carry_ref[p, 0:TOP_ROWS, :] = after_top[p][:TOP_ROWS]
                carry_ref[p, CHUNK:CHUNK + TOP_ROWS, :] = after_top[p][TOP_ROWS:]
        for p in range(pairs):
            if full:
                o2 = functools.reduce(jnp.add, [outs[b, p] for b in full])
                o = jnp.where(first, o2[:CHUNK], o2[CHUNK:])
                acc_ref[:, cols[p]] = o if diagonal else acc_ref[:, cols[p]] + o
            if top:
                o2 = functools.reduce(jnp.add, [outs[b, p] for b in top])
                o = jnp.where(first_top, o2[:TOP_ROWS], o2[TOP_ROWS:])
                acc_ref[0:TOP_ROWS, cols[p]] += o
        return live_top, live_rest

    def sweep_left(j, live_top, live_rest, height):
        def cond(state):
            j, live_top, live_rest = state
            return jnp.logical_and(j >= 0, jnp.logical_or(live_top, live_rest))

        def body(state):
            return (state[0] - 1,) + sweep(state[0], [height])

        lax.while_loop(cond, body, (j, live_top, live_rest))

    for count in (1, 2):
        @pl.when(qb == count - 1)
        def _(count=count):
            sweep(qb, [CHUNK] * count, diagonal=True)

    @pl.when(qb >= 2)
    def _():
        live_top, live_rest = sweep(qb, [CHUNK, CHUNK, TOP_ROWS], diagonal=True)

        @pl.when(live_rest)
        def _():
            flags = sweep(qb - 2, [CHUNK], fresh=True, ahead=False, rest_only=True)
            sweep_left(qb - 3, *flags, CHUNK)

        @pl.when(jnp.logical_not(live_rest))
        def _():
            sweep_left(qb - 3, live_top, jnp.bool_(False), TOP_ROWS)

    o_ref[...] = acc_ref[...].astype(BF16)


def _attention(q, k, v, batch, seq):
    q3, k3, v3 = (t.reshape(batch, seq, WIDTH) for t in (q, k, v))
    blk = pl.BlockSpec((None, CHUNK, WIDTH), lambda b, qb: (b, qb, 0))
    full = pl.BlockSpec((None, seq, WIDTH), lambda b, qb: (b, 0, 0), pipeline_mode=pl.Buffered(1))
    o = pl.pallas_call(
        _attn_kernel,
        grid=(batch, seq // CHUNK),
        in_specs=[blk, full, full],
        out_specs=blk,
        out_shape=jax.ShapeDtypeStruct((batch, seq, WIDTH), BF16),
        scratch_shapes=[pltpu.VMEM((WIDTH // LANES, 2 * CHUNK, LANES), F32),
                        pltpu.VMEM((CHUNK, WIDTH), F32),
                        pltpu.VMEM((WIDTH // LANES, 2 * CHUNK, LANES), F32)],
        compiler_params=pltpu.CompilerParams(
            dimension_semantics=("arbitrary", "arbitrary"), vmem_limit_bytes=VMEM_LIMIT),
        name="attn",
    )(q3, k3, v3)
    return o.reshape(batch * seq, WIDTH)


def _choose_tile(hb, wr_ref, br_ref):
    logits = jnp.dot(hb, wr_ref[...], preferred_element_type=F32) + br_ref[...]
    lane_t = lax.broadcasted_iota(jnp.int32, (TM_MIX, LANES), 1)
    lanef = lane_t.astype(F32)
    neg = -jnp.inf
    far = float(LANES)

    def first_max(vals):
        m = jnp.max(vals, axis=-1, keepdims=True)
        idx = jnp.min(jnp.where(vals == m, lanef, far), axis=-1, keepdims=True)
        return m, idx

    gl = jnp.where(lane_t < N_GROUPS, logits, neg)
    gmax, grp = first_max(gl)
    grp_w = 1.0 / jnp.sum(jnp.exp(gl - gmax), axis=-1, keepdims=True)
    lo_lane = ROUTE_LANE0 + EXPERTS_PER_GROUP * grp
    in_group = jnp.logical_and(lanef >= lo_lane, lanef < lo_lane + EXPERTS_PER_GROUP)
    el = jnp.where(in_group, logits, neg)
    m1, i1 = first_max(el)
    m2, i2 = first_max(jnp.where(lanef == i1, neg, el))
    e21 = jnp.exp(m2 - m1)
    w1 = grp_w / (1.0 + e21)
    w2 = w1 * e21
    return i1, i2, w1, w2


def _rank_tile(choice, run_ref, valid):
    i1, i2, w1, w2 = choice
    lane_t = lax.broadcasted_iota(jnp.int32, (TM_MIX, LANES), 1)
    lanef = lane_t.astype(F32)
    onehot = jnp.logical_and(jnp.logical_or(lanef == i1, lanef == i2), valid)
    rt = lax.broadcasted_iota(jnp.int32, (TM_MIX, TM_MIX), 0)
    ct = lax.broadcasted_iota(jnp.int32, (TM_MIX, TM_MIX), 1)
    before = jnp.where(ct < rt, 1.0, 0.0).astype(BF16)
    prior = jnp.dot(before, jnp.where(onehot, 1.0, 0.0).astype(BF16),
                    preferred_element_type=F32) + run_ref[...]
    r1 = jnp.sum(jnp.where(lanef == i1, prior, 0.0), axis=-1, keepdims=True)
    r2 = jnp.sum(jnp.where(lanef == i2, prior, 0.0), axis=-1, keepdims=True)
    run_ref[...] += jnp.sum(jnp.where(onehot, 1.0, 0.0), axis=0, keepdims=True)

    fields = (i1 - ROUTE_LANE0, i2 - ROUTE_LANE0, w1, w2, r1, r2)
    route = jnp.zeros((TM_MIX, LANES), F32)
    for pos, val in enumerate(fields):
        route = jnp.where(lane_t == pos, val, route)
    return route


def _mix_kernel(x_ref, oa_ref, u_ref, vs_ref, ga_ref, gb_ref,
                wsp_ref, bsp_ref, wua_ref, wub_ref, wo_ref, gf_ref, wr_ref, br_ref,
                xo_ref, h_ref, route_ref, cnt_ref, run_ref, xprev_ref):
    step = pl.program_id(0)

    @pl.when(step == 0)
    def _():
        run_ref[...] = jnp.zeros_like(run_ref)
        xprev_ref[...] = jnp.zeros_like(xprev_ref)

    lane = lax.broadcasted_iota(jnp.int32, (CHUNK, LANES), 1)
    row = lax.broadcasted_iota(jnp.int32, (CHUNK, LANES), 0)
    first = lane < HEAD_DIM
    tril = lane <= row

    w_pairs = []
    for gp in range(WIDTH // LANES):
        w_pairs.append(jnp.concatenate(
            [jnp.where(tril, wsp_ref[g], 0.0).astype(BF16) for g in (2 * gp, 2 * gp + 1)],
            axis=1))
    ob_chunks = []
    for c in range(TM_MIX // CHUNK):
        rows = slice(c * CHUNK, (c + 1) * CHUNK)
        cols = []
        for gp in range(WIDTH // LANES):
            vpair = vs_ref[rows, gp * LANES:(gp + 1) * LANES]
            zero = jnp.zeros_like(vpair)
            stacked = jnp.concatenate([jnp.where(first, vpair, zero),
                                       jnp.where(first, zero, vpair)], axis=0)
            cols.append(jnp.dot(w_pairs[gp], stacked, preferred_element_type=F32))
        mixed = jnp.concatenate(cols, axis=-1) + bsp_ref[...]
        ob_chunks.append((u_ref[rows, :].astype(F32) * mixed).astype(BF16))
    ob = jnp.concatenate(ob_chunks, axis=0)

    h = _rms(xprev_ref[...], gf_ref[...])
    _store_row_tiles(h_ref, h)
    choice = _choose_tile(h.astype(BF16), wr_ref, br_ref)

    up_a = jnp.dot(oa_ref[...], wua_ref[...], preferred_element_type=F32)
    up_b = jnp.dot(ob, wub_ref[...], preferred_element_type=F32)
    merged = ga_ref[...].astype(F32) * up_a + gb_ref[...].astype(F32) * up_b
    x = x_ref[...] + jnp.dot(merged.astype(BF16), wo_ref[...], preferred_element_type=F32)
    xo_ref[...] = x
    xprev_ref[...] = x

    route_ref[...] = _rank_tile(choice, run_ref, step > 0)
    cnt_ref[...] = run_ref[...]


def _mix(layer, x, oa, u, vs, ga, gb, wsp, bsp, wua, wub, wo, gf, wr, br):
    n = x.shape[0]
    tiles = n // TM_MIX
    cur = lambda i: jnp.minimum(i, tiles - 1)
    prev = lambda i: jnp.maximum(i - 1, 0)
    row = lambda width: pl.BlockSpec((TM_MIX, width), lambda i: (cur(i), 0))
    lay = lambda *shape: pl.BlockSpec((None,) + shape, lambda i: (layer,) + (0,) * len(shape))
    return pl.pallas_call(
        _mix_kernel,
        grid=(tiles + 1,),
        in_specs=[row(D_MODEL), row(WIDTH), row(WIDTH), row(WIDTH), row(D_MODEL), row(D_MODEL),
                  lay(HEADS, CHUNK, CHUNK), lay(CHUNK, WIDTH), lay(WIDTH, D_MODEL),
                  lay(WIDTH, D_MODEL), lay(D_MODEL, D_MODEL), lay(1, D_MODEL),
                  lay(D_MODEL, LANES), lay(1, LANES)],
        out_specs=[row(D_MODEL),
                   pl.BlockSpec((TM_MIX * ROW_TILE, LANES), lambda i: (prev(i), 0)),
                   pl.BlockSpec((TM_MIX, LANES), lambda i: (prev(i), 0)),
                   pl.BlockSpec((1, LANES), lambda i: (0, 0))],
        out_shape=[jax.ShapeDtypeStruct((n, D_MODEL), F32),
                   jax.ShapeDtypeStruct((n * ROW_TILE, LANES), U32),
                   jax.ShapeDtypeStruct((n, LANES), F32),
                   jax.ShapeDtypeStruct((1, LANES), F32)],
        scratch_shapes=[pltpu.VMEM((1, LANES), F32), pltpu.VMEM((TM_MIX, D_MODEL), F32)],
        compiler_params=pltpu.CompilerParams(
            dimension_semantics=("arbitrary",), vmem_limit_bytes=VMEM_LIMIT),
        name="mix",
    )(x, oa, u, vs, ga, gb, wsp, bsp, wua, wub, wo, gf, wr, br)


def _dispatch_kernel(dest_ref, zrow_ref, h_ref, xs_ref, zero_ref, sem, zsem, tsem):
    i = pl.program_id(0)
    block_rows = MOE_ROWS * ROW_TILE
    nblk = xs_ref.shape[0] // block_rows
    nused = zrow_ref[N_EXPERTS]

    def zero_copy(first_row, zero_sem):
        return pltpu.make_async_copy(zero_ref, xs_ref.at[pl.ds(first_row, block_rows), :],
                                     zero_sem)

    def tail_copy(j):
        return zero_copy((nused + j) * block_rows, tsem)

    @pl.when(i == 0)
    def _():
        zero_ref[...] = jnp.zeros_like(zero_ref)
        for j in range(N_EXPERTS):
            pl.when(nused + j < nblk)(lambda j=j: tail_copy(j).start())
        for e in range(N_EXPERTS):
            pl.when(zrow_ref[e] >= 0)(lambda e=e: zero_copy(zrow_ref[e], zsem).start())
        for e in range(N_EXPERTS):
            pl.when(zrow_ref[e] >= 0)(lambda e=e: zero_copy(zrow_ref[e], zsem).wait())

    n = dest_ref.shape[0] // 2
    base = i * TM_DISP
    for t in range(TM_DISP):
        for k in range(2):
            d = dest_ref[k * n + base + t]
            pltpu.make_async_copy(h_ref.at[pl.ds(t * ROW_TILE, ROW_TILE), :],
                                  xs_ref.at[pl.ds(d * ROW_TILE, ROW_TILE), :],
                                  sem).start(priority=k)
    for k in range(2):
        pltpu.make_async_copy(h_ref, xs_ref.at[pl.ds(0, TM_DISP * ROW_TILE), :], sem).wait()

    @pl.when(i == pl.num_programs(0) - 1)
    def _():
        for j in range(N_EXPERTS):
            pl.when(nused + j < nblk)(lambda j=j: tail_copy(j).wait())


def _dispatch(dest, zrow, h, cap):
    n = h.shape[0] // ROW_TILE
    return pl.pallas_call(
        _dispatch_kernel,
        grid_spec=pltpu.PrefetchScalarGridSpec(
            num_scalar_prefetch=2,
            grid=(n // TM_DISP,),
            in_specs=[pl.BlockSpec((TM_DISP * ROW_TILE, LANES), lambda i, dest, zrow: (i, 0))],
            out_specs=pl.BlockSpec(memory_space=pl.ANY),
            scratch_shapes=[pltpu.VMEM((MOE_ROWS * ROW_TILE, LANES), U32),
                            pltpu.SemaphoreType.DMA(()), pltpu.SemaphoreType.DMA(()),
                            pltpu.SemaphoreType.DMA(())],
        ),
        out_shape=jax.ShapeDtypeStruct((cap * ROW_TILE, LANES), U32),
        compiler_params=pltpu.CompilerParams(
            dimension_semantics=("arbitrary",), vmem_limit_bytes=VMEM_LIMIT),
        name="dispatch",
    )(dest, zrow, h)


def _expert_kernel(layer, blk_e_ref, nused_ref, half_ref, next_ref, xs_ref, wi_hbm, wo_hbm,
                   ys_ref, wi_f, wo_f, wi_b, wo_b, sem):
    b = pl.program_id(0)
    used = b < nused_ref[0]
    expert = blk_e_ref[b]

    def fetch(e, half):
        return (pltpu.make_async_copy(wi_hbm.at[layer, e], wi_f.at[half], sem.at[half, 0]),
                pltpu.make_async_copy(wo_hbm.at[layer, e], wo_f.at[half], sem.at[half, 1]))

    @pl.when(b == 0)
    def _():
        for copy in fetch(expert, half_ref[expert]):
            copy.start()

    @pl.when(jnp.logical_not(used))
    def _():
        ys_ref[...] = jnp.zeros_like(ys_ref)

    new_expert = jnp.logical_or(b == 0, expert != blk_e_ref[jnp.maximum(b - 1, 0)])

    @pl.when(jnp.logical_and(used, new_expert))
    def _():
        half = half_ref[expert]
        for copy in fetch(expert, half):
            copy.wait()
        following = next_ref[expert]

        @pl.when(following >= 0)
        def _():
            for copy in fetch(following, 1 - half):
                copy.start(priority=1)

        wi_b[...] = wi_f[half].astype(BF16)
        wo_b[...] = wo_f[half].astype(BF16)

    @pl.when(used)
    def _():
        xs = _load_row_tiles(xs_ref, MOE_ROWS).astype(BF16)
        gu = jnp.dot(xs, wi_b[...], preferred_element_type=F32)
        act = jax.nn.silu(gu[:, :EXPERT_FF]) * gu[:, EXPERT_FF:]
        y = jnp.dot(act.astype(BF16), wo_b[...], preferred_element_type=F32)
        _store_row_tiles(ys_ref, y)


def _experts(layer, blk_e, nused, half, following, xs, w_e_in, w_e_out):
    cap = xs.shape[0] // ROW_TILE
    rows = MOE_ROWS * ROW_TILE
    return pl.pallas_call(
        functools.partial(_expert_kernel, layer),
        grid_spec=pltpu.PrefetchScalarGridSpec(
            num_scalar_prefetch=4,
            grid=(cap // MOE_ROWS,),
            in_specs=[pl.BlockSpec((rows, LANES),
                                   lambda b, blk_e, nused, *_: (jnp.minimum(b, nused[0] - 1), 0)),
                      pl.BlockSpec(memory_space=pl.ANY),
                      pl.BlockSpec(memory_space=pl.ANY)],
            out_specs=pl.BlockSpec((rows, LANES), lambda b, *_: (b, 0)),
            scratch_shapes=[pltpu.VMEM((2, D_MODEL, 2 * EXPERT_FF), F32),
                            pltpu.VMEM((2, EXPERT_FF, D_MODEL), F32),
                            pltpu.VMEM((D_MODEL, 2 * EXPERT_FF), BF16),
                            pltpu.VMEM((EXPERT_FF, D_MODEL), BF16),
                            pltpu.SemaphoreType.DMA((2, 2))],
        ),
        out_shape=jax.ShapeDtypeStruct((cap * ROW_TILE, LANES), U32),
        compiler_params=pltpu.CompilerParams(
            dimension_semantics=("arbitrary",), vmem_limit_bytes=VMEM_LIMIT),
        name="experts",
    )(blk_e, nused, half, following, xs, w_e_in, w_e_out)


def _ple_kernel(dest_ref, x_ref, route_ref, p_ref, ys_ref, gp_ref, wg_ref, wp_ref,
                xo_ref, y00, y01, y10, y11, sem):
    ybuf = ((y00, y01), (y10, y11))
    j = pl.program_id(0)
    last_step = pl.num_programs(0) - 1

    def start_gather(tile, half):
        n = dest_ref.shape[0] // 2
        base = tile * TM_ROW
        for t in range(TM_ROW):
            for k in range(2):
                d = dest_ref[k * n + base + t]
                pltpu.make_async_copy(ys_ref.at[pl.ds(d * ROW_TILE, ROW_TILE), :],
                                      ybuf[half][k].at[pl.ds(t * ROW_TILE, ROW_TILE), :],
                                      sem.at[half]).start(priority=k)

    def wait_gather(half):
        for k in range(2):
            pltpu.make_async_copy(ys_ref.at[pl.ds(0, TM_ROW * ROW_TILE), :],
                                  ybuf[half][k], sem.at[half]).wait()

    def combine(half):
        rows = slice(half * TM_ROW, (half + 1) * TM_ROW)
        pe = jnp.dot(p_ref[rows, :].astype(BF16), wp_ref[...], preferred_element_type=F32)
        wait_gather(half)
        route = route_ref[rows, :]
        y0 = _load_row_tiles(ybuf[half][0], TM_ROW)
        y1 = _load_row_tiles(ybuf[half][1], TM_ROW)
        x = x_ref[rows, :] + route[:, 2:3] * y0 + route[:, 3:4] * y1
        gate = jax.nn.sigmoid(jnp.dot(_rms(x, gp_ref[...]).astype(BF16), wg_ref[...],
                                      preferred_element_type=F32))
        xo_ref[rows, :] = x + gate * pe

    @pl.when(j == 0)
    def _():
        start_gather(0, 0)

    start_gather(2 * j + 1, 1)
    combine(0)
    start_gather(jnp.minimum(2 * j + 2, 2 * last_step + 1), 0)
    combine(1)

    @pl.when(j == last_step)
    def _():
        wait_gather(0)


def _ple(layer, dest, x, route, p, ys, gp, wg, wp):
    n = x.shape[0]
    row = lambda width: pl.BlockSpec((2 * TM_ROW, width), lambda j, dest: (j, 0))
    lay = lambda *shape: pl.BlockSpec((None,) + shape,
                                      lambda j, dest: (layer,) + (0,) * len(shape))
    return pl.pallas_call(
        _ple_kernel,
        grid_spec=pltpu.PrefetchScalarGridSpec(
            num_scalar_prefetch=1,
            grid=(n // (2 * TM_ROW),),
            in_specs=[row(D_MODEL), row(LANES),
                      pl.BlockSpec((None, 2 * TM_ROW, PLE_DIM), lambda j, dest: (layer, j, 0)),
                      pl.BlockSpec(memory_space=pl.ANY),
                      lay(1, D_MODEL), lay(D_MODEL, D_MODEL), lay(PLE_DIM, D_MODEL)],
            out_specs=row(D_MODEL),
            scratch_shapes=[pltpu.VMEM((TM_ROW * ROW_TILE, LANES), U32)] * 4
            + [pltpu.SemaphoreType.DMA((2,))],
        ),
        out_shape=jax.ShapeDtypeStruct((n, D_MODEL), F32),
        compiler_params=pltpu.CompilerParams(
            dimension_semantics=("arbitrary",), vmem_limit_bytes=VMEM_LIMIT),
        name="ple",
    )(dest, x, route, p, ys, gp, wg, wp)


def _slot_layout(route, counts, cap):
    cnt = counts[0, ROUTE_LANE0:ROUTE_LANE0 + N_EXPERTS].astype(jnp.int32)
    padded = (cnt + MOE_ROWS - 1) // MOE_ROWS * MOE_ROWS
    pend = jnp.cumsum(padded)
    poff = pend - padded
    ids = jnp.arange(N_EXPERTS, dtype=jnp.int32)

    def slots(k):
        eid = route[:, k].astype(jnp.int32)
        rank = route[:, 4 + k].astype(jnp.int32)
        return jnp.sum(jnp.where(eid[:, None] == ids, poff, 0), axis=-1) + rank

    dest = jnp.concatenate([slots(0), slots(1)]).astype(jnp.int32)
    blk_start = jnp.arange(cap // MOE_ROWS, dtype=jnp.int32) * MOE_ROWS
    blk_e = jnp.minimum(jnp.sum(pend[None, :] <= blk_start[:, None], axis=1), N_EXPERTS - 1)
    nused = (pend[-1:] // MOE_ROWS).astype(jnp.int32)
    zrow = jnp.where(cnt > 0, (pend - MOE_ROWS) * ROW_TILE, -1).astype(jnp.int32)
    nonempty = cnt > 0
    half = ((jnp.cumsum(nonempty) - 1) % 2).astype(jnp.int32)
    later = jnp.logical_and(nonempty[None, :], ids[None, :] > ids[:, None])
    following = jnp.min(jnp.where(later, ids[None, :], N_EXPERTS), axis=1)
    following = jnp.where(following < N_EXPERTS, following, -1).astype(jnp.int32)
    return (dest, jnp.concatenate([zrow, nused]), blk_e.astype(jnp.int32), nused, half,
            following)


def kernel(x, p, norm_mix, w_in, q_norm, k_norm, sgu_norm, w_spatial, b_spatial, w_up_a, w_up_b,
           w_out, norm_ffn, w_group_router, b_group_router, w_expert_router, b_expert_router,
           w_expert_in, w_expert_out, norm_ple, w_ple_gate, w_ple_proj):
    batch, seq, d = x.shape
    depth = w_in.shape[0]
    n = batch * seq
    cap = 2 * n + N_EXPERTS * MOE_ROWS

    vec = lambda a: a[:, None, :]
    w_ua_b, w_ub_b, w_o_b = (a.astype(BF16) for a in (w_up_a, w_up_b, w_out))
    w_pg_b, w_pp_b = w_ple_gate.astype(BF16), w_ple_proj.astype(BF16)
    qn2, kn2 = vec(jnp.tile(q_norm, (1, 2))), vec(jnp.tile(k_norm, (1, 2)))
    b_sp = jnp.repeat(jnp.swapaxes(b_spatial, 1, 2), HEAD_DIM, axis=2)
    pad = LANES - N_GROUPS - N_EXPERTS
    w_r = jnp.pad(jnp.concatenate([w_group_router, w_expert_router], axis=2),
                  ((0, 0), (0, 0), (0, pad))).astype(BF16)
    b_r = vec(jnp.pad(jnp.concatenate([b_group_router, b_expert_router], axis=1),
                      ((0, 0), (0, pad))))
    p2 = p.reshape(depth, n, PLE_DIM)

    xf = x.reshape(n, d)
    for i in range(depth):
        q, k, v, u, vs, ga, gb = _inproj(i, xf, vec(norm_mix), w_in, qn2, kn2, vec(sgu_norm))
        oa = _attention(q, k, v, batch, seq)
        xf, h, route, counts = _mix(i, xf, oa, u, vs, ga, gb, w_spatial, b_sp, w_ua_b, w_ub_b,
                                    w_o_b, vec(norm_ffn), w_r, b_r)
        dest, zrow, blk_e, nused, half, following = _slot_layout(route, counts, cap)
        xs = _dispatch(dest, zrow, h, cap)
        ys = _experts(i, blk_e, nused, half, following, xs, w_expert_in, w_expert_out)
        xf = _ple(i, dest, xf, route, p2, ys, vec(norm_ple), w_pg_b, w_pp_b)
    return xf.reshape(batch, seq, d)
```

```python
import functools
import math

import jax
import jax.numpy as jnp
from jax import lax
from jax.experimental import pallas as pl
from jax.experimental.pallas import tpu as pltpu

F32 = jnp.float32
BF16 = jnp.bfloat16

D_MODEL = 1024
HEADS = 8
HEAD_DIM = 64
WIDTH = HEADS * HEAD_DIM
CHUNK = 128
IN_WIDTH = 5 * WIDTH + 2 * D_MODEL
N_GROUPS = 4
EXPERTS_PER_GROUP = 8
N_EXPERTS = N_GROUPS * EXPERTS_PER_GROUP
EXPERT_FF = 512
PLE_DIM = 256
EPS = 1e-6

LANES = 128
ROW_TILE = D_MODEL // (2 * LANES)
U32 = jnp.uint32
ROUTE_LANE0 = N_GROUPS
TM_IN = 256
TM_MIX = 256
TM_DISP = 512
TM_ROW = 256
MOE_ROWS = 256
TOP_ROWS = 64
LOG2E = 1.4426950408889634
DEAD_LOG2 = -105.0 * LOG2E
VMEM_LIMIT = 48 * 1024 * 1024


def _rms(x, gain):
    return x * lax.rsqrt(jnp.mean(x * x, axis=-1, keepdims=True) + EPS) * gain


def _store_row_tiles(ref, value, lead=()):
    rows = value.shape[0]
    bits = lambda v: lax.bitcast_convert_type(v.astype(BF16).astype(F32), U32)
    for c in range(ROW_TILE):
        low = bits(value[:, 2 * c * LANES:(2 * c + 1) * LANES]) >> 16
        high = bits(value[:, (2 * c + 1) * LANES:(2 * c + 2) * LANES]) & U32(0xFFFF0000)
        ref[lead + (pl.ds(c, rows, stride=ROW_TILE), slice(None))] = low | high


def _load_row_tiles(ref, rows, lead=()):
    chunks = []
    for c in range(ROW_TILE):
        words = ref[lead + (pl.ds(c, rows, stride=ROW_TILE), slice(None))]
        chunks.append(lax.bitcast_convert_type(words << 16, F32))
        chunks.append(lax.bitcast_convert_type(words & U32(0xFFFF0000), F32))
    return jnp.concatenate(chunks, axis=1)


def _inproj_kernel(x_ref, g_ref, wf_ref, qn_ref, kn_ref, sn_ref,
                   q_ref, k_ref, v_ref, u_ref, vs_ref, ga_ref, gb_ref, w_ref):
    @pl.when(pl.program_id(0) == 0)
    def _():
        w_ref[...] = wf_ref[...].astype(BF16)

    hb = _rms(x_ref[...], g_ref[...]).astype(BF16)

    def proj(lo, hi):
        return jnp.dot(hb, w_ref[:, lo:hi], preferred_element_type=F32)

    first = lax.broadcasted_iota(jnp.int32, (1, LANES), 1) < HEAD_DIM

    def head_norm(z, gain, scale):
        outs = []
        for c in range(WIDTH // LANES):
            zc = z[:, c * LANES:(c + 1) * LANES]
            sq = zc * zc
            sa = jnp.sum(jnp.where(first, sq, 0.0), axis=-1, keepdims=True)
            sb = jnp.sum(jnp.where(first, 0.0, sq), axis=-1, keepdims=True)
            ms = jnp.where(first, sa, sb) * (1.0 / HEAD_DIM)
            outs.append(zc * lax.rsqrt(ms + EPS) * (gain * scale))
        return jnp.concatenate(outs, axis=-1)

    w = WIDTH
    q_ref[...] = head_norm(proj(0, w), qn_ref[...], LOG2E / math.sqrt(HEAD_DIM)).astype(BF16)
    k_ref[...] = head_norm(proj(w, 2 * w), kn_ref[...], 1.0).astype(BF16)
    v_ref[...] = proj(2 * w, 3 * w).astype(BF16)
    u_ref[...] = jax.nn.gelu(proj(3 * w, 4 * w)).astype(BF16)
    vs_ref[...] = _rms(jax.nn.gelu(proj(4 * w, 5 * w)), sn_ref[...]).astype(BF16)
    ga_ref[...] = jax.nn.sigmoid(proj(5 * w, 5 * w + D_MODEL)).astype(BF16)
    gb_ref[...] = jax.nn.sigmoid(proj(5 * w + D_MODEL, IN_WIDTH)).astype(BF16)


def _inproj(layer, x, norm_mix, w_in, qn, kn, sn):
    n = x.shape[0]
    row = lambda width: pl.BlockSpec((TM_IN, width), lambda i: (i, 0))
    vec = lambda width: pl.BlockSpec((None, 1, width), lambda i: (layer, 0, 0))
    out = lambda width: jax.ShapeDtypeStruct((n, width), BF16)
    return pl.pallas_call(
        _inproj_kernel,
        grid=(n // TM_IN,),
        in_specs=[row(D_MODEL), vec(D_MODEL),
                  pl.BlockSpec((None, D_MODEL, IN_WIDTH), lambda i: (layer, 0, 0),
                               pipeline_mode=pl.Buffered(1)),
                  vec(LANES), vec(LANES), vec(WIDTH)],
        out_specs=[row(WIDTH)] * 5 + [row(D_MODEL)] * 2,
        out_shape=[out(WIDTH)] * 5 + [out(D_MODEL)] * 2,
        scratch_shapes=[pltpu.VMEM((D_MODEL, IN_WIDTH), BF16)],
        compiler_params=pltpu.CompilerParams(
            dimension_semantics=("arbitrary",), vmem_limit_bytes=VMEM_LIMIT),
        name="inproj",
    )(x, norm_mix, w_in, qn, kn, sn)


def _attn_kernel(q_ref, k_ref, v_ref, o_ref, carry_ref, acc_ref, z_ref):
    qb = pl.program_id(1)
    pairs = WIDTH // LANES
    first = lax.broadcasted_iota(jnp.int32, (CHUNK, LANES), 1) < HEAD_DIM
    first_top = lax.broadcasted_iota(jnp.int32, (TOP_ROWS, LANES), 1) < HEAD_DIM
    kr = lax.broadcasted_iota(jnp.int32, (LANES, 2 * LANES), 0)
    kc = lax.broadcasted_iota(jnp.int32, (LANES, 2 * LANES), 1)
    tail = jnp.where(jnp.logical_or(kr > kc, kc >= LANES), 1.0, 0.0).astype(BF16)
    sign = jnp.uint32(0x80000000)
    cols = [slice(p * LANES, (p + 1) * LANES) for p in range(pairs)]

    def stack(value, rows):
        if rows == CHUNK:
            return value
        return jnp.concatenate([value[:rows], value[CHUNK:CHUNK + rows]], axis=0)

    queries, causal = {}, {}
    for rows in (CHUNK, TOP_ROWS):
        lane = lax.broadcasted_iota(jnp.int32, (2 * rows, LANES), 1)
        row = lax.broadcasted_iota(jnp.int32, (2 * rows, LANES), 0)
        upper = row < rows
        own = jnp.logical_xor(lane < HEAD_DIM, jnp.logical_not(upper))
        causal[rows] = lane < jnp.where(upper, row, row - rows)
        queries[rows] = []
        for p in range(pairs):
            qp = q_ref[:rows, cols[p]]
            q2 = jnp.concatenate([qp, qp], axis=0)
            queries[rows].append(jnp.where(own, q2, jnp.zeros_like(q2)))
    full_row = lax.broadcasted_iota(jnp.int32, (2 * CHUNK, LANES), 0)
    is_top = jnp.where(full_row < CHUNK, full_row, full_row - CHUNK) < TOP_ROWS

    def scores(j, rows):
        start = pl.multiple_of(j * CHUNK, CHUNK)
        return [lax.dot_general(queries[rows][p], k_ref[pl.ds(start, CHUNK), cols[p]],
                                (((1,), (1,)), ((), ())), preferred_element_type=F32)
                for p in range(pairs)]

    def sweep(j, heights, diagonal=False, fresh=False, ahead=True, rest_only=False):
        blocks = range(len(heights))
        full = [b for b in blocks if heights[b] == CHUNK]
        top = [b for b in blocks if heights[b] == TOP_ROWS]
        starts = [pl.multiple_of((j - b) * CHUNK, CHUNK) for b in blocks]
        zs = []
        for b in blocks:
            if b > 0 or fresh or diagonal:
                zs.append(scores(j - b, heights[b]))
            else:
                zs.append([stack(z_ref[p], heights[b]) for p in range(pairs)])
        if ahead:
            for p, z in enumerate(scores(jnp.maximum(j - len(heights), 0), CHUNK)):
                z_ref[p] = z
        log_betas, splits = {}, {}
        for b in blocks:
            for p in range(pairs):
                z = zs[b][p]
                neg_abs = lax.bitcast_convert_type(
                    lax.bitcast_convert_type(z, jnp.uint32) | sign, F32)
                log1p = jnp.log(1.0 + jnp.exp2(neg_abs)) * LOG2E
                log_beta = jnp.minimum(z, 0.0) - log1p
                log_keep = log_beta - z
                if diagonal and b == 0:
                    log_keep = jnp.where(causal[heights[b]], log_keep, 0.0)
                if rest_only:
                    log_keep = jnp.where(is_top, 0.0, log_keep)
                log_betas[b, p] = log_beta
                splits[b, p] = log_keep.astype(BF16)
        sums = {key: jnp.dot(s, tail, preferred_element_type=F32) for key, s in splits.items()}
        entering, after_full, after_top = {}, [], []
        for p in range(pairs):
            carry = 0.0 if diagonal else carry_ref[p]
            for b in full:
                entering[b, p] = carry
                carry = carry + sums[b, p][:, LANES:]
            after_full.append(carry)
            carry = stack(carry, TOP_ROWS) if top else None
            for b in top:
                entering[b, p] = carry
                carry = carry + sums[b, p][:, LANES:]
            after_top.append(carry)
        alive = lambda c: jnp.max(c) > DEAD_LOG2
        worst = functools.reduce(jnp.maximum, after_full)
        live_rest = alive(jnp.where(is_top, -jnp.inf, worst)) if full else jnp.bool_(False)
        if top:
            live_top = alive(functools.reduce(jnp.maximum, after_top))
        else:
            live_top = alive(jnp.where(is_top, worst, -jnp.inf))
        outs = {}
        for b in blocks:
            for p in range(pairs):
                a = jnp.exp2(log_betas[b, p] + sums[b, p][:, :LANES] + entering[b, p])
                if diagonal and b == 0:
                    a = jnp.where(causal[heights[b]], a, 0.0)
                if rest_only:
                    a = jnp.where(is_top, 0.0, a)
                outs[b, p] = jnp.dot(a.astype(BF16), v_ref[pl.ds(starts[b], CHUNK), cols[p]],
                                     preferred_element_type=F32)
        for p in range(pairs):
            if full:
                carry_ref[p] = after_full[p]
            if top:
                carry_ref[p, 0:TOP_ROWS, :] = after_top[p][:TOP_ROWS]
                carry_ref[p, CHUNK:CHUNK + TOP_ROWS, :] = after_top[p][TOP_ROWS:]
        for p in range(pairs):
            if full:
                o2 = functools.reduce(jnp.add, [outs[b, p] for b in full])
                o = jnp.where(first, o2[:CHUNK], o2[CHUNK:])
                acc_ref[:, cols[p]] = o if diagonal else acc_ref[:, cols[p]] + o
            if top:
                o2 = functools.reduce(jnp.add, [outs[b, p] for b in top])
                o = jnp.where(first_top, o2[:TOP_ROWS], o2[TOP_ROWS:])
                acc_ref[0:TOP_ROWS, cols[p]] += o
        return live_top, live_rest

    def sweep_left(j, live_top, live_rest, height):
        def cond(state):
            j, live_top, live_rest = state
            return jnp.logical_and(j >= 0, jnp.logical_or(live_top, live_rest))

        def body(state):
            return (state[0] - 1,) + sweep(state[0], [height])

        lax.while_loop(cond, body, (j, live_top, live_rest))

    def start(n_full, n_top):
        live_top, live_rest = sweep(qb, [CHUNK] * n_full + [TOP_ROWS] * n_top, diagonal=True)
        if n_top == 0:
            return
        left = qb - n_full - n_top

        @pl.when(live_rest)
        def _():
            flags = sweep(qb - n_full, [CHUNK] * n_top, fresh=True, ahead=False, rest_only=True)
            sweep_left(left, *flags, CHUNK)

        @pl.when(jnp.logical_not(live_rest))
        def _():
            sweep_left(left, live_top, jnp.bool_(False), TOP_ROWS)

    for blocks_left, shape in enumerate(((1, 0), (2, 0))):
        pl.when(qb == blocks_left)(functools.partial(start, *shape))
    pl.when(qb >= 2)(functools.partial(start, 2, 1))

    o_ref[...] = acc_ref[...].astype(BF16)


def _attention(q, k, v, batch, seq):
    q3, k3, v3 = (t.reshape(batch, seq, WIDTH) for t in (q, k, v))
    blk = pl.BlockSpec((None, CHUNK, WIDTH), lambda b, qb: (b, qb, 0))
    full = pl.BlockSpec((None, seq, WIDTH), lambda b, qb: (b, 0, 0), pipeline_mode=pl.Buffered(1))
    o = pl.pallas_call(
        _attn_kernel,
        grid=(batch, seq // CHUNK),
        in_specs=[blk, full, full],
        out_specs=blk,
        out_shape=jax.ShapeDtypeStruct((batch, seq, WIDTH), BF16),
        scratch_shapes=[pltpu.VMEM((WIDTH // LANES, 2 * CHUNK, LANES), F32),
                        pltpu.VMEM((CHUNK, WIDTH), F32),
                        pltpu.VMEM((WIDTH // LANES, 2 * CHUNK, LANES), F32)],
        compiler_params=pltpu.CompilerParams(
            dimension_semantics=("arbitrary", "arbitrary"), vmem_limit_bytes=VMEM_LIMIT),
        name="attn",
    )(q3, k3, v3)
    return o.reshape(batch * seq, WIDTH)


def _choose_tile(hb, wr_ref, br_ref):
    logits = jnp.dot(hb, wr_ref[...], preferred_element_type=F32) + br_ref[...]
    lane_t = lax.broadcasted_iota(jnp.int32, (TM_MIX, LANES), 1)
    lanef = lane_t.astype(F32)
    neg = -jnp.inf
    far = float(LANES)

    def first_max(vals):
        m = jnp.max(vals, axis=-1, keepdims=True)
        idx = jnp.min(jnp.where(vals == m, lanef, far), axis=-1, keepdims=True)
        return m, idx

    gl = jnp.where(lane_t < N_GROUPS, logits, neg)
    gmax, grp = first_max(gl)
    grp_w = 1.0 / jnp.sum(jnp.exp(gl - gmax), axis=-1, keepdims=True)
    lo_lane = ROUTE_LANE0 + EXPERTS_PER_GROUP * grp
    in_group = jnp.logical_and(lanef >= lo_lane, lanef < lo_lane + EXPERTS_PER_GROUP)
    el = jnp.where(in_group, logits, neg)
    m1, i1 = first_max(el)
    m2, i2 = first_max(jnp.where(lanef == i1, neg, el))
    e21 = jnp.exp(m2 - m1)
    w1 = grp_w / (1.0 + e21)
    w2 = w1 * e21
    return i1, i2, w1, w2


def _rank_tile(choice, run_ref, valid):
    i1, i2, w1, w2 = choice
    lane_t = lax.broadcasted_iota(jnp.int32, (TM_MIX, LANES), 1)
    lanef = lane_t.astype(F32)
    onehot = jnp.logical_and(jnp.logical_or(lanef == i1, lanef == i2), valid)
    rt = lax.broadcasted_iota(jnp.int32, (TM_MIX, TM_MIX), 0)
    ct = lax.broadcasted_iota(jnp.int32, (TM_MIX, TM_MIX), 1)
    before = jnp.where(ct < rt, 1.0, 0.0).astype(BF16)
    prior = jnp.dot(before, jnp.where(onehot, 1.0, 0.0).astype(BF16),
                    preferred_element_type=F32) + run_ref[...]
    r1 = jnp.sum(jnp.where(lanef == i1, prior, 0.0), axis=-1, keepdims=True)
    r2 = jnp.sum(jnp.where(lanef == i2, prior, 0.0), axis=-1, keepdims=True)
    run_ref[...] += jnp.sum(jnp.where(onehot, 1.0, 0.0), axis=0, keepdims=True)

    fields = (i1 - ROUTE_LANE0, i2 - ROUTE_LANE0, w1, w2, r1, r2)
    route = jnp.zeros((TM_MIX, LANES), F32)
    for pos, val in enumerate(fields):
        route = jnp.where(lane_t == pos, val, route)
    return route


def _mix_kernel(x_ref, oa_ref, u_ref, vs_ref, ga_ref, gb_ref,
                wsp_ref, bsp_ref, wua_ref, wub_ref, wo_ref, gf_ref, wr_ref, br_ref,
                xo_ref, h_ref, route_ref, cnt_ref, run_ref, xprev_ref):
    step = pl.program_id(0)

    @pl.when(step == 0)
    def _():
        run_ref[...] = jnp.zeros_like(run_ref)
        xprev_ref[...] = jnp.zeros_like(xprev_ref)

    lane = lax.broadcasted_iota(jnp.int32, (CHUNK, LANES), 1)
    row = lax.broadcasted_iota(jnp.int32, (CHUNK, LANES), 0)
    first = lane < HEAD_DIM
    tril = lane <= row

    w_pairs = []
    for gp in range(WIDTH // LANES):
        w_pairs.append(jnp.concatenate(
            [jnp.where(tril, wsp_ref[g], 0.0).astype(BF16) for g in (2 * gp, 2 * gp + 1)],
            axis=1))
    ob_chunks = []
    for c in range(TM_MIX // CHUNK):
        rows = slice(c * CHUNK, (c + 1) * CHUNK)
        cols = []
        for gp in range(WIDTH // LANES):
            vpair = vs_ref[rows, gp * LANES:(gp + 1) * LANES]
            zero = jnp.zeros_like(vpair)
            stacked = jnp.concatenate([jnp.where(first, vpair, zero),
                                       jnp.where(first, zero, vpair)], axis=0)
            cols.append(jnp.dot(w_pairs[gp], stacked, preferred_element_type=F32))
        mixed = jnp.concatenate(cols, axis=-1) + bsp_ref[...]
        ob_chunks.append((u_ref[rows, :].astype(F32) * mixed).astype(BF16))
    ob = jnp.concatenate(ob_chunks, axis=0)

    h = _rms(xprev_ref[...], gf_ref[...])
    _store_row_tiles(h_ref, h)
    choice = _choose_tile(h.astype(BF16), wr_ref, br_ref)

    up_a = jnp.dot(oa_ref[...], wua_ref[...], preferred_element_type=F32)
    up_b = jnp.dot(ob, wub_ref[...], preferred_element_type=F32)
    merged = ga_ref[...].astype(F32) * up_a + gb_ref[...].astype(F32) * up_b
    x = x_ref[...] + jnp.dot(merged.astype(BF16), wo_ref[...], preferred_element_type=F32)
    xo_ref[...] = x
    xprev_ref[...] = x

    route_ref[...] = _rank_tile(choice, run_ref, step > 0)
    cnt_ref[...] = run_ref[...]


def _mix(layer, x, oa, u, vs, ga, gb, wsp, bsp, wua, wub, wo, gf, wr, br):
    n = x.shape[0]
    tiles = n // TM_MIX
    cur = lambda i: jnp.minimum(i, tiles - 1)
    prev = lambda i: jnp.maximum(i - 1, 0)
    row = lambda width: pl.BlockSpec((TM_MIX, width), lambda i: (cur(i), 0))
    lay = lambda *shape: pl.BlockSpec((None,) + shape, lambda i: (layer,) + (0,) * len(shape))
    return pl.pallas_call(
        _mix_kernel,
        grid=(tiles + 1,),
        in_specs=[row(D_MODEL), row(WIDTH), row(WIDTH), row(WIDTH), row(D_MODEL), row(D_MODEL),
                  lay(HEADS, CHUNK, CHUNK), lay(CHUNK, WIDTH), lay(WIDTH, D_MODEL),
                  lay(WIDTH, D_MODEL), lay(D_MODEL, D_MODEL), lay(1, D_MODEL),
                  lay(D_MODEL, LANES), lay(1, LANES)],
        out_specs=[row(D_MODEL),
                   pl.BlockSpec((TM_MIX * ROW_TILE, LANES), lambda i: (prev(i), 0)),
                   pl.BlockSpec((TM_MIX, LANES), lambda i: (prev(i), 0)),
                   pl.BlockSpec((1, LANES), lambda i: (0, 0))],
        out_shape=[jax.ShapeDtypeStruct((n, D_MODEL), F32),
                   jax.ShapeDtypeStruct((n * ROW_TILE, LANES), U32),
                   jax.ShapeDtypeStruct((n, LANES), F32),
                   jax.ShapeDtypeStruct((1, LANES), F32)],
        scratch_shapes=[pltpu.VMEM((1, LANES), F32), pltpu.VMEM((TM_MIX, D_MODEL), F32)],
        compiler_params=pltpu.CompilerParams(
            dimension_semantics=("arbitrary",), vmem_limit_bytes=VMEM_LIMIT),
        name="mix",
    )(x, oa, u, vs, ga, gb, wsp, bsp, wua, wub, wo, gf, wr, br)


def _dispatch_kernel(dest_ref, zrow_ref, h_ref, xs_ref, zero_ref, sem, zsem, tsem):
    i = pl.program_id(0)
    block_rows = MOE_ROWS * ROW_TILE
    nblk = xs_ref.shape[0] // block_rows
    nused = zrow_ref[N_EXPERTS]

    def zero_copy(first_row, zero_sem):
        return pltpu.make_async_copy(zero_ref, xs_ref.at[pl.ds(first_row, block_rows), :],
                                     zero_sem)

    def tail_copy(j):
        return zero_copy((nused + j) * block_rows, tsem)

    @pl.when(i == 0)
    def _():
        zero_ref[...] = jnp.zeros_like(zero_ref)
        for j in range(N_EXPERTS):
            pl.when(nused + j < nblk)(lambda j=j: tail_copy(j).start())
        for e in range(N_EXPERTS):
            pl.when(zrow_ref[e] >= 0)(lambda e=e: zero_copy(zrow_ref[e], zsem).start())
        for e in range(N_EXPERTS):
            pl.when(zrow_ref[e] >= 0)(lambda e=e: zero_copy(zrow_ref[e], zsem).wait())

    n = dest_ref.shape[0] // 2
    base = i * TM_DISP
    for t in range(TM_DISP):
        for k in range(2):
            d = dest_ref[k * n + base + t]
            pltpu.make_async_copy(h_ref.at[pl.ds(t * ROW_TILE, ROW_TILE), :],
                                  xs_ref.at[pl.ds(d * ROW_TILE, ROW_TILE), :],
                                  sem).start(priority=k)
    for k in range(2):
        pltpu.make_async_copy(h_ref, xs_ref.at[pl.ds(0, TM_DISP * ROW_TILE), :], sem).wait()

    @pl.when(i == pl.num_programs(0) - 1)
    def _():
        for j in range(N_EXPERTS):
            pl.when(nused + j < nblk)(lambda j=j: tail_copy(j).wait())


def _dispatch(dest, zrow, h, cap):
    n = h.shape[0] // ROW_TILE
    return pl.pallas_call(
        _dispatch_kernel,
        grid_spec=pltpu.PrefetchScalarGridSpec(
            num_scalar_prefetch=2,
            grid=(n // TM_DISP,),
            in_specs=[pl.BlockSpec((TM_DISP * ROW_TILE, LANES), lambda i, dest, zrow: (i, 0))],
            out_specs=pl.BlockSpec(memory_space=pl.ANY),
            scratch_shapes=[pltpu.VMEM((MOE_ROWS * ROW_TILE, LANES), U32),
                            pltpu.SemaphoreType.DMA(()), pltpu.SemaphoreType.DMA(()),
                            pltpu.SemaphoreType.DMA(())],
        ),
        out_shape=jax.ShapeDtypeStruct((cap * ROW_TILE, LANES), U32),
        compiler_params=pltpu.CompilerParams(
            dimension_semantics=("arbitrary",), vmem_limit_bytes=VMEM_LIMIT),
        name="dispatch",
    )(dest, zrow, h)


def _expert_kernel(layer, blk_e_ref, nused_ref, half_ref, next_ref, xs_ref, wi_hbm, wo_hbm,
                   ys_ref, wi_f, wo_f, wi_b, wo_b, sem):
    b = pl.program_id(0)
    used = b < nused_ref[0]
    expert = blk_e_ref[b]

    def fetch(e, half):
        return (pltpu.make_async_copy(wi_hbm.at[layer, e], wi_f.at[half], sem.at[half, 0]),
                pltpu.make_async_copy(wo_hbm.at[layer, e], wo_f.at[half], sem.at[half, 1]))

    @pl.when(b == 0)
    def _():
        for copy in fetch(expert, half_ref[expert]):
            copy.start()

    @pl.when(jnp.logical_not(used))
    def _():
        ys_ref[...] = jnp.zeros_like(ys_ref)

    new_expert = jnp.logical_or(b == 0, expert != blk_e_ref[jnp.maximum(b - 1, 0)])

    @pl.when(jnp.logical_and(used, new_expert))
    def _():
        half = half_ref[expert]
        for copy in fetch(expert, half):
            copy.wait()
        following = next_ref[expert]

        @pl.when(following >= 0)
        def _():
            for copy in fetch(following, 1 - half):
                copy.start()

        wi_b[...] = wi_f[half].astype(BF16)
        wo_b[...] = wo_f[half].astype(BF16)

    @pl.when(used)
    def _():
        xs = _load_row_tiles(xs_ref, MOE_ROWS).astype(BF16)
        gu = jnp.dot(xs, wi_b[...], preferred_element_type=F32)
        act = jax.nn.silu(gu[:, :EXPERT_FF]) * gu[:, EXPERT_FF:]
        y = jnp.dot(act.astype(BF16), wo_b[...], preferred_element_type=F32)
        _store_row_tiles(ys_ref, y)


def _experts(layer, blk_e, nused, half, following, xs, w_e_in, w_e_out):
    cap = xs.shape[0] // ROW_TILE
    rows = MOE_ROWS * ROW_TILE
    return pl.pallas_call(
        functools.partial(_expert_kernel, layer),
        grid_spec=pltpu.PrefetchScalarGridSpec(
            num_scalar_prefetch=4,
            grid=(cap // MOE_ROWS,),
            in_specs=[pl.BlockSpec((rows, LANES),
                                   lambda b, blk_e, nused, *_: (jnp.minimum(b, nused[0] - 1), 0)),
                      pl.BlockSpec(memory_space=pl.ANY),
                      pl.BlockSpec(memory_space=pl.ANY)],
            out_specs=pl.BlockSpec((rows, LANES), lambda b, *_: (b, 0)),
            scratch_shapes=[pltpu.VMEM((2, D_MODEL, 2 * EXPERT_FF), F32),
                            pltpu.VMEM((2, EXPERT_FF, D_MODEL), F32),
                            pltpu.VMEM((D_MODEL, 2 * EXPERT_FF), BF16),
                            pltpu.VMEM((EXPERT_FF, D_MODEL), BF16),
                            pltpu.SemaphoreType.DMA((2, 2))],
        ),
        out_shape=jax.ShapeDtypeStruct((cap * ROW_TILE, LANES), U32),
        compiler_params=pltpu.CompilerParams(
            dimension_semantics=("arbitrary",), vmem_limit_bytes=VMEM_LIMIT),
        name="experts",
    )(blk_e, nused, half, following, xs, w_e_in, w_e_out)


def _ple_kernel(dest_ref, x_ref, route_ref, p_ref, ys_ref, gp_ref, wg_ref, wp_ref,
                xo_ref, y00, y01, y10, y11, sem):
    ybuf = ((y00, y01), (y10, y11))
    j = pl.program_id(0)
    last_step = pl.num_programs(0) - 1

    def start_gather(tile, half):
        n = dest_ref.shape[0] // 2
        base = tile * TM_ROW
        for t in range(TM_ROW):
            for k in range(2):
                d = dest_ref[k * n + base + t]
                pltpu.make_async_copy(ys_ref.at[pl.ds(d * ROW_TILE, ROW_TILE), :],
                                      ybuf[half][k].at[pl.ds(t * ROW_TILE, ROW_TILE), :],
                                      sem.at[half]).start(priority=k)

    def wait_gather(half):
        for k in range(2):
            pltpu.make_async_copy(ys_ref.at[pl.ds(0, TM_ROW * ROW_TILE), :],
                                  ybuf[half][k], sem.at[half]).wait()

    def combine(half):
        rows = slice(half * TM_ROW, (half + 1) * TM_ROW)
        pe = jnp.dot(p_ref[rows, :].astype(BF16), wp_ref[...], preferred_element_type=F32)
        wait_gather(half)
        route = route_ref[rows, :]
        y0 = _load_row_tiles(ybuf[half][0], TM_ROW)
        y1 = _load_row_tiles(ybuf[half][1], TM_ROW)
        x = x_ref[rows, :] + route[:, 2:3] * y0 + route[:, 3:4] * y1
        gate = jax.nn.sigmoid(jnp.dot(_rms(x, gp_ref[...]).astype(BF16), wg_ref[...],
                                      preferred_element_type=F32))
        xo_ref[rows, :] = x + gate * pe

    @pl.when(j == 0)
    def _():
        start_gather(0, 0)

    start_gather(2 * j + 1, 1)
    combine(0)
    start_gather(jnp.minimum(2 * j + 2, 2 * last_step + 1), 0)
    combine(1)

    @pl.when(j == last_step)
    def _():
        wait_gather(0)


def _ple(layer, dest, x, route, p, ys, gp, wg, wp):
    n = x.shape[0]
    row = lambda width: pl.BlockSpec((2 * TM_ROW, width), lambda j, dest: (j, 0))
    lay = lambda *shape: pl.BlockSpec((None,) + shape,
                                      lambda j, dest: (layer,) + (0,) * len(shape))
    return pl.pallas_call(
        _ple_kernel,
        grid_spec=pltpu.PrefetchScalarGridSpec(
            num_scalar_prefetch=1,
            grid=(n // (2 * TM_ROW),),
            in_specs=[row(D_MODEL), row(LANES),
                      pl.BlockSpec((None, 2 * TM_ROW, PLE_DIM), lambda j, dest: (layer, j, 0)),
                      pl.BlockSpec(memory_space=pl.ANY),
                      lay(1, D_MODEL), lay(D_MODEL, D_MODEL), lay(PLE_DIM, D_MODEL)],
            out_specs=row(D_MODEL),
            scratch_shapes=[pltpu.VMEM((TM_ROW * ROW_TILE, LANES), U32)] * 4
            + [pltpu.SemaphoreType.DMA((2,))],
        ),
        out_shape=jax.ShapeDtypeStruct((n, D_MODEL), F32),
        compiler_params=pltpu.CompilerParams(
            dimension_semantics=("arbitrary",), vmem_limit_bytes=VMEM_LIMIT),
        name="ple",
    )(dest, x, route, p, ys, gp, wg, wp)


def _slot_layout(route, counts, cap):
    cnt = counts[0, ROUTE_LANE0:ROUTE_LANE0 + N_EXPERTS].astype(jnp.int32)
    padded = (cnt + MOE_ROWS - 1) // MOE_ROWS * MOE_ROWS
    pend = jnp.cumsum(padded)
    poff = pend - padded
    ids = jnp.arange(N_EXPERTS, dtype=jnp.int32)

    def slots(k):
        eid = route[:, k].astype(jnp.int32)
        rank = route[:, 4 + k].astype(jnp.int32)
        return jnp.sum(jnp.where(eid[:, None] == ids, poff, 0), axis=-1) + rank

    dest = jnp.concatenate([slots(0), slots(1)]).astype(jnp.int32)
    blk_start = jnp.arange(cap // MOE_ROWS, dtype=jnp.int32) * MOE_ROWS
    blk_e = jnp.minimum(jnp.sum(pend[None, :] <= blk_start[:, None], axis=1), N_EXPERTS - 1)
    nused = (pend[-1:] // MOE_ROWS).astype(jnp.int32)
    zrow = jnp.where(cnt > 0, (pend - MOE_ROWS) * ROW_TILE, -1).astype(jnp.int32)
    nonempty = cnt > 0
    half = ((jnp.cumsum(nonempty) - 1) % 2).astype(jnp.int32)
    later = jnp.logical_and(nonempty[None, :], ids[None, :] > ids[:, None])
    following = jnp.min(jnp.where(later, ids[None, :], N_EXPERTS), axis=1)
    following = jnp.where(following < N_EXPERTS, following, -1).astype(jnp.int32)
    return (dest, jnp.concatenate([zrow, nused]), blk_e.astype(jnp.int32), nused, half,
            following)


def kernel(x, p, norm_mix, w_in, q_norm, k_norm, sgu_norm, w_spatial, b_spatial, w_up_a, w_up_b,
           w_out, norm_ffn, w_group_router, b_group_router, w_expert_router, b_expert_router,
           w_expert_in, w_expert_out, norm_ple, w_ple_gate, w_ple_proj):
    batch, seq, d = x.shape
    depth = w_in.shape[0]
    n = batch * seq
    cap = 2 * n + N_EXPERTS * MOE_ROWS

    vec = lambda a: a[:, None, :]
    w_ua_b, w_ub_b, w_o_b = (a.astype(BF16) for a in (w_up_a, w_up_b, w_out))
    w_pg_b, w_pp_b = w_ple_gate.astype(BF16), w_ple_proj.astype(BF16)
    qn2, kn2 = vec(jnp.tile(q_norm, (1, 2))), vec(jnp.tile(k_norm, (1, 2)))
    b_sp = jnp.repeat(jnp.swapaxes(b_spatial, 1, 2), HEAD_DIM, axis=2)
    pad = LANES - N_GROUPS - N_EXPERTS
    w_r = jnp.pad(jnp.concatenate([w_group_router, w_expert_router], axis=2),
                  ((0, 0), (0, 0), (0, pad))).astype(BF16)
    b_r = vec(jnp.pad(jnp.concatenate([b_group_router, b_expert_router], axis=1),
                      ((0, 0), (0, pad))))
    p2 = p.reshape(depth, n, PLE_DIM)

    xf = x.reshape(n, d)
    for i in range(depth):
        q, k, v, u, vs, ga, gb = _inproj(i, xf, vec(norm_mix), w_in, qn2, kn2, vec(sgu_norm))
        oa = _attention(q, k, v, batch, seq)
        xf, h, route, counts = _mix(i, xf, oa, u, vs, ga, gb, w_spatial, b_sp, w_ua_b, w_ub_b,
                                    w_o_b, vec(norm_ffn), w_r, b_r)
        dest, zrow, blk_e, nused, half, following = _slot_layout(route, counts, cap)
        xs = _dispatch(dest, zrow, h, cap)
        ys = _experts(i, blk_e, nused, half, following, xs, w_expert_in, w_expert_out)
        xf = _ple(i, dest, xf, route, p2, ys, vec(norm_ple), w_pg_b, w_pp_b)
    return xf.reshape(batch, seq, d)
```

```python
import functools
import math

import jax
import jax.numpy as jnp
from jax import lax
from jax.experimental import pallas as pl
from jax.experimental.pallas import tpu as pltpu

F32 = jnp.float32
BF16 = jnp.bfloat16

D_MODEL = 1024
HEADS = 8
HEAD_DIM = 64
WIDTH = HEADS * HEAD_DIM
CHUNK = 128
IN_WIDTH = 5 * WIDTH + 2 * D_MODEL
N_GROUPS = 4
EXPERTS_PER_GROUP = 8
N_EXPERTS = N_GROUPS * EXPERTS_PER_GROUP
EXPERT_FF = 512
PLE_DIM = 256
EPS = 1e-6

LANES = 128
ROW_TILE = D_MODEL // (2 * LANES)
U32 = jnp.uint32
ROUTE_LANE0 = N_GROUPS
TM_IN = 256
TM_MIX = 256
TM_DISP = 512
TM_ROW = 256
MOE_ROWS = 256
TOP_ROWS = 64
LOG2E = 1.4426950408889634
DEAD_LOG2 = -105.0 * LOG2E
VMEM_LIMIT = 48 * 1024 * 1024


def _rms(x, gain):
    return x * lax.rsqrt(jnp.mean(x * x, axis=-1, keepdims=True) + EPS) * gain


def _store_row_tiles(ref, value, lead=()):
    rows = value.shape[0]
    bits = lambda v: lax.bitcast_convert_type(v.astype(BF16).astype(F32), U32)
    for c in range(ROW_TILE):
        low = bits(value[:, 2 * c * LANES:(2 * c + 1) * LANES]) >> 16
        high = bits(value[:, (2 * c + 1) * LANES:(2 * c + 2) * LANES]) & U32(0xFFFF0000)
        ref[lead + (pl.ds(c, rows, stride=ROW_TILE), slice(None))] = low | high


def _load_row_tiles(ref, rows, lead=()):
    chunks = []
    for c in range(ROW_TILE):
        words = ref[lead + (pl.ds(c, rows, stride=ROW_TILE), slice(None))]
        chunks.append(lax.bitcast_convert_type(words << 16, F32))
        chunks.append(lax.bitcast_convert_type(words & U32(0xFFFF0000), F32))
    return jnp.concatenate(chunks, axis=1)


def _inproj_kernel(x_ref, g_ref, wf_ref, qn_ref, kn_ref, sn_ref,
                   q_ref, k_ref, v_ref, u_ref, vs_ref, ga_ref, gb_ref, w_ref):
    @pl.when(pl.program_id(0) == 0)
    def _():
        w_ref[...] = wf_ref[...].astype(BF16)

    hb = _rms(x_ref[...], g_ref[...]).astype(BF16)

    def proj(lo, hi):
        return jnp.dot(hb, w_ref[:, lo:hi], preferred_element_type=F32)

    first = lax.broadcasted_iota(jnp.int32, (1, LANES), 1) < HEAD_DIM

    def head_norm(z, gain, scale):
        outs = []
        for c in range(WIDTH // LANES):
            zc = z[:, c * LANES:(c + 1) * LANES]
            sq = zc * zc
            sa = jnp.sum(jnp.where(first, sq, 0.0), axis=-1, keepdims=True)
            sb = jnp.sum(jnp.where(first, 0.0, sq), axis=-1, keepdims=True)
            ms = jnp.where(first, sa, sb) * (1.0 / HEAD_DIM)
            outs.append(zc * lax.rsqrt(ms + EPS) * (gain * scale))
        return jnp.concatenate(outs, axis=-1)

    w = WIDTH
    q_ref[...] = head_norm(proj(0, w), qn_ref[...], LOG2E / math.sqrt(HEAD_DIM)).astype(BF16)
    k_ref[...] = head_norm(proj(w, 2 * w), kn_ref[...], 1.0).astype(BF16)
    v_ref[...] = proj(2 * w, 3 * w).astype(BF16)
    u_ref[...] = jax.nn.gelu(proj(3 * w, 4 * w)).astype(BF16)
    vs_ref[...] = _rms(jax.nn.gelu(proj(4 * w, 5 * w)), sn_ref[...]).astype(BF16)
    ga_ref[...] = jax.nn.sigmoid(proj(5 * w, 5 * w + D_MODEL)).astype(BF16)
    gb_ref[...] = jax.nn.sigmoid(proj(5 * w + D_MODEL, IN_WIDTH)).astype(BF16)


def _inproj(layer, x, norm_mix, w_in, qn, kn, sn):
    n = x.shape[0]
    row = lambda width: pl.BlockSpec((TM_IN, width), lambda i: (i, 0))
    vec = lambda width: pl.BlockSpec((None, 1, width), lambda i: (layer, 0, 0))
    out = lambda width: jax.ShapeDtypeStruct((n, width), BF16)
    return pl.pallas_call(
        _inproj_kernel,
        grid=(n // TM_IN,),
        in_specs=[row(D_MODEL), vec(D_MODEL),
                  pl.BlockSpec((None, D_MODEL, IN_WIDTH), lambda i: (layer, 0, 0),
                               pipeline_mode=pl.Buffered(1)),
                  vec(LANES), vec(LANES), vec(WIDTH)],
        out_specs=[row(WIDTH)] * 5 + [row(D_MODEL)] * 2,
        out_shape=[out(WIDTH)] * 5 + [out(D_MODEL)] * 2,
        scratch_shapes=[pltpu.VMEM((D_MODEL, IN_WIDTH), BF16)],
        compiler_params=pltpu.CompilerParams(
            dimension_semantics=("arbitrary",), vmem_limit_bytes=VMEM_LIMIT),
        name="inproj",
    )(x, norm_mix, w_in, qn, kn, sn)


def _attn_kernel(q_ref, k_ref, v_ref, o_ref, carry_ref, acc_ref, z_ref):
    qb = pl.program_id(1)
    pairs = WIDTH // LANES
    first = lax.broadcasted_iota(jnp.int32, (CHUNK, LANES), 1) < HEAD_DIM
    first_top = lax.broadcasted_iota(jnp.int32, (TOP_ROWS, LANES), 1) < HEAD_DIM
    kr = lax.broadcasted_iota(jnp.int32, (LANES, 2 * LANES), 0)
    kc = lax.broadcasted_iota(jnp.int32, (LANES, 2 * LANES), 1)
    tail = jnp.where(jnp.logical_or(kr > kc, kc >= LANES), 1.0, 0.0).astype(BF16)
    sign = jnp.uint32(0x80000000)
    cols = [slice(p * LANES, (p + 1) * LANES) for p in range(pairs)]

    def stack(value, rows):
        if rows == CHUNK:
            return value
        return jnp.concatenate([value[:rows], value[CHUNK:CHUNK + rows]], axis=0)

    queries, causal = {}, {}
    for rows in (CHUNK, TOP_ROWS):
        lane = lax.broadcasted_iota(jnp.int32, (2 * rows, LANES), 1)
        row = lax.broadcasted_iota(jnp.int32, (2 * rows, LANES), 0)
        upper = row < rows
        own = jnp.logical_xor(lane < HEAD_DIM, jnp.logical_not(upper))
        causal[rows] = lane < jnp.where(upper, row, row - rows)
        queries[rows] = []
        for p in range(pairs):
            qp = q_ref[:rows, cols[p]]
            q2 = jnp.concatenate([qp, qp], axis=0)
            queries[rows].append(jnp.where(own, q2, jnp.zeros_like(q2)))
    full_row = lax.broadcasted_iota(jnp.int32, (2 * CHUNK, LANES), 0)
    is_top = jnp.where(full_row < CHUNK, full_row, full_row - CHUNK) < TOP_ROWS

    def scores(j, rows):
        start = pl.multiple_of(j * CHUNK, CHUNK)
        return [lax.dot_general(queries[rows][p], k_ref[pl.ds(start, CHUNK), cols[p]],
                                (((1,), (1,)), ((), ())), preferred_element_type=F32)
                for p in range(pairs)]

    def sweep(j, heights, diagonal=False, fresh=False, ahead=True, rest_only=False):
        blocks = range(len(heights))
        full = [b for b in blocks if heights[b] == CHUNK]
        top = [b for b in blocks if heights[b] == TOP_ROWS]
        starts = [pl.multiple_of((j - b) * CHUNK, CHUNK) for b in blocks]
        zs = []
        for b in blocks:
            if b > 0 or fresh or diagonal:
                zs.append(scores(j - b, heights[b]))
            else:
                zs.append([stack(z_ref[p], heights[b]) for p in range(pairs)])
        if ahead:
            for p, z in enumerate(scores(jnp.maximum(j - len(heights), 0), CHUNK)):
                z_ref[p] = z
        log_betas, splits = {}, {}
        for b in blocks:
            for p in range(pairs):
                z = zs[b][p]
                neg_abs = lax.bitcast_convert_type(
                    lax.bitcast_convert_type(z, jnp.uint32) | sign, F32)
                log1p = jnp.log(1.0 + jnp.exp2(neg_abs)) * LOG2E
                log_beta = jnp.minimum(z, 0.0) - log1p
                log_keep = log_beta - z
                if diagonal and b == 0:
                    log_keep = jnp.where(causal[heights[b]], log_keep, 0.0)
                if rest_only:
                    log_keep = jnp.where(is_top, 0.0, log_keep)
                log_betas[b, p] = log_beta
                splits[b, p] = log_keep.astype(BF16)
        sums = {key: jnp.dot(s, tail, preferred_element_type=F32) for key, s in splits.items()}
        entering, after_full, after_top = {}, [], []
        for p in range(pairs):
            carry = 0.0 if diagonal else carry_ref[p]
            for b in full:
                entering[b, p] = carry
                carry = carry + sums[b, p][:, LANES:]
            after_full.append(carry)
            carry = stack(carry, TOP_ROWS) if top else None
            for b in top:
                entering[b, p] = carry
                carry = carry + sums[b, p][:, LANES:]
            after_top.append(carry)
        alive = lambda c: jnp.max(c) > DEAD_LOG2
        worst = functools.reduce(jnp.maximum, after_full)
        live_rest = alive(jnp.where(is_top, -jnp.inf, worst)) if full else jnp.bool_(False)
        if top:
            live_top = alive(functools.reduce(jnp.maximum, after_top))
        else:
            live_top = alive(jnp.where(is_top, worst, -jnp.inf))
        outs = {}
        for b in blocks:
            for p in range(pairs):
                a = jnp.exp2(log_betas[b, p] + sums[b, p][:, :LANES] + entering[b, p])
                if diagonal and b == 0:
                    a = jnp.where(causal[heights[b]], a, 0.0)
                if rest_only:
                    a = jnp.where(is_top, 0.0, a)
                outs[b, p] = jnp.dot(a.astype(BF16), v_ref[pl.ds(starts[b], CHUNK), cols[p]],
                                     preferred_element_type=F32)
        for p in range(pairs):
            if full:
                carry_ref[p] = after_full[p]
            if top:
                carry_ref[p, 0:TOP_ROWS, :] = after_top[p][:TOP_ROWS]
                carry_ref[p, CHUNK:CHUNK + TOP_ROWS, :] = after_top[p][TOP_ROWS:]
        for p in range(pairs):
            if full:
                o2 = functools.reduce(jnp.add, [outs[b, p] for b in full])
                o = jnp.where(first, o2[:CHUNK], o2[CHUNK:])
                acc_ref[:, cols[p]] = o if diagonal else acc_ref[:, cols[p]] + o
            if top:
                o2 = functools.reduce(jnp.add, [outs[b, p] for b in top])
                o = jnp.where(first_top, o2[:TOP_ROWS], o2[TOP_ROWS:])
                acc_ref[0:TOP_ROWS, cols[p]] += o
        return live_top, live_rest

    def sweep_left(j, live_top, live_rest, height):
        def cond(state):
            j, live_top, live_rest = state
            return jnp.logical_and(j >= 0, jnp.logical_or(live_top, live_rest))

        def body(state):
            return (state[0] - 1,) + sweep(state[0], [height])

        lax.while_loop(cond, body, (j, live_top, live_rest))

    def start(n_full, n_top):
        live_top, live_rest = sweep(qb, [CHUNK] * n_full + [TOP_ROWS] * n_top, diagonal=True)
        if n_top == 0:
            return
        left = qb - n_full - n_top

        @pl.when(live_rest)
        def _():
            flags = sweep(qb - n_full, [CHUNK] * n_top, fresh=True, ahead=False, rest_only=True)
            sweep_left(left, *flags, CHUNK)

        @pl.when(jnp.logical_not(live_rest))
        def _():
            sweep_left(left, live_top, jnp.bool_(False), TOP_ROWS)

    for blocks_left, shape in enumerate(((1, 0), (2, 0))):
        pl.when(qb == blocks_left)(functools.partial(start, *shape))
    pl.when(qb >= 2)(functools.partial(start, 2, 1))

    o_ref[...] = acc_ref[...].astype(BF16)


def _attention(q, k, v, batch, seq):
    q3, k3, v3 = (t.reshape(batch, seq, WIDTH) for t in (q, k, v))
    blk = pl.BlockSpec((None, CHUNK, WIDTH), lambda b, qb: (b, qb, 0))
    full = pl.BlockSpec((None, seq, WIDTH), lambda b, qb: (b, 0, 0), pipeline_mode=pl.Buffered(1))
    o = pl.pallas_call(
        _attn_kernel,
        grid=(batch, seq // CHUNK),
        in_specs=[blk, full, full],
        out_specs=blk,
        out_shape=jax.ShapeDtypeStruct((batch, seq, WIDTH), BF16),
        scratch_shapes=[pltpu.VMEM((WIDTH // LANES, 2 * CHUNK, LANES), F32),
                        pltpu.VMEM((CHUNK, WIDTH), F32),
                        pltpu.VMEM((WIDTH // LANES, 2 * CHUNK, LANES), F32)],
        compiler_params=pltpu.CompilerParams(
            dimension_semantics=("arbitrary", "arbitrary"), vmem_limit_bytes=VMEM_LIMIT),
        name="attn",
    )(q3, k3, v3)
    return o.reshape(batch * seq, WIDTH)


def _choose_tile(hb, wr_ref, br_ref):
    logits = jnp.dot(hb, wr_ref[...], preferred_element_type=F32) + br_ref[...]
    lane_t = lax.broadcasted_iota(jnp.int32, (TM_MIX, LANES), 1)
    lanef = lane_t.astype(F32)
    neg = -jnp.inf
    far = float(LANES)

    def first_max(vals):
        m = jnp.max(vals, axis=-1, keepdims=True)
        idx = jnp.min(jnp.where(vals == m, lanef, far), axis=-1, keepdims=True)
        return m, idx

    gl = jnp.where(lane_t < N_GROUPS, logits, neg)
    gmax, grp = first_max(gl)
    grp_w = 1.0 / jnp.sum(jnp.exp(gl - gmax), axis=-1, keepdims=True)
    lo_lane = ROUTE_LANE0 + EXPERTS_PER_GROUP * grp
    in_group = jnp.logical_and(lanef >= lo_lane, lanef < lo_lane + EXPERTS_PER_GROUP)
    el = jnp.where(in_group, logits, neg)
    m1, i1 = first_max(el)
    m2, i2 = first_max(jnp.where(lanef == i1, neg, el))
    e21 = jnp.exp(m2 - m1)
    w1 = grp_w / (1.0 + e21)
    w2 = w1 * e21
    return i1, i2, w1, w2


def _rank_tile(choice, run_ref, valid):
    i1, i2, w1, w2 = choice
    lane_t = lax.broadcasted_iota(jnp.int32, (TM_MIX, LANES), 1)
    lanef = lane_t.astype(F32)
    onehot = jnp.logical_and(jnp.logical_or(lanef == i1, lanef == i2), valid)
    rt = lax.broadcasted_iota(jnp.int32, (TM_MIX, TM_MIX), 0)
    ct = lax.broadcasted_iota(jnp.int32, (TM_MIX, TM_MIX), 1)
    before = jnp.where(ct < rt, 1.0, 0.0).astype(BF16)
    prior = jnp.dot(before, jnp.where(onehot, 1.0, 0.0).astype(BF16),
                    preferred_element_type=F32) + run_ref[...]
    r1 = jnp.sum(jnp.where(lanef == i1, prior, 0.0), axis=-1, keepdims=True)
    r2 = jnp.sum(jnp.where(lanef == i2, prior, 0.0), axis=-1, keepdims=True)
    run_ref[...] += jnp.sum(jnp.where(onehot, 1.0, 0.0), axis=0, keepdims=True)

    fields = (i1 - ROUTE_LANE0, i2 - ROUTE_LANE0, w1, w2, r1, r2)
    route = jnp.zeros((TM_MIX, LANES), F32)
    for pos, val in enumerate(fields):
        route = jnp.where(lane_t == pos, val, route)
    return route


def _mix_kernel(x_ref, oa_ref, u_ref, vs_ref, ga_ref, gb_ref,
                wsp_ref, bsp_ref, wua_ref, wub_ref, wo_ref, gf_ref, wr_ref, br_ref,
                xo_ref, h_ref, route_ref, cnt_ref, run_ref, xprev_ref, wua_b, wub_b, wo_b):
    step = pl.program_id(0)

    @pl.when(step == 0)
    def _():
        run_ref[...] = jnp.zeros_like(run_ref)
        xprev_ref[...] = jnp.zeros_like(xprev_ref)
        wua_b[...] = wua_ref[...].astype(BF16)
        wub_b[...] = wub_ref[...].astype(BF16)
        wo_b[...] = wo_ref[...].astype(BF16)

    lane = lax.broadcasted_iota(jnp.int32, (CHUNK, LANES), 1)
    row = lax.broadcasted_iota(jnp.int32, (CHUNK, LANES), 0)
    first = lane < HEAD_DIM
    tril = lane <= row

    w_pairs = []
    for gp in range(WIDTH // LANES):
        w_pairs.append(jnp.concatenate(
            [jnp.where(tril, wsp_ref[g], 0.0).astype(BF16) for g in (2 * gp, 2 * gp + 1)],
            axis=1))
    ob_chunks = []
    for c in range(TM_MIX // CHUNK):
        rows = slice(c * CHUNK, (c + 1) * CHUNK)
        cols = []
        for gp in range(WIDTH // LANES):
            vpair = vs_ref[rows, gp * LANES:(gp + 1) * LANES]
            zero = jnp.zeros_like(vpair)
            stacked = jnp.concatenate([jnp.where(first, vpair, zero),
                                       jnp.where(first, zero, vpair)], axis=0)
            cols.append(jnp.dot(w_pairs[gp], stacked, preferred_element_type=F32))
        mixed = jnp.concatenate(cols, axis=-1) + bsp_ref[...]
        ob_chunks.append((u_ref[rows, :].astype(F32) * mixed).astype(BF16))
    ob = jnp.concatenate(ob_chunks, axis=0)

    h = _rms(xprev_ref[...], gf_ref[...])
    _store_row_tiles(h_ref, h)
    choice = _choose_tile(h.astype(BF16), wr_ref, br_ref)

    up_a = jnp.dot(oa_ref[...], wua_b[...], preferred_element_type=F32)
    up_b = jnp.dot(ob, wub_b[...], preferred_element_type=F32)
    merged = ga_ref[...].astype(F32) * up_a + gb_ref[...].astype(F32) * up_b
    x = x_ref[...] + jnp.dot(merged.astype(BF16), wo_b[...], preferred_element_type=F32)
    xo_ref[...] = x
    xprev_ref[...] = x

    route_ref[...] = _rank_tile(choice, run_ref, step > 0)
    cnt_ref[...] = run_ref[...]


def _mix(layer, x, oa, u, vs, ga, gb, wsp, bsp, wua, wub, wo, gf, wr, br):
    n = x.shape[0]
    tiles = n // TM_MIX
    cur = lambda i: jnp.minimum(i, tiles - 1)
    prev = lambda i: jnp.maximum(i - 1, 0)
    row = lambda width: pl.BlockSpec((TM_MIX, width), lambda i: (cur(i), 0))
    lay = lambda *shape: pl.BlockSpec((None,) + shape, lambda i: (layer,) + (0,) * len(shape))
    once = lambda *shape: pl.BlockSpec((None,) + shape, lambda i: (layer,) + (0,) * len(shape),
                                       pipeline_mode=pl.Buffered(1))
    return pl.pallas_call(
        _mix_kernel,
        grid=(tiles + 1,),
        in_specs=[row(D_MODEL), row(WIDTH), row(WIDTH), row(WIDTH), row(D_MODEL), row(D_MODEL),
                  lay(HEADS, CHUNK, CHUNK), lay(CHUNK, WIDTH), once(WIDTH, D_MODEL),
                  once(WIDTH, D_MODEL), once(D_MODEL, D_MODEL), lay(1, D_MODEL),
                  lay(D_MODEL, LANES), lay(1, LANES)],
        out_specs=[row(D_MODEL),
                   pl.BlockSpec((TM_MIX * ROW_TILE, LANES), lambda i: (prev(i), 0)),
                   pl.BlockSpec((TM_MIX, LANES), lambda i: (prev(i), 0)),
                   pl.BlockSpec((1, LANES), lambda i: (0, 0))],
        out_shape=[jax.ShapeDtypeStruct((n, D_MODEL), F32),
                   jax.ShapeDtypeStruct((n * ROW_TILE, LANES), U32),
                   jax.ShapeDtypeStruct((n, LANES), F32),
                   jax.ShapeDtypeStruct((1, LANES), F32)],
        scratch_shapes=[pltpu.VMEM((1, LANES), F32), pltpu.VMEM((TM_MIX, D_MODEL), F32),
                        pltpu.VMEM((WIDTH, D_MODEL), BF16), pltpu.VMEM((WIDTH, D_MODEL), BF16),
                        pltpu.VMEM((D_MODEL, D_MODEL), BF16)],
        compiler_params=pltpu.CompilerParams(
            dimension_semantics=("arbitrary",), vmem_limit_bytes=VMEM_LIMIT),
        name="mix",
    )(x, oa, u, vs, ga, gb, wsp, bsp, wua, wub, wo, gf, wr, br)


def _dispatch_kernel(dest_ref, zrow_ref, h_ref, xs_ref, zero_ref, sem, zsem, tsem):
    i = pl.program_id(0)
    block_rows = MOE_ROWS * ROW_TILE
    nblk = xs_ref.shape[0] // block_rows
    nused = zrow_ref[N_EXPERTS]

    def zero_copy(first_row, zero_sem):
        return pltpu.make_async_copy(zero_ref, xs_ref.at[pl.ds(first_row, block_rows), :],
                                     zero_sem)

    def tail_copy(j):
        return zero_copy((nused + j) * block_rows, tsem)

    @pl.when(i == 0)
    def _():
        zero_ref[...] = jnp.zeros_like(zero_ref)
        for j in range(N_EXPERTS):
            pl.when(nused + j < nblk)(lambda j=j: tail_copy(j).start())
        for e in range(N_EXPERTS):
            pl.when(zrow_ref[e] >= 0)(lambda e=e: zero_copy(zrow_ref[e], zsem).start())
        for e in range(N_EXPERTS):
            pl.when(zrow_ref[e] >= 0)(lambda e=e: zero_copy(zrow_ref[e], zsem).wait())

    n = dest_ref.shape[0] // 2
    base = i * TM_DISP
    for t in range(TM_DISP):
        for k in range(2):
            d = dest_ref[k * n + base + t]
            pltpu.make_async_copy(h_ref.at[pl.ds(t * ROW_TILE, ROW_TILE), :],
                                  xs_ref.at[pl.ds(d * ROW_TILE, ROW_TILE), :],
                                  sem).start(priority=k)
    for k in range(2):
        pltpu.make_async_copy(h_ref, xs_ref.at[pl.ds(0, TM_DISP * ROW_TILE), :], sem).wait()

    @pl.when(i == pl.num_programs(0) - 1)
    def _():
        for j in range(N_EXPERTS):
            pl.when(nused + j < nblk)(lambda j=j: tail_copy(j).wait())


def _dispatch(dest, zrow, h, cap):
    n = h.shape[0] // ROW_TILE
    return pl.pallas_call(
        _dispatch_kernel,
        grid_spec=pltpu.PrefetchScalarGridSpec(
            num_scalar_prefetch=2,
            grid=(n // TM_DISP,),
            in_specs=[pl.BlockSpec((TM_DISP * ROW_TILE, LANES), lambda i, dest, zrow: (i, 0))],
            out_specs=pl.BlockSpec(memory_space=pl.ANY),
            scratch_shapes=[pltpu.VMEM((MOE_ROWS * ROW_TILE, LANES), U32),
                            pltpu.SemaphoreType.DMA(()), pltpu.SemaphoreType.DMA(()),
                            pltpu.SemaphoreType.DMA(())],
        ),
        out_shape=jax.ShapeDtypeStruct((cap * ROW_TILE, LANES), U32),
        compiler_params=pltpu.CompilerParams(
            dimension_semantics=("arbitrary",), vmem_limit_bytes=VMEM_LIMIT),
        name="dispatch",
    )(dest, zrow, h)


def _expert_kernel(layer, blk_e_ref, nused_ref, half_ref, next_ref, xs_ref, wi_hbm, wo_hbm,
                   ys_ref, wi_f, wo_f, wi_b, wo_b, sem):
    b = pl.program_id(0)
    used = b < nused_ref[0]
    expert = blk_e_ref[b]

    def fetch(e, half):
        return (pltpu.make_async_copy(wi_hbm.at[layer, e], wi_f.at[half], sem.at[half, 0]),
                pltpu.make_async_copy(wo_hbm.at[layer, e], wo_f.at[half], sem.at[half, 1]))

    @pl.when(b == 0)
    def _():
        for copy in fetch(expert, half_ref[expert]):
            copy.start()

    @pl.when(jnp.logical_not(used))
    def _():
        ys_ref[...] = jnp.zeros_like(ys_ref)

    new_expert = jnp.logical_or(b == 0, expert != blk_e_ref[jnp.maximum(b - 1, 0)])

    @pl.when(jnp.logical_and(used, new_expert))
    def _():
        half = half_ref[expert]
        for copy in fetch(expert, half):
            copy.wait()
        following = next_ref[expert]

        @pl.when(following >= 0)
        def _():
            for copy in fetch(following, 1 - half):
                copy.start()

        wi_b[...] = wi_f[half].astype(BF16)
        wo_b[...] = wo_f[half].astype(BF16)

    @pl.when(used)
    def _():
        xs = _load_row_tiles(xs_ref, MOE_ROWS).astype(BF16)
        gu = jnp.dot(xs, wi_b[...], preferred_element_type=F32)
        act = jax.nn.silu(gu[:, :EXPERT_FF]) * gu[:, EXPERT_FF:]
        y = jnp.dot(act.astype(BF16), wo_b[...], preferred_element_type=F32)
        _store_row_tiles(ys_ref, y)


def _experts(layer, blk_e, nused, half, following, xs, w_e_in, w_e_out):
    cap = xs.shape[0] // ROW_TILE
    rows = MOE_ROWS * ROW_TILE
    return pl.pallas_call(
        functools.partial(_expert_kernel, layer),
        grid_spec=pltpu.PrefetchScalarGridSpec(
            num_scalar_prefetch=4,
            grid=(cap // MOE_ROWS,),
            in_specs=[pl.BlockSpec((rows, LANES),
                                   lambda b, blk_e, nused, *_: (jnp.minimum(b, nused[0] - 1), 0)),
                      pl.BlockSpec(memory_space=pl.ANY),
                      pl.BlockSpec(memory_space=pl.ANY)],
            out_specs=pl.BlockSpec((rows, LANES), lambda b, *_: (b, 0)),
            scratch_shapes=[pltpu.VMEM((2, D_MODEL, 2 * EXPERT_FF), F32),
                            pltpu.VMEM((2, EXPERT_FF, D_MODEL), F32),
                            pltpu.VMEM((D_MODEL, 2 * EXPERT_FF), BF16),
                            pltpu.VMEM((EXPERT_FF, D_MODEL), BF16),
                            pltpu.SemaphoreType.DMA((2, 2))],
        ),
        out_shape=jax.ShapeDtypeStruct((cap * ROW_TILE, LANES), U32),
        compiler_params=pltpu.CompilerParams(
            dimension_semantics=("arbitrary",), vmem_limit_bytes=VMEM_LIMIT),
        name="experts",
    )(blk_e, nused, half, following, xs, w_e_in, w_e_out)


def _ple_kernel(dest_ref, x_ref, route_ref, p_ref, ys_ref, gp_ref, wg_ref, wp_ref,
                xo_ref, y00, y01, y10, y11, wg_b, wp_b, sem):
    ybuf = ((y00, y01), (y10, y11))
    j = pl.program_id(0)

    @pl.when(j == 0)
    def _():
        wg_b[...] = wg_ref[...].astype(BF16)
        wp_b[...] = wp_ref[...].astype(BF16)
    last_step = pl.num_programs(0) - 1

    def start_gather(tile, half):
        n = dest_ref.shape[0] // 2
        base = tile * TM_ROW
        for t in range(TM_ROW):
            for k in range(2):
                d = dest_ref[k * n + base + t]
                pltpu.make_async_copy(ys_ref.at[pl.ds(d * ROW_TILE, ROW_TILE), :],
                                      ybuf[half][k].at[pl.ds(t * ROW_TILE, ROW_TILE), :],
                                      sem.at[half]).start(priority=k)

    def wait_gather(half):
        for k in range(2):
            pltpu.make_async_copy(ys_ref.at[pl.ds(0, TM_ROW * ROW_TILE), :],
                                  ybuf[half][k], sem.at[half]).wait()

    def combine(half):
        rows = slice(half * TM_ROW, (half + 1) * TM_ROW)
        pe = jnp.dot(p_ref[rows, :].astype(BF16), wp_b[...], preferred_element_type=F32)
        wait_gather(half)
        route = route_ref[rows, :]
        y0 = _load_row_tiles(ybuf[half][0], TM_ROW)
        y1 = _load_row_tiles(ybuf[half][1], TM_ROW)
        x = x_ref[rows, :] + route[:, 2:3] * y0 + route[:, 3:4] * y1
        gate = jax.nn.sigmoid(jnp.dot(_rms(x, gp_ref[...]).astype(BF16), wg_b[...],
                                      preferred_element_type=F32))
        xo_ref[rows, :] = x + gate * pe

    @pl.when(j == 0)
    def _():
        start_gather(0, 0)

    start_gather(2 * j + 1, 1)
    combine(0)
    start_gather(jnp.minimum(2 * j + 2, 2 * last_step + 1), 0)
    combine(1)

    @pl.when(j == last_step)
    def _():
        wait_gather(0)


def _ple(layer, dest, x, route, p, ys, gp, wg, wp):
    n = x.shape[0]
    row = lambda width: pl.BlockSpec((2 * TM_ROW, width), lambda j, dest: (j, 0))
    lay = lambda *shape: pl.BlockSpec((None,) + shape,
                                      lambda j, dest: (layer,) + (0,) * len(shape))
    once = lambda *shape: pl.BlockSpec((None,) + shape,
                                       lambda j, dest: (layer,) + (0,) * len(shape),
                                       pipeline_mode=pl.Buffered(1))
    return pl.pallas_call(
        _ple_kernel,
        grid_spec=pltpu.PrefetchScalarGridSpec(
            num_scalar_prefetch=1,
            grid=(n // (2 * TM_ROW),),
            in_specs=[row(D_MODEL), row(LANES),
                      pl.BlockSpec((None, 2 * TM_ROW, PLE_DIM), lambda j, dest: (layer, j, 0)),
                      pl.BlockSpec(memory_space=pl.ANY),
                      lay(1, D_MODEL), once(D_MODEL, D_MODEL), once(PLE_DIM, D_MODEL)],
            out_specs=row(D_MODEL),
            scratch_shapes=[pltpu.VMEM((TM_ROW * ROW_TILE, LANES), U32)] * 4
            + [pltpu.VMEM((D_MODEL, D_MODEL), BF16), pltpu.VMEM((PLE_DIM, D_MODEL), BF16),
               pltpu.SemaphoreType.DMA((2,))],
        ),
        out_shape=jax.ShapeDtypeStruct((n, D_MODEL), F32),
        compiler_params=pltpu.CompilerParams(
            dimension_semantics=("arbitrary",), vmem_limit_bytes=VMEM_LIMIT),
        name="ple",
    )(dest, x, route, p, ys, gp, wg, wp)


def _slot_layout(route, counts, cap):
    cnt = counts[0, ROUTE_LANE0:ROUTE_LANE0 + N_EXPERTS].astype(jnp.int32)
    padded = (cnt + MOE_ROWS - 1) // MOE_ROWS * MOE_ROWS
    pend = jnp.cumsum(padded)
    poff = pend - padded
    ids = jnp.arange(N_EXPERTS, dtype=jnp.int32)

    def slots(k):
        eid = route[:, k].astype(jnp.int32)
        slot = route[:, 4 + k].astype(jnp.int32)
        for e in range(N_EXPERTS):
            slot = slot + jnp.where(eid == e, poff[e], 0)
        return slot

    dest = jnp.concatenate([slots(0), slots(1)]).astype(jnp.int32)
    blk_start = jnp.arange(cap // MOE_ROWS, dtype=jnp.int32) * MOE_ROWS
    blk_e = jnp.minimum(jnp.sum(pend[None, :] <= blk_start[:, None], axis=1), N_EXPERTS - 1)
    nused = (pend[-1:] // MOE_ROWS).astype(jnp.int32)
    zrow = jnp.where(cnt > 0, (pend - MOE_ROWS) * ROW_TILE, -1).astype(jnp.int32)
    nonempty = cnt > 0
    half = ((jnp.cumsum(nonempty) - 1) % 2).astype(jnp.int32)
    later = jnp.logical_and(nonempty[None, :], ids[None, :] > ids[:, None])
    following = jnp.min(jnp.where(later, ids[None, :], N_EXPERTS), axis=1)
    following = jnp.where(following < N_EXPERTS, following, -1).astype(jnp.int32)
    return (dest, jnp.concatenate([zrow, nused]), blk_e.astype(jnp.int32), nused, half,
            following)


def kernel(x, p, norm_mix, w_in, q_norm, k_norm, sgu_norm, w_spatial, b_spatial, w_up_a, w_up_b,
           w_out, norm_ffn, w_group_router, b_group_router, w_expert_router, b_expert_router,
           w_expert_in, w_expert_out, norm_ple, w_ple_gate, w_ple_proj):
    batch, seq, d = x.shape
    depth = w_in.shape[0]
    n = batch * seq
    cap = 2 * n + N_EXPERTS * MOE_ROWS

    vec = lambda a: a[:, None, :]
    qn2, kn2 = vec(jnp.tile(q_norm, (1, 2))), vec(jnp.tile(k_norm, (1, 2)))
    b_sp = jnp.repeat(jnp.swapaxes(b_spatial, 1, 2), HEAD_DIM, axis=2)
    pad = LANES - N_GROUPS - N_EXPERTS
    w_r = jnp.pad(jnp.concatenate([w_group_router, w_expert_router], axis=2),
                  ((0, 0), (0, 0), (0, pad))).astype(BF16)
    b_r = vec(jnp.pad(jnp.concatenate([b_group_router, b_expert_router], axis=1),
                      ((0, 0), (0, pad))))
    p2 = p.reshape(depth, n, PLE_DIM)

    xf = x.reshape(n, d)
    for i in range(depth):
        q, k, v, u, vs, ga, gb = _inproj(i, xf, vec(norm_mix), w_in, qn2, kn2, vec(sgu_norm))
        oa = _attention(q, k, v, batch, seq)
        xf, h, route, counts = _mix(i, xf, oa, u, vs, ga, gb, w_spatial, b_sp, w_up_a, w_up_b,
                                    w_out, vec(norm_ffn), w_r, b_r)
        dest, zrow, blk_e, nused, half, following = _slot_layout(route, counts, cap)
        xs = _dispatch(dest, zrow, h, cap)
        ys = _experts(i, blk_e, nused, half, following, xs, w_expert_in, w_expert_out)
        xf = _ple(i, dest, xf, route, p2, ys, vec(norm_ple), w_ple_gate, w_ple_proj)
    return xf.reshape(batch, seq, d)
```

```python
import functools
import math

import jax
import jax.numpy as jnp
from jax import lax
from jax.experimental import pallas as pl
from jax.experimental.pallas import tpu as pltpu

F32 = jnp.float32
BF16 = jnp.bfloat16

D_MODEL = 1024
HEADS = 8
HEAD_DIM = 64
WIDTH = HEADS * HEAD_DIM
CHUNK = 128
IN_WIDTH = 5 * WIDTH + 2 * D_MODEL
N_GROUPS = 4
EXPERTS_PER_GROUP = 8
N_EXPERTS = N_GROUPS * EXPERTS_PER_GROUP
EXPERT_FF = 512
PLE_DIM = 256
EPS = 1e-6

LANES = 128
ROW_TILE = D_MODEL // (2 * LANES)
U32 = jnp.uint32
ROUTE_LANE0 = N_GROUPS
TM_IN = 256
TM_MIX = 256
TM_DISP = 512
TM_ROW = 256
MOE_ROWS = 256
TOP_ROWS = 64
LOG2E = 1.4426950408889634
DEAD_LOG2 = -105.0 * LOG2E
VMEM_LIMIT = 48 * 1024 * 1024


def _rms(x, gain):
    return x * lax.rsqrt(jnp.mean(x * x, axis=-1, keepdims=True) + EPS) * gain


def _store_row_tiles(ref, value, lead=()):
    rows = value.shape[0]
    bits = lambda v: lax.bitcast_convert_type(v.astype(BF16).astype(F32), U32)
    for c in range(ROW_TILE):
        low = bits(value[:, 2 * c * LANES:(2 * c + 1) * LANES]) >> 16
        high = bits(value[:, (2 * c + 1) * LANES:(2 * c + 2) * LANES]) & U32(0xFFFF0000)
        ref[lead + (pl.ds(c, rows, stride=ROW_TILE), slice(None))] = low | high


def _load_row_tiles(ref, rows, lead=()):
    chunks = []
    for c in range(ROW_TILE):
        words = ref[lead + (pl.ds(c, rows, stride=ROW_TILE), slice(None))]
        chunks.append(lax.bitcast_convert_type(words << 16, F32))
        chunks.append(lax.bitcast_convert_type(words & U32(0xFFFF0000), F32))
    return jnp.concatenate(chunks, axis=1)


def _inproj_kernel(x_ref, g_ref, wf_ref, qn_ref, kn_ref, sn_ref,
                   q_ref, k_ref, v_ref, u_ref, vs_ref, ga_ref, gb_ref, w_ref):
    @pl.when(pl.program_id(0) == 0)
    def _():
        w_ref[...] = wf_ref[...].astype(BF16)

    hb = _rms(x_ref[...], g_ref[...]).astype(BF16)

    def proj(lo, hi):
        return jnp.dot(hb, w_ref[:, lo:hi], preferred_element_type=F32)

    first = lax.broadcasted_iota(jnp.int32, (1, LANES), 1) < HEAD_DIM

    def head_norm(z, gain, scale):
        outs = []
        for c in range(WIDTH // LANES):
            zc = z[:, c * LANES:(c + 1) * LANES]
            sq = zc * zc
            sa = jnp.sum(jnp.where(first, sq, 0.0), axis=-1, keepdims=True)
            sb = jnp.sum(jnp.where(first, 0.0, sq), axis=-1, keepdims=True)
            ms = jnp.where(first, sa, sb) * (1.0 / HEAD_DIM)
            outs.append(zc * lax.rsqrt(ms + EPS) * (gain * scale))
        return jnp.concatenate(outs, axis=-1)

    w = WIDTH
    q_ref[...] = head_norm(proj(0, w), qn_ref[...], LOG2E / math.sqrt(HEAD_DIM)).astype(BF16)
    k_ref[...] = head_norm(proj(w, 2 * w), kn_ref[...], 1.0).astype(BF16)
    v_ref[...] = proj(2 * w, 3 * w).astype(BF16)
    u_ref[...] = jax.nn.gelu(proj(3 * w, 4 * w)).astype(BF16)
    vs_ref[...] = _rms(jax.nn.gelu(proj(4 * w, 5 * w)), sn_ref[...]).astype(BF16)
    ga_ref[...] = jax.nn.sigmoid(proj(5 * w, 5 * w + D_MODEL)).astype(BF16)
    gb_ref[...] = jax.nn.sigmoid(proj(5 * w + D_MODEL, IN_WIDTH)).astype(BF16)


def _inproj(layer, x, norm_mix, w_in, qn, kn, sn):
    n = x.shape[0]
    row = lambda width: pl.BlockSpec((TM_IN, width), lambda i: (i, 0))
    vec = lambda width: pl.BlockSpec((None, 1, width), lambda i: (layer, 0, 0))
    out = lambda width: jax.ShapeDtypeStruct((n, width), BF16)
    return pl.pallas_call(
        _inproj_kernel,
        grid=(n // TM_IN,),
        in_specs=[row(D_MODEL), vec(D_MODEL),
                  pl.BlockSpec((None, D_MODEL, IN_WIDTH), lambda i: (layer, 0, 0),
                               pipeline_mode=pl.Buffered(1)),
                  vec(LANES), vec(LANES), vec(WIDTH)],
        out_specs=[row(WIDTH)] * 5 + [row(D_MODEL)] * 2,
        out_shape=[out(WIDTH)] * 5 + [out(D_MODEL)] * 2,
        scratch_shapes=[pltpu.VMEM((D_MODEL, IN_WIDTH), BF16)],
        compiler_params=pltpu.CompilerParams(
            dimension_semantics=("arbitrary",), vmem_limit_bytes=VMEM_LIMIT),
        name="inproj",
    )(x, norm_mix, w_in, qn, kn, sn)


def _attn_kernel(q_ref, k_ref, v_ref, o_ref, carry_ref, acc_ref, z_ref):
    qb = pl.program_id(1)
    pairs = WIDTH // LANES
    first = lax.broadcasted_iota(jnp.int32, (CHUNK, LANES), 1) < HEAD_DIM
    first_top = lax.broadcasted_iota(jnp.int32, (TOP_ROWS, LANES), 1) < HEAD_DIM
    kr = lax.broadcasted_iota(jnp.int32, (LANES, 2 * LANES), 0)
    kc = lax.broadcasted_iota(jnp.int32, (LANES, 2 * LANES), 1)
    tail = jnp.where(jnp.logical_or(kr > kc, kc >= LANES), 1.0, 0.0).astype(BF16)
    sign = jnp.uint32(0x80000000)
    cols = [slice(p * LANES, (p + 1) * LANES) for p in range(pairs)]

    def stack(value, rows):
        if rows == CHUNK:
            return value
        return jnp.concatenate([value[:rows], value[CHUNK:CHUNK + rows]], axis=0)

    queries, causal = {}, {}
    for rows in (CHUNK, TOP_ROWS):
        lane = lax.broadcasted_iota(jnp.int32, (2 * rows, LANES), 1)
        row = lax.broadcasted_iota(jnp.int32, (2 * rows, LANES), 0)
        upper = row < rows
        own = jnp.logical_xor(lane < HEAD_DIM, jnp.logical_not(upper))
        causal[rows] = lane < jnp.where(upper, row, row - rows)
        queries[rows] = []
        for p in range(pairs):
            qp = q_ref[:rows, cols[p]]
            q2 = jnp.concatenate([qp, qp], axis=0)
            queries[rows].append(jnp.where(own, q2, jnp.zeros_like(q2)))
    full_row = lax.broadcasted_iota(jnp.int32, (2 * CHUNK, LANES), 0)
    is_top = jnp.where(full_row < CHUNK, full_row, full_row - CHUNK) < TOP_ROWS

    def scores(j, rows):
        start = pl.multiple_of(j * CHUNK, CHUNK)
        return [lax.dot_general(queries[rows][p], k_ref[pl.ds(start, CHUNK), cols[p]],
                                (((1,), (1,)), ((), ())), preferred_element_type=F32)
                for p in range(pairs)]

    def sweep(j, heights, diagonal=False, fresh=False, ahead=True, rest_only=False):
        blocks = range(len(heights))
        full = [b for b in blocks if heights[b] == CHUNK]
        top = [b for b in blocks if heights[b] == TOP_ROWS]
        starts = [pl.multiple_of((j - b) * CHUNK, CHUNK) for b in blocks]
        zs = []
        for b in blocks:
            if b > 0 or fresh or diagonal:
                zs.append(scores(j - b, heights[b]))
            else:
                zs.append([stack(z_ref[p], heights[b]) for p in range(pairs)])
        if ahead:
            for p, z in enumerate(scores(jnp.maximum(j - len(heights), 0), CHUNK)):
                z_ref[p] = z
        log_betas, splits = {}, {}
        for b in blocks:
            for p in range(pairs):
                z = zs[b][p]
                neg_abs = lax.bitcast_convert_type(
                    lax.bitcast_convert_type(z, jnp.uint32) | sign, F32)
                log1p = jnp.log(1.0 + jnp.exp2(neg_abs)) * LOG2E
                log_beta = jnp.minimum(z, 0.0) - log1p
                log_keep = log_beta - z
                if diagonal and b == 0:
                    log_keep = jnp.where(causal[heights[b]], log_keep, 0.0)
                if rest_only:
                    log_keep = jnp.where(is_top, 0.0, log_keep)
                log_betas[b, p] = log_beta
                splits[b, p] = log_keep.astype(BF16)
        sums = {key: jnp.dot(s, tail, preferred_element_type=F32) for key, s in splits.items()}
        entering, after_full, after_top = {}, [], []
        for p in range(pairs):
            carry = 0.0 if diagonal else carry_ref[p]
            for b in full:
                entering[b, p] = carry
                carry = carry + sums[b, p][:, LANES:]
            after_full.append(carry)
            carry = stack(carry, TOP_ROWS) if top else None
            for b in top:
                entering[b, p] = carry
                carry = carry + sums[b, p][:, LANES:]
            after_top.append(carry)
        alive = lambda c: jnp.max(c) > DEAD_LOG2
        worst = functools.reduce(jnp.maximum, after_full)
        live_rest = alive(jnp.where(is_top, -jnp.inf, worst)) if full else jnp.bool_(False)
        if top:
            live_top = alive(functools.reduce(jnp.maximum, after_top))
        else:
            live_top = alive(jnp.where(is_top, worst, -jnp.inf))
        outs = {}
        for b in blocks:
            for p in range(pairs):
                a = jnp.exp2(log_betas[b, p] + sums[b, p][:, :LANES] + entering[b, p])
                if diagonal and b == 0:
                    a = jnp.where(causal[heights[b]], a, 0.0)
                if rest_only:
                    a = jnp.where(is_top, 0.0, a)
                outs[b, p] = jnp.dot(a.astype(BF16), v_ref[pl.ds(starts[b], CHUNK), cols[p]],
                                     preferred_element_type=F32)
        for p in range(pairs):
            if full:
                carry_ref[p] = after_full[p]
            if top:
                carry_ref[p, 0:TOP_ROWS, :] = after_top[p][:TOP_ROWS]
                carry_ref[p, CHUNK:CHUNK + TOP_ROWS, :] = after_top[p][TOP_ROWS:]
        for p in range(pairs):
            if full:
                o2 = functools.reduce(jnp.add, [outs[b, p] for b in full])
                o = jnp.where(first, o2[:CHUNK], o2[CHUNK:])
                acc_ref[:, cols[p]] = o if diagonal else acc_ref[:, cols[p]] + o
            if top:
                o2 = functools.reduce(jnp.add, [outs[b, p] for b in top])
                o = jnp.where(first_top, o2[:TOP_ROWS], o2[TOP_ROWS:])
                acc_ref[0:TOP_ROWS, cols[p]] += o
        return live_top, live_rest

    def sweep_left(j, live_top, live_rest, height):
        def cond(state):
            j, live_top, live_rest = state
            return jnp.logical_and(j >= 0, jnp.logical_or(live_top, live_rest))

        def body(state):
            return (state[0] - 1,) + sweep(state[0], [height])

        lax.while_loop(cond, body, (j, live_top, live_rest))

    def start(n_full, n_top):
        live_top, live_rest = sweep(qb, [CHUNK] * n_full + [TOP_ROWS] * n_top, diagonal=True)
        if n_top == 0:
            return
        left = qb - n_full - n_top

        @pl.when(live_rest)
        def _():
            flags = sweep(qb - n_full, [CHUNK] * n_top, fresh=True, ahead=False, rest_only=True)
            sweep_left(left, *flags, CHUNK)

        @pl.when(jnp.logical_not(live_rest))
        def _():
            sweep_left(left, live_top, jnp.bool_(False), TOP_ROWS)

    for blocks_left, shape in enumerate(((1, 0), (2, 0))):
        pl.when(qb == blocks_left)(functools.partial(start, *shape))
    pl.when(qb >= 2)(functools.partial(start, 2, 1))

    o_ref[...] = acc_ref[...].astype(BF16)


def _attention(q, k, v, batch, seq):
    q3, k3, v3 = (t.reshape(batch, seq, WIDTH) for t in (q, k, v))
    blk = pl.BlockSpec((None, CHUNK, WIDTH), lambda b, qb: (b, qb, 0))
    full = pl.BlockSpec((None, seq, WIDTH), lambda b, qb: (b, 0, 0), pipeline_mode=pl.Buffered(1))
    o = pl.pallas_call(
        _attn_kernel,
        grid=(batch, seq // CHUNK),
        in_specs=[blk, full, full],
        out_specs=blk,
        out_shape=jax.ShapeDtypeStruct((batch, seq, WIDTH), BF16),
        scratch_shapes=[pltpu.VMEM((WIDTH // LANES, 2 * CHUNK, LANES), F32),
                        pltpu.VMEM((CHUNK, WIDTH), F32),
                        pltpu.VMEM((WIDTH // LANES, 2 * CHUNK, LANES), F32)],
        compiler_params=pltpu.CompilerParams(
            dimension_semantics=("arbitrary", "arbitrary"), vmem_limit_bytes=VMEM_LIMIT),
        name="attn",
    )(q3, k3, v3)
    return o.reshape(batch * seq, WIDTH)


def _choose_tile(hb, wr_ref, br_ref):
    logits = jnp.dot(hb, wr_ref[...], preferred_element_type=F32) + br_ref[...]
    lane_t = lax.broadcasted_iota(jnp.int32, (TM_MIX, LANES), 1)
    lanef = lane_t.astype(F32)
    neg = -jnp.inf
    far = float(LANES)

    def first_max(vals):
        m = jnp.max(vals, axis=-1, keepdims=True)
        idx = jnp.min(jnp.where(vals == m, lanef, far), axis=-1, keepdims=True)
        return m, idx

    gl = jnp.where(lane_t < N_GROUPS, logits, neg)
    gmax, grp = first_max(gl)
    grp_w = 1.0 / jnp.sum(jnp.exp(gl - gmax), axis=-1, keepdims=True)
    lo_lane = ROUTE_LANE0 + EXPERTS_PER_GROUP * grp
    in_group = jnp.logical_and(lanef >= lo_lane, lanef < lo_lane + EXPERTS_PER_GROUP)
    el = jnp.where(in_group, logits, neg)
    m1, i1 = first_max(el)
    m2, i2 = first_max(jnp.where(lanef == i1, neg, el))
    e21 = jnp.exp(m2 - m1)
    w1 = grp_w / (1.0 + e21)
    w2 = w1 * e21
    return i1, i2, w1, w2


def _rank_tile(choice, run_ref, valid):
    i1, i2, w1, w2 = choice
    lane_t = lax.broadcasted_iota(jnp.int32, (TM_MIX, LANES), 1)
    lanef = lane_t.astype(F32)
    onehot = jnp.logical_and(jnp.logical_or(lanef == i1, lanef == i2), valid)
    rt = lax.broadcasted_iota(jnp.int32, (TM_MIX, TM_MIX), 0)
    ct = lax.broadcasted_iota(jnp.int32, (TM_MIX, TM_MIX), 1)
    before = jnp.where(ct < rt, 1.0, 0.0).astype(BF16)
    prior = jnp.dot(before, jnp.where(onehot, 1.0, 0.0).astype(BF16),
                    preferred_element_type=F32) + run_ref[...]
    r1 = jnp.sum(jnp.where(lanef == i1, prior, 0.0), axis=-1, keepdims=True)
    r2 = jnp.sum(jnp.where(lanef == i2, prior, 0.0), axis=-1, keepdims=True)
    run_ref[...] += jnp.sum(jnp.where(onehot, 1.0, 0.0), axis=0, keepdims=True)

    fields = (i1 - ROUTE_LANE0, i2 - ROUTE_LANE0, w1, w2, r1, r2)
    route = jnp.zeros((TM_MIX, LANES), F32)
    for pos, val in enumerate(fields):
        route = jnp.where(lane_t == pos, val, route)
    return route


def _mix_kernel(x_ref, oa_ref, u_ref, vs_ref, ga_ref, gb_ref,
                wsp_ref, bsp_ref, wua_ref, wub_ref, wo_ref, gf_ref, wr_ref, br_ref,
                xo_ref, h_ref, route_ref, cnt_ref, run_ref, xprev_ref, wua_b, wub_b, wo_b):
    step = pl.program_id(0)

    @pl.when(step == 0)
    def _():
        run_ref[...] = jnp.zeros_like(run_ref)
        xprev_ref[...] = jnp.zeros_like(xprev_ref)
        wua_b[...] = wua_ref[...].astype(BF16)
        wub_b[...] = wub_ref[...].astype(BF16)
        wo_b[...] = wo_ref[...].astype(BF16)

    lane = lax.broadcasted_iota(jnp.int32, (CHUNK, LANES), 1)
    row = lax.broadcasted_iota(jnp.int32, (CHUNK, LANES), 0)
    first = lane < HEAD_DIM
    tril = lane <= row

    w_pairs = []
    for gp in range(WIDTH // LANES):
        w_pairs.append(jnp.concatenate(
            [jnp.where(tril, wsp_ref[g], 0.0).astype(BF16) for g in (2 * gp, 2 * gp + 1)],
            axis=1))
    ob_chunks = []
    for c in range(TM_MIX // CHUNK):
        rows = slice(c * CHUNK, (c + 1) * CHUNK)
        cols = []
        for gp in range(WIDTH // LANES):
            vpair = vs_ref[rows, gp * LANES:(gp + 1) * LANES]
            zero = jnp.zeros_like(vpair)
            stacked = jnp.concatenate([jnp.where(first, vpair, zero),
                                       jnp.where(first, zero, vpair)], axis=0)
            cols.append(jnp.dot(w_pairs[gp], stacked, preferred_element_type=F32))
        mixed = jnp.concatenate(cols, axis=-1) + bsp_ref[...]
        ob_chunks.append((u_ref[rows, :].astype(F32) * mixed).astype(BF16))
    ob = jnp.concatenate(ob_chunks, axis=0)

    h = _rms(xprev_ref[...], gf_ref[...])
    _store_row_tiles(h_ref, h)
    choice = _choose_tile(h.astype(BF16), wr_ref, br_ref)

    up_a = jnp.dot(oa_ref[...], wua_b[...], preferred_element_type=F32)
    up_b = jnp.dot(ob, wub_b[...], preferred_element_type=F32)
    merged = ga_ref[...].astype(F32) * up_a + gb_ref[...].astype(F32) * up_b
    x = x_ref[...] + jnp.dot(merged.astype(BF16), wo_b[...], preferred_element_type=F32)
    xo_ref[...] = x
    xprev_ref[...] = x

    route_ref[...] = _rank_tile(choice, run_ref, step > 0)
    cnt_ref[...] = run_ref[...]


def _mix(layer, x, oa, u, vs, ga, gb, wsp, bsp, wua, wub, wo, gf, wr, br):
    n = x.shape[0]
    tiles = n // TM_MIX
    cur = lambda i: jnp.minimum(i, tiles - 1)
    prev = lambda i: jnp.maximum(i - 1, 0)
    row = lambda width: pl.BlockSpec((TM_MIX, width), lambda i: (cur(i), 0))
    lay = lambda *shape: pl.BlockSpec((None,) + shape, lambda i: (layer,) + (0,) * len(shape))
    once = lambda *shape: pl.BlockSpec((None,) + shape, lambda i: (layer,) + (0,) * len(shape),
                                       pipeline_mode=pl.Buffered(1))
    return pl.pallas_call(
        _mix_kernel,
        grid=(tiles + 1,),
        in_specs=[row(D_MODEL), row(WIDTH), row(WIDTH), row(WIDTH), row(D_MODEL), row(D_MODEL),
                  lay(HEADS, CHUNK, CHUNK), lay(CHUNK, WIDTH), once(WIDTH, D_MODEL),
                  once(WIDTH, D_MODEL), once(D_MODEL, D_MODEL), lay(1, D_MODEL),
                  lay(D_MODEL, LANES), lay(1, LANES)],
        out_specs=[row(D_MODEL),
                   pl.BlockSpec((TM_MIX * ROW_TILE, LANES), lambda i: (prev(i), 0)),
                   pl.BlockSpec((TM_MIX, LANES), lambda i: (prev(i), 0)),
                   pl.BlockSpec((1, LANES), lambda i: (0, 0))],
        out_shape=[jax.ShapeDtypeStruct((n, D_MODEL), F32),
                   jax.ShapeDtypeStruct((n * ROW_TILE, LANES), U32),
                   jax.ShapeDtypeStruct((n, LANES), F32),
                   jax.ShapeDtypeStruct((1, LANES), F32)],
        scratch_shapes=[pltpu.VMEM((1, LANES), F32), pltpu.VMEM((TM_MIX, D_MODEL), F32),
                        pltpu.VMEM((WIDTH, D_MODEL), BF16), pltpu.VMEM((WIDTH, D_MODEL), BF16),
                        pltpu.VMEM((D_MODEL, D_MODEL), BF16)],
        compiler_params=pltpu.CompilerParams(
            dimension_semantics=("arbitrary",), vmem_limit_bytes=VMEM_LIMIT),
        name="mix",
    )(x, oa, u, vs, ga, gb, wsp, bsp, wua, wub, wo, gf, wr, br)


def _dispatch_kernel(dest_ref, zrow_ref, h_ref, xs_ref, zero_ref, sem, zsem, tsem):
    i = pl.program_id(0)
    block_rows = MOE_ROWS * ROW_TILE
    nblk = xs_ref.shape[0] // block_rows
    nused = zrow_ref[N_EXPERTS]

    def zero_copy(first_row, zero_sem):
        return pltpu.make_async_copy(zero_ref, xs_ref.at[pl.ds(first_row, block_rows), :],
                                     zero_sem)

    def tail_copy(j):
        return zero_copy((nused + j) * block_rows, tsem)

    @pl.when(i == 0)
    def _():
        zero_ref[...] = jnp.zeros_like(zero_ref)
        for j in range(N_EXPERTS):
            pl.when(nused + j < nblk)(lambda j=j: tail_copy(j).start())
        for e in range(N_EXPERTS):
            pl.when(zrow_ref[e] >= 0)(lambda e=e: zero_copy(zrow_ref[e], zsem).start())
        for e in range(N_EXPERTS):
            pl.when(zrow_ref[e] >= 0)(lambda e=e: zero_copy(zrow_ref[e], zsem).wait())

    n = dest_ref.shape[0] // 2
    base = i * TM_DISP
    for t in range(TM_DISP):
        for k in range(2):
            d = dest_ref[k * n + base + t]
            pltpu.make_async_copy(h_ref.at[pl.ds(t * ROW_TILE, ROW_TILE), :],
                                  xs_ref.at[pl.ds(d * ROW_TILE, ROW_TILE), :],
                                  sem).start(priority=k)
    for k in range(2):
        pltpu.make_async_copy(h_ref, xs_ref.at[pl.ds(0, TM_DISP * ROW_TILE), :], sem).wait()

    @pl.when(i == pl.num_programs(0) - 1)
    def _():
        for j in range(N_EXPERTS):
            pl.when(nused + j < nblk)(lambda j=j: tail_copy(j).wait())


def _dispatch(dest, zrow, h, cap):
    n = h.shape[0] // ROW_TILE
    return pl.pallas_call(
        _dispatch_kernel,
        grid_spec=pltpu.PrefetchScalarGridSpec(
            num_scalar_prefetch=2,
            grid=(n // TM_DISP,),
            in_specs=[pl.BlockSpec((TM_DISP * ROW_TILE, LANES), lambda i, dest, zrow: (i, 0))],
            out_specs=pl.BlockSpec(memory_space=pl.ANY),
            scratch_shapes=[pltpu.VMEM((MOE_ROWS * ROW_TILE, LANES), U32),
                            pltpu.SemaphoreType.DMA(()), pltpu.SemaphoreType.DMA(()),
                            pltpu.SemaphoreType.DMA(())],
        ),
        out_shape=jax.ShapeDtypeStruct((cap * ROW_TILE, LANES), U32),
        compiler_params=pltpu.CompilerParams(
            dimension_semantics=("arbitrary",), vmem_limit_bytes=VMEM_LIMIT),
        name="dispatch",
    )(dest, zrow, h)


def _expert_kernel(layer, blk_e_ref, nused_ref, half_ref, next_ref, xs_ref, wi_hbm, wo_hbm,
                   ys_ref, wi_f, wo_f, wi_b, wo_b, sem):
    b = pl.program_id(0)
    used = b < nused_ref[0]
    expert = blk_e_ref[b]

    def fetch(e, half):
        return (pltpu.make_async_copy(wi_hbm.at[layer, e], wi_f.at[half], sem.at[half, 0]),
                pltpu.make_async_copy(wo_hbm.at[layer, e], wo_f.at[half], sem.at[half, 1]))

    @pl.when(b == 0)
    def _():
        for copy in fetch(expert, half_ref[expert]):
            copy.start()

    @pl.when(jnp.logical_not(used))
    def _():
        ys_ref[...] = jnp.zeros_like(ys_ref)

    new_expert = jnp.logical_or(b == 0, expert != blk_e_ref[jnp.maximum(b - 1, 0)])

    @pl.when(jnp.logical_and(used, new_expert))
    def _():
        half = half_ref[expert]
        for copy in fetch(expert, half):
            copy.wait()
        following = next_ref[expert]

        @pl.when(following >= 0)
        def _():
            for copy in fetch(following, 1 - half):
                copy.start()

        wi_b[...] = wi_f[half].astype(BF16)
        wo_b[...] = wo_f[half].astype(BF16)

    @pl.when(used)
    def _():
        xs = _load_row_tiles(xs_ref, MOE_ROWS).astype(BF16)
        gu = jnp.dot(xs, wi_b[...], preferred_element_type=F32)
        act = jax.nn.silu(gu[:, :EXPERT_FF]) * gu[:, EXPERT_FF:]
        y = jnp.dot(act.astype(BF16), wo_b[...], preferred_element_type=F32)
        _store_row_tiles(ys_ref, y)


def _experts(layer, blk_e, nused, half, following, xs, w_e_in, w_e_out):
    cap = xs.shape[0] // ROW_TILE
    rows = MOE_ROWS * ROW_TILE
    return pl.pallas_call(
        functools.partial(_expert_kernel, layer),
        grid_spec=pltpu.PrefetchScalarGridSpec(
            num_scalar_prefetch=4,
            grid=(cap // MOE_ROWS,),
            in_specs=[pl.BlockSpec((rows, LANES),
                                   lambda b, blk_e, nused, *_: (jnp.minimum(b, nused[0] - 1), 0)),
                      pl.BlockSpec(memory_space=pl.ANY),
                      pl.BlockSpec(memory_space=pl.ANY)],
            out_specs=pl.BlockSpec((rows, LANES), lambda b, *_: (b, 0)),
            scratch_shapes=[pltpu.VMEM((2, D_MODEL, 2 * EXPERT_FF), F32),
                            pltpu.VMEM((2, EXPERT_FF, D_MODEL), F32),
                            pltpu.VMEM((D_MODEL, 2 * EXPERT_FF), BF16),
                            pltpu.VMEM((EXPERT_FF, D_MODEL), BF16),
                            pltpu.SemaphoreType.DMA((2, 2))],
        ),
        out_shape=jax.ShapeDtypeStruct((cap * ROW_TILE, LANES), U32),
        compiler_params=pltpu.CompilerParams(
            dimension_semantics=("arbitrary",), vmem_limit_bytes=VMEM_LIMIT),
        name="experts",
    )(blk_e, nused, half, following, xs, w_e_in, w_e_out)


def _ple_kernel(dest_ref, x_ref, route_ref, p_ref, ys_ref, gp_ref, wg_ref, wp_ref,
                xo_ref, y00, y01, y10, y11, wg_b, wp_b, sem):
    ybuf = ((y00, y01), (y10, y11))
    j = pl.program_id(0)

    @pl.when(j == 0)
    def _():
        wg_b[...] = wg_ref[...].astype(BF16)
        wp_b[...] = wp_ref[...].astype(BF16)
    last_step = pl.num_programs(0) - 1

    def start_gather(tile, half):
        n = dest_ref.shape[0] // 2
        base = tile * TM_ROW
        for t in range(TM_ROW):
            for k in range(2):
                d = dest_ref[k * n + base + t]
                pltpu.make_async_copy(ys_ref.at[pl.ds(d * ROW_TILE, ROW_TILE), :],
                                      ybuf[half][k].at[pl.ds(t * ROW_TILE, ROW_TILE), :],
                                      sem.at[half]).start(priority=k)

    def wait_gather(half):
        for k in range(2):
            pltpu.make_async_copy(ys_ref.at[pl.ds(0, TM_ROW * ROW_TILE), :],
                                  ybuf[half][k], sem.at[half]).wait()

    def combine(half):
        rows = slice(half * TM_ROW, (half + 1) * TM_ROW)
        pe = jnp.dot(p_ref[rows, :].astype(BF16), wp_b[...], preferred_element_type=F32)
        wait_gather(half)
        route = route_ref[rows, :]
        y0 = _load_row_tiles(ybuf[half][0], TM_ROW)
        y1 = _load_row_tiles(ybuf[half][1], TM_ROW)
        x = x_ref[rows, :] + route[:, 2:3] * y0 + route[:, 3:4] * y1
        gate = jax.nn.sigmoid(jnp.dot(_rms(x, gp_ref[...]).astype(BF16), wg_b[...],
                                      preferred_element_type=F32))
        xo_ref[rows, :] = x + gate * pe

    @pl.when(j == 0)
    def _():
        start_gather(0, 0)

    start_gather(2 * j + 1, 1)
    combine(0)
    start_gather(jnp.minimum(2 * j + 2, 2 * last_step + 1), 0)
    combine(1)

    @pl.when(j == last_step)
    def _():
        wait_gather(0)


def _ple(layer, dest, x, route, p, ys, gp, wg, wp):
    n = x.shape[0]
    row = lambda width: pl.BlockSpec((2 * TM_ROW, width), lambda j, dest: (j, 0))
    lay = lambda *shape: pl.BlockSpec((None,) + shape,
                                      lambda j, dest: (layer,) + (0,) * len(shape))
    once = lambda *shape: pl.BlockSpec((None,) + shape,
                                       lambda j, dest: (layer,) + (0,) * len(shape),
                                       pipeline_mode=pl.Buffered(1))
    return pl.pallas_call(
        _ple_kernel,
        grid_spec=pltpu.PrefetchScalarGridSpec(
            num_scalar_prefetch=1,
            grid=(n // (2 * TM_ROW),),
            in_specs=[row(D_MODEL), row(LANES),
                      pl.BlockSpec((None, 2 * TM_ROW, PLE_DIM), lambda j, dest: (layer, j, 0)),
                      pl.BlockSpec(memory_space=pl.ANY),
                      lay(1, D_MODEL), once(D_MODEL, D_MODEL), once(PLE_DIM, D_MODEL)],
            out_specs=row(D_MODEL),
            scratch_shapes=[pltpu.VMEM((TM_ROW * ROW_TILE, LANES), U32)] * 4
            + [pltpu.VMEM((D_MODEL, D_MODEL), BF16), pltpu.VMEM((PLE_DIM, D_MODEL), BF16),
               pltpu.SemaphoreType.DMA((2,))],
        ),
        out_shape=jax.ShapeDtypeStruct((n, D_MODEL), F32),
        compiler_params=pltpu.CompilerParams(
            dimension_semantics=("arbitrary",), vmem_limit_bytes=VMEM_LIMIT),
        name="ple",
    )(dest, x, route, p, ys, gp, wg, wp)


def _slot_layout(route, counts, cap):
    cnt = counts[0, ROUTE_LANE0:ROUTE_LANE0 + N_EXPERTS].astype(jnp.int32)
    padded = (cnt + MOE_ROWS - 1) // MOE_ROWS * MOE_ROWS
    pend = jnp.cumsum(padded)
    poff = pend - padded
    ids = jnp.arange(N_EXPERTS, dtype=jnp.int32)

    def slots(k):
        eid = route[:, k].astype(jnp.int32)
        rank = route[:, 4 + k].astype(jnp.int32)
        return jnp.sum(jnp.where(eid[:, None] == ids, poff, 0), axis=-1) + rank

    dest = jnp.concatenate([slots(0), slots(1)]).astype(jnp.int32)
    blk_start = jnp.arange(cap // MOE_ROWS, dtype=jnp.int32) * MOE_ROWS
    blk_e = jnp.minimum(jnp.sum(pend[None, :] <= blk_start[:, None], axis=1), N_EXPERTS - 1)
    nused = (pend[-1:] // MOE_ROWS).astype(jnp.int32)
    zrow = jnp.where(cnt > 0, (pend - MOE_ROWS) * ROW_TILE, -1).astype(jnp.int32)
    nonempty = cnt > 0
    half = ((jnp.cumsum(nonempty) - 1) % 2).astype(jnp.int32)
    later = jnp.logical_and(nonempty[None, :], ids[None, :] > ids[:, None])
    following = jnp.min(jnp.where(later, ids[None, :], N_EXPERTS), axis=1)
    following = jnp.where(following < N_EXPERTS, following, -1).astype(jnp.int32)
    return (dest, jnp.concatenate([zrow, nused]), blk_e.astype(jnp.int32), nused, half,
            following)


def kernel(x, p, norm_mix, w_in, q_norm, k_norm, sgu_norm, w_spatial, b_spatial, w_up_a, w_up_b,
           w_out, norm_ffn, w_group_router, b_group_router, w_expert_router, b_expert_router,
           w_expert_in, w_expert_out, norm_ple, w_ple_gate, w_ple_proj):
    batch, seq, d = x.shape
    depth = w_in.shape[0]
    n = batch * seq
    cap = 2 * n + N_EXPERTS * MOE_ROWS

    vec = lambda a: a[:, None, :]
    qn2, kn2 = vec(jnp.tile(q_norm, (1, 2))), vec(jnp.tile(k_norm, (1, 2)))
    b_sp = jnp.repeat(jnp.swapaxes(b_spatial, 1, 2), HEAD_DIM, axis=2)
    pad = LANES - N_GROUPS - N_EXPERTS
    w_r = jnp.pad(jnp.concatenate([w_group_router, w_expert_router], axis=2),
                  ((0, 0), (0, 0), (0, pad))).astype(BF16)
    b_r = vec(jnp.pad(jnp.concatenate([b_group_router, b_expert_router], axis=1),
                      ((0, 0), (0, pad))))
    p2 = p.reshape(depth, n, PLE_DIM)

    xf = x.reshape(n, d)
    for i in range(depth):
        q, k, v, u, vs, ga, gb = _inproj(i, xf, vec(norm_mix), w_in, qn2, kn2, vec(sgu_norm))
        oa = _attention(q, k, v, batch, seq)
        xf, h, route, counts = _mix(i, xf, oa, u, vs, ga, gb, w_spatial, b_sp, w_up_a, w_up_b,
                                    w_out, vec(norm_ffn), w_r, b_r)
        dest, zrow, blk_e, nused, half, following = _slot_layout(route, counts, cap)
        xs = _dispatch(dest, zrow, h, cap)
        ys = _experts(i, blk_e, nused, half, following, xs, w_expert_in, w_expert_out)
        xf = _ple(i, dest, xf, route, p2, ys, vec(norm_ple), w_ple_gate, w_ple_proj)
    return xf.reshape(batch, seq, d)
```

```python
import functools
import math

import jax
import jax.numpy as jnp
from jax import lax
from jax.experimental import pallas as pl
from jax.experimental.pallas import tpu as pltpu

F32 = jnp.float32
BF16 = jnp.bfloat16

D_MODEL = 1024
HEADS = 8
HEAD_DIM = 64
WIDTH = HEADS * HEAD_DIM
CHUNK = 128
IN_WIDTH = 5 * WIDTH + 2 * D_MODEL
N_GROUPS = 4
EXPERTS_PER_GROUP = 8
N_EXPERTS = N_GROUPS * EXPERTS_PER_GROUP
EXPERT_FF = 512
PLE_DIM = 256
EPS = 1e-6

LANES = 128
ROW_TILE = D_MODEL // (2 * LANES)
U32 = jnp.uint32
ROUTE_LANE0 = N_GROUPS
ROUTE_FIELDS = 8
TM_IN = 256
TM_MIX = 256
TM_DISP = 512
TM_ROW = 256
MOE_ROWS = 256
TOP_ROWS = 64
LOG2E = 1.4426950408889634
DEAD_LOG2 = -105.0 * LOG2E
VMEM_LIMIT = 48 * 1024 * 1024


def _rms(x, gain):
    return x * lax.rsqrt(jnp.mean(x * x, axis=-1, keepdims=True) + EPS) * gain


def _store_row_tiles(ref, value, lead=()):
    rows = value.shape[0]
    bits = lambda v: lax.bitcast_convert_type(v.astype(BF16).astype(F32), U32)
    for c in range(ROW_TILE):
        low = bits(value[:, 2 * c * LANES:(2 * c + 1) * LANES]) >> 16
        high = bits(value[:, (2 * c + 1) * LANES:(2 * c + 2) * LANES]) & U32(0xFFFF0000)
        ref[lead + (pl.ds(c, rows, stride=ROW_TILE), slice(None))] = low | high


def _load_row_tiles(ref, rows, lead=()):
    chunks = []
    for c in range(ROW_TILE):
        words = ref[lead + (pl.ds(c, rows, stride=ROW_TILE), slice(None))]
        chunks.append(lax.bitcast_convert_type(words << 16, F32))
        chunks.append(lax.bitcast_convert_type(words & U32(0xFFFF0000), F32))
    return jnp.concatenate(chunks, axis=1)


def _inproj_kernel(x_ref, g_ref, wf_ref, qn_ref, kn_ref, sn_ref,
                   q_ref, k_ref, v_ref, u_ref, vs_ref, ga_ref, gb_ref, w_ref):
    @pl.when(pl.program_id(0) == 0)
    def _():
        w_ref[...] = wf_ref[...].astype(BF16)

    hb = _rms(x_ref[...], g_ref[...]).astype(BF16)

    def proj(lo, hi):
        return jnp.dot(hb, w_ref[:, lo:hi], preferred_element_type=F32)

    first = lax.broadcasted_iota(jnp.int32, (1, LANES), 1) < HEAD_DIM

    def head_norm(z, gain, scale):
        outs = []
        for c in range(WIDTH // LANES):
            zc = z[:, c * LANES:(c + 1) * LANES]
            sq = zc * zc
            sa = jnp.sum(jnp.where(first, sq, 0.0), axis=-1, keepdims=True)
            sb = jnp.sum(jnp.where(first, 0.0, sq), axis=-1, keepdims=True)
            ms = jnp.where(first, sa, sb) * (1.0 / HEAD_DIM)
            outs.append(zc * lax.rsqrt(ms + EPS) * (gain * scale))
        return jnp.concatenate(outs, axis=-1)

    w = WIDTH
    q_ref[...] = head_norm(proj(0, w), qn_ref[...], LOG2E / math.sqrt(HEAD_DIM)).astype(BF16)
    k_ref[...] = head_norm(proj(w, 2 * w), kn_ref[...], 1.0).astype(BF16)
    v_ref[...] = proj(2 * w, 3 * w).astype(BF16)
    u_ref[...] = jax.nn.gelu(proj(3 * w, 4 * w)).astype(BF16)
    vs_ref[...] = _rms(jax.nn.gelu(proj(4 * w, 5 * w)), sn_ref[...]).astype(BF16)
    ga_ref[...] = jax.nn.sigmoid(proj(5 * w, 5 * w + D_MODEL)).astype(BF16)
    gb_ref[...] = jax.nn.sigmoid(proj(5 * w + D_MODEL, IN_WIDTH)).astype(BF16)


def _inproj(layer, x, norm_mix, w_in, qn, kn, sn):
    n = x.shape[0]
    row = lambda width: pl.BlockSpec((TM_IN, width), lambda i: (i, 0))
    vec = lambda width: pl.BlockSpec((None, 1, width), lambda i: (layer, 0, 0))
    out = lambda width: jax.ShapeDtypeStruct((n, width), BF16)
    return pl.pallas_call(
        _inproj_kernel,
        grid=(n // TM_IN,),
        in_specs=[row(D_MODEL), vec(D_MODEL),
                  pl.BlockSpec((None, D_MODEL, IN_WIDTH), lambda i: (layer, 0, 0),
                               pipeline_mode=pl.Buffered(1)),
                  vec(LANES), vec(LANES), vec(WIDTH)],
        out_specs=[row(WIDTH)] * 5 + [row(D_MODEL)] * 2,
        out_shape=[out(WIDTH)] * 5 + [out(D_MODEL)] * 2,
        scratch_shapes=[pltpu.VMEM((D_MODEL, IN_WIDTH), BF16)],
        compiler_params=pltpu.CompilerParams(
            dimension_semantics=("arbitrary",), vmem_limit_bytes=VMEM_LIMIT),
        name="inproj",
    )(x, norm_mix, w_in, qn, kn, sn)


def _attn_kernel(q_ref, k_ref, v_ref, o_ref, carry_ref, acc_ref, z_ref):
    qb = pl.program_id(1)
    pairs = WIDTH // LANES
    first = lax.broadcasted_iota(jnp.int32, (CHUNK, LANES), 1) < HEAD_DIM
    first_top = lax.broadcasted_iota(jnp.int32, (TOP_ROWS, LANES), 1) < HEAD_DIM
    kr = lax.broadcasted_iota(jnp.int32, (LANES, 2 * LANES), 0)
    kc = lax.broadcasted_iota(jnp.int32, (LANES, 2 * LANES), 1)
    tail = jnp.where(jnp.logical_or(kr > kc, kc >= LANES), 1.0, 0.0).astype(BF16)
    sign = jnp.uint32(0x80000000)
    cols = [slice(p * LANES, (p + 1) * LANES) for p in range(pairs)]

    def stack(value, rows):
        if rows == CHUNK:
            return value
        return jnp.concatenate([value[:rows], value[CHUNK:CHUNK + rows]], axis=0)

    queries, causal = {}, {}
    for rows in (CHUNK, TOP_ROWS):
        lane = lax.broadcasted_iota(jnp.int32, (2 * rows, LANES), 1)
        row = lax.broadcasted_iota(jnp.int32, (2 * rows, LANES), 0)
        upper = row < rows
        own = jnp.logical_xor(lane < HEAD_DIM, jnp.logical_not(upper))
        causal[rows] = lane < jnp.where(upper, row, row - rows)
        queries[rows] = []
        for p in range(pairs):
            qp = q_ref[:rows, cols[p]]
            q2 = jnp.concatenate([qp, qp], axis=0)
            queries[rows].append(jnp.where(own, q2, jnp.zeros_like(q2)))
    full_row = lax.broadcasted_iota(jnp.int32, (2 * CHUNK, LANES), 0)
    is_top = jnp.where(full_row < CHUNK, full_row, full_row - CHUNK) < TOP_ROWS

    def scores(j, rows):
        start = pl.multiple_of(j * CHUNK, CHUNK)
        return [lax.dot_general(queries[rows][p], k_ref[pl.ds(start, CHUNK), cols[p]],
                                (((1,), (1,)), ((), ())), preferred_element_type=F32)
                for p in range(pairs)]

    def sweep(j, heights, diagonal=False, fresh=False, ahead=True, rest_only=False):
        blocks = range(len(heights))
        full = [b for b in blocks if heights[b] == CHUNK]
        top = [b for b in blocks if heights[b] == TOP_ROWS]
        starts = [pl.multiple_of((j - b) * CHUNK, CHUNK) for b in blocks]
        zs = []
        for b in blocks:
            if b > 0 or fresh or diagonal:
                zs.append(scores(j - b, heights[b]))
            else:
                zs.append([stack(z_ref[p], heights[b]) for p in range(pairs)])
        if ahead:
            for p, z in enumerate(scores(jnp.maximum(j - len(heights), 0), CHUNK)):
                z_ref[p] = z
        log_betas, splits = {}, {}
        for b in blocks:
            for p in range(pairs):
                z = zs[b][p]
                neg_abs = lax.bitcast_convert_type(
                    lax.bitcast_convert_type(z, jnp.uint32) | sign, F32)
                log1p = jnp.log(1.0 + jnp.exp2(neg_abs)) * LOG2E
                log_beta = jnp.minimum(z, 0.0) - log1p
                log_keep = log_beta - z
                if diagonal and b == 0:
                    log_keep = jnp.where(causal[heights[b]], log_keep, 0.0)
                if rest_only:
                    log_keep = jnp.where(is_top, 0.0, log_keep)
                log_betas[b, p] = log_beta
                splits[b, p] = log_keep.astype(BF16)
        sums = {key: jnp.dot(s, tail, preferred_element_type=F32) for key, s in splits.items()}
        entering, after_full, after_top = {}, [], []
        for p in range(pairs):
            carry = 0.0 if diagonal else carry_ref[p]
            for b in full:
                entering[b, p] = carry
                carry = carry + sums[b, p][:, LANES:]
            after_full.append(carry)
            carry = stack(carry, TOP_ROWS) if top else None
            for b in top:
                entering[b, p] = carry
                carry = carry + sums[b, p][:, LANES:]
            after_top.append(carry)
        alive = lambda c: jnp.max(c) > DEAD_LOG2
        worst = functools.reduce(jnp.maximum, after_full)
        live_rest = alive(jnp.where(is_top, -jnp.inf, worst)) if full else jnp.bool_(False)
        if top:
            live_top = alive(functools.reduce(jnp.maximum, after_top))
        else:
            live_top = alive(jnp.where(is_top, worst, -jnp.inf))
        outs = {}
        for b in blocks:
            for p in range(pairs):
                a = jnp.exp2(log_betas[b, p] + sums[b, p][:, :LANES] + entering[b, p])
                if diagonal and b == 0:
                    a = jnp.where(causal[heights[b]], a, 0.0)
                if rest_only:
                    a = jnp.where(is_top, 0.0, a)
                outs[b, p] = jnp.dot(a.astype(BF16), v_ref[pl.ds(starts[b], CHUNK), cols[p]],
                                     preferred_element_type=F32)
        for p in range(pairs):
            if full:
                carry_ref[p] = after_full[p]
            if top:
                carry_ref[p, 0:TOP_ROWS, :] = after_top[p][:TOP_ROWS]
                carry_ref[p, CHUNK:CHUNK + TOP_ROWS, :] = after_top[p][TOP_ROWS:]
        for p in range(pairs):
            if full:
                o2 = functools.reduce(jnp.add, [outs[b, p] for b in full])
                o = jnp.where(first, o2[:CHUNK], o2[CHUNK:])
                acc_ref[:, cols[p]] = o if diagonal else acc_ref[:, cols[p]] + o
            if top:
                o2 = functools.reduce(jnp.add, [outs[b, p] for b in top])
                o = jnp.where(first_top, o2[:TOP_ROWS], o2[TOP_ROWS:])
                acc_ref[0:TOP_ROWS, cols[p]] += o
        return live_top, live_rest

    def sweep_left(j, live_top, live_rest, height):
        def cond(state):
            j, live_top, live_rest = state
            return jnp.logical_and(j >= 0, jnp.logical_or(live_top, live_rest))

        def body(state):
            return (state[0] - 1,) + sweep(state[0], [height])

        lax.while_loop(cond, body, (j, live_top, live_rest))

    def start(n_full, n_top):
        live_top, live_rest = sweep(qb, [CHUNK] * n_full + [TOP_ROWS] * n_top, diagonal=True)
        if n_top == 0:
            return
        left = qb - n_full - n_top

        @pl.when(live_rest)
        def _():
            flags = sweep(qb - n_full, [CHUNK] * n_top, fresh=True, ahead=False, rest_only=True)
            sweep_left(left, *flags, CHUNK)

        @pl.when(jnp.logical_not(live_rest))
        def _():
            sweep_left(left, live_top, jnp.bool_(False), TOP_ROWS)

    for blocks_left, shape in enumerate(((1, 0), (2, 0))):
        pl.when(qb == blocks_left)(functools.partial(start, *shape))
    pl.when(qb >= 2)(functools.partial(start, 2, 1))

    o_ref[...] = acc_ref[...].astype(BF16)


def _attention(q, k, v, batch, seq):
    q3, k3, v3 = (t.reshape(batch, seq, WIDTH) for t in (q, k, v))
    blk = pl.BlockSpec((None, CHUNK, WIDTH), lambda b, qb: (b, qb, 0))
    full = pl.BlockSpec((None, seq, WIDTH), lambda b, qb: (b, 0, 0), pipeline_mode=pl.Buffered(1))
    o = pl.pallas_call(
        _attn_kernel,
        grid=(batch, seq // CHUNK),
        in_specs=[blk, full, full],
        out_specs=blk,
        out_shape=jax.ShapeDtypeStruct((batch, seq, WIDTH), BF16),
        scratch_shapes=[pltpu.VMEM((WIDTH // LANES, 2 * CHUNK, LANES), F32),
                        pltpu.VMEM((CHUNK, WIDTH), F32),
                        pltpu.VMEM((WIDTH // LANES, 2 * CHUNK, LANES), F32)],
        compiler_params=pltpu.CompilerParams(
            dimension_semantics=("arbitrary", "arbitrary"), vmem_limit_bytes=VMEM_LIMIT),
        name="attn",
    )(q3, k3, v3)
    return o.reshape(batch * seq, WIDTH)


def _choose_tile(hb, wr_ref, br_ref):
    logits = jnp.dot(hb, wr_ref[...], preferred_element_type=F32) + br_ref[...]
    lane_t = lax.broadcasted_iota(jnp.int32, (TM_MIX, LANES), 1)
    lanef = lane_t.astype(F32)
    neg = -jnp.inf
    far = float(LANES)

    def first_max(vals):
        m = jnp.max(vals, axis=-1, keepdims=True)
        idx = jnp.min(jnp.where(vals == m, lanef, far), axis=-1, keepdims=True)
        return m, idx

    gl = jnp.where(lane_t < N_GROUPS, logits, neg)
    gmax, grp = first_max(gl)
    grp_w = 1.0 / jnp.sum(jnp.exp(gl - gmax), axis=-1, keepdims=True)
    lo_lane = ROUTE_LANE0 + EXPERTS_PER_GROUP * grp
    in_group = jnp.logical_and(lanef >= lo_lane, lanef < lo_lane + EXPERTS_PER_GROUP)
    el = jnp.where(in_group, logits, neg)
    m1, i1 = first_max(el)
    m2, i2 = first_max(jnp.where(lanef == i1, neg, el))
    e21 = jnp.exp(m2 - m1)
    w1 = grp_w / (1.0 + e21)
    w2 = w1 * e21
    return i1, i2, w1, w2


def _rank_tile(choice, run_ref, valid):
    i1, i2, w1, w2 = choice
    lane_t = lax.broadcasted_iota(jnp.int32, (TM_MIX, LANES), 1)
    lanef = lane_t.astype(F32)
    onehot = jnp.logical_and(jnp.logical_or(lanef == i1, lanef == i2), valid)
    rt = lax.broadcasted_iota(jnp.int32, (TM_MIX, TM_MIX), 0)
    ct = lax.broadcasted_iota(jnp.int32, (TM_MIX, TM_MIX), 1)
    before = jnp.where(ct < rt, 1.0, 0.0).astype(BF16)
    prior = jnp.dot(before, jnp.where(onehot, 1.0, 0.0).astype(BF16),
                    preferred_element_type=F32) + run_ref[...]
    r1 = jnp.sum(jnp.where(lanef == i1, prior, 0.0), axis=-1, keepdims=True)
    r2 = jnp.sum(jnp.where(lanef == i2, prior, 0.0), axis=-1, keepdims=True)
    run_ref[...] += jnp.sum(jnp.where(onehot, 1.0, 0.0), axis=0, keepdims=True)

    fields = (i1 - ROUTE_LANE0, i2 - ROUTE_LANE0, w1, w2, r1, r2)
    route = jnp.zeros((TM_MIX, LANES), F32)
    for pos, val in enumerate(fields):
        route = jnp.where(lane_t == pos, val, route)
    return route


def _mix_kernel(x_ref, oa_ref, u_ref, vs_ref, ga_ref, gb_ref,
                wsp_ref, bsp_ref, wua_ref, wub_ref, wo_ref, gf_ref, wr_ref, br_ref,
                xo_ref, h_ref, route_ref, fields_ref, cnt_ref, run_ref, xprev_ref,
                wua_b, wub_b, wo_b):
    step = pl.program_id(0)

    @pl.when(step == 0)
    def _():
        run_ref[...] = jnp.zeros_like(run_ref)
        xprev_ref[...] = jnp.zeros_like(xprev_ref)
        wua_b[...] = wua_ref[...].astype(BF16)
        wub_b[...] = wub_ref[...].astype(BF16)
        wo_b[...] = wo_ref[...].astype(BF16)

    lane = lax.broadcasted_iota(jnp.int32, (CHUNK, LANES), 1)
    row = lax.broadcasted_iota(jnp.int32, (CHUNK, LANES), 0)
    first = lane < HEAD_DIM
    tril = lane <= row

    w_pairs = []
    for gp in range(WIDTH // LANES):
        w_pairs.append(jnp.concatenate(
            [jnp.where(tril, wsp_ref[g], 0.0).astype(BF16) for g in (2 * gp, 2 * gp + 1)],
            axis=1))
    ob_chunks = []
    for c in range(TM_MIX // CHUNK):
        rows = slice(c * CHUNK, (c + 1) * CHUNK)
        cols = []
        for gp in range(WIDTH // LANES):
            vpair = vs_ref[rows, gp * LANES:(gp + 1) * LANES]
            zero = jnp.zeros_like(vpair)
            stacked = jnp.concatenate([jnp.where(first, vpair, zero),
                                       jnp.where(first, zero, vpair)], axis=0)
            cols.append(jnp.dot(w_pairs[gp], stacked, preferred_element_type=F32))
        mixed = jnp.concatenate(cols, axis=-1) + bsp_ref[...]
        ob_chunks.append((u_ref[rows, :].astype(F32) * mixed).astype(BF16))
    ob = jnp.concatenate(ob_chunks, axis=0)

    h = _rms(xprev_ref[...], gf_ref[...])
    _store_row_tiles(h_ref, h)
    choice = _choose_tile(h.astype(BF16), wr_ref, br_ref)

    up_a = jnp.dot(oa_ref[...], wua_b[...], preferred_element_type=F32)
    up_b = jnp.dot(ob, wub_b[...], preferred_element_type=F32)
    merged = ga_ref[...].astype(F32) * up_a + gb_ref[...].astype(F32) * up_b
    x = x_ref[...] + jnp.dot(merged.astype(BF16), wo_b[...], preferred_element_type=F32)
    xo_ref[...] = x
    xprev_ref[...] = x

    route = _rank_tile(choice, run_ref, step > 0)
    route_ref[...] = route
    fields_ref[...] = route.T[:ROUTE_FIELDS, :]
    cnt_ref[...] = run_ref[...]


def _mix(layer, x, oa, u, vs, ga, gb, wsp, bsp, wua, wub, wo, gf, wr, br):
    n = x.shape[0]
    tiles = n // TM_MIX
    cur = lambda i: jnp.minimum(i, tiles - 1)
    prev = lambda i: jnp.maximum(i - 1, 0)
    row = lambda width: pl.BlockSpec((TM_MIX, width), lambda i: (cur(i), 0))
    lay = lambda *shape: pl.BlockSpec((None,) + shape, lambda i: (layer,) + (0,) * len(shape))
    once = lambda *shape: pl.BlockSpec((None,) + shape, lambda i: (layer,) + (0,) * len(shape),
                                       pipeline_mode=pl.Buffered(1))
    return pl.pallas_call(
        _mix_kernel,
        grid=(tiles + 1,),
        in_specs=[row(D_MODEL), row(WIDTH), row(WIDTH), row(WIDTH), row(D_MODEL), row(D_MODEL),
                  lay(HEADS, CHUNK, CHUNK), lay(CHUNK, WIDTH), once(WIDTH, D_MODEL),
                  once(WIDTH, D_MODEL), once(D_MODEL, D_MODEL), lay(1, D_MODEL),
                  lay(D_MODEL, LANES), lay(1, LANES)],
        out_specs=[row(D_MODEL),
                   pl.BlockSpec((TM_MIX * ROW_TILE, LANES), lambda i: (prev(i), 0)),
                   pl.BlockSpec((TM_MIX, LANES), lambda i: (prev(i), 0)),
                   pl.BlockSpec((ROUTE_FIELDS, TM_MIX), lambda i: (0, prev(i))),
                   pl.BlockSpec((1, LANES), lambda i: (0, 0))],
        out_shape=[jax.ShapeDtypeStruct((n, D_MODEL), F32),
                   jax.ShapeDtypeStruct((n * ROW_TILE, LANES), U32),
                   jax.ShapeDtypeStruct((n, LANES), F32),
                   jax.ShapeDtypeStruct((ROUTE_FIELDS, n), F32),
                   jax.ShapeDtypeStruct((1, LANES), F32)],
        scratch_shapes=[pltpu.VMEM((1, LANES), F32), pltpu.VMEM((TM_MIX, D_MODEL), F32),
                        pltpu.VMEM((WIDTH, D_MODEL), BF16), pltpu.VMEM((WIDTH, D_MODEL), BF16),
                        pltpu.VMEM((D_MODEL, D_MODEL), BF16)],
        compiler_params=pltpu.CompilerParams(
            dimension_semantics=("arbitrary",), vmem_limit_bytes=VMEM_LIMIT),
        name="mix",
    )(x, oa, u, vs, ga, gb, wsp, bsp, wua, wub, wo, gf, wr, br)


def _dispatch_kernel(dest_ref, zrow_ref, h_ref, xs_ref, zero_ref, sem, zsem, tsem):
    i = pl.program_id(0)
    block_rows = MOE_ROWS * ROW_TILE
    nblk = xs_ref.shape[0] // block_rows
    nused = zrow_ref[N_EXPERTS]

    def zero_copy(first_row, zero_sem):
        return pltpu.make_async_copy(zero_ref, xs_ref.at[pl.ds(first_row, block_rows), :],
                                     zero_sem)

    def tail_copy(j):
        return zero_copy((nused + j) * block_rows, tsem)

    @pl.when(i == 0)
    def _():
        zero_ref[...] = jnp.zeros_like(zero_ref)
        for j in range(N_EXPERTS):
            pl.when(nused + j < nblk)(lambda j=j: tail_copy(j).start())
        for e in range(N_EXPERTS):
            pl.when(zrow_ref[e] >= 0)(lambda e=e: zero_copy(zrow_ref[e], zsem).start())
        for e in range(N_EXPERTS):
            pl.when(zrow_ref[e] >= 0)(lambda e=e: zero_copy(zrow_ref[e], zsem).wait())

    n = dest_ref.shape[0] // 2
    base = i * TM_DISP
    for t in range(TM_DISP):
        for k in range(2):
            d = dest_ref[k * n + base + t]
            pltpu.make_async_copy(h_ref.at[pl.ds(t * ROW_TILE, ROW_TILE), :],
                                  xs_ref.at[pl.ds(d * ROW_TILE, ROW_TILE), :],
                                  sem).start(priority=k)
    for k in range(2):
        pltpu.make_async_copy(h_ref, xs_ref.at[pl.ds(0, TM_DISP * ROW_TILE), :], sem).wait()

    @pl.when(i == pl.num_programs(0) - 1)
    def _():
        for j in range(N_EXPERTS):
            pl.when(nused + j < nblk)(lambda j=j: tail_copy(j).wait())


def _dispatch(dest, zrow, h, cap):
    n = h.shape[0] // ROW_TILE
    return pl.pallas_call(
        _dispatch_kernel,
        grid_spec=pltpu.PrefetchScalarGridSpec(
            num_scalar_prefetch=2,
            grid=(n // TM_DISP,),
            in_specs=[pl.BlockSpec((TM_DISP * ROW_TILE, LANES), lambda i, dest, zrow: (i, 0))],
            out_specs=pl.BlockSpec(memory_space=pl.ANY),
            scratch_shapes=[pltpu.VMEM((MOE_ROWS * ROW_TILE, LANES), U32),
                            pltpu.SemaphoreType.DMA(()), pltpu.SemaphoreType.DMA(()),
                            pltpu.SemaphoreType.DMA(())],
        ),
        out_shape=jax.ShapeDtypeStruct((cap * ROW_TILE, LANES), U32),
        compiler_params=pltpu.CompilerParams(
            dimension_semantics=("arbitrary",), vmem_limit_bytes=VMEM_LIMIT),
        name="dispatch",
    )(dest, zrow, h)


def _expert_kernel(layer, blk_e_ref, nused_ref, half_ref, next_ref, xs_ref, wi_hbm, wo_hbm,
                   ys_ref, wi_f, wo_f, wi_b, wo_b, sem):
    b = pl.program_id(0)
    used = b < nused_ref[0]
    expert = blk_e_ref[b]

    def fetch(e, half):
        return (pltpu.make_async_copy(wi_hbm.at[layer, e], wi_f.at[half], sem.at[half, 0]),
                pltpu.make_async_copy(wo_hbm.at[layer, e], wo_f.at[half], sem.at[half, 1]))

    @pl.when(b == 0)
    def _():
        for copy in fetch(expert, half_ref[expert]):
            copy.start()

    @pl.when(jnp.logical_not(used))
    def _():
        ys_ref[...] = jnp.zeros_like(ys_ref)

    new_expert = jnp.logical_or(b == 0, expert != blk_e_ref[jnp.maximum(b - 1, 0)])

    @pl.when(jnp.logical_and(used, new_expert))
    def _():
        half = half_ref[expert]
        for copy in fetch(expert, half):
            copy.wait()
        following = next_ref[expert]

        @pl.when(following >= 0)
        def _():
            for copy in fetch(following, 1 - half):
                copy.start()

        wi_b[...] = wi_f[half].astype(BF16)
        wo_b[...] = wo_f[half].astype(BF16)

    @pl.when(used)
    def _():
        xs = _load_row_tiles(xs_ref, MOE_ROWS).astype(BF16)
        gu = jnp.dot(xs, wi_b[...], preferred_element_type=F32)
        act = jax.nn.silu(gu[:, :EXPERT_FF]) * gu[:, EXPERT_FF:]
        y = jnp.dot(act.astype(BF16), wo_b[...], preferred_element_type=F32)
        _store_row_tiles(ys_ref, y)


def _experts(layer, blk_e, nused, half, following, xs, w_e_in, w_e_out):
    cap = xs.shape[0] // ROW_TILE
    rows = MOE_ROWS * ROW_TILE
    return pl.pallas_call(
        functools.partial(_expert_kernel, layer),
        grid_spec=pltpu.PrefetchScalarGridSpec(
            num_scalar_prefetch=4,
            grid=(cap // MOE_ROWS,),
            in_specs=[pl.BlockSpec((rows, LANES),
                                   lambda b, blk_e, nused, *_: (jnp.minimum(b, nused[0] - 1), 0)),
                      pl.BlockSpec(memory_space=pl.ANY),
                      pl.BlockSpec(memory_space=pl.ANY)],
            out_specs=pl.BlockSpec((rows, LANES), lambda b, *_: (b, 0)),
            scratch_shapes=[pltpu.VMEM((2, D_MODEL, 2 * EXPERT_FF), F32),
                            pltpu.VMEM((2, EXPERT_FF, D_MODEL), F32),
                            pltpu.VMEM((D_MODEL, 2 * EXPERT_FF), BF16),
                            pltpu.VMEM((EXPERT_FF, D_MODEL), BF16),
                            pltpu.SemaphoreType.DMA((2, 2))],
        ),
        out_shape=jax.ShapeDtypeStruct((cap * ROW_TILE, LANES), U32),
        compiler_params=pltpu.CompilerParams(
            dimension_semantics=("arbitrary",), vmem_limit_bytes=VMEM_LIMIT),
        name="experts",
    )(blk_e, nused, half, following, xs, w_e_in, w_e_out)


def _ple_kernel(dest_ref, x_ref, route_ref, p_ref, ys_ref, gp_ref, wg_ref, wp_ref,
                xo_ref, y00, y01, y10, y11, wg_b, wp_b, sem):
    ybuf = ((y00, y01), (y10, y11))
    j = pl.program_id(0)

    @pl.when(j == 0)
    def _():
        wg_b[...] = wg_ref[...].astype(BF16)
        wp_b[...] = wp_ref[...].astype(BF16)
    last_step = pl.num_programs(0) - 1

    def start_gather(tile, half):
        n = dest_ref.shape[0] // 2
        base = tile * TM_ROW
        for t in range(TM_ROW):
            for k in range(2):
                d = dest_ref[k * n + base + t]
                pltpu.make_async_copy(ys_ref.at[pl.ds(d * ROW_TILE, ROW_TILE), :],
                                      ybuf[half][k].at[pl.ds(t * ROW_TILE, ROW_TILE), :],
                                      sem.at[half]).start(priority=k)

    def wait_gather(half):
        for k in range(2):
            pltpu.make_async_copy(ys_ref.at[pl.ds(0, TM_ROW * ROW_TILE), :],
                                  ybuf[half][k], sem.at[half]).wait()

    def combine(half):
        rows = slice(half * TM_ROW, (half + 1) * TM_ROW)
        pe = jnp.dot(p_ref[rows, :].astype(BF16), wp_b[...], preferred_element_type=F32)
        wait_gather(half)
        route = route_ref[rows, :]
        y0 = _load_row_tiles(ybuf[half][0], TM_ROW)
        y1 = _load_row_tiles(ybuf[half][1], TM_ROW)
        x = x_ref[rows, :] + route[:, 2:3] * y0 + route[:, 3:4] * y1
        gate = jax.nn.sigmoid(jnp.dot(_rms(x, gp_ref[...]).astype(BF16), wg_b[...],
                                      preferred_element_type=F32))
        xo_ref[rows, :] = x + gate * pe

    @pl.when(j == 0)
    def _():
        start_gather(0, 0)

    start_gather(2 * j + 1, 1)
    combine(0)
    start_gather(jnp.minimum(2 * j + 2, 2 * last_step + 1), 0)
    combine(1)

    @pl.when(j == last_step)
    def _():
        wait_gather(0)


def _ple(layer, dest, x, route, p, ys, gp, wg, wp):
    n = x.shape[0]
    row = lambda width: pl.BlockSpec((2 * TM_ROW, width), lambda j, dest: (j, 0))
    lay = lambda *shape: pl.BlockSpec((None,) + shape,
                                      lambda j, dest: (layer,) + (0,) * len(shape))
    once = lambda *shape: pl.BlockSpec((None,) + shape,
                                       lambda j, dest: (layer,) + (0,) * len(shape),
                                       pipeline_mode=pl.Buffered(1))
    return pl.pallas_call(
        _ple_kernel,
        grid_spec=pltpu.PrefetchScalarGridSpec(
            num_scalar_prefetch=1,
            grid=(n // (2 * TM_ROW),),
            in_specs=[row(D_MODEL), row(LANES),
                      pl.BlockSpec((None, 2 * TM_ROW, PLE_DIM), lambda j, dest: (layer, j, 0)),
                      pl.BlockSpec(memory_space=pl.ANY),
                      lay(1, D_MODEL), once(D_MODEL, D_MODEL), once(PLE_DIM, D_MODEL)],
            out_specs=row(D_MODEL),
            scratch_shapes=[pltpu.VMEM((TM_ROW * ROW_TILE, LANES), U32)] * 4
            + [pltpu.VMEM((D_MODEL, D_MODEL), BF16), pltpu.VMEM((PLE_DIM, D_MODEL), BF16),
               pltpu.SemaphoreType.DMA((2,))],
        ),
        out_shape=jax.ShapeDtypeStruct((n, D_MODEL), F32),
        compiler_params=pltpu.CompilerParams(
            dimension_semantics=("arbitrary",), vmem_limit_bytes=VMEM_LIMIT),
        name="ple",
    )(dest, x, route, p, ys, gp, wg, wp)


def _slot_layout(fields, counts, cap):
    cnt = counts[0, ROUTE_LANE0:ROUTE_LANE0 + N_EXPERTS].astype(jnp.int32)
    padded = (cnt + MOE_ROWS - 1) // MOE_ROWS * MOE_ROWS
    pend = jnp.cumsum(padded)
    poff = pend - padded
    ids = jnp.arange(N_EXPERTS, dtype=jnp.int32)

    def slots(k):
        eid = fields[k].astype(jnp.int32)
        rank = fields[4 + k].astype(jnp.int32)
        return jnp.sum(jnp.where(eid[None, :] == ids[:, None], poff[:, None], 0), axis=0) + rank

    dest = jnp.concatenate([slots(0), slots(1)]).astype(jnp.int32)
    blk_start = jnp.arange(cap // MOE_ROWS, dtype=jnp.int32) * MOE_ROWS
    blk_e = jnp.minimum(jnp.sum(pend[None, :] <= blk_start[:, None], axis=1), N_EXPERTS - 1)
    nused = (pend[-1:] // MOE_ROWS).astype(jnp.int32)
    zrow = jnp.where(cnt > 0, (pend - MOE_ROWS) * ROW_TILE, -1).astype(jnp.int32)
    nonempty = cnt > 0
    half = ((jnp.cumsum(nonempty) - 1) % 2).astype(jnp.int32)
    later = jnp.logical_and(nonempty[None, :], ids[None, :] > ids[:, None])
    following = jnp.min(jnp.where(later, ids[None, :], N_EXPERTS), axis=1)
    following = jnp.where(following < N_EXPERTS, following, -1).astype(jnp.int32)
    return (dest, jnp.concatenate([zrow, nused]), blk_e.astype(jnp.int32), nused, half,
            following)


def kernel(x, p, norm_mix, w_in, q_norm, k_norm, sgu_norm, w_spatial, b_spatial, w_up_a, w_up_b,
           w_out, norm_ffn, w_group_router, b_group_router, w_expert_router, b_expert_router,
           w_expert_in, w_expert_out, norm_ple, w_ple_gate, w_ple_proj):
    batch, seq, d = x.shape
    depth = w_in.shape[0]
    n = batch * seq
    cap = 2 * n + N_EXPERTS * MOE_ROWS

    vec = lambda a: a[:, None, :]
    qn2, kn2 = vec(jnp.tile(q_norm, (1, 2))), vec(jnp.tile(k_norm, (1, 2)))
    b_sp = jnp.repeat(jnp.swapaxes(b_spatial, 1, 2), HEAD_DIM, axis=2)
    pad = LANES - N_GROUPS - N_EXPERTS
    w_r = jnp.pad(jnp.concatenate([w_group_router, w_expert_router], axis=2),
                  ((0, 0), (0, 0), (0, pad))).astype(BF16)
    b_r = vec(jnp.pad(jnp.concatenate([b_group_router, b_expert_router], axis=1),
                      ((0, 0), (0, pad))))
    p2 = p.reshape(depth, n, PLE_DIM)

    xf = x.reshape(n, d)
    for i in range(depth):
        q, k, v, u, vs, ga, gb = _inproj(i, xf, vec(norm_mix), w_in, qn2, kn2, vec(sgu_norm))
        oa = _attention(q, k, v, batch, seq)
        xf, h, route, fields, counts = _mix(i, xf, oa, u, vs, ga, gb, w_spatial, b_sp, w_up_a,
                                            w_up_b, w_out, vec(norm_ffn), w_r, b_r)
        dest, zrow, blk_e, nused, half, following = _slot_layout(fields, counts, cap)
        xs = _dispatch(dest, zrow, h, cap)
        ys = _experts(i, blk_e, nused, half, following, xs, w_expert_in, w_expert_out)
        xf = _ple(i, dest, xf, route, p2, ys, vec(norm_ple), w_ple_gate, w_ple_proj)
    return xf.reshape(batch, seq, d)
```

```python
import functools
import math

import jax
import jax.numpy as jnp
from jax import lax
from jax.experimental import pallas as pl
from jax.experimental.pallas import tpu as pltpu

F32 = jnp.float32
BF16 = jnp.bfloat16

D_MODEL = 1024
HEADS = 8
HEAD_DIM = 64
WIDTH = HEADS * HEAD_DIM
CHUNK = 128
IN_WIDTH = 5 * WIDTH + 2 * D_MODEL
N_GROUPS = 4
EXPERTS_PER_GROUP = 8
N_EXPERTS = N_GROUPS * EXPERTS_PER_GROUP
EXPERT_FF = 512
PLE_DIM = 256
EPS = 1e-6

LANES = 128
ROW_TILE = D_MODEL // (2 * LANES)
U32 = jnp.uint32
ROUTE_LANE0 = N_GROUPS
ROUTE_FIELDS = 8
TM_IN = 256
TM_MIX = 256
TM_DISP = 1024
TM_ROW = 256
MOE_ROWS = 256
TOP_ROWS = 48
LOG2E = 1.4426950408889634
DEAD_LOG2 = -105.0 * LOG2E
VMEM_LIMIT = 48 * 1024 * 1024


def _rms(x, gain):
    return x * lax.rsqrt(jnp.mean(x * x, axis=-1, keepdims=True) + EPS) * gain


def _store_row_tiles(ref, value, lead=()):
    rows = value.shape[0]
    bits = lambda v: lax.bitcast_convert_type(v.astype(BF16).astype(F32), U32)
    for c in range(ROW_TILE):
        low = bits(value[:, 2 * c * LANES:(2 * c + 1) * LANES]) >> 16
        high = bits(value[:, (2 * c + 1) * LANES:(2 * c + 2) * LANES]) & U32(0xFFFF0000)
        ref[lead + (pl.ds(c, rows, stride=ROW_TILE), slice(None))] = low | high


def _load_row_tiles(ref, rows, lead=()):
    chunks = []
    for c in range(ROW_TILE):
        words = ref[lead + (pl.ds(c, rows, stride=ROW_TILE), slice(None))]
        chunks.append(lax.bitcast_convert_type(words << 16, F32))
        chunks.append(lax.bitcast_convert_type(words & U32(0xFFFF0000), F32))
    return jnp.concatenate(chunks, axis=1)


def _inproj_kernel(x_ref, g_ref, wf_ref, qn_ref, kn_ref, sn_ref,
                   q_ref, k_ref, v_ref, u_ref, vs_ref, ga_ref, gb_ref, w_ref):
    @pl.when(pl.program_id(0) == 0)
    def _():
        w_ref[...] = wf_ref[...].astype(BF16)

    hb = _rms(x_ref[...], g_ref[...]).astype(BF16)

    def proj(lo, hi):
        return jnp.dot(hb, w_ref[:, lo:hi], preferred_element_type=F32)

    first = lax.broadcasted_iota(jnp.int32, (1, LANES), 1) < HEAD_DIM

    def head_norm(z, gain, scale):
        outs = []
        for c in range(WIDTH // LANES):
            zc = z[:, c * LANES:(c + 1) * LANES]
            sq = zc * zc
            sa = jnp.sum(jnp.where(first, sq, 0.0), axis=-1, keepdims=True)
            sb = jnp.sum(jnp.where(first, 0.0, sq), axis=-1, keepdims=True)
            ms = jnp.where(first, sa, sb) * (1.0 / HEAD_DIM)
            outs.append(zc * lax.rsqrt(ms + EPS) * (gain * scale))
        return jnp.concatenate(outs, axis=-1)

    w = WIDTH
    q_ref[...] = head_norm(proj(0, w), qn_ref[...], LOG2E / math.sqrt(HEAD_DIM)).astype(BF16)
    k_ref[...] = head_norm(proj(w, 2 * w), kn_ref[...], 1.0).astype(BF16)
    v_ref[...] = proj(2 * w, 3 * w).astype(BF16)
    u_ref[...] = jax.nn.gelu(proj(3 * w, 4 * w)).astype(BF16)
    vs_ref[...] = _rms(jax.nn.gelu(proj(4 * w, 5 * w)), sn_ref[...]).astype(BF16)
    ga_ref[...] = jax.nn.sigmoid(proj(5 * w, 5 * w + D_MODEL)).astype(BF16)
    gb_ref[...] = jax.nn.sigmoid(proj(5 * w + D_MODEL, IN_WIDTH)).astype(BF16)


def _inproj(layer, x, norm_mix, w_in, qn, kn, sn):
    n = x.shape[0]
    row = lambda width: pl.BlockSpec((TM_IN, width), lambda i: (i, 0))
    vec = lambda width: pl.BlockSpec((None, 1, width), lambda i: (layer, 0, 0))
    out = lambda width: jax.ShapeDtypeStruct((n, width), BF16)
    return pl.pallas_call(
        _inproj_kernel,
        grid=(n // TM_IN,),
        in_specs=[row(D_MODEL), vec(D_MODEL),
                  pl.BlockSpec((None, D_MODEL, IN_WIDTH), lambda i: (layer, 0, 0),
                               pipeline_mode=pl.Buffered(1)),
                  vec(LANES), vec(LANES), vec(WIDTH)],
        out_specs=[row(WIDTH)] * 5 + [row(D_MODEL)] * 2,
        out_shape=[out(WIDTH)] * 5 + [out(D_MODEL)] * 2,
        scratch_shapes=[pltpu.VMEM((D_MODEL, IN_WIDTH), BF16)],
        compiler_params=pltpu.CompilerParams(
            dimension_semantics=("arbitrary",), vmem_limit_bytes=VMEM_LIMIT),
        name="inproj",
    )(x, norm_mix, w_in, qn, kn, sn)


def _attn_kernel(q_ref, k_ref, v_ref, o_ref, carry_ref, acc_ref, z_ref):
    qb = pl.program_id(1)
    pairs = WIDTH // LANES
    first = lax.broadcasted_iota(jnp.int32, (CHUNK, LANES), 1) < HEAD_DIM
    first_top = lax.broadcasted_iota(jnp.int32, (TOP_ROWS, LANES), 1) < HEAD_DIM
    kr = lax.broadcasted_iota(jnp.int32, (LANES, 2 * LANES), 0)
    kc = lax.broadcasted_iota(jnp.int32, (LANES, 2 * LANES), 1)
    tail = jnp.where(jnp.logical_or(kr > kc, kc >= LANES), 1.0, 0.0).astype(BF16)
    sign = jnp.uint32(0x80000000)
    cols = [slice(p * LANES, (p + 1) * LANES) for p in range(pairs)]

    def stack(value, rows):
        if rows == CHUNK:
            return value
        return jnp.concatenate([value[:rows], value[CHUNK:CHUNK + rows]], axis=0)

    queries, causal = {}, {}
    for rows in (CHUNK, TOP_ROWS):
        lane = lax.broadcasted_iota(jnp.int32, (2 * rows, LANES), 1)
        row = lax.broadcasted_iota(jnp.int32, (2 * rows, LANES), 0)
        upper = row < rows
        own = jnp.logical_xor(lane < HEAD_DIM, jnp.logical_not(upper))
        causal[rows] = lane < jnp.where(upper, row, row - rows)
        queries[rows] = []
        for p in range(pairs):
            qp = q_ref[:rows, cols[p]]
            q2 = jnp.concatenate([qp, qp], axis=0)
            queries[rows].append(jnp.where(own, q2, jnp.zeros_like(q2)))
    full_row = lax.broadcasted_iota(jnp.int32, (2 * CHUNK, LANES), 0)
    is_top = jnp.where(full_row < CHUNK, full_row, full_row - CHUNK) < TOP_ROWS

    def scores(j, rows):
        start = pl.multiple_of(j * CHUNK, CHUNK)
        return [lax.dot_general(queries[rows][p], k_ref[pl.ds(start, CHUNK), cols[p]],
                                (((1,), (1,)), ((), ())), preferred_element_type=F32)
                for p in range(pairs)]

    def sweep(j, heights, diagonal=False, fresh=False, ahead=True, rest_only=False):
        blocks = range(len(heights))
        full = [b for b in blocks if heights[b] == CHUNK]
        top = [b for b in blocks if heights[b] == TOP_ROWS]
        starts = [pl.multiple_of((j - b) * CHUNK, CHUNK) for b in blocks]
        zs = []
        for b in blocks:
            if b > 0 or fresh or diagonal:
                zs.append(scores(j - b, heights[b]))
            else:
                zs.append([stack(z_ref[p], heights[b]) for p in range(pairs)])
        if ahead:
            for p, z in enumerate(scores(jnp.maximum(j - len(heights), 0), CHUNK)):
                z_ref[p] = z
        log_betas, splits = {}, {}
        for b in blocks:
            for p in range(pairs):
                z = zs[b][p]
                neg_abs = lax.bitcast_convert_type(
                    lax.bitcast_convert_type(z, jnp.uint32) | sign, F32)
                log1p = jnp.log(1.0 + jnp.exp2(neg_abs)) * LOG2E
                log_beta = jnp.minimum(z, 0.0) - log1p
                log_keep = log_beta - z
                if diagonal and b == 0:
                    log_keep = jnp.where(causal[heights[b]], log_keep, 0.0)
                if rest_only:
                    log_keep = jnp.where(is_top, 0.0, log_keep)
                log_betas[b, p] = log_beta
                splits[b, p] = log_keep.astype(BF16)
        sums = {key: jnp.dot(s, tail, preferred_element_type=F32) for key, s in splits.items()}
        entering, after_full, after_top = {}, [], []
        for p in range(pairs):
            carry = 0.0 if diagonal else carry_ref[p]
            for b in full:
                entering[b, p] = carry
                carry = carry + sums[b, p][:, LANES:]
            after_full.append(carry)
            carry = stack(carry, TOP_ROWS) if top else None
            for b in top:
                entering[b, p] = carry
                carry = carry + sums[b, p][:, LANES:]
            after_top.append(carry)
        alive = lambda c: jnp.max(c) > DEAD_LOG2
        worst = functools.reduce(jnp.maximum, after_full)
        live_rest = alive(jnp.where(is_top, -jnp.inf, worst)) if full else jnp.bool_(False)
        if top:
            live_top = alive(functools.reduce(jnp.maximum, after_top))
        else:
            live_top = alive(jnp.where(is_top, worst, -jnp.inf))
        outs = {}
        for b in blocks:
            for p in range(pairs):
                a = jnp.exp2(log_betas[b, p] + sums[b, p][:, :LANES] + entering[b, p])
                if diagonal and b == 0:
                    a = jnp.where(causal[heights[b]], a, 0.0)
                if rest_only:
                    a = jnp.where(is_top, 0.0, a)
                outs[b, p] = jnp.dot(a.astype(BF16), v_ref[pl.ds(starts[b], CHUNK), cols[p]],
                                     preferred_element_type=F32)
        for p in range(pairs):
            if full:
                carry_ref[p] = after_full[p]
            if top:
                carry_ref[p, 0:TOP_ROWS, :] = after_top[p][:TOP_ROWS]
                carry_ref[p, CHUNK:CHUNK + TOP_ROWS, :] = after_top[p][TOP_ROWS:]
        for p in range(pairs):
            if full:
                o2 = functools.reduce(jnp.add, [outs[b, p] for b in full])
                o = jnp.where(first, o2[:CHUNK], o2[CHUNK:])
                acc_ref[:, cols[p]] = o if diagonal else acc_ref[:, cols[p]] + o
            if top:
                o2 = functools.reduce(jnp.add, [outs[b, p] for b in top])
                o = jnp.where(first_top, o2[:TOP_ROWS], o2[TOP_ROWS:])
                acc_ref[0:TOP_ROWS, cols[p]] += o
        return live_top, live_rest

    def sweep_left(j, live_top, live_rest, height):
        def cond(state):
            j, live_top, live_rest = state
            return jnp.logical_and(j >= 0, jnp.logical_or(live_top, live_rest))

        def body(state):
            return (state[0] - 1,) + sweep(state[0], [height])

        lax.while_loop(cond, body, (j, live_top, live_rest))

    def start(n_full, n_top):
        live_top, live_rest = sweep(qb, [CHUNK] * n_full + [TOP_ROWS] * n_top, diagonal=True)
        if n_top == 0:
            return
        left = qb - n_full - n_top

        @pl.when(live_rest)
        def _():
            flags = sweep(qb - n_full, [CHUNK] * n_top, fresh=True, ahead=False, rest_only=True)
            sweep_left(left, *flags, CHUNK)

        @pl.when(jnp.logical_not(live_rest))
        def _():
            sweep_left(left, live_top, jnp.bool_(False), TOP_ROWS)

    for blocks_left, shape in enumerate(((1, 0), (2, 0))):
        pl.when(qb == blocks_left)(functools.partial(start, *shape))
    pl.when(qb >= 2)(functools.partial(start, 2, 1))

    o_ref[...] = acc_ref[...].astype(BF16)


def _attention(q, k, v, batch, seq):
    q3, k3, v3 = (t.reshape(batch, seq, WIDTH) for t in (q, k, v))
    blk = pl.BlockSpec((None, CHUNK, WIDTH), lambda b, qb: (b, qb, 0))
    full = pl.BlockSpec((None, seq, WIDTH), lambda b, qb: (b, 0, 0), pipeline_mode=pl.Buffered(1))
    o = pl.pallas_call(
        _attn_kernel,
        grid=(batch, seq // CHUNK),
        in_specs=[blk, full, full],
        out_specs=blk,
        out_shape=jax.ShapeDtypeStruct((batch, seq, WIDTH), BF16),
        scratch_shapes=[pltpu.VMEM((WIDTH // LANES, 2 * CHUNK, LANES), F32),
                        pltpu.VMEM((CHUNK, WIDTH), F32),
                        pltpu.VMEM((WIDTH // LANES, 2 * CHUNK, LANES), F32)],
        compiler_params=pltpu.CompilerParams(
            dimension_semantics=("arbitrary", "arbitrary"), vmem_limit_bytes=VMEM_LIMIT),
        name="attn",
    )(q3, k3, v3)
    return o.reshape(batch * seq, WIDTH)


def _choose_tile(hb, wr_ref, br_ref):
    logits = jnp.dot(hb, wr_ref[...], preferred_element_type=F32) + br_ref[...]
    lane_t = lax.broadcasted_iota(jnp.int32, (TM_MIX, LANES), 1)
    lanef = lane_t.astype(F32)
    neg = -jnp.inf
    far = float(LANES)

    def first_max(vals):
        m = jnp.max(vals, axis=-1, keepdims=True)
        idx = jnp.min(jnp.where(vals == m, lanef, far), axis=-1, keepdims=True)
        return m, idx

    gl = jnp.where(lane_t < N_GROUPS, logits, neg)
    gmax, grp = first_max(gl)
    grp_w = 1.0 / jnp.sum(jnp.exp(gl - gmax), axis=-1, keepdims=True)
    lo_lane = ROUTE_LANE0 + EXPERTS_PER_GROUP * grp
    in_group = jnp.logical_and(lanef >= lo_lane, lanef < lo_lane + EXPERTS_PER_GROUP)
    el = jnp.where(in_group, logits, neg)
    m1, i1 = first_max(el)
    m2, i2 = first_max(jnp.where(lanef == i1, neg, el))
    e21 = jnp.exp(m2 - m1)
    w1 = grp_w / (1.0 + e21)
    w2 = w1 * e21
    return i1, i2, w1, w2


def _rank_tile(choice, run_ref, valid):
    i1, i2, w1, w2 = choice
    lane_t = lax.broadcasted_iota(jnp.int32, (TM_MIX, LANES), 1)
    lanef = lane_t.astype(F32)
    onehot = jnp.logical_and(jnp.logical_or(lanef == i1, lanef == i2), valid)
    rt = lax.broadcasted_iota(jnp.int32, (TM_MIX, TM_MIX), 0)
    ct = lax.broadcasted_iota(jnp.int32, (TM_MIX, TM_MIX), 1)
    before = jnp.where(ct < rt, 1.0, 0.0).astype(BF16)
    prior = jnp.dot(before, jnp.where(onehot, 1.0, 0.0).astype(BF16),
                    preferred_element_type=F32) + run_ref[...]
    r1 = jnp.sum(jnp.where(lanef == i1, prior, 0.0), axis=-1, keepdims=True)
    r2 = jnp.sum(jnp.where(lanef == i2, prior, 0.0), axis=-1, keepdims=True)
    run_ref[...] += jnp.sum(jnp.where(onehot, 1.0, 0.0), axis=0, keepdims=True)

    fields = (i1 - ROUTE_LANE0, i2 - ROUTE_LANE0, w1, w2, r1, r2)
    route = jnp.zeros((TM_MIX, LANES), F32)
    for pos, val in enumerate(fields):
        route = jnp.where(lane_t == pos, val, route)
    return route


def _mix_kernel(x_ref, oa_ref, u_ref, vs_ref, ga_ref, gb_ref,
                wsp_ref, bsp_ref, wua_ref, wub_ref, wo_ref, gf_ref, wr_ref, br_ref,
                xo_ref, h_ref, route_ref, fields_ref, cnt_ref, run_ref, xprev_ref,
                wua_b, wub_b, wo_b):
    step = pl.program_id(0)

    @pl.when(step == 0)
    def _():
        run_ref[...] = jnp.zeros_like(run_ref)
        xprev_ref[...] = jnp.zeros_like(xprev_ref)
        wua_b[...] = wua_ref[...].astype(BF16)
        wub_b[...] = wub_ref[...].astype(BF16)
        wo_b[...] = wo_ref[...].astype(BF16)

    lane = lax.broadcasted_iota(jnp.int32, (CHUNK, LANES), 1)
    row = lax.broadcasted_iota(jnp.int32, (CHUNK, LANES), 0)
    first = lane < HEAD_DIM
    tril = lane <= row

    w_pairs = []
    for gp in range(WIDTH // LANES):
        w_pairs.append(jnp.concatenate(
            [jnp.where(tril, wsp_ref[g], 0.0).astype(BF16) for g in (2 * gp, 2 * gp + 1)],
            axis=1))
    ob_chunks = []
    for c in range(TM_MIX // CHUNK):
        rows = slice(c * CHUNK, (c + 1) * CHUNK)
        cols = []
        for gp in range(WIDTH // LANES):
            vpair = vs_ref[rows, gp * LANES:(gp + 1) * LANES]
            zero = jnp.zeros_like(vpair)
            stacked = jnp.concatenate([jnp.where(first, vpair, zero),
                                       jnp.where(first, zero, vpair)], axis=0)
            cols.append(jnp.dot(w_pairs[gp], stacked, preferred_element_type=F32))
        mixed = jnp.concatenate(cols, axis=-1) + bsp_ref[...]
        ob_chunks.append((u_ref[rows, :].astype(F32) * mixed).astype(BF16))
    ob = jnp.concatenate(ob_chunks, axis=0)

    h = _rms(xprev_ref[...], gf_ref[...])
    _store_row_tiles(h_ref, h)
    choice = _choose_tile(h.astype(BF16), wr_ref, br_ref)

    up_a = jnp.dot(oa_ref[...], wua_b[...], preferred_element_type=F32)
    up_b = jnp.dot(ob, wub_b[...], preferred_element_type=F32)
    merged = ga_ref[...].astype(F32) * up_a + gb_ref[...].astype(F32) * up_b
    x = x_ref[...] + jnp.dot(merged.astype(BF16), wo_b[...], preferred_element_type=F32)
    xo_ref[...] = x
    xprev_ref[...] = x

    route = _rank_tile(choice, run_ref, step > 0)
    route_ref[...] = route
    fields_ref[...] = route.T[:ROUTE_FIELDS, :]
    cnt_ref[...] = run_ref[...]


def _mix(layer, x, oa, u, vs, ga, gb, wsp, bsp, wua, wub, wo, gf, wr, br):
    n = x.shape[0]
    tiles = n // TM_MIX
    cur = lambda i: jnp.minimum(i, tiles - 1)
    prev = lambda i: jnp.maximum(i - 1, 0)
    row = lambda width: pl.BlockSpec((TM_MIX, width), lambda i: (cur(i), 0))
    lay = lambda *shape: pl.BlockSpec((None,) + shape, lambda i: (layer,) + (0,) * len(shape))
    once = lambda *shape: pl.BlockSpec((None,) + shape, lambda i: (layer,) + (0,) * len(shape),
                                       pipeline_mode=pl.Buffered(1))
    return pl.pallas_call(
        _mix_kernel,
        grid=(tiles + 1,),
        in_specs=[row(D_MODEL), row(WIDTH), row(WIDTH), row(WIDTH), row(D_MODEL), row(D_MODEL),
                  lay(HEADS, CHUNK, CHUNK), lay(CHUNK, WIDTH), once(WIDTH, D_MODEL),
                  once(WIDTH, D_MODEL), once(D_MODEL, D_MODEL), lay(1, D_MODEL),
                  lay(D_MODEL, LANES), lay(1, LANES)],
        out_specs=[row(D_MODEL),
                   pl.BlockSpec((TM_MIX * ROW_TILE, LANES), lambda i: (prev(i), 0)),
                   pl.BlockSpec((TM_MIX, LANES), lambda i: (prev(i), 0)),
                   pl.BlockSpec((ROUTE_FIELDS, TM_MIX), lambda i: (0, prev(i))),
                   pl.BlockSpec((1, LANES), lambda i: (0, 0))],
        out_shape=[jax.ShapeDtypeStruct((n, D_MODEL), F32),
                   jax.ShapeDtypeStruct((n * ROW_TILE, LANES), U32),
                   jax.ShapeDtypeStruct((n, LANES), F32),
                   jax.ShapeDtypeStruct((ROUTE_FIELDS, n), F32),
                   jax.ShapeDtypeStruct((1, LANES), F32)],
        scratch_shapes=[pltpu.VMEM((1, LANES), F32), pltpu.VMEM((TM_MIX, D_MODEL), F32),
                        pltpu.VMEM((WIDTH, D_MODEL), BF16), pltpu.VMEM((WIDTH, D_MODEL), BF16),
                        pltpu.VMEM((D_MODEL, D_MODEL), BF16)],
        compiler_params=pltpu.CompilerParams(
            dimension_semantics=("arbitrary",), vmem_limit_bytes=VMEM_LIMIT),
        name="mix",
    )(x, oa, u, vs, ga, gb, wsp, bsp, wua, wub, wo, gf, wr, br)


def _dispatch_kernel(dest_ref, zrow_ref, h_ref, xs_ref, zero_ref, sem, zsem, tsem):
    i = pl.program_id(0)
    block_rows = MOE_ROWS * ROW_TILE
    nblk = xs_ref.shape[0] // block_rows
    nused = zrow_ref[N_EXPERTS]

    def zero_copy(first_row, zero_sem):
        return pltpu.make_async_copy(zero_ref, xs_ref.at[pl.ds(first_row, block_rows), :],
                                     zero_sem)

    def tail_copy(j):
        return zero_copy((nused + j) * block_rows, tsem)

    @pl.when(i == 0)
    def _():
        zero_ref[...] = jnp.zeros_like(zero_ref)
        for j in range(N_EXPERTS):
            pl.when(nused + j < nblk)(lambda j=j: tail_copy(j).start())
        for e in range(N_EXPERTS):
            pl.when(zrow_ref[e] >= 0)(lambda e=e: zero_copy(zrow_ref[e], zsem).start())
        for e in range(N_EXPERTS):
            pl.when(zrow_ref[e] >= 0)(lambda e=e: zero_copy(zrow_ref[e], zsem).wait())

    n = dest_ref.shape[0] // 2
    base = i * TM_DISP
    for t in range(TM_DISP):
        for k in range(2):
            d = dest_ref[k * n + base + t]
            pltpu.make_async_copy(h_ref.at[pl.ds(t * ROW_TILE, ROW_TILE), :],
                                  xs_ref.at[pl.ds(d * ROW_TILE, ROW_TILE), :],
                                  sem).start(priority=k)
    for k in range(2):
        pltpu.make_async_copy(h_ref, xs_ref.at[pl.ds(0, TM_DISP * ROW_TILE), :], sem).wait()

    @pl.when(i == pl.num_programs(0) - 1)
    def _():
        for j in range(N_EXPERTS):
            pl.when(nused + j < nblk)(lambda j=j: tail_copy(j).wait())


def _dispatch(dest, zrow, h, cap):
    n = h.shape[0] // ROW_TILE
    return pl.pallas_call(
        _dispatch_kernel,
        grid_spec=pltpu.PrefetchScalarGridSpec(
            num_scalar_prefetch=2,
            grid=(n // TM_DISP,),
            in_specs=[pl.BlockSpec((TM_DISP * ROW_TILE, LANES), lambda i, dest, zrow: (i, 0))],
            out_specs=pl.BlockSpec(memory_space=pl.ANY),
            scratch_shapes=[pltpu.VMEM((MOE_ROWS * ROW_TILE, LANES), U32),
                            pltpu.SemaphoreType.DMA(()), pltpu.SemaphoreType.DMA(()),
                            pltpu.SemaphoreType.DMA(())],
        ),
        out_shape=jax.ShapeDtypeStruct((cap * ROW_TILE, LANES), U32),
        compiler_params=pltpu.CompilerParams(
            dimension_semantics=("arbitrary",), vmem_limit_bytes=VMEM_LIMIT),
        name="dispatch",
    )(dest, zrow, h)


def _expert_kernel(layer, blk_e_ref, nused_ref, half_ref, next_ref, xs_ref, wi_hbm, wo_hbm,
                   ys_ref, wi_f, wo_f, wi_b, wo_b, sem):
    b = pl.program_id(0)
    used = b < nused_ref[0]
    expert = blk_e_ref[b]

    def fetch(e, half):
        return (pltpu.make_async_copy(wi_hbm.at[layer, e], wi_f.at[half], sem.at[half, 0]),
                pltpu.make_async_copy(wo_hbm.at[layer, e], wo_f.at[half], sem.at[half, 1]))

    @pl.when(b == 0)
    def _():
        for copy in fetch(expert, half_ref[expert]):
            copy.start()

    @pl.when(jnp.logical_not(used))
    def _():
        ys_ref[...] = jnp.zeros_like(ys_ref)

    new_expert = jnp.logical_or(b == 0, expert != blk_e_ref[jnp.maximum(b - 1, 0)])

    @pl.when(jnp.logical_and(used, new_expert))
    def _():
        half = half_ref[expert]
        for copy in fetch(expert, half):
            copy.wait()
        following = next_ref[expert]

        @pl.when(following >= 0)
        def _():
            for copy in fetch(following, 1 - half):
                copy.start()

        wi_b[...] = wi_f[half].astype(BF16)
        wo_b[...] = wo_f[half].astype(BF16)

    @pl.when(used)
    def _():
        xs = _load_row_tiles(xs_ref, MOE_ROWS).astype(BF16)
        gu = jnp.dot(xs, wi_b[...], preferred_element_type=F32)
        act = jax.nn.silu(gu[:, :EXPERT_FF]) * gu[:, EXPERT_FF:]
        y = jnp.dot(act.astype(BF16), wo_b[...], preferred_element_type=F32)
        _store_row_tiles(ys_ref, y)


def _experts(layer, blk_e, nused, half, following, xs, w_e_in, w_e_out):
    cap = xs.shape[0] // ROW_TILE
    rows = MOE_ROWS * ROW_TILE
    return pl.pallas_call(
        functools.partial(_expert_kernel, layer),
        grid_spec=pltpu.PrefetchScalarGridSpec(
            num_scalar_prefetch=4,
            grid=(cap // MOE_ROWS,),
            in_specs=[pl.BlockSpec((rows, LANES),
                                   lambda b, blk_e, nused, *_: (jnp.minimum(b, nused[0] - 1), 0)),
                      pl.BlockSpec(memory_space=pl.ANY),
                      pl.BlockSpec(memory_space=pl.ANY)],
            out_specs=pl.BlockSpec((rows, LANES), lambda b, *_: (b, 0)),
            scratch_shapes=[pltpu.VMEM((2, D_MODEL, 2 * EXPERT_FF), F32),
                            pltpu.VMEM((2, EXPERT_FF, D_MODEL), F32),
                            pltpu.VMEM((D_MODEL, 2 * EXPERT_FF), BF16),
                            pltpu.VMEM((EXPERT_FF, D_MODEL), BF16),
                            pltpu.SemaphoreType.DMA((2, 2))],
        ),
        out_shape=jax.ShapeDtypeStruct((cap * ROW_TILE, LANES), U32),
        compiler_params=pltpu.CompilerParams(
            dimension_semantics=("arbitrary",), vmem_limit_bytes=VMEM_LIMIT),
        name="experts",
    )(blk_e, nused, half, following, xs, w_e_in, w_e_out)


def _ple_kernel(dest_ref, x_ref, route_ref, p_ref, ys_ref, gp_ref, wg_ref, wp_ref,
                xo_ref, y00, y01, y10, y11, wg_b, wp_b, sem):
    ybuf = ((y00, y01), (y10, y11))
    j = pl.program_id(0)

    @pl.when(j == 0)
    def _():
        wg_b[...] = wg_ref[...].astype(BF16)
        wp_b[...] = wp_ref[...].astype(BF16)
    last_step = pl.num_programs(0) - 1

    def start_gather(tile, half):
        n = dest_ref.shape[0] // 2
        base = tile * TM_ROW
        for t in range(TM_ROW):
            for k in range(2):
                d = dest_ref[k * n + base + t]
                pltpu.make_async_copy(ys_ref.at[pl.ds(d * ROW_TILE, ROW_TILE), :],
                                      ybuf[half][k].at[pl.ds(t * ROW_TILE, ROW_TILE), :],
                                      sem.at[half]).start(priority=k)

    def wait_gather(half):
        for k in range(2):
            pltpu.make_async_copy(ys_ref.at[pl.ds(0, TM_ROW * ROW_TILE), :],
                                  ybuf[half][k], sem.at[half]).wait()

    def combine(half):
        rows = slice(half * TM_ROW, (half + 1) * TM_ROW)
        pe = jnp.dot(p_ref[rows, :].astype(BF16), wp_b[...], preferred_element_type=F32)
        wait_gather(half)
        route = route_ref[rows, :]
        y0 = _load_row_tiles(ybuf[half][0], TM_ROW)
        y1 = _load_row_tiles(ybuf[half][1], TM_ROW)
        x = x_ref[rows, :] + route[:, 2:3] * y0 + route[:, 3:4] * y1
        gate = jax.nn.sigmoid(jnp.dot(_rms(x, gp_ref[...]).astype(BF16), wg_b[...],
                                      preferred_element_type=F32))
        xo_ref[rows, :] = x + gate * pe

    @pl.when(j == 0)
    def _():
        start_gather(0, 0)

    start_gather(2 * j + 1, 1)
    combine(0)
    start_gather(jnp.minimum(2 * j + 2, 2 * last_step + 1), 0)
    combine(1)

    @pl.when(j == last_step)
    def _():
        wait_gather(0)


def _ple(layer, dest, x, route, p, ys, gp, wg, wp):
    n = x.shape[0]
    row = lambda width: pl.BlockSpec((2 * TM_ROW, width), lambda j, dest: (j, 0))
    lay = lambda *shape: pl.BlockSpec((None,) + shape,
                                      lambda j, dest: (layer,) + (0,) * len(shape))
    once = lambda *shape: pl.BlockSpec((None,) + shape,
                                       lambda j, dest: (layer,) + (0,) * len(shape),
                                       pipeline_mode=pl.Buffered(1))
    return pl.pallas_call(
        _ple_kernel,
        grid_spec=pltpu.PrefetchScalarGridSpec(
            num_scalar_prefetch=1,
            grid=(n // (2 * TM_ROW),),
            in_specs=[row(D_MODEL), row(LANES),
                      pl.BlockSpec((None, 2 * TM_ROW, PLE_DIM), lambda j, dest: (layer, j, 0)),
                      pl.BlockSpec(memory_space=pl.ANY),
                      lay(1, D_MODEL), once(D_MODEL, D_MODEL), once(PLE_DIM, D_MODEL)],
            out_specs=row(D_MODEL),
            scratch_shapes=[pltpu.VMEM((TM_ROW * ROW_TILE, LANES), U32)] * 4
            + [pltpu.VMEM((D_MODEL, D_MODEL), BF16), pltpu.VMEM((PLE_DIM, D_MODEL), BF16),
               pltpu.SemaphoreType.DMA((2,))],
        ),
        out_shape=jax.ShapeDtypeStruct((n, D_MODEL), F32),
        compiler_params=pltpu.CompilerParams(
            dimension_semantics=("arbitrary",), vmem_limit_bytes=VMEM_LIMIT),
        name="ple",
    )(dest, x, route, p, ys, gp, wg, wp)


def _slot_layout(fields, counts, cap):
    cnt = counts[0, ROUTE_LANE0:ROUTE_LANE0 + N_EXPERTS].astype(jnp.int32)
    padded = (cnt + MOE_ROWS - 1) // MOE_ROWS * MOE_ROWS
    pend = jnp.cumsum(padded)
    poff = pend - padded
    ids = jnp.arange(N_EXPERTS, dtype=jnp.int32)

    def slots(k):
        eid = fields[k].astype(jnp.int32)
        rank = fields[4 + k].astype(jnp.int32)
        return jnp.sum(jnp.where(eid[None, :] == ids[:, None], poff[:, None], 0), axis=0) + rank

    dest = jnp.concatenate([slots(0), slots(1)]).astype(jnp.int32)
    blk_start = jnp.arange(cap // MOE_ROWS, dtype=jnp.int32) * MOE_ROWS
    blk_e = jnp.minimum(jnp.sum(pend[None, :] <= blk_start[:, None], axis=1), N_EXPERTS - 1)
    nused = (pend[-1:] // MOE_ROWS).astype(jnp.int32)
    zrow = jnp.where(cnt > 0, (pend - MOE_ROWS) * ROW_TILE, -1).astype(jnp.int32)
    nonempty = cnt > 0
    half = ((jnp.cumsum(nonempty) - 1) % 2).astype(jnp.int32)
    later = jnp.logical_and(nonempty[None, :], ids[None, :] > ids[:, None])
    following = jnp.min(jnp.where(later, ids[None, :], N_EXPERTS), axis=1)
    following = jnp.where(following < N_EXPERTS, following, -1).astype(jnp.int32)
    return (dest, jnp.concatenate([zrow, nused]), blk_e.astype(jnp.int32), nused, half,
            following)


def kernel(x, p, norm_mix, w_in, q_norm, k_norm, sgu_norm, w_spatial, b_spatial, w_up_a, w_up_b,
           w_out, norm_ffn, w_group_router, b_group_router, w_expert_router, b_expert_router,
           w_expert_in, w_expert_out, norm_ple, w_ple_gate, w_ple_proj):
    batch, seq, d = x.shape
    depth = w_in.shape[0]
    n = batch * seq
    cap = 2 * n + N_EXPERTS * MOE_ROWS

    vec = lambda a: a[:, None, :]
    qn2, kn2 = vec(jnp.tile(q_norm, (1, 2))), vec(jnp.tile(k_norm, (1, 2)))
    b_sp = jnp.repeat(jnp.swapaxes(b_spatial, 1, 2), HEAD_DIM, axis=2)
    pad = LANES - N_GROUPS - N_EXPERTS
    w_r = jnp.pad(jnp.concatenate([w_group_router, w_expert_router], axis=2),
                  ((0, 0), (0, 0), (0, pad))).astype(BF16)
    b_r = vec(jnp.pad(jnp.concatenate([b_group_router, b_expert_router], axis=1),
                      ((0, 0), (0, pad))))
    p2 = p.reshape(depth, n, PLE_DIM)

    xf = x.reshape(n, d)
    for i in range(depth):
        q, k, v, u, vs, ga, gb = _inproj(i, xf, vec(norm_mix), w_in, qn2, kn2, vec(sgu_norm))
        oa = _attention(q, k, v, batch, seq)
        xf, h, route, fields, counts = _mix(i, xf, oa, u, vs, ga, gb, w_spatial, b_sp, w_up_a,
                                            w_up_b, w_out, vec(norm_ffn), w_r, b_r)
        dest, zrow, blk_e, nused, half, following = _slot_layout(fields, counts, cap)
        xs = _dispatch(dest, zrow, h, cap)
        ys = _experts(i, blk_e, nused, half, following, xs, w_expert_in, w_expert_out)
        xf = _ple(i, dest, xf, route, p2, ys, vec(norm_ple), w_ple_gate, w_ple_proj)
    return xf.reshape(batch, seq, d)
```

```python
import functools
import math

import jax
import jax.numpy as jnp
from jax import lax
from jax.experimental import pallas as pl
from jax.experimental.pallas import tpu as pltpu

F32 = jnp.float32
BF16 = jnp.bfloat16

D_MODEL = 1024
HEADS = 8
HEAD_DIM = 64
WIDTH = HEADS * HEAD_DIM
CHUNK = 128
IN_WIDTH = 5 * WIDTH + 2 * D_MODEL
N_GROUPS = 4
EXPERTS_PER_GROUP = 8
N_EXPERTS = N_GROUPS * EXPERTS_PER_GROUP
EXPERT_FF = 512
PLE_DIM = 256
EPS = 1e-6

LANES = 128
ROW_TILE = D_MODEL // (2 * LANES)
U32 = jnp.uint32
ROUTE_LANE0 = N_GROUPS
ROUTE_FIELDS = 8
TM_IN = 256
TM_MIX = 256
TM_DISP = 1024
TM_ROW = 256
MOE_ROWS = 256
TOP_ROWS = 64
LOG2E = 1.4426950408889634
DEAD_LOG2 = -105.0 * LOG2E
VMEM_LIMIT = 48 * 1024 * 1024


def _rms(x, gain):
    return x * lax.rsqrt(jnp.mean(x * x, axis=-1, keepdims=True) + EPS) * gain


def _store_row_tiles(ref, value, lead=()):
    rows = value.shape[0]
    bits = lambda v: lax.bitcast_convert_type(v.astype(BF16).astype(F32), U32)
    for c in range(ROW_TILE):
        low = bits(value[:, 2 * c * LANES:(2 * c + 1) * LANES]) >> 16
        high = bits(value[:, (2 * c + 1) * LANES:(2 * c + 2) * LANES]) & U32(0xFFFF0000)
        ref[lead + (pl.ds(c, rows, stride=ROW_TILE), slice(None))] = low | high


def _load_row_tiles(ref, rows, lead=()):
    chunks = []
    for c in range(ROW_TILE):
        words = ref[lead + (pl.ds(c, rows, stride=ROW_TILE), slice(None))]
        chunks.append(lax.bitcast_convert_type(words << 16, F32))
        chunks.append(lax.bitcast_convert_type(words & U32(0xFFFF0000), F32))
    return jnp.concatenate(chunks, axis=1)


def _inproj_kernel(x_ref, g_ref, wf_ref, qn_ref, kn_ref, sn_ref,
                   q_ref, k_ref, v_ref, u_ref, vs_ref, ga_ref, gb_ref, w_ref):
    @pl.when(pl.program_id(0) == 0)
    def _():
        w_ref[...] = wf_ref[...].astype(BF16)

    hb = _rms(x_ref[...], g_ref[...]).astype(BF16)

    def proj(lo, hi):
        return jnp.dot(hb, w_ref[:, lo:hi], preferred_element_type=F32)

    first = lax.broadcasted_iota(jnp.int32, (1, LANES), 1) < HEAD_DIM

    def head_norm(z, gain, scale):
        outs = []
        for c in range(WIDTH // LANES):
            zc = z[:, c * LANES:(c + 1) * LANES]
            sq = zc * zc
            sa = jnp.sum(jnp.where(first, sq, 0.0), axis=-1, keepdims=True)
            sb = jnp.sum(jnp.where(first, 0.0, sq), axis=-1, keepdims=True)
            ms = jnp.where(first, sa, sb) * (1.0 / HEAD_DIM)
            outs.append(zc * lax.rsqrt(ms + EPS) * (gain * scale))
        return jnp.concatenate(outs, axis=-1)

    w = WIDTH
    q_ref[...] = head_norm(proj(0, w), qn_ref[...], LOG2E / math.sqrt(HEAD_DIM)).astype(BF16)
    k_ref[...] = head_norm(proj(w, 2 * w), kn_ref[...], 1.0).astype(BF16)
    v_ref[...] = proj(2 * w, 3 * w).astype(BF16)
    u_ref[...] = jax.nn.gelu(proj(3 * w, 4 * w)).astype(BF16)
    vs_ref[...] = _rms(jax.nn.gelu(proj(4 * w, 5 * w)), sn_ref[...]).astype(BF16)
    ga_ref[...] = jax.nn.sigmoid(proj(5 * w, 5 * w + D_MODEL)).astype(BF16)
    gb_ref[...] = jax.nn.sigmoid(proj(5 * w + D_MODEL, IN_WIDTH)).astype(BF16)


def _inproj(layer, x, norm_mix, w_in, qn, kn, sn):
    n = x.shape[0]
    row = lambda width: pl.BlockSpec((TM_IN, width), lambda i: (i, 0))
    vec = lambda width: pl.BlockSpec((None, 1, width), lambda i: (layer, 0, 0))
    out = lambda width: jax.ShapeDtypeStruct((n, width), BF16)
    return pl.pallas_call(
        _inproj_kernel,
        grid=(n // TM_IN,),
        in_specs=[row(D_MODEL), vec(D_MODEL),
                  pl.BlockSpec((None, D_MODEL, IN_WIDTH), lambda i: (layer, 0, 0),
                               pipeline_mode=pl.Buffered(1)),
                  vec(LANES), vec(LANES), vec(WIDTH)],
        out_specs=[row(WIDTH)] * 5 + [row(D_MODEL)] * 2,
        out_shape=[out(WIDTH)] * 5 + [out(D_MODEL)] * 2,
        scratch_shapes=[pltpu.VMEM((D_MODEL, IN_WIDTH), BF16)],
        compiler_params=pltpu.CompilerParams(
            dimension_semantics=("arbitrary",), vmem_limit_bytes=VMEM_LIMIT),
        name="inproj",
    )(x, norm_mix, w_in, qn, kn, sn)


def _attn_kernel(q_ref, k_ref, v_ref, o_ref, carry_ref, acc_ref, z_ref):
    qb = pl.program_id(1)
    pairs = WIDTH // LANES
    first = lax.broadcasted_iota(jnp.int32, (CHUNK, LANES), 1) < HEAD_DIM
    first_top = lax.broadcasted_iota(jnp.int32, (TOP_ROWS, LANES), 1) < HEAD_DIM
    kr = lax.broadcasted_iota(jnp.int32, (LANES, 2 * LANES), 0)
    kc = lax.broadcasted_iota(jnp.int32, (LANES, 2 * LANES), 1)
    tail = jnp.where(jnp.logical_or(kr > kc, kc >= LANES), 1.0, 0.0).astype(BF16)
    sign = jnp.uint32(0x80000000)
    cols = [slice(p * LANES, (p + 1) * LANES) for p in range(pairs)]

    def stack(value, rows):
        if rows == CHUNK:
            return value
        return jnp.concatenate([value[:rows], value[CHUNK:CHUNK + rows]], axis=0)

    queries, causal = {}, {}
    for rows in (CHUNK, TOP_ROWS):
        lane = lax.broadcasted_iota(jnp.int32, (2 * rows, LANES), 1)
        row = lax.broadcasted_iota(jnp.int32, (2 * rows, LANES), 0)
        upper = row < rows
        own = jnp.logical_xor(lane < HEAD_DIM, jnp.logical_not(upper))
        causal[rows] = lane < jnp.where(upper, row, row - rows)
        queries[rows] = []
        for p in range(pairs):
            qp = q_ref[:rows, cols[p]]
            q2 = jnp.concatenate([qp, qp], axis=0)
            queries[rows].append(jnp.where(own, q2, jnp.zeros_like(q2)))
    full_row = lax.broadcasted_iota(jnp.int32, (2 * CHUNK, LANES), 0)
    is_top = jnp.where(full_row < CHUNK, full_row, full_row - CHUNK) < TOP_ROWS

    def scores(j, rows):
        start = pl.multiple_of(j * CHUNK, CHUNK)
        return [lax.dot_general(queries[rows][p], k_ref[pl.ds(start, CHUNK), cols[p]],
                                (((1,), (1,)), ((), ())), preferred_element_type=F32)
                for p in range(pairs)]

    def sweep(j, heights, diagonal=False, fresh=False, ahead=True, rest_only=False):
        blocks = range(len(heights))
        full = [b for b in blocks if heights[b] == CHUNK]
        top = [b for b in blocks if heights[b] == TOP_ROWS]
        starts = [pl.multiple_of((j - b) * CHUNK, CHUNK) for b in blocks]
        zs = []
        for b in blocks:
            if b > 0 or fresh or diagonal:
                zs.append(scores(j - b, heights[b]))
            else:
                zs.append([stack(z_ref[p], heights[b]) for p in range(pairs)])
        if ahead:
            for p, z in enumerate(scores(jnp.maximum(j - len(heights), 0), CHUNK)):
                z_ref[p] = z
        log_betas, splits = {}, {}
        for b in blocks:
            for p in range(pairs):
                z = zs[b][p]
                neg_abs = lax.bitcast_convert_type(
                    lax.bitcast_convert_type(z, jnp.uint32) | sign, F32)
                log1p = jnp.log(1.0 + jnp.exp2(neg_abs)) * LOG2E
                log_beta = jnp.minimum(z, 0.0) - log1p
                log_keep = log_beta - z
                if diagonal and b == 0:
                    log_keep = jnp.where(causal[heights[b]], log_keep, 0.0)
                if rest_only:
                    log_keep = jnp.where(is_top, 0.0, log_keep)
                log_betas[b, p] = log_beta
                splits[b, p] = log_keep.astype(BF16)
        sums = {key: jnp.dot(s, tail, preferred_element_type=F32) for key, s in splits.items()}
        entering, after_full, after_top = {}, [], []
        for p in range(pairs):
            carry = 0.0 if diagonal else carry_ref[p]
            for b in full:
                entering[b, p] = carry
                carry = carry + sums[b, p][:, LANES:]
            after_full.append(carry)
            carry = stack(carry, TOP_ROWS) if top else None
            for b in top:
                entering[b, p] = carry
                carry = carry + sums[b, p][:, LANES:]
            after_top.append(carry)
        alive = lambda c: jnp.max(c) > DEAD_LOG2
        worst = functools.reduce(jnp.maximum, after_full)
        live_rest = alive(jnp.where(is_top, -jnp.inf, worst)) if full else jnp.bool_(False)
        if top:
            live_top = alive(functools.reduce(jnp.maximum, after_top))
        else:
            live_top = alive(jnp.where(is_top, worst, -jnp.inf))
        outs = {}
        for b in blocks:
            for p in range(pairs):
                a = jnp.exp2(log_betas[b, p] + sums[b, p][:, :LANES] + entering[b, p])
                if diagonal and b == 0:
                    a = jnp.where(causal[heights[b]], a, 0.0)
                if rest_only:
                    a = jnp.where(is_top, 0.0, a)
                outs[b, p] = jnp.dot(a.astype(BF16), v_ref[pl.ds(starts[b], CHUNK), cols[p]],
                                     preferred_element_type=F32)
        for p in range(pairs):
            if full:
                carry_ref[p] = after_full[p]
            if top:
                carry_ref[p, 0:TOP_ROWS, :] = after_top[p][:TOP_ROWS]
                carry_ref[p, CHUNK:CHUNK + TOP_ROWS, :] = after_top[p][TOP_ROWS:]
        for p in range(pairs):
            if full:
                o2 = functools.reduce(jnp.add, [outs[b, p] for b in full])
                o = jnp.where(first, o2[:CHUNK], o2[CHUNK:])
                acc_ref[:, cols[p]] = o if diagonal else acc_ref[:, cols[p]] + o
            if top:
                o2 = functools.reduce(jnp.add, [outs[b, p] for b in top])
                o = jnp.where(first_top, o2[:TOP_ROWS], o2[TOP_ROWS:])
                acc_ref[0:TOP_ROWS, cols[p]] += o
        return live_top, live_rest

    def sweep_left(j, live_top, live_rest, height):
        def cond(state):
            j, live_top, live_rest = state
            return jnp.logical_and(j >= 0, jnp.logical_or(live_top, live_rest))

        def body(state):
            return (state[0] - 1,) + sweep(state[0], [height])

        lax.while_loop(cond, body, (j, live_top, live_rest))

    def start(n_full, n_top):
        live_top, live_rest = sweep(qb, [CHUNK] * n_full + [TOP_ROWS] * n_top, diagonal=True)
        if n_top == 0:
            return
        left = qb - n_full - n_top

        @pl.when(live_rest)
        def _():
            flags = sweep(qb - n_full, [CHUNK] * n_top, fresh=True, ahead=False, rest_only=True)
            sweep_left(left, *flags, CHUNK)

        @pl.when(jnp.logical_not(live_rest))
        def _():
            sweep_left(left, live_top, jnp.bool_(False), TOP_ROWS)

    for blocks_left, shape in enumerate(((1, 0), (2, 0))):
        pl.when(qb == blocks_left)(functools.partial(start, *shape))
    pl.when(qb >= 2)(functools.partial(start, 2, 1))

    o_ref[...] = acc_ref[...].astype(BF16)


def _attention(q, k, v, batch, seq):
    q3, k3, v3 = (t.reshape(batch, seq, WIDTH) for t in (q, k, v))
    blk = pl.BlockSpec((None, CHUNK, WIDTH), lambda b, qb: (b, qb, 0))
    full = pl.BlockSpec((None, seq, WIDTH), lambda b, qb: (b, 0, 0), pipeline_mode=pl.Buffered(1))
    o = pl.pallas_call(
        _attn_kernel,
        grid=(batch, seq // CHUNK),
        in_specs=[blk, full, full],
        out_specs=blk,
        out_shape=jax.ShapeDtypeStruct((batch, seq, WIDTH), BF16),
        scratch_shapes=[pltpu.VMEM((WIDTH // LANES, 2 * CHUNK, LANES), F32),
                        pltpu.VMEM((CHUNK, WIDTH), F32),
                        pltpu.VMEM((WIDTH // LANES, 2 * CHUNK, LANES), F32)],
        compiler_params=pltpu.CompilerParams(
            dimension_semantics=("arbitrary", "arbitrary"), vmem_limit_bytes=VMEM_LIMIT),
        name="attn",
    )(q3, k3, v3)
    return o.reshape(batch * seq, WIDTH)


def _choose_tile(hb, wr_ref, br_ref):
    logits = jnp.dot(hb, wr_ref[...], preferred_element_type=F32) + br_ref[...]
    lane_t = lax.broadcasted_iota(jnp.int32, (TM_MIX, LANES), 1)
    lanef = lane_t.astype(F32)
    neg = -jnp.inf
    far = float(LANES)

    def first_max(vals):
        m = jnp.max(vals, axis=-1, keepdims=True)
        idx = jnp.min(jnp.where(vals == m, lanef, far), axis=-1, keepdims=True)
        return m, idx

    gl = jnp.where(lane_t < N_GROUPS, logits, neg)
    gmax, grp = first_max(gl)
    grp_w = 1.0 / jnp.sum(jnp.exp(gl - gmax), axis=-1, keepdims=True)
    lo_lane = ROUTE_LANE0 + EXPERTS_PER_GROUP * grp
    in_group = jnp.logical_and(lanef >= lo_lane, lanef < lo_lane + EXPERTS_PER_GROUP)
    el = jnp.where(in_group, logits, neg)
    m1, i1 = first_max(el)
    m2, i2 = first_max(jnp.where(lanef == i1, neg, el))
    e21 = jnp.exp(m2 - m1)
    w1 = grp_w / (1.0 + e21)
    w2 = w1 * e21
    return i1, i2, w1, w2


def _rank_tile(choice, run_ref, valid):
    i1, i2, w1, w2 = choice
    lane_t = lax.broadcasted_iota(jnp.int32, (TM_MIX, LANES), 1)
    lanef = lane_t.astype(F32)
    onehot = jnp.logical_and(jnp.logical_or(lanef == i1, lanef == i2), valid)
    rt = lax.broadcasted_iota(jnp.int32, (TM_MIX, TM_MIX), 0)
    ct = lax.broadcasted_iota(jnp.int32, (TM_MIX, TM_MIX), 1)
    before = jnp.where(ct < rt, 1.0, 0.0).astype(BF16)
    prior = jnp.dot(before, jnp.where(onehot, 1.0, 0.0).astype(BF16),
                    preferred_element_type=F32) + run_ref[...]
    r1 = jnp.sum(jnp.where(lanef == i1, prior, 0.0), axis=-1, keepdims=True)
    r2 = jnp.sum(jnp.where(lanef == i2, prior, 0.0), axis=-1, keepdims=True)
    run_ref[...] += jnp.sum(jnp.where(onehot, 1.0, 0.0), axis=0, keepdims=True)

    fields = (i1 - ROUTE_LANE0, i2 - ROUTE_LANE0, w1, w2, r1, r2)
    route = jnp.zeros((TM_MIX, LANES), F32)
    for pos, val in enumerate(fields):
        route = jnp.where(lane_t == pos, val, route)
    return route


def _mix_kernel(x_ref, oa_ref, u_ref, vs_ref, ga_ref, gb_ref,
                wsp_ref, bsp_ref, wua_ref, wub_ref, wo_ref, gf_ref, wr_ref, br_ref,
                xo_ref, h_ref, route_ref, fields_ref, cnt_ref, run_ref, xprev_ref,
                wua_b, wub_b, wo_b):
    step = pl.program_id(0)

    @pl.when(step == 0)
    def _():
        run_ref[...] = jnp.zeros_like(run_ref)
        xprev_ref[...] = jnp.zeros_like(xprev_ref)
        wua_b[...] = wua_ref[...].astype(BF16)
        wub_b[...] = wub_ref[...].astype(BF16)
        wo_b[...] = wo_ref[...].astype(BF16)

    lane = lax.broadcasted_iota(jnp.int32, (CHUNK, LANES), 1)
    row = lax.broadcasted_iota(jnp.int32, (CHUNK, LANES), 0)
    first = lane < HEAD_DIM
    tril = lane <= row

    w_pairs = []
    for gp in range(WIDTH // LANES):
        w_pairs.append(jnp.concatenate(
            [jnp.where(tril, wsp_ref[g], 0.0).astype(BF16) for g in (2 * gp, 2 * gp + 1)],
            axis=1))
    ob_chunks = []
    for c in range(TM_MIX // CHUNK):
        rows = slice(c * CHUNK, (c + 1) * CHUNK)
        cols = []
        for gp in range(WIDTH // LANES):
            vpair = vs_ref[rows, gp * LANES:(gp + 1) * LANES]
            zero = jnp.zeros_like(vpair)
            stacked = jnp.concatenate([jnp.where(first, vpair, zero),
                                       jnp.where(first, zero, vpair)], axis=0)
            cols.append(jnp.dot(w_pairs[gp], stacked, preferred_element_type=F32))
        mixed = jnp.concatenate(cols, axis=-1) + bsp_ref[...]
        ob_chunks.append((u_ref[rows, :].astype(F32) * mixed).astype(BF16))
    ob = jnp.concatenate(ob_chunks, axis=0)

    h = _rms(xprev_ref[...], gf_ref[...])
    _store_row_tiles(h_ref, h)
    choice = _choose_tile(h.astype(BF16), wr_ref, br_ref)

    up_a = jnp.dot(oa_ref[...], wua_b[...], preferred_element_type=F32)
    up_b = jnp.dot(ob, wub_b[...], preferred_element_type=F32)
    merged = ga_ref[...].astype(F32) * up_a + gb_ref[...].astype(F32) * up_b
    x = x_ref[...] + jnp.dot(merged.astype(BF16), wo_b[...], preferred_element_type=F32)
    xo_ref[...] = x
    xprev_ref[...] = x

    route = _rank_tile(choice, run_ref, step > 0)
    route_ref[...] = route
    fields_ref[...] = route.T[:ROUTE_FIELDS, :]
    cnt_ref[...] = run_ref[...]


def _mix(layer, x, oa, u, vs, ga, gb, wsp, bsp, wua, wub, wo, gf, wr, br):
    n = x.shape[0]
    tiles = n // TM_MIX
    cur = lambda i: jnp.minimum(i, tiles - 1)
    prev = lambda i: jnp.maximum(i - 1, 0)
    row = lambda width: pl.BlockSpec((TM_MIX, width), lambda i: (cur(i), 0))
    lay = lambda *shape: pl.BlockSpec((None,) + shape, lambda i: (layer,) + (0,) * len(shape))
    once = lambda *shape: pl.BlockSpec((None,) + shape, lambda i: (layer,) + (0,) * len(shape),
                                       pipeline_mode=pl.Buffered(1))
    return pl.pallas_call(
        _mix_kernel,
        grid=(tiles + 1,),
        in_specs=[row(D_MODEL), row(WIDTH), row(WIDTH), row(WIDTH), row(D_MODEL), row(D_MODEL),
                  lay(HEADS, CHUNK, CHUNK), lay(CHUNK, WIDTH), once(WIDTH, D_MODEL),
                  once(WIDTH, D_MODEL), once(D_MODEL, D_MODEL), lay(1, D_MODEL),
                  lay(D_MODEL, LANES), lay(1, LANES)],
        out_specs=[row(D_MODEL),
                   pl.BlockSpec((TM_MIX * ROW_TILE, LANES), lambda i: (prev(i), 0)),
                   pl.BlockSpec((TM_MIX, LANES), lambda i: (prev(i), 0)),
                   pl.BlockSpec((ROUTE_FIELDS, TM_MIX), lambda i: (0, prev(i))),
                   pl.BlockSpec((1, LANES), lambda i: (0, 0))],
        out_shape=[jax.ShapeDtypeStruct((n, D_MODEL), F32),
                   jax.ShapeDtypeStruct((n * ROW_TILE, LANES), U32),
                   jax.ShapeDtypeStruct((n, LANES), F32),
                   jax.ShapeDtypeStruct((ROUTE_FIELDS, n), F32),
                   jax.ShapeDtypeStruct((1, LANES), F32)],
        scratch_shapes=[pltpu.VMEM((1, LANES), F32), pltpu.VMEM((TM_MIX, D_MODEL), F32),
                        pltpu.VMEM((WIDTH, D_MODEL), BF16), pltpu.VMEM((WIDTH, D_MODEL), BF16),
                        pltpu.VMEM((D_MODEL, D_MODEL), BF16)],
        compiler_params=pltpu.CompilerParams(
            dimension_semantics=("arbitrary",), vmem_limit_bytes=VMEM_LIMIT),
        name="mix",
    )(x, oa, u, vs, ga, gb, wsp, bsp, wua, wub, wo, gf, wr, br)


def _dispatch_kernel(dest_ref, zrow_ref, h_ref, xs_ref, zero_ref, sem, zsem, tsem):
    i = pl.program_id(0)
    block_rows = MOE_ROWS * ROW_TILE
    nblk = xs_ref.shape[0] // block_rows
    nused = zrow_ref[N_EXPERTS]

    def zero_copy(first_row, zero_sem):
        return pltpu.make_async_copy(zero_ref, xs_ref.at[pl.ds(first_row, block_rows), :],
                                     zero_sem)

    def tail_copy(j):
        return zero_copy((nused + j) * block_rows, tsem)

    @pl.when(i == 0)
    def _():
        zero_ref[...] = jnp.zeros_like(zero_ref)
        for j in range(N_EXPERTS):
            pl.when(nused + j < nblk)(lambda j=j: tail_copy(j).start())
        for e in range(N_EXPERTS):
            pl.when(zrow_ref[e] >= 0)(lambda e=e: zero_copy(zrow_ref[e], zsem).start())
        for e in range(N_EXPERTS):
            pl.when(zrow_ref[e] >= 0)(lambda e=e: zero_copy(zrow_ref[e], zsem).wait())

    n = dest_ref.shape[0] // 2
    base = i * TM_DISP
    for t in range(TM_DISP):
        for k in range(2):
            d = dest_ref[k * n + base + t]
            pltpu.make_async_copy(h_ref.at[pl.ds(t * ROW_TILE, ROW_TILE), :],
                                  xs_ref.at[pl.ds(d * ROW_TILE, ROW_TILE), :],
                                  sem).start(priority=k)
    for k in range(2):
        pltpu.make_async_copy(h_ref, xs_ref.at[pl.ds(0, TM_DISP * ROW_TILE), :], sem).wait()

    @pl.when(i == pl.num_programs(0) - 1)
    def _():
        for j in range(N_EXPERTS):
            pl.when(nused + j < nblk)(lambda j=j: tail_copy(j).wait())


def _dispatch(dest, zrow, h, cap):
    n = h.shape[0] // ROW_TILE
    return pl.pallas_call(
        _dispatch_kernel,
        grid_spec=pltpu.PrefetchScalarGridSpec(
            num_scalar_prefetch=2,
            grid=(n // TM_DISP,),
            in_specs=[pl.BlockSpec((TM_DISP * ROW_TILE, LANES), lambda i, dest, zrow: (i, 0))],
            out_specs=pl.BlockSpec(memory_space=pl.ANY),
            scratch_shapes=[pltpu.VMEM((MOE_ROWS * ROW_TILE, LANES), U32),
                            pltpu.SemaphoreType.DMA(()), pltpu.SemaphoreType.DMA(()),
                            pltpu.SemaphoreType.DMA(())],
        ),
        out_shape=jax.ShapeDtypeStruct((cap * ROW_TILE, LANES), U32),
        compiler_params=pltpu.CompilerParams(
            dimension_semantics=("arbitrary",), vmem_limit_bytes=VMEM_LIMIT),
        name="dispatch",
    )(dest, zrow, h)


def _expert_kernel(layer, blk_e_ref, nused_ref, half_ref, next_ref, xs_ref, wi_hbm, wo_hbm,
                   ys_ref, wi_f, wo_f, wi_b, wo_b, sem):
    b = pl.program_id(0)
    used = b < nused_ref[0]
    expert = blk_e_ref[b]

    def fetch(e, half):
        return (pltpu.make_async_copy(wi_hbm.at[layer, e], wi_f.at[half], sem.at[half, 0]),
                pltpu.make_async_copy(wo_hbm.at[layer, e], wo_f.at[half], sem.at[half, 1]))

    @pl.when(b == 0)
    def _():
        for copy in fetch(expert, half_ref[expert]):
            copy.start()

    @pl.when(jnp.logical_not(used))
    def _():
        ys_ref[...] = jnp.zeros_like(ys_ref)

    new_expert = jnp.logical_or(b == 0, expert != blk_e_ref[jnp.maximum(b - 1, 0)])

    @pl.when(jnp.logical_and(used, new_expert))
    def _():
        half = half_ref[expert]
        for copy in fetch(expert, half):
            copy.wait()
        following = next_ref[expert]

        @pl.when(following >= 0)
        def _():
            for copy in fetch(following, 1 - half):
                copy.start()

        wi_b[...] = wi_f[half].astype(BF16)
        wo_b[...] = wo_f[half].astype(BF16)

    @pl.when(used)
    def _():
        xs = _load_row_tiles(xs_ref, MOE_ROWS).astype(BF16)
        gu = jnp.dot(xs, wi_b[...], preferred_element_type=F32)
        act = jax.nn.silu(gu[:, :EXPERT_FF]) * gu[:, EXPERT_FF:]
        y = jnp.dot(act.astype(BF16), wo_b[...], preferred_element_type=F32)
        _store_row_tiles(ys_ref, y)


def _experts(layer, blk_e, nused, half, following, xs, w_e_in, w_e_out):
    cap = xs.shape[0] // ROW_TILE
    rows = MOE_ROWS * ROW_TILE
    return pl.pallas_call(
        functools.partial(_expert_kernel, layer),
        grid_spec=pltpu.PrefetchScalarGridSpec(
            num_scalar_prefetch=4,
            grid=(cap // MOE_ROWS,),
            in_specs=[pl.BlockSpec((rows, LANES),
                                   lambda b, blk_e, nused, *_: (jnp.minimum(b, nused[0] - 1), 0)),
                      pl.BlockSpec(memory_space=pl.ANY),
                      pl.BlockSpec(memory_space=pl.ANY)],
            out_specs=pl.BlockSpec((rows, LANES), lambda b, *_: (b, 0)),
            scratch_shapes=[pltpu.VMEM((2, D_MODEL, 2 * EXPERT_FF), F32),
                            pltpu.VMEM((2, EXPERT_FF, D_MODEL), F32),
                            pltpu.VMEM((D_MODEL, 2 * EXPERT_FF), BF16),
                            pltpu.VMEM((EXPERT_FF, D_MODEL), BF16),
                            pltpu.SemaphoreType.DMA((2, 2))],
        ),
        out_shape=jax.ShapeDtypeStruct((cap * ROW_TILE, LANES), U32),
        compiler_params=pltpu.CompilerParams(
            dimension_semantics=("arbitrary",), vmem_limit_bytes=VMEM_LIMIT),
        name="experts",
    )(blk_e, nused, half, following, xs, w_e_in, w_e_out)


def _ple_kernel(dest_ref, x_ref, route_ref, p_ref, ys_ref, gp_ref, wg_ref, wp_ref,
                xo_ref, y00, y01, y10, y11, wg_b, wp_b, sem):
    ybuf = ((y00, y01), (y10, y11))
    j = pl.program_id(0)

    @pl.when(j == 0)
    def _():
        wg_b[...] = wg_ref[...].astype(BF16)
        wp_b[...] = wp_ref[...].astype(BF16)
    last_step = pl.num_programs(0) - 1

    def start_gather(tile, half):
        n = dest_ref.shape[0] // 2
        base = tile * TM_ROW
        for t in range(TM_ROW):
            for k in range(2):
                d = dest_ref[k * n + base + t]
                pltpu.make_async_copy(ys_ref.at[pl.ds(d * ROW_TILE, ROW_TILE), :],
                                      ybuf[half][k].at[pl.ds(t * ROW_TILE, ROW_TILE), :],
                                      sem.at[half]).start(priority=k)

    def wait_gather(half):
        for k in range(2):
            pltpu.make_async_copy(ys_ref.at[pl.ds(0, TM_ROW * ROW_TILE), :],
                                  ybuf[half][k], sem.at[half]).wait()

    def combine(half):
        rows = slice(half * TM_ROW, (half + 1) * TM_ROW)
        pe = jnp.dot(p_ref[rows, :].astype(BF16), wp_b[...], preferred_element_type=F32)
        wait_gather(half)
        route = route_ref[rows, :]
        y0 = _load_row_tiles(ybuf[half][0], TM_ROW)
        y1 = _load_row_tiles(ybuf[half][1], TM_ROW)
        x = x_ref[rows, :] + route[:, 2:3] * y0 + route[:, 3:4] * y1
        gate = jax.nn.sigmoid(jnp.dot(_rms(x, gp_ref[...]).astype(BF16), wg_b[...],
                                      preferred_element_type=F32))
        xo_ref[rows, :] = x + gate * pe

    @pl.when(j == 0)
    def _():
        start_gather(0, 0)

    start_gather(2 * j + 1, 1)
    combine(0)
    start_gather(jnp.minimum(2 * j + 2, 2 * last_step + 1), 0)
    combine(1)

    @pl.when(j == last_step)
    def _():
        wait_gather(0)


def _ple(layer, dest, x, route, p, ys, gp, wg, wp):
    n = x.shape[0]
    row = lambda width: pl.BlockSpec((2 * TM_ROW, width), lambda j, dest: (j, 0))
    lay = lambda *shape: pl.BlockSpec((None,) + shape,
                                      lambda j, dest: (layer,) + (0,) * len(shape))
    once = lambda *shape: pl.BlockSpec((None,) + shape,
                                       lambda j, dest: (layer,) + (0,) * len(shape),
                                       pipeline_mode=pl.Buffered(1))
    return pl.pallas_call(
        _ple_kernel,
        grid_spec=pltpu.PrefetchScalarGridSpec(
            num_scalar_prefetch=1,
            grid=(n // (2 * TM_ROW),),
            in_specs=[row(D_MODEL), row(LANES),
                      pl.BlockSpec((None, 2 * TM_ROW, PLE_DIM), lambda j, dest: (layer, j, 0)),
                      pl.BlockSpec(memory_space=pl.ANY),
                      lay(1, D_MODEL), once(D_MODEL, D_MODEL), once(PLE_DIM, D_MODEL)],
            out_specs=row(D_MODEL),
            scratch_shapes=[pltpu.VMEM((TM_ROW * ROW_TILE, LANES), U32)] * 4
            + [pltpu.VMEM((D_MODEL, D_MODEL), BF16), pltpu.VMEM((PLE_DIM, D_MODEL), BF16),
               pltpu.SemaphoreType.DMA((2,))],
        ),
        out_shape=jax.ShapeDtypeStruct((n, D_MODEL), F32),
        compiler_params=pltpu.CompilerParams(
            dimension_semantics=("arbitrary",), vmem_limit_bytes=VMEM_LIMIT),
        name="ple",
    )(dest, x, route, p, ys, gp, wg, wp)


def _slot_layout(fields, counts, cap):
    cnt = counts[0, ROUTE_LANE0:ROUTE_LANE0 + N_EXPERTS].astype(jnp.int32)
    padded = (cnt + MOE_ROWS - 1) // MOE_ROWS * MOE_ROWS
    pend = jnp.cumsum(padded)
    poff = pend - padded
    ids = jnp.arange(N_EXPERTS, dtype=jnp.int32)

    def slots(k):
        eid = fields[k].astype(jnp.int32)
        rank = fields[4 + k].astype(jnp.int32)
        return jnp.sum(jnp.where(eid[None, :] == ids[:, None], poff[:, None], 0), axis=0) + rank

    dest = jnp.concatenate([slots(0), slots(1)]).astype(jnp.int32)
    blk_start = jnp.arange(cap // MOE_ROWS, dtype=jnp.int32) * MOE_ROWS
    blk_e = jnp.minimum(jnp.sum(pend[None, :] <= blk_start[:, None], axis=1), N_EXPERTS - 1)
    nused = (pend[-1:] // MOE_ROWS).astype(jnp.int32)
    zrow = jnp.where(cnt > 0, (pend - MOE_ROWS) * ROW_TILE, -1).astype(jnp.int32)
    nonempty = cnt > 0
    half = ((jnp.cumsum(nonempty) - 1) % 2).astype(jnp.int32)
    later = jnp.logical_and(nonempty[None, :], ids[None, :] > ids[:, None])
    following = jnp.min(jnp.where(later, ids[None, :], N_EXPERTS), axis=1)
    following = jnp.where(following < N_EXPERTS, following, -1).astype(jnp.int32)
    return (dest, jnp.concatenate([zrow, nused]), blk_e.astype(jnp.int32), nused, half,
            following)


def kernel(x, p, norm_mix, w_in, q_norm, k_norm, sgu_norm, w_spatial, b_spatial, w_up_a, w_up_b,
           w_out, norm_ffn, w_group_router, b_group_router, w_expert_router, b_expert_router,
           w_expert_in, w_expert_out, norm_ple, w_ple_gate, w_ple_proj):
    batch, seq, d = x.shape
    depth = w_in.shape[0]
    n = batch * seq
    cap = 2 * n + N_EXPERTS * MOE_ROWS

    vec = lambda a: a[:, None, :]
    qn2, kn2 = vec(jnp.tile(q_norm, (1, 2))), vec(jnp.tile(k_norm, (1, 2)))
    b_sp = jnp.repeat(jnp.swapaxes(b_spatial, 1, 2), HEAD_DIM, axis=2)
    pad = LANES - N_GROUPS - N_EXPERTS
    w_r = jnp.pad(jnp.concatenate([w_group_router, w_expert_router], axis=2),
                  ((0, 0), (0, 0), (0, pad))).astype(BF16)
    b_r = vec(jnp.pad(jnp.concatenate([b_group_router, b_expert_router], axis=1),
                      ((0, 0), (0, pad))))
    p2 = p.reshape(depth, n, PLE_DIM)

    xf = x.reshape(n, d)
    for i in range(depth):
        q, k, v, u, vs, ga, gb = _inproj(i, xf, vec(norm_mix), w_in, qn2, kn2, vec(sgu_norm))
        oa = _attention(q, k, v, batch, seq)
        xf, h, route, fields, counts = _mix(i, xf, oa, u, vs, ga, gb, w_spatial, b_sp, w_up_a,
                                            w_up_b, w_out, vec(norm_ffn), w_r, b_r)
        dest, zrow, blk_e, nused, half, following = _slot_layout(fields, counts, cap)
        xs = _dispatch(dest, zrow, h, cap)
        ys = _experts(i, blk_e, nused, half, following, xs, w_expert_in, w_expert_out)
        xf = _ple(i, dest, xf, route, p2, ys, vec(norm_ple), w_ple_gate, w_ple_proj)
    return xf.reshape(batch, seq, d)
```

```python
import functools
import math

import jax
import jax.numpy as jnp
from jax import lax
from jax.experimental import pallas as pl
from jax.experimental.pallas import tpu as pltpu

F32 = jnp.float32
BF16 = jnp.bfloat16

D_MODEL = 1024
HEADS = 8
HEAD_DIM = 64
WIDTH = HEADS * HEAD_DIM
CHUNK = 128
IN_WIDTH = 5 * WIDTH + 2 * D_MODEL
N_GROUPS = 4
EXPERTS_PER_GROUP = 8
N_EXPERTS = N_GROUPS * EXPERTS_PER_GROUP
EXPERT_FF = 512
PLE_DIM = 256
EPS = 1e-6

LANES = 128
ROW_TILE = D_MODEL // (2 * LANES)
U32 = jnp.uint32
ROUTE_LANE0 = N_GROUPS
ROUTE_FIELDS = 8
TM_IN = 256
TM_MIX = 512
TM_DISP = 1024
TM_ROW = 256
MOE_ROWS = 256
TOP_ROWS = 64
LOG2E = 1.4426950408889634
DEAD_LOG2 = -105.0 * LOG2E
VMEM_LIMIT = 48 * 1024 * 1024


def _rms(x, gain):
    return x * lax.rsqrt(jnp.mean(x * x, axis=-1, keepdims=True) + EPS) * gain


def _store_row_tiles(ref, value, lead=()):
    rows = value.shape[0]
    bits = lambda v: lax.bitcast_convert_type(v.astype(BF16).astype(F32), U32)
    for c in range(ROW_TILE):
        low = bits(value[:, 2 * c * LANES:(2 * c + 1) * LANES]) >> 16
        high = bits(value[:, (2 * c + 1) * LANES:(2 * c + 2) * LANES]) & U32(0xFFFF0000)
        ref[lead + (pl.ds(c, rows, stride=ROW_TILE), slice(None))] = low | high


def _load_row_tiles(ref, rows, lead=()):
    chunks = []
    for c in range(ROW_TILE):
        words = ref[lead + (pl.ds(c, rows, stride=ROW_TILE), slice(None))]
        chunks.append(lax.bitcast_convert_type(words << 16, F32))
        chunks.append(lax.bitcast_convert_type(words & U32(0xFFFF0000), F32))
    return jnp.concatenate(chunks, axis=1)


def _inproj_kernel(x_ref, g_ref, wf_ref, qn_ref, kn_ref, sn_ref,
                   q_ref, k_ref, v_ref, u_ref, vs_ref, ga_ref, gb_ref, w_ref):
    @pl.when(pl.program_id(0) == 0)
    def _():
        w_ref[...] = wf_ref[...].astype(BF16)

    hb = _rms(x_ref[...], g_ref[...]).astype(BF16)

    def proj(lo, hi):
        return jnp.dot(hb, w_ref[:, lo:hi], preferred_element_type=F32)

    first = lax.broadcasted_iota(jnp.int32, (1, LANES), 1) < HEAD_DIM

    def head_norm(z, gain, scale):
        outs = []
        for c in range(WIDTH // LANES):
            zc = z[:, c * LANES:(c + 1) * LANES]
            sq = zc * zc
            sa = jnp.sum(jnp.where(first, sq, 0.0), axis=-1, keepdims=True)
            sb = jnp.sum(jnp.where(first, 0.0, sq), axis=-1, keepdims=True)
            ms = jnp.where(first, sa, sb) * (1.0 / HEAD_DIM)
            outs.append(zc * lax.rsqrt(ms + EPS) * (gain * scale))
        return jnp.concatenate(outs, axis=-1)

    w = WIDTH
    q_ref[...] = head_norm(proj(0, w), qn_ref[...], LOG2E / math.sqrt(HEAD_DIM)).astype(BF16)
    k_ref[...] = head_norm(proj(w, 2 * w), kn_ref[...], 1.0).astype(BF16)
    v_ref[...] = proj(2 * w, 3 * w).astype(BF16)
    u_ref[...] = jax.nn.gelu(proj(3 * w, 4 * w)).astype(BF16)
    vs_ref[...] = _rms(jax.nn.gelu(proj(4 * w, 5 * w)), sn_ref[...]).astype(BF16)
    ga_ref[...] = jax.nn.sigmoid(proj(5 * w, 5 * w + D_MODEL)).astype(BF16)
    gb_ref[...] = jax.nn.sigmoid(proj(5 * w + D_MODEL, IN_WIDTH)).astype(BF16)


def _inproj(layer, x, norm_mix, w_in, qn, kn, sn):
    n = x.shape[0]
    row = lambda width: pl.BlockSpec((TM_IN, width), lambda i: (i, 0))
    vec = lambda width: pl.BlockSpec((None, 1, width), lambda i: (layer, 0, 0))
    out = lambda width: jax.ShapeDtypeStruct((n, width), BF16)
    return pl.pallas_call(
        _inproj_kernel,
        grid=(n // TM_IN,),
        in_specs=[row(D_MODEL), vec(D_MODEL),
                  pl.BlockSpec((None, D_MODEL, IN_WIDTH), lambda i: (layer, 0, 0),
                               pipeline_mode=pl.Buffered(1)),
                  vec(LANES), vec(LANES), vec(WIDTH)],
        out_specs=[row(WIDTH)] * 5 + [row(D_MODEL)] * 2,
        out_shape=[out(WIDTH)] * 5 + [out(D_MODEL)] * 2,
        scratch_shapes=[pltpu.VMEM((D_MODEL, IN_WIDTH), BF16)],
        compiler_params=pltpu.CompilerParams(
            dimension_semantics=("arbitrary",), vmem_limit_bytes=VMEM_LIMIT),
        name="inproj",
    )(x, norm_mix, w_in, qn, kn, sn)


def _attn_kernel(q_ref, k_ref, v_ref, o_ref, carry_ref, acc_ref, z_ref):
    qb = pl.program_id(1)
    pairs = WIDTH // LANES
    first = lax.broadcasted_iota(jnp.int32, (CHUNK, LANES), 1) < HEAD_DIM
    first_top = lax.broadcasted_iota(jnp.int32, (TOP_ROWS, LANES), 1) < HEAD_DIM
    kr = lax.broadcasted_iota(jnp.int32, (LANES, 2 * LANES), 0)
    kc = lax.broadcasted_iota(jnp.int32, (LANES, 2 * LANES), 1)
    tail = jnp.where(jnp.logical_or(kr > kc, kc >= LANES), 1.0, 0.0).astype(BF16)
    sign = jnp.uint32(0x80000000)
    cols = [slice(p * LANES, (p + 1) * LANES) for p in range(pairs)]

    def stack(value, rows):
        if rows == CHUNK:
            return value
        return jnp.concatenate([value[:rows], value[CHUNK:CHUNK + rows]], axis=0)

    queries, causal = {}, {}
    for rows in (CHUNK, TOP_ROWS):
        lane = lax.broadcasted_iota(jnp.int32, (2 * rows, LANES), 1)
        row = lax.broadcasted_iota(jnp.int32, (2 * rows, LANES), 0)
        upper = row < rows
        own = jnp.logical_xor(lane < HEAD_DIM, jnp.logical_not(upper))
        causal[rows] = lane < jnp.where(upper, row, row - rows)
        queries[rows] = []
        for p in range(pairs):
            qp = q_ref[:rows, cols[p]]
            q2 = jnp.concatenate([qp, qp], axis=0)
            queries[rows].append(jnp.where(own, q2, jnp.zeros_like(q2)))
    full_row = lax.broadcasted_iota(jnp.int32, (2 * CHUNK, LANES), 0)
    is_top = jnp.where(full_row < CHUNK, full_row, full_row - CHUNK) < TOP_ROWS

    def scores(j, rows):
        start = pl.multiple_of(j * CHUNK, CHUNK)
        return [lax.dot_general(queries[rows][p], k_ref[pl.ds(start, CHUNK), cols[p]],
                                (((1,), (1,)), ((), ())), preferred_element_type=F32)
                for p in range(pairs)]

    def sweep(j, heights, diagonal=False, fresh=False, ahead=True, rest_only=False):
        blocks = range(len(heights))
        full = [b for b in blocks if heights[b] == CHUNK]
        top = [b for b in blocks if heights[b] == TOP_ROWS]
        starts = [pl.multiple_of((j - b) * CHUNK, CHUNK) for b in blocks]
        zs = []
        for b in blocks:
            if b > 0 or fresh or diagonal:
                zs.append(scores(j - b, heights[b]))
            else:
                zs.append([stack(z_ref[p], heights[b]) for p in range(pairs)])
        if ahead:
            for p, z in enumerate(scores(jnp.maximum(j - len(heights), 0), CHUNK)):
                z_ref[p] = z
        log_betas, splits = {}, {}
        for b in blocks:
            for p in range(pairs):
                z = zs[b][p]
                neg_abs = lax.bitcast_convert_type(
                    lax.bitcast_convert_type(z, jnp.uint32) | sign, F32)
                log1p = jnp.log(1.0 + jnp.exp2(neg_abs)) * LOG2E
                log_beta = jnp.minimum(z, 0.0) - log1p
                log_keep = log_beta - z
                if diagonal and b == 0:
                    log_keep = jnp.where(causal[heights[b]], log_keep, 0.0)
                if rest_only:
                    log_keep = jnp.where(is_top, 0.0, log_keep)
                log_betas[b, p] = log_beta
                splits[b, p] = log_keep.astype(BF16)
        sums = {key: jnp.dot(s, tail, preferred_element_type=F32) for key, s in splits.items()}
        entering, after_full, after_top = {}, [], []
        for p in range(pairs):
            carry = 0.0 if diagonal else carry_ref[p]
            for b in full:
                entering[b, p] = carry
                carry = carry + sums[b, p][:, LANES:]
            after_full.append(carry)
            carry = stack(carry, TOP_ROWS) if top else None
            for b in top:
                entering[b, p] = carry
                carry = carry + sums[b, p][:, LANES:]
            after_top.append(carry)
        alive = lambda c: jnp.max(c) > DEAD_LOG2
        worst = functools.reduce(jnp.maximum, after_full)
        live_rest = alive(jnp.where(is_top, -jnp.inf, worst)) if full else jnp.bool_(False)
        if top:
            live_top = alive(functools.reduce(jnp.maximum, after_top))
        else:
            live_top = alive(jnp.where(is_top, worst, -jnp.inf))
        outs = {}
        for b in blocks:
            for p in range(pairs):
                a = jnp.exp2(log_betas[b, p] + sums[b, p][:, :LANES] + entering[b, p])
                if diagonal and b == 0:
                    a = jnp.where(causal[heights[b]], a, 0.0)
                if rest_only:
                    a = jnp.where(is_top, 0.0, a)
                outs[b, p] = jnp.dot(a.astype(BF16), v_ref[pl.ds(starts[b], CHUNK), cols[p]],
                                     preferred_element_type=F32)
        for p in range(pairs):
            if full:
                carry_ref[p] = after_full[p]
            if top:
                carry_ref[p, 0:TOP_ROWS, :] = after_top[p][:TOP_ROWS]
                carry_ref[p, CHUNK:CHUNK + TOP_ROWS, :] = after_top[p][TOP_ROWS:]
        for p in range(pairs):
            if full:
                o2 = functools.reduce(jnp.add, [outs[b, p] for b in full])
                o = jnp.where(first, o2[:CHUNK], o2[CHUNK:])
                acc_ref[:, cols[p]] = o if diagonal else acc_ref[:, cols[p]] + o
            if top:
                o2 = functools.reduce(jnp.add, [outs[b, p] for b in top])
                o = jnp.where(first_top, o2[:TOP_ROWS], o2[TOP_ROWS:])
                acc_ref[0:TOP_ROWS, cols[p]] += o
        return live_top, live_rest

    def sweep_left(j, live_top, live_rest, height):
        def cond(state):
            j, live_top, live_rest = state
            return jnp.logical_and(j >= 0, jnp.logical_or(live_top, live_rest))

        def body(state):
            return (state[0] - 1,) + sweep(state[0], [height])

        lax.while_loop(cond, body, (j, live_top, live_rest))

    def start(n_full, n_top):
        live_top, live_rest = sweep(qb, [CHUNK] * n_full + [TOP_ROWS] * n_top, diagonal=True)
        if n_top == 0:
            return
        left = qb - n_full - n_top

        @pl.when(live_rest)
        def _():
            flags = sweep(qb - n_full, [CHUNK] * n_top, fresh=True, ahead=False, rest_only=True)
            sweep_left(left, *flags, CHUNK)

        @pl.when(jnp.logical_not(live_rest))
        def _():
            sweep_left(left, live_top, jnp.bool_(False), TOP_ROWS)

    for blocks_left, shape in enumerate(((1, 0), (2, 0))):
        pl.when(qb == blocks_left)(functools.partial(start, *shape))
    pl.when(qb >= 2)(functools.partial(start, 2, 1))

    o_ref[...] = acc_ref[...].astype(BF16)


def _attention(q, k, v, batch, seq):
    q3, k3, v3 = (t.reshape(batch, seq, WIDTH) for t in (q, k, v))
    blk = pl.BlockSpec((None, CHUNK, WIDTH), lambda b, qb: (b, qb, 0))
    full = pl.BlockSpec((None, seq, WIDTH), lambda b, qb: (b, 0, 0), pipeline_mode=pl.Buffered(1))
    o = pl.pallas_call(
        _attn_kernel,
        grid=(batch, seq // CHUNK),
        in_specs=[blk, full, full],
        out_specs=blk,
        out_shape=jax.ShapeDtypeStruct((batch, seq, WIDTH), BF16),
        scratch_shapes=[pltpu.VMEM((WIDTH // LANES, 2 * CHUNK, LANES), F32),
                        pltpu.VMEM((CHUNK, WIDTH), F32),
                        pltpu.VMEM((WIDTH // LANES, 2 * CHUNK, LANES), F32)],
        compiler_params=pltpu.CompilerParams(
            dimension_semantics=("arbitrary", "arbitrary"), vmem_limit_bytes=VMEM_LIMIT),
        name="attn",
    )(q3, k3, v3)
    return o.reshape(batch * seq, WIDTH)


def _choose_tile(hb, wr_ref, br_ref):
    logits = jnp.dot(hb, wr_ref[...], preferred_element_type=F32) + br_ref[...]
    lane_t = lax.broadcasted_iota(jnp.int32, (TM_MIX, LANES), 1)
    lanef = lane_t.astype(F32)
    neg = -jnp.inf
    far = float(LANES)

    def first_max(vals):
        m = jnp.max(vals, axis=-1, keepdims=True)
        idx = jnp.min(jnp.where(vals == m, lanef, far), axis=-1, keepdims=True)
        return m, idx

    gl = jnp.where(lane_t < N_GROUPS, logits, neg)
    gmax, grp = first_max(gl)
    grp_w = 1.0 / jnp.sum(jnp.exp(gl - gmax), axis=-1, keepdims=True)
    lo_lane = ROUTE_LANE0 + EXPERTS_PER_GROUP * grp
    in_group = jnp.logical_and(lanef >= lo_lane, lanef < lo_lane + EXPERTS_PER_GROUP)
    el = jnp.where(in_group, logits, neg)
    m1, i1 = first_max(el)
    m2, i2 = first_max(jnp.where(lanef == i1, neg, el))
    e21 = jnp.exp(m2 - m1)
    w1 = grp_w / (1.0 + e21)
    w2 = w1 * e21
    return i1, i2, w1, w2


def _rank_tile(choice, run_ref, valid):
    i1, i2, w1, w2 = choice
    lane_t = lax.broadcasted_iota(jnp.int32, (TM_MIX, LANES), 1)
    lanef = lane_t.astype(F32)
    onehot = jnp.logical_and(jnp.logical_or(lanef == i1, lanef == i2), valid)
    rt = lax.broadcasted_iota(jnp.int32, (TM_MIX, TM_MIX), 0)
    ct = lax.broadcasted_iota(jnp.int32, (TM_MIX, TM_MIX), 1)
    before = jnp.where(ct < rt, 1.0, 0.0).astype(BF16)
    prior = jnp.dot(before, jnp.where(onehot, 1.0, 0.0).astype(BF16),
                    preferred_element_type=F32) + run_ref[...]
    r1 = jnp.sum(jnp.where(lanef == i1, prior, 0.0), axis=-1, keepdims=True)
    r2 = jnp.sum(jnp.where(lanef == i2, prior, 0.0), axis=-1, keepdims=True)
    run_ref[...] += jnp.sum(jnp.where(onehot, 1.0, 0.0), axis=0, keepdims=True)

    fields = (i1 - ROUTE_LANE0, i2 - ROUTE_LANE0, w1, w2, r1, r2)
    route = jnp.zeros((TM_MIX, LANES), F32)
    for pos, val in enumerate(fields):
        route = jnp.where(lane_t == pos, val, route)
    return route


def _mix_kernel(x_ref, oa_ref, u_ref, vs_ref, ga_ref, gb_ref,
                wsp_ref, bsp_ref, wua_ref, wub_ref, wo_ref, gf_ref, wr_ref, br_ref,
                xo_ref, h_ref, route_ref, fields_ref, cnt_ref, run_ref, xprev_ref,
                wua_b, wub_b, wo_b):
    step = pl.program_id(0)

    @pl.when(step == 0)
    def _():
        run_ref[...] = jnp.zeros_like(run_ref)
        xprev_ref[...] = jnp.zeros_like(xprev_ref)
        wua_b[...] = wua_ref[...].astype(BF16)
        wub_b[...] = wub_ref[...].astype(BF16)
        wo_b[...] = wo_ref[...].astype(BF16)

    lane = lax.broadcasted_iota(jnp.int32, (CHUNK, LANES), 1)
    row = lax.broadcasted_iota(jnp.int32, (CHUNK, LANES), 0)
    first = lane < HEAD_DIM
    tril = lane <= row

    w_pairs = []
    for gp in range(WIDTH // LANES):
        w_pairs.append(jnp.concatenate(
            [jnp.where(tril, wsp_ref[g], 0.0).astype(BF16) for g in (2 * gp, 2 * gp + 1)],
            axis=1))
    ob_chunks = []
    for c in range(TM_MIX // CHUNK):
        rows = slice(c * CHUNK, (c + 1) * CHUNK)
        cols = []
        for gp in range(WIDTH // LANES):
            vpair = vs_ref[rows, gp * LANES:(gp + 1) * LANES]
            zero = jnp.zeros_like(vpair)
            stacked = jnp.concatenate([jnp.where(first, vpair, zero),
                                       jnp.where(first, zero, vpair)], axis=0)
            cols.append(jnp.dot(w_pairs[gp], stacked, preferred_element_type=F32))
        mixed = jnp.concatenate(cols, axis=-1) + bsp_ref[...]
        ob_chunks.append((u_ref[rows, :].astype(F32) * mixed).astype(BF16))
    ob = jnp.concatenate(ob_chunks, axis=0)

    h = _rms(xprev_ref[...], gf_ref[...])
    _store_row_tiles(h_ref, h)
    choice = _choose_tile(h.astype(BF16), wr_ref, br_ref)

    up_a = jnp.dot(oa_ref[...], wua_b[...], preferred_element_type=F32)
    up_b = jnp.dot(ob, wub_b[...], preferred_element_type=F32)
    merged = ga_ref[...].astype(F32) * up_a + gb_ref[...].astype(F32) * up_b
    x = x_ref[...] + jnp.dot(merged.astype(BF16), wo_b[...], preferred_element_type=F32)
    xo_ref[...] = x
    xprev_ref[...] = x

    route = _rank_tile(choice, run_ref, step > 0)
    route_ref[...] = route
    fields_ref[...] = route.T[:ROUTE_FIELDS, :]
    cnt_ref[...] = run_ref[...]


def _mix(layer, x, oa, u, vs, ga, gb, wsp, bsp, wua, wub, wo, gf, wr, br):
    n = x.shape[0]
    tiles = n // TM_MIX
    cur = lambda i: jnp.minimum(i, tiles - 1)
    prev = lambda i: jnp.maximum(i - 1, 0)
    row = lambda width: pl.BlockSpec((TM_MIX, width), lambda i: (cur(i), 0))
    lay = lambda *shape: pl.BlockSpec((None,) + shape, lambda i: (layer,) + (0,) * len(shape))
    once = lambda *shape: pl.BlockSpec((None,) + shape, lambda i: (layer,) + (0,) * len(shape),
                                       pipeline_mode=pl.Buffered(1))
    return pl.pallas_call(
        _mix_kernel,
        grid=(tiles + 1,),
        in_specs=[row(D_MODEL), row(WIDTH), row(WIDTH), row(WIDTH), row(D_MODEL), row(D_MODEL),
                  lay(HEADS, CHUNK, CHUNK), lay(CHUNK, WIDTH), once(WIDTH, D_MODEL),
                  once(WIDTH, D_MODEL), once(D_MODEL, D_MODEL), lay(1, D_MODEL),
                  lay(D_MODEL, LANES), lay(1, LANES)],
        out_specs=[row(D_MODEL),
                   pl.BlockSpec((TM_MIX * ROW_TILE, LANES), lambda i: (prev(i), 0)),
                   pl.BlockSpec((TM_MIX, LANES), lambda i: (prev(i), 0)),
                   pl.BlockSpec((ROUTE_FIELDS, TM_MIX), lambda i: (0, prev(i))),
                   pl.BlockSpec((1, LANES), lambda i: (0, 0))],
        out_shape=[jax.ShapeDtypeStruct((n, D_MODEL), F32),
                   jax.ShapeDtypeStruct((n * ROW_TILE, LANES), U32),
                   jax.ShapeDtypeStruct((n, LANES), F32),
                   jax.ShapeDtypeStruct((ROUTE_FIELDS, n), F32),
                   jax.ShapeDtypeStruct((1, LANES), F32)],
        scratch_shapes=[pltpu.VMEM((1, LANES), F32), pltpu.VMEM((TM_MIX, D_MODEL), F32),
                        pltpu.VMEM((WIDTH, D_MODEL), BF16), pltpu.VMEM((WIDTH, D_MODEL), BF16),
                        pltpu.VMEM((D_MODEL, D_MODEL), BF16)],
        compiler_params=pltpu.CompilerParams(
            dimension_semantics=("arbitrary",), vmem_limit_bytes=VMEM_LIMIT),
        name="mix",
    )(x, oa, u, vs, ga, gb, wsp, bsp, wua, wub, wo, gf, wr, br)


def _dispatch_kernel(dest_ref, zrow_ref, h_ref, xs_ref, zero_ref, sem, zsem, tsem):
    i = pl.program_id(0)
    block_rows = MOE_ROWS * ROW_TILE
    nblk = xs_ref.shape[0] // block_rows
    nused = zrow_ref[N_EXPERTS]

    def zero_copy(first_row, zero_sem):
        return pltpu.make_async_copy(zero_ref, xs_ref.at[pl.ds(first_row, block_rows), :],
                                     zero_sem)

    def tail_copy(j):
        return zero_copy((nused + j) * block_rows, tsem)

    @pl.when(i == 0)
    def _():
        zero_ref[...] = jnp.zeros_like(zero_ref)
        for j in range(N_EXPERTS):
            pl.when(nused + j < nblk)(lambda j=j: tail_copy(j).start())
        for e in range(N_EXPERTS):
            pl.when(zrow_ref[e] >= 0)(lambda e=e: zero_copy(zrow_ref[e], zsem).start())
        for e in range(N_EXPERTS):
            pl.when(zrow_ref[e] >= 0)(lambda e=e: zero_copy(zrow_ref[e], zsem).wait())

    n = dest_ref.shape[0] // 2
    base = i * TM_DISP
    for t in range(TM_DISP):
        for k in range(2):
            d = dest_ref[k * n + base + t]
            pltpu.make_async_copy(h_ref.at[pl.ds(t * ROW_TILE, ROW_TILE), :],
                                  xs_ref.at[pl.ds(d * ROW_TILE, ROW_TILE), :],
                                  sem).start(priority=k)
    for k in range(2):
        pltpu.make_async_copy(h_ref, xs_ref.at[pl.ds(0, TM_DISP * ROW_TILE), :], sem).wait()

    @pl.when(i == pl.num_programs(0) - 1)
    def _():
        for j in range(N_EXPERTS):
            pl.when(nused + j < nblk)(lambda j=j: tail_copy(j).wait())


def _dispatch(dest, zrow, h, cap):
    n = h.shape[0] // ROW_TILE
    return pl.pallas_call(
        _dispatch_kernel,
        grid_spec=pltpu.PrefetchScalarGridSpec(
            num_scalar_prefetch=2,
            grid=(n // TM_DISP,),
            in_specs=[pl.BlockSpec((TM_DISP * ROW_TILE, LANES), lambda i, dest, zrow: (i, 0))],
            out_specs=pl.BlockSpec(memory_space=pl.ANY),
            scratch_shapes=[pltpu.VMEM((MOE_ROWS * ROW_TILE, LANES), U32),
                            pltpu.SemaphoreType.DMA(()), pltpu.SemaphoreType.DMA(()),
                            pltpu.SemaphoreType.DMA(())],
        ),
        out_shape=jax.ShapeDtypeStruct((cap * ROW_TILE, LANES), U32),
        compiler_params=pltpu.CompilerParams(
            dimension_semantics=("arbitrary",), vmem_limit_bytes=VMEM_LIMIT),
        name="dispatch",
    )(dest, zrow, h)


def _expert_kernel(layer, blk_e_ref, nused_ref, half_ref, next_ref, xs_ref, wi_hbm, wo_hbm,
                   ys_ref, wi_f, wo_f, wi_b, wo_b, sem):
    b = pl.program_id(0)
    used = b < nused_ref[0]
    expert = blk_e_ref[b]

    def fetch(e, half):
        return (pltpu.make_async_copy(wi_hbm.at[layer, e], wi_f.at[half], sem.at[half, 0]),
                pltpu.make_async_copy(wo_hbm.at[layer, e], wo_f.at[half], sem.at[half, 1]))

    @pl.when(b == 0)
    def _():
        for copy in fetch(expert, half_ref[expert]):
            copy.start()

    @pl.when(jnp.logical_not(used))
    def _():
        ys_ref[...] = jnp.zeros_like(ys_ref)

    new_expert = jnp.logical_or(b == 0, expert != blk_e_ref[jnp.maximum(b - 1, 0)])

    @pl.when(jnp.logical_and(used, new_expert))
    def _():
        half = half_ref[expert]
        for copy in fetch(expert, half):
            copy.wait()
        following = next_ref[expert]

        @pl.when(following >= 0)
        def _():
            for copy in fetch(following, 1 - half):
                copy.start()

        wi_b[...] = wi_f[half].astype(BF16)
        wo_b[...] = wo_f[half].astype(BF16)

    @pl.when(used)
    def _():
        xs = _load_row_tiles(xs_ref, MOE_ROWS).astype(BF16)
        gu = jnp.dot(xs, wi_b[...], preferred_element_type=F32)
        act = jax.nn.silu(gu[:, :EXPERT_FF]) * gu[:, EXPERT_FF:]
        y = jnp.dot(act.astype(BF16), wo_b[...], preferred_element_type=F32)
        _store_row_tiles(ys_ref, y)


def _experts(layer, blk_e, nused, half, following, xs, w_e_in, w_e_out):
    cap = xs.shape[0] // ROW_TILE
    rows = MOE_ROWS * ROW_TILE
    return pl.pallas_call(
        functools.partial(_expert_kernel, layer),
        grid_spec=pltpu.PrefetchScalarGridSpec(
            num_scalar_prefetch=4,
            grid=(cap // MOE_ROWS,),
            in_specs=[pl.BlockSpec((rows, LANES),
                                   lambda b, blk_e, nused, *_: (jnp.minimum(b, nused[0] - 1), 0)),
                      pl.BlockSpec(memory_space=pl.ANY),
                      pl.BlockSpec(memory_space=pl.ANY)],
            out_specs=pl.BlockSpec((rows, LANES), lambda b, *_: (b, 0)),
            scratch_shapes=[pltpu.VMEM((2, D_MODEL, 2 * EXPERT_FF), F32),
                            pltpu.VMEM((2, EXPERT_FF, D_MODEL), F32),
                            pltpu.VMEM((D_MODEL, 2 * EXPERT_FF), BF16),
                            pltpu.VMEM((EXPERT_FF, D_MODEL), BF16),
                            pltpu.SemaphoreType.DMA((2, 2))],
        ),
        out_shape=jax.ShapeDtypeStruct((cap * ROW_TILE, LANES), U32),
        compiler_params=pltpu.CompilerParams(
            dimension_semantics=("arbitrary",), vmem_limit_bytes=VMEM_LIMIT),
        name="experts",
    )(blk_e, nused, half, following, xs, w_e_in, w_e_out)


def _ple_kernel(dest_ref, x_ref, route_ref, p_ref, ys_ref, gp_ref, wg_ref, wp_ref,
                xo_ref, y00, y01, y10, y11, wg_b, wp_b, sem):
    ybuf = ((y00, y01), (y10, y11))
    j = pl.program_id(0)

    @pl.when(j == 0)
    def _():
        wg_b[...] = wg_ref[...].astype(BF16)
        wp_b[...] = wp_ref[...].astype(BF16)
    last_step = pl.num_programs(0) - 1

    def start_gather(tile, half):
        n = dest_ref.shape[0] // 2
        base = tile * TM_ROW
        for t in range(TM_ROW):
            for k in range(2):
                d = dest_ref[k * n + base + t]
                pltpu.make_async_copy(ys_ref.at[pl.ds(d * ROW_TILE, ROW_TILE), :],
                                      ybuf[half][k].at[pl.ds(t * ROW_TILE, ROW_TILE), :],
                                      sem.at[half]).start(priority=k)

    def wait_gather(half):
        for k in range(2):
            pltpu.make_async_copy(ys_ref.at[pl.ds(0, TM_ROW * ROW_TILE), :],
                                  ybuf[half][k], sem.at[half]).wait()

    def combine(half):
        rows = slice(half * TM_ROW, (half + 1) * TM_ROW)
        pe = jnp.dot(p_ref[rows, :].astype(BF16), wp_b[...], preferred_element_type=F32)
        wait_gather(half)
        route = route_ref[rows, :]
        y0 = _load_row_tiles(ybuf[half][0], TM_ROW)
        y1 = _load_row_tiles(ybuf[half][1], TM_ROW)
        x = x_ref[rows, :] + route[:, 2:3] * y0 + route[:, 3:4] * y1
        gate = jax.nn.sigmoid(jnp.dot(_rms(x, gp_ref[...]).astype(BF16), wg_b[...],
                                      preferred_element_type=F32))
        xo_ref[rows, :] = x + gate * pe

    @pl.when(j == 0)
    def _():
        start_gather(0, 0)

    start_gather(2 * j + 1, 1)
    combine(0)
    start_gather(jnp.minimum(2 * j + 2, 2 * last_step + 1), 0)
    combine(1)

    @pl.when(j == last_step)
    def _():
        wait_gather(0)


def _ple(layer, dest, x, route, p, ys, gp, wg, wp):
    n = x.shape[0]
    row = lambda width: pl.BlockSpec((2 * TM_ROW, width), lambda j, dest: (j, 0))
    lay = lambda *shape: pl.BlockSpec((None,) + shape,
                                      lambda j, dest: (layer,) + (0,) * len(shape))
    once = lambda *shape: pl.BlockSpec((None,) + shape,
                                       lambda j, dest: (layer,) + (0,) * len(shape),
                                       pipeline_mode=pl.Buffered(1))
    return pl.pallas_call(
        _ple_kernel,
        grid_spec=pltpu.PrefetchScalarGridSpec(
            num_scalar_prefetch=1,
            grid=(n // (2 * TM_ROW),),
            in_specs=[row(D_MODEL), row(LANES),
                      pl.BlockSpec((None, 2 * TM_ROW, PLE_DIM), lambda j, dest: (layer, j, 0)),
                      pl.BlockSpec(memory_space=pl.ANY),
                      lay(1, D_MODEL), once(D_MODEL, D_MODEL), once(PLE_DIM, D_MODEL)],
            out_specs=row(D_MODEL),
            scratch_shapes=[pltpu.VMEM((TM_ROW * ROW_TILE, LANES), U32)] * 4
            + [pltpu.VMEM((D_MODEL, D_MODEL), BF16), pltpu.VMEM((PLE_DIM, D_MODEL), BF16),
               pltpu.SemaphoreType.DMA((2,))],
        ),
        out_shape=jax.ShapeDtypeStruct((n, D_MODEL), F32),
        compiler_params=pltpu.CompilerParams(
            dimension_semantics=("arbitrary",), vmem_limit_bytes=VMEM_LIMIT),
        name="ple",
    )(dest, x, route, p, ys, gp, wg, wp)


def _slot_layout(fields, counts, cap):
    cnt = counts[0, ROUTE_LANE0:ROUTE_LANE0 + N_EXPERTS].astype(jnp.int32)
    padded = (cnt + MOE_ROWS - 1) // MOE_ROWS * MOE_ROWS
    pend = jnp.cumsum(padded)
    poff = pend - padded
    ids = jnp.arange(N_EXPERTS, dtype=jnp.int32)

    def slots(k):
        eid = fields[k].astype(jnp.int32)
        rank = fields[4 + k].astype(jnp.int32)
        return jnp.sum(jnp.where(eid[None, :] == ids[:, None], poff[:, None], 0), axis=0) + rank

    dest = jnp.concatenate([slots(0), slots(1)]).astype(jnp.int32)
    blk_start = jnp.arange(cap // MOE_ROWS, dtype=jnp.int32) * MOE_ROWS
    blk_e = jnp.minimum(jnp.sum(pend[None, :] <= blk_start[:, None], axis=1), N_EXPERTS - 1)
    nused = (pend[-1:] // MOE_ROWS).astype(jnp.int32)
    zrow = jnp.where(cnt > 0, (pend - MOE_ROWS) * ROW_TILE, -1).astype(jnp.int32)
    nonempty = cnt > 0
    half = ((jnp.cumsum(nonempty) - 1) % 2).astype(jnp.int32)
    later = jnp.logical_and(nonempty[None, :], ids[None, :] > ids[:, None])
    following = jnp.min(jnp.where(later, ids[None, :], N_EXPERTS), axis=1)
    following = jnp.where(following < N_EXPERTS, following, -1).astype(jnp.int32)
    return (dest, jnp.concatenate([zrow, nused]), blk_e.astype(jnp.int32), nused, half,
            following)


def kernel(x, p, norm_mix, w_in, q_norm, k_norm, sgu_norm, w_spatial, b_spatial, w_up_a, w_up_b,
           w_out, norm_ffn, w_group_router, b_group_router, w_expert_router, b_expert_router,
           w_expert_in, w_expert_out, norm_ple, w_ple_gate, w_ple_proj):
    batch, seq, d = x.shape
    depth = w_in.shape[0]
    n = batch * seq
    cap = 2 * n + N_EXPERTS * MOE_ROWS

    vec = lambda a: a[:, None, :]
    qn2, kn2 = vec(jnp.tile(q_norm, (1, 2))), vec(jnp.tile(k_norm, (1, 2)))
    b_sp = jnp.repeat(jnp.swapaxes(b_spatial, 1, 2), HEAD_DIM, axis=2)
    pad = LANES - N_GROUPS - N_EXPERTS
    w_r = jnp.pad(jnp.concatenate([w_group_router, w_expert_router], axis=2),
                  ((0, 0), (0, 0), (0, pad))).astype(BF16)
    b_r = vec(jnp.pad(jnp.concatenate([b_group_router, b_expert_router], axis=1),
                      ((0, 0), (0, pad))))
    p2 = p.reshape(depth, n, PLE_DIM)

    xf = x.reshape(n, d)
    for i in range(depth):
        q, k, v, u, vs, ga, gb = _inproj(i, xf, vec(norm_mix), w_in, qn2, kn2, vec(sgu_norm))
        oa = _attention(q, k, v, batch, seq)
        xf, h, route, fields, counts = _mix(i, xf, oa, u, vs, ga, gb, w_spatial, b_sp, w_up_a,
                                            w_up_b, w_out, vec(norm_ffn), w_r, b_r)
        dest, zrow, blk_e, nused, half, following = _slot_layout(fields, counts, cap)
        xs = _dispatch(dest, zrow, h, cap)
        ys = _experts(i, blk_e, nused, half, following, xs, w_expert_in, w_expert_out)
        xf = _ple(i, dest, xf, route, p2, ys, vec(norm_ple), w_ple_gate, w_ple_proj)
    return xf.reshape(batch, seq, d)
```

```python
import functools
import math

import jax
import jax.numpy as jnp
from jax import lax
from jax.experimental import pallas as pl
from jax.experimental.pallas import tpu as pltpu

F32 = jnp.float32
BF16 = jnp.bfloat16

D_MODEL = 1024
HEADS = 8
HEAD_DIM = 64
WIDTH = HEADS * HEAD_DIM
CHUNK = 128
IN_WIDTH = 5 * WIDTH + 2 * D_MODEL
N_GROUPS = 4
EXPERTS_PER_GROUP = 8
N_EXPERTS = N_GROUPS * EXPERTS_PER_GROUP
EXPERT_FF = 512
PLE_DIM = 256
EPS = 1e-6

LANES = 128
ROW_TILE = D_MODEL // (2 * LANES)
U32 = jnp.uint32
ROUTE_LANE0 = N_GROUPS
ROUTE_FIELDS = 8
TM_IN = 256
TM_MIX = 512
TM_DISP = 1024
TM_ROW = 256
MOE_ROWS = 512
TOP_ROWS = 64
LOG2E = 1.4426950408889634
DEAD_LOG2 = -105.0 * LOG2E
VMEM_LIMIT = 48 * 1024 * 1024


def _rms(x, gain):
    return x * lax.rsqrt(jnp.mean(x * x, axis=-1, keepdims=True) + EPS) * gain


def _store_row_tiles(ref, value, lead=()):
    rows = value.shape[0]
    bits = lambda v: lax.bitcast_convert_type(v.astype(BF16).astype(F32), U32)
    for c in range(ROW_TILE):
        low = bits(value[:, 2 * c * LANES:(2 * c + 1) * LANES]) >> 16
        high = bits(value[:, (2 * c + 1) * LANES:(2 * c + 2) * LANES]) & U32(0xFFFF0000)
        ref[lead + (pl.ds(c, rows, stride=ROW_TILE), slice(None))] = low | high


def _load_row_tiles(ref, rows, lead=()):
    chunks = []
    for c in range(ROW_TILE):
        words = ref[lead + (pl.ds(c, rows, stride=ROW_TILE), slice(None))]
        chunks.append(lax.bitcast_convert_type(words << 16, F32))
        chunks.append(lax.bitcast_convert_type(words & U32(0xFFFF0000), F32))
    return jnp.concatenate(chunks, axis=1)


def _inproj_kernel(x_ref, g_ref, wf_ref, qn_ref, kn_ref, sn_ref,
                   q_ref, k_ref, v_ref, u_ref, vs_ref, ga_ref, gb_ref, w_ref):
    @pl.when(pl.program_id(0) == 0)
    def _():
        w_ref[...] = wf_ref[...].astype(BF16)

    hb = _rms(x_ref[...], g_ref[...]).astype(BF16)

    def proj(lo, hi):
        return jnp.dot(hb, w_ref[:, lo:hi], preferred_element_type=F32)

    first = lax.broadcasted_iota(jnp.int32, (1, LANES), 1) < HEAD_DIM

    def head_norm(z, gain, scale):
        outs = []
        for c in range(WIDTH // LANES):
            zc = z[:, c * LANES:(c + 1) * LANES]
            sq = zc * zc
            sa = jnp.sum(jnp.where(first, sq, 0.0), axis=-1, keepdims=True)
            sb = jnp.sum(jnp.where(first, 0.0, sq), axis=-1, keepdims=True)
            ms = jnp.where(first, sa, sb) * (1.0 / HEAD_DIM)
            outs.append(zc * lax.rsqrt(ms + EPS) * (gain * scale))
        return jnp.concatenate(outs, axis=-1)

    w = WIDTH
    q_ref[...] = head_norm(proj(0, w), qn_ref[...], LOG2E / math.sqrt(HEAD_DIM)).astype(BF16)
    k_ref[...] = head_norm(proj(w, 2 * w), kn_ref[...], 1.0).astype(BF16)
    v_ref[...] = proj(2 * w, 3 * w).astype(BF16)
    u_ref[...] = jax.nn.gelu(proj(3 * w, 4 * w)).astype(BF16)
    vs_ref[...] = _rms(jax.nn.gelu(proj(4 * w, 5 * w)), sn_ref[...]).astype(BF16)
    ga_ref[...] = jax.nn.sigmoid(proj(5 * w, 5 * w + D_MODEL)).astype(BF16)
    gb_ref[...] = jax.nn.sigmoid(proj(5 * w + D_MODEL, IN_WIDTH)).astype(BF16)


def _inproj(layer, x, norm_mix, w_in, qn, kn, sn):
    n = x.shape[0]
    row = lambda width: pl.BlockSpec((TM_IN, width), lambda i: (i, 0))
    vec = lambda width: pl.BlockSpec((None, 1, width), lambda i: (layer, 0, 0))
    out = lambda width: jax.ShapeDtypeStruct((n, width), BF16)
    return pl.pallas_call(
        _inproj_kernel,
        grid=(n // TM_IN,),
        in_specs=[row(D_MODEL), vec(D_MODEL),
                  pl.BlockSpec((None, D_MODEL, IN_WIDTH), lambda i: (layer, 0, 0),
                               pipeline_mode=pl.Buffered(1)),
                  vec(LANES), vec(LANES), vec(WIDTH)],
        out_specs=[row(WIDTH)] * 5 + [row(D_MODEL)] * 2,
        out_shape=[out(WIDTH)] * 5 + [out(D_MODEL)] * 2,
        scratch_shapes=[pltpu.VMEM((D_MODEL, IN_WIDTH), BF16)],
        compiler_params=pltpu.CompilerParams(
            dimension_semantics=("arbitrary",), vmem_limit_bytes=VMEM_LIMIT),
        name="inproj",
    )(x, norm_mix, w_in, qn, kn, sn)


def _attn_kernel(q_ref, k_ref, v_ref, o_ref, carry_ref, acc_ref, z_ref):
    qb = pl.program_id(1)
    pairs = WIDTH // LANES
    first = lax.broadcasted_iota(jnp.int32, (CHUNK, LANES), 1) < HEAD_DIM
    first_top = lax.broadcasted_iota(jnp.int32, (TOP_ROWS, LANES), 1) < HEAD_DIM
    kr = lax.broadcasted_iota(jnp.int32, (LANES, 2 * LANES), 0)
    kc = lax.broadcasted_iota(jnp.int32, (LANES, 2 * LANES), 1)
    tail = jnp.where(jnp.logical_or(kr > kc, kc >= LANES), 1.0, 0.0).astype(BF16)
    sign = jnp.uint32(0x80000000)
    cols = [slice(p * LANES, (p + 1) * LANES) for p in range(pairs)]

    def stack(value, rows):
        if rows == CHUNK:
            return value
        return jnp.concatenate([value[:rows], value[CHUNK:CHUNK + rows]], axis=0)

    queries, causal = {}, {}
    for rows in (CHUNK, TOP_ROWS):
        lane = lax.broadcasted_iota(jnp.int32, (2 * rows, LANES), 1)
        row = lax.broadcasted_iota(jnp.int32, (2 * rows, LANES), 0)
        upper = row < rows
        own = jnp.logical_xor(lane < HEAD_DIM, jnp.logical_not(upper))
        causal[rows] = lane < jnp.where(upper, row, row - rows)
        queries[rows] = []
        for p in range(pairs):
            qp = q_ref[:rows, cols[p]]
            q2 = jnp.concatenate([qp, qp], axis=0)
            queries[rows].append(jnp.where(own, q2, jnp.zeros_like(q2)))
    full_row = lax.broadcasted_iota(jnp.int32, (2 * CHUNK, LANES), 0)
    is_top = jnp.where(full_row < CHUNK, full_row, full_row - CHUNK) < TOP_ROWS

    def scores(j, rows):
        start = pl.multiple_of(j * CHUNK, CHUNK)
        return [lax.dot_general(queries[rows][p], k_ref[pl.ds(start, CHUNK), cols[p]],
                                (((1,), (1,)), ((), ())), preferred_element_type=F32)
                for p in range(pairs)]

    def sweep(j, heights, diagonal=False, fresh=False, ahead=True, rest_only=False):
        blocks = range(len(heights))
        full = [b for b in blocks if heights[b] == CHUNK]
        top = [b for b in blocks if heights[b] == TOP_ROWS]
        starts = [pl.multiple_of((j - b) * CHUNK, CHUNK) for b in blocks]
        zs = []
        for b in blocks:
            if b > 0 or fresh or diagonal:
                zs.append(scores(j - b, heights[b]))
            else:
                zs.append([stack(z_ref[p], heights[b]) for p in range(pairs)])
        if ahead:
            for p, z in enumerate(scores(jnp.maximum(j - len(heights), 0), CHUNK)):
                z_ref[p] = z
        log_betas, splits = {}, {}
        for b in blocks:
            for p in range(pairs):
                z = zs[b][p]
                neg_abs = lax.bitcast_convert_type(
                    lax.bitcast_convert_type(z, jnp.uint32) | sign, F32)
                log1p = jnp.log(1.0 + jnp.exp2(neg_abs)) * LOG2E
                log_beta = jnp.minimum(z, 0.0) - log1p
                log_keep = log_beta - z
                if diagonal and b == 0:
                    log_keep = jnp.where(causal[heights[b]], log_keep, 0.0)
                if rest_only:
                    log_keep = jnp.where(is_top, 0.0, log_keep)
                log_betas[b, p] = log_beta
                splits[b, p] = log_keep.astype(BF16)
        sums = {key: jnp.dot(s, tail, preferred_element_type=F32) for key, s in splits.items()}
        entering, after_full, after_top = {}, [], []
        for p in range(pairs):
            carry = 0.0 if diagonal else carry_ref[p]
            for b in full:
                entering[b, p] = carry
                carry = carry + sums[b, p][:, LANES:]
            after_full.append(carry)
            carry = stack(carry, TOP_ROWS) if top else None
            for b in top:
                entering[b, p] = carry
                carry = carry + sums[b, p][:, LANES:]
            after_top.append(carry)
        alive = lambda c: jnp.max(c) > DEAD_LOG2
        worst = functools.reduce(jnp.maximum, after_full)
        live_rest = alive(jnp.where(is_top, -jnp.inf, worst)) if full else jnp.bool_(False)
        if top:
            live_top = alive(functools.reduce(jnp.maximum, after_top))
        else:
            live_top = alive(jnp.where(is_top, worst, -jnp.inf))
        outs = {}
        for b in blocks:
            for p in range(pairs):
                a = jnp.exp2(log_betas[b, p] + sums[b, p][:, :LANES] + entering[b, p])
                if diagonal and b == 0:
                    a = jnp.where(causal[heights[b]], a, 0.0)
                if rest_only:
                    a = jnp.where(is_top, 0.0, a)
                outs[b, p] = jnp.dot(a.astype(BF16), v_ref[pl.ds(starts[b], CHUNK), cols[p]],
                                     preferred_element_type=F32)
        for p in range(pairs):
            if full:
                carry_ref[p] = after_full[p]
            if top:
                carry_ref[p, 0:TOP_ROWS, :] = after_top[p][:TOP_ROWS]
                carry_ref[p, CHUNK:CHUNK + TOP_ROWS, :] = after_top[p][TOP_ROWS:]
        for p in range(pairs):
            if full:
                o2 = functools.reduce(jnp.add, [outs[b, p] for b in full])
                o = jnp.where(first, o2[:CHUNK], o2[CHUNK:])
                acc_ref[:, cols[p]] = o if diagonal else acc_ref[:, cols[p]] + o
            if top:
                o2 = functools.reduce(jnp.add, [outs[b, p] for b in top])
                o = jnp.where(first_top, o2[:TOP_ROWS], o2[TOP_ROWS:])
                acc_ref[0:TOP_ROWS, cols[p]] += o
        return live_top, live_rest

    def sweep_left(j, live_top, live_rest, height):
        def cond(state):
            j, live_top, live_rest = state
            return jnp.logical_and(j >= 0, jnp.logical_or(live_top, live_rest))

        def body(state):
            return (state[0] - 1,) + sweep(state[0], [height])

        lax.while_loop(cond, body, (j, live_top, live_rest))

    def start(n_full, n_top):
        live_top, live_rest = sweep(qb, [CHUNK] * n_full + [TOP_ROWS] * n_top, diagonal=True)
        if n_top == 0:
            return
        left = qb - n_full - n_top

        @pl.when(live_rest)
        def _():
            flags = sweep(qb - n_full, [CHUNK] * n_top, fresh=True, ahead=False, rest_only=True)
            sweep_left(left, *flags, CHUNK)

        @pl.when(jnp.logical_not(live_rest))
        def _():
            sweep_left(left, live_top, jnp.bool_(False), TOP_ROWS)

    for blocks_left, shape in enumerate(((1, 0), (2, 0))):
        pl.when(qb == blocks_left)(functools.partial(start, *shape))
    pl.when(qb >= 2)(functools.partial(start, 2, 1))

    o_ref[...] = acc_ref[...].astype(BF16)


def _attention(q, k, v, batch, seq):
    q3, k3, v3 = (t.reshape(batch, seq, WIDTH) for t in (q, k, v))
    blk = pl.BlockSpec((None, CHUNK, WIDTH), lambda b, qb: (b, qb, 0))
    full = pl.BlockSpec((None, seq, WIDTH), lambda b, qb: (b, 0, 0), pipeline_mode=pl.Buffered(1))
    o = pl.pallas_call(
        _attn_kernel,
        grid=(batch, seq // CHUNK),
        in_specs=[blk, full, full],
        out_specs=blk,
        out_shape=jax.ShapeDtypeStruct((batch, seq, WIDTH), BF16),
        scratch_shapes=[pltpu.VMEM((WIDTH // LANES, 2 * CHUNK, LANES), F32),
                        pltpu.VMEM((CHUNK, WIDTH), F32),
                        pltpu.VMEM((WIDTH // LANES, 2 * CHUNK, LANES), F32)],
        compiler_params=pltpu.CompilerParams(
            dimension_semantics=("arbitrary", "arbitrary"), vmem_limit_bytes=VMEM_LIMIT),
        name="attn",
    )(q3, k3, v3)
    return o.reshape(batch * seq, WIDTH)


def _choose_tile(hb, wr_ref, br_ref):
    logits = jnp.dot(hb, wr_ref[...], preferred_element_type=F32) + br_ref[...]
    lane_t = lax.broadcasted_iota(jnp.int32, (TM_MIX, LANES), 1)
    lanef = lane_t.astype(F32)
    neg = -jnp.inf
    far = float(LANES)

    def first_max(vals):
        m = jnp.max(vals, axis=-1, keepdims=True)
        idx = jnp.min(jnp.where(vals == m, lanef, far), axis=-1, keepdims=True)
        return m, idx

    gl = jnp.where(lane_t < N_GROUPS, logits, neg)
    gmax, grp = first_max(gl)
    grp_w = 1.0 / jnp.sum(jnp.exp(gl - gmax), axis=-1, keepdims=True)
    lo_lane = ROUTE_LANE0 + EXPERTS_PER_GROUP * grp
    in_group = jnp.logical_and(lanef >= lo_lane, lanef < lo_lane + EXPERTS_PER_GROUP)
    el = jnp.where(in_group, logits, neg)
    m1, i1 = first_max(el)
    m2, i2 = first_max(jnp.where(lanef == i1, neg, el))
    e21 = jnp.exp(m2 - m1)
    w1 = grp_w / (1.0 + e21)
    w2 = w1 * e21
    return i1, i2, w1, w2


def _rank_tile(choice, run_ref, valid):
    i1, i2, w1, w2 = choice
    lane_t = lax.broadcasted_iota(jnp.int32, (TM_MIX, LANES), 1)
    lanef = lane_t.astype(F32)
    onehot = jnp.logical_and(jnp.logical_or(lanef == i1, lanef == i2), valid)
    rt = lax.broadcasted_iota(jnp.int32, (TM_MIX, TM_MIX), 0)
    ct = lax.broadcasted_iota(jnp.int32, (TM_MIX, TM_MIX), 1)
    before = jnp.where(ct < rt, 1.0, 0.0).astype(BF16)
    prior = jnp.dot(before, jnp.where(onehot, 1.0, 0.0).astype(BF16),
                    preferred_element_type=F32) + run_ref[...]
    r1 = jnp.sum(jnp.where(lanef == i1, prior, 0.0), axis=-1, keepdims=True)
    r2 = jnp.sum(jnp.where(lanef == i2, prior, 0.0), axis=-1, keepdims=True)
    run_ref[...] += jnp.sum(jnp.where(onehot, 1.0, 0.0), axis=0, keepdims=True)

    fields = (i1 - ROUTE_LANE0, i2 - ROUTE_LANE0, w1, w2, r1, r2)
    route = jnp.zeros((TM_MIX, LANES), F32)
    for pos, val in enumerate(fields):
        route = jnp.where(lane_t == pos, val, route)
    return route


def _mix_kernel(x_ref, oa_ref, u_ref, vs_ref, ga_ref, gb_ref,
                wsp_ref, bsp_ref, wua_ref, wub_ref, wo_ref, gf_ref, wr_ref, br_ref,
                xo_ref, h_ref, route_ref, fields_ref, cnt_ref, run_ref, xprev_ref,
                wua_b, wub_b, wo_b):
    step = pl.program_id(0)

    @pl.when(step == 0)
    def _():
        run_ref[...] = jnp.zeros_like(run_ref)
        xprev_ref[...] = jnp.zeros_like(xprev_ref)
        wua_b[...] = wua_ref[...].astype(BF16)
        wub_b[...] = wub_ref[...].astype(BF16)
        wo_b[...] = wo_ref[...].astype(BF16)

    lane = lax.broadcasted_iota(jnp.int32, (CHUNK, LANES), 1)
    row = lax.broadcasted_iota(jnp.int32, (CHUNK, LANES), 0)
    first = lane < HEAD_DIM
    tril = lane <= row

    w_pairs = []
    for gp in range(WIDTH // LANES):
        w_pairs.append(jnp.concatenate(
            [jnp.where(tril, wsp_ref[g], 0.0).astype(BF16) for g in (2 * gp, 2 * gp + 1)],
            axis=1))
    ob_chunks = []
    for c in range(TM_MIX // CHUNK):
        rows = slice(c * CHUNK, (c + 1) * CHUNK)
        cols = []
        for gp in range(WIDTH // LANES):
            vpair = vs_ref[rows, gp * LANES:(gp + 1) * LANES]
            zero = jnp.zeros_like(vpair)
            stacked = jnp.concatenate([jnp.where(first, vpair, zero),
                                       jnp.where(first, zero, vpair)], axis=0)
            cols.append(jnp.dot(w_pairs[gp], stacked, preferred_element_type=F32))
        mixed = jnp.concatenate(cols, axis=-1) + bsp_ref[...]
        ob_chunks.append((u_ref[rows, :].astype(F32) * mixed).astype(BF16))
    ob = jnp.concatenate(ob_chunks, axis=0)

    h = _rms(xprev_ref[...], gf_ref[...])
    _store_row_tiles(h_ref, h)
    choice = _choose_tile(h.astype(BF16), wr_ref, br_ref)

    up_a = jnp.dot(oa_ref[...], wua_b[...], preferred_element_type=F32)
    up_b = jnp.dot(ob, wub_b[...], preferred_element_type=F32)
    merged = ga_ref[...].astype(F32) * up_a + gb_ref[...].astype(F32) * up_b
    x = x_ref[...] + jnp.dot(merged.astype(BF16), wo_b[...], preferred_element_type=F32)
    xo_ref[...] = x
    xprev_ref[...] = x

    route = _rank_tile(choice, run_ref, step > 0)
    route_ref[...] = route
    fields_ref[...] = route.T[:ROUTE_FIELDS, :]
    cnt_ref[...] = run_ref[...]


def _mix(layer, x, oa, u, vs, ga, gb, wsp, bsp, wua, wub, wo, gf, wr, br):
    n = x.shape[0]
    tiles = n // TM_MIX
    cur = lambda i: jnp.minimum(i, tiles - 1)
    prev = lambda i: jnp.maximum(i - 1, 0)
    row = lambda width: pl.BlockSpec((TM_MIX, width), lambda i: (cur(i), 0))
    lay = lambda *shape: pl.BlockSpec((None,) + shape, lambda i: (layer,) + (0,) * len(shape))
    once = lambda *shape: pl.BlockSpec((None,) + shape, lambda i: (layer,) + (0,) * len(shape),
                                       pipeline_mode=pl.Buffered(1))
    return pl.pallas_call(
        _mix_kernel,
        grid=(tiles + 1,),
        in_specs=[row(D_MODEL), row(WIDTH), row(WIDTH), row(WIDTH), row(D_MODEL), row(D_MODEL),
                  lay(HEADS, CHUNK, CHUNK), lay(CHUNK, WIDTH), once(WIDTH, D_MODEL),
                  once(WIDTH, D_MODEL), once(D_MODEL, D_MODEL), lay(1, D_MODEL),
                  lay(D_MODEL, LANES), lay(1, LANES)],
        out_specs=[row(D_MODEL),
                   pl.BlockSpec((TM_MIX * ROW_TILE, LANES), lambda i: (prev(i), 0)),
                   pl.BlockSpec((TM_MIX, LANES), lambda i: (prev(i), 0)),
                   pl.BlockSpec((ROUTE_FIELDS, TM_MIX), lambda i: (0, prev(i))),
                   pl.BlockSpec((1, LANES), lambda i: (0, 0))],
        out_shape=[jax.ShapeDtypeStruct((n, D_MODEL), F32),
                   jax.ShapeDtypeStruct((n * ROW_TILE, LANES), U32),
                   jax.ShapeDtypeStruct((n, LANES), F32),
                   jax.ShapeDtypeStruct((ROUTE_FIELDS, n), F32),
                   jax.ShapeDtypeStruct((1, LANES), F32)],
        scratch_shapes=[pltpu.VMEM((1, LANES), F32), pltpu.VMEM((TM_MIX, D_MODEL), F32),
                        pltpu.VMEM((WIDTH, D_MODEL), BF16), pltpu.VMEM((WIDTH, D_MODEL), BF16),
                        pltpu.VMEM((D_MODEL, D_MODEL), BF16)],
        compiler_params=pltpu.CompilerParams(
            dimension_semantics=("arbitrary",), vmem_limit_bytes=VMEM_LIMIT),
        name="mix",
    )(x, oa, u, vs, ga, gb, wsp, bsp, wua, wub, wo, gf, wr, br)


def _dispatch_kernel(dest_ref, zrow_ref, h_ref, xs_ref, zero_ref, sem, zsem, tsem):
    i = pl.program_id(0)
    block_rows = MOE_ROWS * ROW_TILE
    nblk = xs_ref.shape[0] // block_rows
    nused = zrow_ref[N_EXPERTS]

    def zero_copy(first_row, zero_sem):
        return pltpu.make_async_copy(zero_ref, xs_ref.at[pl.ds(first_row, block_rows), :],
                                     zero_sem)

    def tail_copy(j):
        return zero_copy((nused + j) * block_rows, tsem)

    @pl.when(i == 0)
    def _():
        zero_ref[...] = jnp.zeros_like(zero_ref)
        for j in range(N_EXPERTS):
            pl.when(nused + j < nblk)(lambda j=j: tail_copy(j).start())
        for e in range(N_EXPERTS):
            pl.when(zrow_ref[e] >= 0)(lambda e=e: zero_copy(zrow_ref[e], zsem).start())
        for e in range(N_EXPERTS):
            pl.when(zrow_ref[e] >= 0)(lambda e=e: zero_copy(zrow_ref[e], zsem).wait())

    n = dest_ref.shape[0] // 2
    base = i * TM_DISP
    for t in range(TM_DISP):
        for k in range(2):
            d = dest_ref[k * n + base + t]
            pltpu.make_async_copy(h_ref.at[pl.ds(t * ROW_TILE, ROW_TILE), :],
                                  xs_ref.at[pl.ds(d * ROW_TILE, ROW_TILE), :],
                                  sem).start(priority=k)
    for k in range(2):
        pltpu.make_async_copy(h_ref, xs_ref.at[pl.ds(0, TM_DISP * ROW_TILE), :], sem).wait()

    @pl.when(i == pl.num_programs(0) - 1)
    def _():
        for j in range(N_EXPERTS):
            pl.when(nused + j < nblk)(lambda j=j: tail_copy(j).wait())


def _dispatch(dest, zrow, h, cap):
    n = h.shape[0] // ROW_TILE
    return pl.pallas_call(
        _dispatch_kernel,
        grid_spec=pltpu.PrefetchScalarGridSpec(
            num_scalar_prefetch=2,
            grid=(n // TM_DISP,),
            in_specs=[pl.BlockSpec((TM_DISP * ROW_TILE, LANES), lambda i, dest, zrow: (i, 0))],
            out_specs=pl.BlockSpec(memory_space=pl.ANY),
            scratch_shapes=[pltpu.VMEM((MOE_ROWS * ROW_TILE, LANES), U32),
                            pltpu.SemaphoreType.DMA(()), pltpu.SemaphoreType.DMA(()),
                            pltpu.SemaphoreType.DMA(())],
        ),
        out_shape=jax.ShapeDtypeStruct((cap * ROW_TILE, LANES), U32),
        compiler_params=pltpu.CompilerParams(
            dimension_semantics=("arbitrary",), vmem_limit_bytes=VMEM_LIMIT),
        name="dispatch",
    )(dest, zrow, h)


def _expert_kernel(layer, blk_e_ref, nused_ref, half_ref, next_ref, xs_ref, wi_hbm, wo_hbm,
                   ys_ref, wi_f, wo_f, wi_b, wo_b, sem):
    b = pl.program_id(0)
    used = b < nused_ref[0]
    expert = blk_e_ref[b]

    def fetch(e, half):
        return (pltpu.make_async_copy(wi_hbm.at[layer, e], wi_f.at[half], sem.at[half, 0]),
                pltpu.make_async_copy(wo_hbm.at[layer, e], wo_f.at[half], sem.at[half, 1]))

    @pl.when(b == 0)
    def _():
        for copy in fetch(expert, half_ref[expert]):
            copy.start()

    @pl.when(jnp.logical_not(used))
    def _():
        ys_ref[...] = jnp.zeros_like(ys_ref)

    new_expert = jnp.logical_or(b == 0, expert != blk_e_ref[jnp.maximum(b - 1, 0)])

    @pl.when(jnp.logical_and(used, new_expert))
    def _():
        half = half_ref[expert]
        for copy in fetch(expert, half):
            copy.wait()
        following = next_ref[expert]

        @pl.when(following >= 0)
        def _():
            for copy in fetch(following, 1 - half):
                copy.start()

        wi_b[...] = wi_f[half].astype(BF16)
        wo_b[...] = wo_f[half].astype(BF16)

    @pl.when(used)
    def _():
        xs = _load_row_tiles(xs_ref, MOE_ROWS).astype(BF16)
        gu = jnp.dot(xs, wi_b[...], preferred_element_type=F32)
        act = jax.nn.silu(gu[:, :EXPERT_FF]) * gu[:, EXPERT_FF:]
        y = jnp.dot(act.astype(BF16), wo_b[...], preferred_element_type=F32)
        _store_row_tiles(ys_ref, y)


def _experts(layer, blk_e, nused, half, following, xs, w_e_in, w_e_out):
    cap = xs.shape[0] // ROW_TILE
    rows = MOE_ROWS * ROW_TILE
    return pl.pallas_call(
        functools.partial(_expert_kernel, layer),
        grid_spec=pltpu.PrefetchScalarGridSpec(
            num_scalar_prefetch=4,
            grid=(cap // MOE_ROWS,),
            in_specs=[pl.BlockSpec((rows, LANES),
                                   lambda b, blk_e, nused, *_: (jnp.minimum(b, nused[0] - 1), 0)),
                      pl.BlockSpec(memory_space=pl.ANY),
                      pl.BlockSpec(memory_space=pl.ANY)],
            out_specs=pl.BlockSpec((rows, LANES), lambda b, *_: (b, 0)),
            scratch_shapes=[pltpu.VMEM((2, D_MODEL, 2 * EXPERT_FF), F32),
                            pltpu.VMEM((2, EXPERT_FF, D_MODEL), F32),
                            pltpu.VMEM((D_MODEL, 2 * EXPERT_FF), BF16),
                            pltpu.VMEM((EXPERT_FF, D_MODEL), BF16),
                            pltpu.SemaphoreType.DMA((2, 2))],
        ),
        out_shape=jax.ShapeDtypeStruct((cap * ROW_TILE, LANES), U32),
        compiler_params=pltpu.CompilerParams(
            dimension_semantics=("arbitrary",), vmem_limit_bytes=VMEM_LIMIT),
        name="experts",
    )(blk_e, nused, half, following, xs, w_e_in, w_e_out)


def _ple_kernel(dest_ref, x_ref, route_ref, p_ref, ys_ref, gp_ref, wg_ref, wp_ref,
                xo_ref, y00, y01, y10, y11, wg_b, wp_b, sem):
    ybuf = ((y00, y01), (y10, y11))
    j = pl.program_id(0)

    @pl.when(j == 0)
    def _():
        wg_b[...] = wg_ref[...].astype(BF16)
        wp_b[...] = wp_ref[...].astype(BF16)
    last_step = pl.num_programs(0) - 1

    def start_gather(tile, half):
        n = dest_ref.shape[0] // 2
        base = tile * TM_ROW
        for t in range(TM_ROW):
            for k in range(2):
                d = dest_ref[k * n + base + t]
                pltpu.make_async_copy(ys_ref.at[pl.ds(d * ROW_TILE, ROW_TILE), :],
                                      ybuf[half][k].at[pl.ds(t * ROW_TILE, ROW_TILE), :],
                                      sem.at[half]).start(priority=k)

    def wait_gather(half):
        for k in range(2):
            pltpu.make_async_copy(ys_ref.at[pl.ds(0, TM_ROW * ROW_TILE), :],
                                  ybuf[half][k], sem.at[half]).wait()

    def combine(half):
        rows = slice(half * TM_ROW, (half + 1) * TM_ROW)
        pe = jnp.dot(p_ref[rows, :].astype(BF16), wp_b[...], preferred_element_type=F32)
        wait_gather(half)
        route = route_ref[rows, :]
        y0 = _load_row_tiles(ybuf[half][0], TM_ROW)
        y1 = _load_row_tiles(ybuf[half][1], TM_ROW)
        x = x_ref[rows, :] + route[:, 2:3] * y0 + route[:, 3:4] * y1
        gate = jax.nn.sigmoid(jnp.dot(_rms(x, gp_ref[...]).astype(BF16), wg_b[...],
                                      preferred_element_type=F32))
        xo_ref[rows, :] = x + gate * pe

    @pl.when(j == 0)
    def _():
        start_gather(0, 0)

    start_gather(2 * j + 1, 1)
    combine(0)
    start_gather(jnp.minimum(2 * j + 2, 2 * last_step + 1), 0)
    combine(1)

    @pl.when(j == last_step)
    def _():
        wait_gather(0)


def _ple(layer, dest, x, route, p, ys, gp, wg, wp):
    n = x.shape[0]
    row = lambda width: pl.BlockSpec((2 * TM_ROW, width), lambda j, dest: (j, 0))
    lay = lambda *shape: pl.BlockSpec((None,) + shape,
                                      lambda j, dest: (layer,) + (0,) * len(shape))
    once = lambda *shape: pl.BlockSpec((None,) + shape,
                                       lambda j, dest: (layer,) + (0,) * len(shape),
                                       pipeline_mode=pl.Buffered(1))
    return pl.pallas_call(
        _ple_kernel,
        grid_spec=pltpu.PrefetchScalarGridSpec(
            num_scalar_prefetch=1,
            grid=(n // (2 * TM_ROW),),
            in_specs=[row(D_MODEL), row(LANES),
                      pl.BlockSpec((None, 2 * TM_ROW, PLE_DIM), lambda j, dest: (layer, j, 0)),
                      pl.BlockSpec(memory_space=pl.ANY),
                      lay(1, D_MODEL), once(D_MODEL, D_MODEL), once(PLE_DIM, D_MODEL)],
            out_specs=row(D_MODEL),
            scratch_shapes=[pltpu.VMEM((TM_ROW * ROW_TILE, LANES), U32)] * 4
            + [pltpu.VMEM((D_MODEL, D_MODEL), BF16), pltpu.VMEM((PLE_DIM, D_MODEL), BF16),
               pltpu.SemaphoreType.DMA((2,))],
        ),
        out_shape=jax.ShapeDtypeStruct((n, D_MODEL), F32),
        compiler_params=pltpu.CompilerParams(
            dimension_semantics=("arbitrary",), vmem_limit_bytes=VMEM_LIMIT),
        name="ple",
    )(dest, x, route, p, ys, gp, wg, wp)


def _slot_layout(fields, counts, cap):
    cnt = counts[0, ROUTE_LANE0:ROUTE_LANE0 + N_EXPERTS].astype(jnp.int32)
    padded = (cnt + MOE_ROWS - 1) // MOE_ROWS * MOE_ROWS
    pend = jnp.cumsum(padded)
    poff = pend - padded
    ids = jnp.arange(N_EXPERTS, dtype=jnp.int32)

    def slots(k):
        eid = fields[k].astype(jnp.int32)
        rank = fields[4 + k].astype(jnp.int32)
        return jnp.sum(jnp.where(eid[None, :] == ids[:, None], poff[:, None], 0), axis=0) + rank

    dest = jnp.concatenate([slots(0), slots(1)]).astype(jnp.int32)
    blk_start = jnp.arange(cap // MOE_ROWS, dtype=jnp.int32) * MOE_ROWS
    blk_e = jnp.minimum(jnp.sum(pend[None, :] <= blk_start[:, None], axis=1), N_EXPERTS - 1)
    nused = (pend[-1:] // MOE_ROWS).astype(jnp.int32)
    zrow = jnp.where(cnt > 0, (pend - MOE_ROWS) * ROW_TILE, -1).astype(jnp.int32)
    nonempty = cnt > 0
    half = ((jnp.cumsum(nonempty) - 1) % 2).astype(jnp.int32)
    later = jnp.logical_and(nonempty[None, :], ids[None, :] > ids[:, None])
    following = jnp.min(jnp.where(later, ids[None, :], N_EXPERTS), axis=1)
    following = jnp.where(following < N_EXPERTS, following, -1).astype(jnp.int32)
    return (dest, jnp.concatenate([zrow, nused]), blk_e.astype(jnp.int32), nused, half,
            following)


def kernel(x, p, norm_mix, w_in, q_norm, k_norm, sgu_norm, w_spatial, b_spatial, w_up_a, w_up_b,
           w_out, norm_ffn, w_group_router, b_group_router, w_expert_router, b_expert_router,
           w_expert_in, w_expert_out, norm_ple, w_ple_gate, w_ple_proj):
    batch, seq, d = x.shape
    depth = w_in.shape[0]
    n = batch * seq
    cap = 2 * n + N_EXPERTS * MOE_ROWS

    vec = lambda a: a[:, None, :]
    qn2, kn2 = vec(jnp.tile(q_norm, (1, 2))), vec(jnp.tile(k_norm, (1, 2)))
    b_sp = jnp.repeat(jnp.swapaxes(b_spatial, 1, 2), HEAD_DIM, axis=2)
    pad = LANES - N_GROUPS - N_EXPERTS
    w_r = jnp.pad(jnp.concatenate([w_group_router, w_expert_router], axis=2),
                  ((0, 0), (0, 0), (0, pad))).astype(BF16)
    b_r = vec(jnp.pad(jnp.concatenate([b_group_router, b_expert_router], axis=1),
                      ((0, 0), (0, pad))))
    p2 = p.reshape(depth, n, PLE_DIM)

    xf = x.reshape(n, d)
    for i in range(depth):
        q, k, v, u, vs, ga, gb = _inproj(i, xf, vec(norm_mix), w_in, qn2, kn2, vec(sgu_norm))
        oa = _attention(q, k, v, batch, seq)
        xf, h, route, fields, counts = _mix(i, xf, oa, u, vs, ga, gb, w_spatial, b_sp, w_up_a,
                                            w_up_b, w_out, vec(norm_ffn), w_r, b_r)
        dest, zrow, blk_e, nused, half, following = _slot_layout(fields, counts, cap)
        xs = _dispatch(dest, zrow, h, cap)
        ys = _experts(i, blk_e, nused, half, following, xs, w_expert_in, w_expert_out)
        xf = _ple(i, dest, xf, route, p2, ys, vec(norm_ple), w_ple_gate, w_ple_proj)
    return xf.reshape(batch, seq, d)
```
